```python
import jax, jax.numpy as jnp
from jax import lax
import numpy as np

D_MODEL = 4096
BATCH = 1
SEQ = 8192
DEPTH = 1

NORM_EPS = 1e-6
RWKV_HEAD_DIM = 64
RWKV_WIDTH = D_MODEL // 2
RWKV_HEADS = RWKV_WIDTH // RWKV_HEAD_DIM
DECAY_LORA = 128
AAA_LORA = 128
GATE_LORA = 480
LN_X_EPS = 64e-5
MOBA_HEAD_DIM = 128
MOBA_WIDTH = D_MODEL // 2
MOBA_HEADS = MOBA_WIDTH // MOBA_HEAD_DIM
MOBA_BLOCK = 256
MOBA_TOPK = 3
MOBA_Q_CHUNK = 64
ROPE_THETA = 10000.0
PEER_HEADS = 8
PEER_N_KEYS = 128
PEER_N_EXPERTS = PEER_N_KEYS * PEER_N_KEYS
PEER_QUERY_DIM = 256
PEER_HALF = PEER_QUERY_DIM // 2
PEER_TOPK = 16
PEER_TOK_CHUNK = 128

RWKV_SHIFT_WIDTH = 3 * RWKV_WIDTH + DECAY_LORA + AAA_LORA + GATE_LORA
IN_WIDTH = RWKV_SHIFT_WIDTH + 3 * MOBA_WIDTH + 2 * D_MODEL
RWKV_SPLIT_IDX = (RWKV_WIDTH, 2 * RWKV_WIDTH, 3 * RWKV_WIDTH, 3 * RWKV_WIDTH + DECAY_LORA, 3 * RWKV_WIDTH + DECAY_LORA + AAA_LORA)
MOBA_SPLIT_IDX = (MOBA_WIDTH, 2 * MOBA_WIDTH, 3 * MOBA_WIDTH, 3 * MOBA_WIDTH + D_MODEL)

kernel_name = 'hybrid_rwkv7_moba_peer_block'


def _rms_norm(x, g):
    xf = x.astype(jnp.float32)
    y = xf * lax.rsqrt(jnp.mean(xf * xf, axis=-1, keepdims=True) + NORM_EPS)
    return (y * g.astype(jnp.float32)).astype(x.dtype)


def _token_shift(z, mu):
    z_prev = jnp.pad(z, ((0, 0), (1, 0), (0, 0)))[:, :-1]
    return z + (z_prev - z) * mu


def _rope(x, positions):
    half = x.shape[-1] // 2
    inv_freq = ROPE_THETA ** (-jnp.arange(half, dtype=jnp.float32) / half)
    ang = positions.astype(jnp.float32)[:, None] * inv_freq[None, :]
    cos = jnp.cos(ang)[None, :, None, :]
    sin = jnp.sin(ang)[None, :, None, :]
    xf = x.astype(jnp.float32)
    x1, x2 = xf[..., :half], xf[..., half:]
    return jnp.concatenate([x1 * cos - x2 * sin, x2 * cos + x1 * sin], axis=-1).astype(x.dtype)


def _rwkv7_time_mix(z, w0, w_up, a0, a_up, g_up, k_k, k_a, r_k, lnx_g, lnx_b):
    B, S, _ = z.shape
    f32 = jnp.float32
    r, k, v, wl, al, gl = jnp.split(z, RWKV_SPLIT_IDX, axis=-1)
    w = -jax.nn.softplus(-(w0 + jnp.tanh(wl) @ w_up)) - 0.5
    decay = jnp.exp(-jnp.exp(w.astype(f32)))
    a = jax.nn.sigmoid(a0 + al @ a_up)
    g = jax.nn.sigmoid(gl) @ g_up
    kk = k * k_k
    k = k * (1.0 + (a - 1.0) * k_a)

    def heads(t):
        return t.astype(f32).reshape(B, S, RWKV_HEADS, RWKV_HEAD_DIM)

    r_h, k_h, v_h, w_h, a_h, kk_h = (heads(t) for t in (r, k, v, decay, a, kk))
    kk_h = kk_h / jnp.maximum(jnp.sqrt(jnp.sum(kk_h * kk_h, axis=-1, keepdims=True)), 1e-12)

    def step(state, inp):
        r_t, w_t, k_t, v_t, kk_t, a_t = inp
        sa = jnp.einsum('bhvk,bhk->bhv', state, -kk_t)
        state = (state * w_t[:, :, None, :]
                 + sa[..., None] * (kk_t * a_t)[:, :, None, :]
                 + v_t[..., None] * k_t[:, :, None, :])
        return state, jnp.einsum('bhvk,bhk->bhv', state, r_t)

    xs = tuple(t.swapaxes(0, 1) for t in (r_h, w_h, k_h, v_h, kk_h, a_h))
    state0 = jnp.zeros((B, RWKV_HEADS, RWKV_HEAD_DIM, RWKV_HEAD_DIM), f32)
    _, y = lax.scan(step, state0, xs)
    y = y.swapaxes(0, 1)
    mu = jnp.mean(y, axis=-1, keepdims=True)
    var = jnp.mean(jnp.square(y - mu), axis=-1, keepdims=True)
    y = ((y - mu) * lax.rsqrt(var + LN_X_EPS)).reshape(B, S, RWKV_WIDTH)
    y = y * lnx_g.astype(f32) + lnx_b.astype(f32)
    bonus = (jnp.sum(r_h * k_h * r_k.astype(f32), axis=-1, keepdims=True) * v_h).reshape(B, S, RWKV_WIDTH)
    return ((y + bonus) * g.astype(f32)).astype(z.dtype)


def _moba_attention(q, k, v):
    B, S, H, Dh = q.shape
    nb = -(-S // MOBA_BLOCK)
    pad = nb * MOBA_BLOCK - S
    kp = jnp.pad(k, ((0, 0), (0, pad), (0, 0), (0, 0)))
    vp = jnp.pad(v, ((0, 0), (0, pad), (0, 0), (0, 0)))
    k_blocks = kp.reshape(B, nb, MOBA_BLOCK, H, Dh)
    k_mean = jnp.mean(k_blocks, axis=2)
    kb = k_blocks.transpose(0, 3, 1, 2, 4)
    vb = vp.reshape(B, nb, MOBA_BLOCK, H, Dh).transpose(0, 3, 1, 2, 4)
    n_sel = max(1, min(MOBA_TOPK, nb - 1))
    scale = Dh ** -0.5
    b_idx = jnp.arange(B)[:, None, None, None]
    h_idx = jnp.arange(H)[None, :, None, None]
    n_chunks = S // MOBA_Q_CHUNK
    qc = q.reshape(B, n_chunks, MOBA_Q_CHUNK, H, Dh).swapaxes(0, 1)

    def one_chunk(args):
        c, q_c = args
        start = c * MOBA_Q_CHUNK
        blk = start // MOBA_BLOCK
        q_pos = start + jnp.arange(MOBA_Q_CHUNK)
        gate = jnp.einsum('bqhd,bnhd->bhqn', q_c, k_mean).astype(jnp.float32)
        gate = jnp.where(jnp.arange(nb) < blk, gate, -jnp.inf)
        _, sel = lax.top_k(gate, n_sel)
        valid = jnp.arange(n_sel) < blk
        k_sel = kb[b_idx, h_idx, sel]
        v_sel = vb[b_idx, h_idx, sel]
        s_sel = jnp.einsum('bqhd,bhqjkd->bhqjk', q_c, k_sel).astype(jnp.float32) * scale
        s_sel = jnp.where(valid[:, None], s_sel, -jnp.inf)
        k_own = lax.dynamic_slice_in_dim(kp, blk * MOBA_BLOCK, MOBA_BLOCK, axis=1)
        v_own = lax.dynamic_slice_in_dim(vp, blk * MOBA_BLOCK, MOBA_BLOCK, axis=1)
        s_own = jnp.einsum('bqhd,bkhd->bhqk', q_c, k_own).astype(jnp.float32) * scale
        k_pos = blk * MOBA_BLOCK + jnp.arange(MOBA_BLOCK)
        s_own = jnp.where(k_pos[None, :] <= q_pos[:, None], s_own, -jnp.inf)
        scores = jnp.concatenate([s_own, s_sel.reshape(B, H, MOBA_Q_CHUNK, n_sel * MOBA_BLOCK)], axis=-1)
        p = jax.nn.softmax(scores, axis=-1).astype(v.dtype)
        p_own = p[..., :MOBA_BLOCK]
        p_sel = p[..., MOBA_BLOCK:].reshape(B, H, MOBA_Q_CHUNK, n_sel, MOBA_BLOCK)
        return (jnp.einsum('bhqk,bkhd->bqhd', p_own, v_own)
                + jnp.einsum('bhqjk,bhqjkd->bqhd', p_sel, v_sel))

    out = lax.map(one_chunk, (jnp.arange(n_chunks), qc))
    return out.swapaxes(0, 1).reshape(B, S, H * Dh)


def _peer(h, w_q, sub_keys, expert_u, expert_v):
    B, S, D = h.shape
    q = (h @ w_q).reshape(B, S, PEER_HEADS, 2, PEER_HALF)
    s = jnp.einsum('bshpd,hpnd->bshpn', q, sub_keys).astype(jnp.float32)
    s_top, i_top = lax.top_k(s, PEER_TOPK)
    n_cand = PEER_TOPK * PEER_TOPK
    cand_s = (s_top[..., 0, :, None] + s_top[..., 1, None, :]).reshape(B, S, PEER_HEADS, n_cand)
    cand_i = (i_top[..., 0, :, None] * PEER_N_KEYS + i_top[..., 1, None, :]).reshape(B, S, PEER_HEADS, n_cand)
    f_s, f_pos = lax.top_k(cand_s, PEER_TOPK)
    idx = jnp.take_along_axis(cand_i, f_pos, axis=-1)
    g = jax.nn.softmax(f_s, axis=-1)
    n_e = PEER_HEADS * PEER_TOPK
    n_chunks = (B * S) // PEER_TOK_CHUNK
    hc = h.reshape(n_chunks, PEER_TOK_CHUNK, D)
    ic = idx.reshape(n_chunks, PEER_TOK_CHUNK, n_e)
    gc = g.astype(h.dtype).reshape(n_chunks, PEER_TOK_CHUNK, n_e)

    def one_chunk(args):
        h_c, i_c, g_c = args
        u = expert_u[i_c]
        v = expert_v[i_c]
        act = jax.nn.gelu(jnp.einsum('td,ted->te', h_c, u), approximate=False)
        return jnp.einsum('te,ted->td', g_c * act, v)

    out = lax.map(one_chunk, (hc, ic, gc))
    return out.reshape(B, S, D)


def setup_inputs(seed: int = 0) -> dict:
    key = jax.random.key(seed)
    ks = jax.random.split(key, 23)
    f32 = jnp.float32
    L = DEPTH

    def nrm(k, shape, scale):
        return jax.random.normal(k, shape, f32) * scale

    def gain(k, shape):
        return 1.0 + 0.02 * jax.random.normal(k, shape, f32)

    return {
        'x': nrm(ks[0], (BATCH, SEQ, D_MODEL), 1.0),
        'norm1_g': gain(ks[1], (L, D_MODEL)),
        'w_in': nrm(ks[2], (L, D_MODEL, IN_WIDTH), D_MODEL ** -0.5),
        'rwkv_mu': jax.random.uniform(ks[3], (L, RWKV_SHIFT_WIDTH), f32),
        'rwkv_w0': jax.random.uniform(ks[4], (L, RWKV_WIDTH), f32, -6.0, -1.0),
        'rwkv_w_up': nrm(ks[5], (L, DECAY_LORA, RWKV_WIDTH), 0.5 * DECAY_LORA ** -0.5),
        'rwkv_a0': nrm(ks[6], (L, RWKV_WIDTH), 0.1),
        'rwkv_a_up': nrm(ks[7], (L, AAA_LORA, RWKV_WIDTH), AAA_LORA ** -0.5),
        'rwkv_g_up': nrm(ks[8], (L, GATE_LORA, RWKV_WIDTH), GATE_LORA ** -0.5),
        'rwkv_k_k': 0.85 + 0.02 * jax.random.normal(ks[9], (L, RWKV_WIDTH), f32),
        'rwkv_k_a': gain(ks[10], (L, RWKV_WIDTH)),
        'rwkv_r_k': nrm(ks[11], (L, RWKV_HEADS, RWKV_HEAD_DIM), 0.1),
        'rwkv_lnx_g': gain(ks[12], (L, RWKV_WIDTH)),
        'rwkv_lnx_b': nrm(ks[13], (L, RWKV_WIDTH), 0.01),
        'w_branch_rwkv': nrm(ks[14], (L, RWKV_WIDTH, D_MODEL), RWKV_WIDTH ** -0.5),
        'w_branch_moba': nrm(ks[15], (L, MOBA_WIDTH, D_MODEL), MOBA_WIDTH ** -0.5),
        'w_out': nrm(ks[16], (L, D_MODEL, D_MODEL), D_MODEL ** -0.5),
        'norm2_g': gain(ks[17], (L, D_MODEL)),
        'peer_w_q': nrm(ks[18], (L, D_MODEL, PEER_HEADS * PEER_QUERY_DIM), D_MODEL ** -0.5),
        'peer_sub_keys': nrm(ks[19], (L, PEER_HEADS, 2, PEER_N_KEYS, PEER_HALF), PEER_HALF ** -0.5),
        'peer_u': nrm(ks[20], (L, PEER_N_EXPERTS, D_MODEL), D_MODEL ** -0.5),
        'peer_v': nrm(ks[21], (L, PEER_N_EXPERTS, D_MODEL), PEER_TOPK ** -0.5),
        'final_g': gain(ks[22], (D_MODEL,)),
    }


def reference(x, norm1_g, w_in, rwkv_mu, rwkv_w0, rwkv_w_up, rwkv_a0, rwkv_a_up, rwkv_g_up,
              rwkv_k_k, rwkv_k_a, rwkv_r_k, rwkv_lnx_g, rwkv_lnx_b, w_branch_rwkv, w_branch_moba,
              w_out, norm2_g, peer_w_q, peer_sub_keys, peer_u, peer_v, final_g):
    B, S, _ = x.shape
    positions = jnp.arange(S, dtype=jnp.int32)
    for l in range(DEPTH):
        h = _rms_norm(x, norm1_g[l])
        proj = h @ w_in[l]
        rwkv_in = _token_shift(proj[..., :RWKV_SHIFT_WIDTH], rwkv_mu[l])
        q_m, k_m, v_m, gate_a, gate_b = jnp.split(proj[..., RWKV_SHIFT_WIDTH:], MOBA_SPLIT_IDX, axis=-1)
        y_a = _rwkv7_time_mix(rwkv_in, rwkv_w0[l], rwkv_w_up[l], rwkv_a0[l], rwkv_a_up[l], rwkv_g_up[l],
                              rwkv_k_k[l], rwkv_k_a[l], rwkv_r_k[l], rwkv_lnx_g[l], rwkv_lnx_b[l]) @ w_branch_rwkv[l]
        q_m = _rope(q_m.reshape(B, S, MOBA_HEADS, MOBA_HEAD_DIM), positions)
        k_m = _rope(k_m.reshape(B, S, MOBA_HEADS, MOBA_HEAD_DIM), positions)
        v_m = v_m.reshape(B, S, MOBA_HEADS, MOBA_HEAD_DIM)
        y_b = _moba_attention(q_m, k_m, v_m) @ w_branch_moba[l]
        mixed = jax.nn.sigmoid(gate_a) * y_a + jax.nn.sigmoid(gate_b) * y_b
        x = x + mixed @ w_out[l]
        h2 = _rms_norm(x, norm2_g[l])
        x = x + _peer(h2, peer_w_q[l], peer_sub_keys[l], peer_u[l], peer_v[l])
    return _rms_norm(x, final_g)
```

```python
import functools

import jax
import jax.numpy as jnp
from jax import lax
from jax.experimental import pallas as pl
from jax.experimental.pallas import tpu as pltpu

F32 = jnp.float32
BF16 = jnp.bfloat16

NORM_EPS = 1e-6
LANES = 128
RWKV_HEAD_DIM = 64
DECAY_LORA = 128
AAA_LORA = 128
GATE_LORA = 480
GATE_LORA_PAD = 512
LN_X_EPS = 64e-5
CHUNK = 128
MOBA_HEAD_DIM = 128
MOBA_BLOCK = 256
MOBA_TOPK = 3
ROPE_THETA = 10000.0
PEER_HEADS = 8
PEER_N_KEYS = 128
PEER_HALF = 128
PEER_TOPK = 16

VMEM_LIMIT = 56 * 1024 * 1024

_NT = (((1,), (1,)), ((), ()))


def _params(sem, vmem=VMEM_LIMIT):
    return pltpu.CompilerParams(dimension_semantics=sem, vmem_limit_bytes=vmem)


def _dot(a, b):
    return jnp.dot(a, b, preferred_element_type=F32)


def _dot_nt(a, b):
    return lax.dot_general(a, b, _NT, preferred_element_type=F32)


def _split3(x):
    hi = x.astype(BF16)
    r1 = x - hi.astype(F32)
    mid = r1.astype(BF16)
    lo = (r1 - mid.astype(F32)).astype(BF16)
    return hi, mid, lo


def _dot_l3(l_bf16, x):
    hi, mid, lo = _split3(x)
    return _dot(l_bf16, hi) + _dot(l_bf16, mid) + _dot(l_bf16, lo)


def _dot_r3(x, r_bf16):
    hi, mid, lo = _split3(x)
    return _dot(hi, r_bf16) + _dot(mid, r_bf16) + _dot(lo, r_bf16)


def _rmsnorm_kernel(x_ref, g_ref, o_ref):
    x = x_ref[...]
    ms = jnp.mean(x * x, axis=-1, keepdims=True)
    o_ref[...] = (x * lax.rsqrt(ms + NORM_EPS) * g_ref[...]).astype(o_ref.dtype)


def _rmsnorm(x, g, out_dtype, tm=256):
    s, d = x.shape
    return pl.pallas_call(
        _rmsnorm_kernel,
        grid=(s // tm,),
        in_specs=[pl.BlockSpec((tm, d), lambda i: (i, 0)),
                  pl.BlockSpec((1, d), lambda i: (0, 0))],
        out_specs=pl.BlockSpec((tm, d), lambda i: (i, 0)),
        out_shape=jax.ShapeDtypeStruct((s, d), out_dtype),
        compiler_params=_params(("parallel",)),
        name="rmsnorm",
    )(x, g.reshape(1, d))


def _mm_kernel(a_ref, b_ref, o_ref):
    o_ref[...] = _dot(a_ref[...], b_ref[...]).astype(o_ref.dtype)


def _matmul(a, b, out_dtype, tm, tn, name):
    m, k = a.shape
    n = b.shape[1]
    return pl.pallas_call(
        _mm_kernel,
        grid=(m // tm, n // tn),
        in_specs=[pl.BlockSpec((tm, k), lambda i, j: (i, 0)),
                  pl.BlockSpec((k, tn), lambda i, j: (0, j))],
        out_specs=pl.BlockSpec((tm, tn), lambda i, j: (i, j)),
        out_shape=jax.ShapeDtypeStruct((m, n), out_dtype),
        compiler_params=_params(("parallel", "parallel")),
        name=name,
    )(a, b)


BF16_SUBLANES = 16


def _mm_shift_kernel(a_ref, ap_ref, b_ref, mu_ref, o_ref):
    i = pl.program_id(0)
    b = b_ref[...]
    z = _dot(a_ref[...], b)
    zp = _dot(ap_ref[...], b)
    prev = jnp.where(i > 0, zp[BF16_SUBLANES - 1:BF16_SUBLANES, :], 0.0)
    row = lax.broadcasted_iota(jnp.int32, z.shape, 0)
    z_prev = jnp.where(row == 0, prev, pltpu.roll(z, 1, axis=0))
    o_ref[...] = z + (z_prev - z) * mu_ref[...]


def _matmul_shift(a, b, mu, tm, tn):
    m, k = a.shape
    n = b.shape[1]
    per = tm // BF16_SUBLANES
    return pl.pallas_call(
        _mm_shift_kernel,
        grid=(m // tm, n // tn),
        in_specs=[pl.BlockSpec((tm, k), lambda i, j: (i, 0)),
                  pl.BlockSpec((BF16_SUBLANES, k), lambda i, j: (jnp.maximum(i * per - 1, 0), 0)),
                  pl.BlockSpec((k, tn), lambda i, j: (0, j)),
                  pl.BlockSpec((1, tn), lambda i, j: (0, j))],
        out_specs=pl.BlockSpec((tm, tn), lambda i, j: (i, j)),
        out_shape=jax.ShapeDtypeStruct((m, n), F32),
        compiler_params=_params(("parallel", "parallel")),
        name="in_proj_rwkv",
    )(a, a, b, mu)


def _gated_pair_kernel(ya_ref, wa_ref, yb_ref, wb_ref, ga_ref, gb_ref, o_ref):
    pa = _dot(ya_ref[...], wa_ref[...])
    pb = _dot(yb_ref[...], wb_ref[...])
    mixed = jax.nn.sigmoid(ga_ref[...]) * pa + jax.nn.sigmoid(gb_ref[...]) * pb
    o_ref[...] = mixed.astype(o_ref.dtype)


def _gated_pair(ya, wa, yb, wb, gates, tm, tn):
    m, k = ya.shape
    n = wa.shape[1]
    nb = n // tn
    return pl.pallas_call(
        _gated_pair_kernel,
        grid=(m // tm, nb),
        in_specs=[pl.BlockSpec((tm, k), lambda i, j: (i, 0)),
                  pl.BlockSpec((k, tn), lambda i, j: (0, j)),
                  pl.BlockSpec((tm, k), lambda i, j: (i, 0)),
                  pl.BlockSpec((k, tn), lambda i, j: (0, j)),
                  pl.BlockSpec((tm, tn), lambda i, j: (i, j)),
                  pl.BlockSpec((tm, tn), lambda i, j: (i, j + nb))],
        out_specs=pl.BlockSpec((tm, tn), lambda i, j: (i, j)),
        out_shape=jax.ShapeDtypeStruct((m, n), BF16),
        compiler_params=_params(("parallel", "parallel")),
        name="branch_merge",
    )(ya, wa, yb, wb, gates, gates)


def _mm_res_kernel(a_ref, b_ref, r_ref, o_ref):
    o_ref[...] = r_ref[...] + _dot(a_ref[...], b_ref[...])


def _matmul_residual(a, b, res, tm, tn):
    m, k = a.shape
    n = b.shape[1]
    return pl.pallas_call(
        _mm_res_kernel,
        grid=(m // tm, n // tn),
        in_specs=[pl.BlockSpec((tm, k), lambda i, j: (i, 0)),
                  pl.BlockSpec((k, tn), lambda i, j: (0, j)),
                  pl.BlockSpec((tm, tn), lambda i, j: (i, j))],
        out_specs=pl.BlockSpec((tm, tn), lambda i, j: (i, j)),
        out_shape=jax.ShapeDtypeStruct((m, n), F32),
        compiler_params=_params(("parallel", "parallel")),
        name="out_proj",
    )(a, b, res)


def _rwkv_lora_kernel(wl_ref, al_ref, gl0_ref, gl1_ref, w0_ref, wup_ref, a0_ref, aup_ref,
                      gup_ref, wpre_ref, a_ref, g_ref):
    half = GATE_LORA_PAD // 2
    wpre_ref[...] = w0_ref[...] + _dot(jnp.tanh(wl_ref[...]).astype(BF16), wup_ref[...])
    a_ref[...] = jax.nn.sigmoid(a0_ref[...] + _dot(al_ref[...].astype(BF16), aup_ref[...]))
    g_ref[...] = (_dot(jax.nn.sigmoid(gl0_ref[...]).astype(BF16), gup_ref[:half, :])
                  + _dot(jax.nn.sigmoid(gl1_ref[...]).astype(BF16), gup_ref[half:, :]))


def _rwkv_lora(proj_r, w0, w_up, a0, a_up, g_up_pad, tm=256):
    s = proj_r.shape[0]
    w = w0.shape[-1]
    half = GATE_LORA_PAD // 2
    c_wl = 3 * w // DECAY_LORA
    c_al = (3 * w + DECAY_LORA) // AAA_LORA
    c_gl = (3 * w + DECAY_LORA + AAA_LORA) // half
    row = lambda i: (i, 0)
    const = lambda i: (0, 0)
    out = jax.ShapeDtypeStruct((s, w), F32)
    return pl.pallas_call(
        _rwkv_lora_kernel,
        grid=(s // tm,),
        in_specs=[pl.BlockSpec((tm, DECAY_LORA), lambda i: (i, c_wl)),
                  pl.BlockSpec((tm, AAA_LORA), lambda i: (i, c_al)),
                  pl.BlockSpec((tm, half), lambda i: (i, c_gl)),
                  pl.BlockSpec((tm, half), lambda i: (i, c_gl + 1)),
                  pl.BlockSpec((1, w), const),
                  pl.BlockSpec((DECAY_LORA, w), const),
                  pl.BlockSpec((1, w), const),
                  pl.BlockSpec((AAA_LORA, w), const),
                  pl.BlockSpec((GATE_LORA_PAD, w), const)],
        out_specs=[pl.BlockSpec((tm, w), row)] * 3,
        out_shape=[out, out, out],
        compiler_params=_params(("parallel",)),
        name="rwkv_lora",
    )(proj_r, proj_r, proj_r, proj_r, w0.reshape(1, w), w_up, a0.reshape(1, w), a_up, g_up_pad)


def _head_pair_ones():
    r = lax.broadcasted_iota(jnp.int32, (LANES, LANES), 0) // RWKV_HEAD_DIM
    c = lax.broadcasted_iota(jnp.int32, (LANES, LANES), 1) // RWKV_HEAD_DIM
    return r == c


def _rwkv_prep_kernel(r_ref, k_ref, v_ref, wpre_ref, a_ref, kk_ref, ka_ref, rk_ref,
                      rt_ref, at_ref, kh_ref, bh_ref, kb_ref, bb_ref, pc_ref, bonus_ref):
    tm = r_ref.shape[0]
    r = r_ref[...]
    k = k_ref[...]
    v = v_ref[...]
    a = a_ref[...]
    same_head = jnp.where(_head_pair_ones(), 1.0, 0.0).astype(BF16)

    x = -wpre_ref[...]
    softplus = jnp.maximum(x, 0.0) + jnp.log1p(jnp.exp(-jnp.abs(x)))
    lw = -jnp.exp(-softplus - 0.5)

    kk = k * kk_ref[...]
    ss = _dot_r3(kk * kk, same_head)
    kkn = kk / jnp.maximum(jnp.sqrt(ss), 1e-12)
    k2 = k * (1.0 + (a - 1.0) * ka_ref[...])
    bonus_ref[...] = _dot_r3(r * k2 * rk_ref[...], same_head) * v

    ri = lax.broadcasted_iota(jnp.int32, (tm, tm), 0)
    ci = lax.broadcasted_iota(jnp.int32, (tm, tm), 1)
    same_chunk = (ri // CHUNK) == (ci // CHUNK)
    tri = jnp.where(same_chunk & (ci <= ri), 1.0, 0.0).astype(BF16)
    allc = jnp.where(same_chunk, 1.0, 0.0).astype(BF16)
    cum = _dot_l3(tri, lw)
    tot = _dot_l3(allc, lw)

    p_in = jnp.exp(cum)
    p_inv = jnp.exp(-cum)
    p_tail = jnp.exp(tot - cum)
    b = kkn * a
    rt_ref[...] = r * p_in
    at_ref[...] = -kkn * jnp.exp(cum - lw)
    kh_ref[...] = k2 * p_inv
    bh_ref[...] = b * p_inv
    kb_ref[...] = k2 * p_tail
    bb_ref[...] = b * p_tail
    pc_ref[...] = jnp.exp(tot)


def _rwkv_prep(proj_r, wpre, a, k_k, k_a, r_k, tm=256):
    s, w = wpre.shape
    npair = w // LANES
    blk = lambda off: pl.BlockSpec((tm, LANES), lambda i, j, off=off: (i, j + off))
    vec = pl.BlockSpec((1, LANES), lambda i, j: (0, j))
    out = jax.ShapeDtypeStruct((s, w), F32)
    return pl.pallas_call(
        _rwkv_prep_kernel,
        grid=(s // tm, npair),
        in_specs=[blk(0), blk(npair), blk(2 * npair), blk(0), blk(0), vec, vec, vec],
        out_specs=[blk(0)] * 8,
        out_shape=[out] * 8,
        compiler_params=_params(("parallel", "parallel")),
        name="rwkv_prep",
    )(proj_r, proj_r, proj_r, wpre, a, k_k.reshape(1, w), k_a.reshape(1, w), r_k.reshape(1, w))


def _rwkv_chunk_kernel(rt_ref, at_ref, kh_ref, bh_ref, kb_ref, bb_ref, pc_ref, v_ref,
                       x_ref, y_ref, op_ref, qb_ref, bbt_ref, g_ref, pcm_ref):
    c = CHUNK
    at = at_ref[...]
    rt = rt_ref[...]
    v = v_ref[...]
    kh = kh_ref[...].astype(BF16)
    bh = bh_ref[...].astype(BF16)
    lane = lax.broadcasted_iota(jnp.int32, (c, LANES), 1)
    ri = lax.broadcasted_iota(jnp.int32, (c, c), 0)
    ci = lax.broadcasted_iota(jnp.int32, (c, c), 1)
    strict = ci < ri
    incl = ci <= ri
    eye = jnp.where(ri == ci, 1.0, 0.0)

    x_acc = jnp.zeros((c, LANES), F32)
    y_acc = jnp.zeros((c, LANES), F32)
    o_acc = jnp.zeros((c, LANES), F32)
    for h in range(LANES // RWKV_HEAD_DIM):
        mh = (lane // RWKV_HEAD_DIM) == h
        a_h = jnp.where(mh, at, 0.0).astype(BF16)
        r_h = jnp.where(mh, rt, 0.0).astype(BF16)
        v_h = jnp.where(mh, v, 0.0).astype(BF16)
        n_ab = jnp.where(strict, _dot_nt(a_h, bh), 0.0)
        n_ak = jnp.where(strict, _dot_nt(a_h, kh), 0.0)
        q_k = jnp.where(incl, _dot_nt(r_h, kh), 0.0)
        q_b = jnp.where(incl, _dot_nt(r_h, bh), 0.0)
        t = eye + n_ab
        p = n_ab
        steps = (c - 1).bit_length() - 1
        for _ in range(steps):
            pb = p.astype(BF16)
            p = _dot(pb, pb)
            t = t + _dot(t.astype(BF16), p.astype(BF16))
        tb = t.astype(BF16)
        x_acc = x_acc + _dot(tb, a_h)
        y_acc = y_acc + _dot(tb, _dot(n_ak.astype(BF16), v_h).astype(BF16))
        o_acc = o_acc + _dot(q_k.astype(BF16), v_h)
        qb_ref[:, h * c:(h + 1) * c] = q_b.astype(qb_ref.dtype)
    x_ref[...] = x_acc.astype(x_ref.dtype)
    y_ref[...] = y_acc
    op_ref[...] = o_acc

    kbt = kb_ref[...].T.astype(BF16)
    g_ref[...] = jnp.where(_head_pair_ones(), _dot(kbt, v.astype(BF16)), 0.0)
    bbt_ref[...] = bb_ref[...].T.astype(bbt_ref.dtype)
    pcm_ref[...] = pc_ref[...].T


def _rwkv_chunk(rt, at, kh, bh, kb, bb, pc, proj_r):
    s, w = rt.shape
    npair = w // LANES
    c = CHUNK
    blk = lambda off: pl.BlockSpec((c, LANES), lambda i, j, off=off: (i, j + off))
    f32o = jax.ShapeDtypeStruct((s, w), F32)
    return pl.pallas_call(
        _rwkv_chunk_kernel,
        grid=(s // c, npair),
        in_specs=[blk(0)] * 7 + [blk(2 * npair)],
        out_specs=[blk(0), blk(0), blk(0),
                   pl.BlockSpec((c, 2 * c), lambda i, j: (i, j)),
                   pl.BlockSpec((LANES, c), lambda i, j: (j, i)),
                   blk(0), blk(0)],
        out_shape=[jax.ShapeDtypeStruct((s, w), BF16), f32o, f32o,
                   jax.ShapeDtypeStruct((s, 2 * w), BF16),
                   jax.ShapeDtypeStruct((w, s), BF16),
                   f32o, f32o],
        compiler_params=_params(("parallel", "parallel")),
        name="rwkv_chunk",
    )(rt, at, kh, bh, kb, bb, pc, proj_r)


def _rwkv_scan_kernel(x_ref, y_ref, rt_ref, op_ref, qb_ref, bbt_ref, g_ref, pcm_ref,
                      o_ref, h_ref):
    c = CHUNK

    @pl.when(pl.program_id(1) == 0)
    def _():
        h_ref[...] = jnp.zeros_like(h_ref)

    lane = lax.broadcasted_iota(jnp.int32, (c, LANES), 1)
    first = lane < RWKV_HEAD_DIM
    same_head = _head_pair_ones()
    for q in range(x_ref.shape[0] // c):
        rows = slice(q * c, (q + 1) * c)
        h = h_ref[...]
        hb = h.astype(BF16)
        u = _dot(x_ref[rows, :], hb) + y_ref[rows, :]
        u2 = jnp.concatenate([jnp.where(first, u, 0.0), jnp.where(first, 0.0, u)], axis=0).astype(BF16)
        o_ref[rows, :] = (_dot(rt_ref[rows, :].astype(BF16), hb) + op_ref[rows, :]
                          + _dot(qb_ref[rows, :], u2))
        upd = _dot(bbt_ref[:, rows], u.astype(BF16))
        h_ref[...] = pcm_ref[rows, :] * h + g_ref[rows, :] + jnp.where(same_head, upd, 0.0)


def _rwkv_scan(x, y, rt, opre, qb2, bbt, g, pcm, ts=512):
    s, w = y.shape
    npair = w // LANES
    ts = min(ts, s)
    blk = pl.BlockSpec((ts, LANES), lambda j, i: (i, j))
    return pl.pallas_call(
        _rwkv_scan_kernel,
        grid=(npair, s // ts),
        in_specs=[blk, blk, blk, blk,
                  pl.BlockSpec((ts, 2 * CHUNK), lambda j, i: (i, j)),
                  pl.BlockSpec((LANES, ts), lambda j, i: (j, i)),
                  blk, blk],
        out_specs=blk,
        out_shape=jax.ShapeDtypeStruct((s, w), F32),
        scratch_shapes=[pltpu.VMEM((LANES, LANES), F32)],
        compiler_params=_params(("parallel", "arbitrary")),
        name="rwkv_scan",
    )(x, y, rt, opre, qb2, bbt, g, pcm)


def _rwkv_post_kernel(o_ref, bonus_ref, g_ref, lng_ref, lnb_ref, y_ref):
    same_head = jnp.where(_head_pair_ones(), 1.0, 0.0).astype(BF16)
    o = o_ref[...]
    mu = _dot_r3(o, same_head) * (1.0 / RWKV_HEAD_DIM)
    d = o - mu
    var = _dot_r3(d * d, same_head) * (1.0 / RWKV_HEAD_DIM)
    y = d * lax.rsqrt(var + LN_X_EPS) * lng_ref[...] + lnb_ref[...]
    y_ref[...] = ((y + bonus_ref[...]) * g_ref[...]).astype(y_ref.dtype)


def _rwkv_post(o, bonus, g, lnx_g, lnx_b, tm=512):
    s, w = o.shape
    tm = min(tm, s)
    blk = pl.BlockSpec((tm, LANES), lambda i, j: (i, j))
    vec = pl.BlockSpec((1, LANES), lambda i, j: (0, j))
    return pl.pallas_call(
        _rwkv_post_kernel,
        grid=(s // tm, w // LANES),
        in_specs=[blk, blk, blk, vec, vec],
        out_specs=blk,
        out_shape=jax.ShapeDtypeStruct((s, w), BF16),
        compiler_params=_params(("parallel", "parallel")),
        name="rwkv_post",
    )(o, bonus, g, lnx_g.reshape(1, w), lnx_b.reshape(1, w))


def _moba_prep_kernel(q_ref, k_ref, v_ref, cos_ref, sin_ref, qo_ref, ko_ref, vo_ref, km_ref):
    cos = cos_ref[...]
    sin = sin_ref[...]
    half = MOBA_HEAD_DIM // 2
    for h in range(q_ref.shape[1] // MOBA_HEAD_DIM):
        cols = slice(h * MOBA_HEAD_DIM, (h + 1) * MOBA_HEAD_DIM)
        q = q_ref[:, cols]
        k = k_ref[:, cols]
        qr = q * cos + pltpu.roll(q, half, axis=1) * sin
        kr = k * cos + pltpu.roll(k, half, axis=1) * sin
        qo_ref[:, cols] = qr.astype(qo_ref.dtype)
        ko_ref[:, cols] = kr.astype(ko_ref.dtype)
        km_ref[0, :, cols] = jnp.mean(kr, axis=0, keepdims=True)
    vo_ref[...] = v_ref[...].astype(vo_ref.dtype)


def _moba_prep(proj_m, cos, sin):
    s = proj_m.shape[0]
    w = proj_m.shape[1] // 3
    tb = MOBA_BLOCK
    nb = s // tb
    blk = lambda off: pl.BlockSpec((tb, w), lambda i, off=off: (i, off))
    tab = pl.BlockSpec((tb, MOBA_HEAD_DIM), lambda i: (i, 0))
    bo = jax.ShapeDtypeStruct((s, w), BF16)
    return pl.pallas_call(
        _moba_prep_kernel,
        grid=(nb,),
        in_specs=[blk(0), blk(1), blk(2), tab, tab],
        out_specs=[blk(0), blk(0), blk(0), pl.BlockSpec((1, 1, w), lambda i: (i, 0, 0))],
        out_shape=[bo, bo, bo, jax.ShapeDtypeStruct((nb, 1, w), F32)],
        compiler_params=_params(("parallel",)),
        name="moba_prep",
    )(proj_m, proj_m, proj_m, cos, sin)


def _moba_attn_kernel(q_ref, k_ref, v_ref, km_ref, o_ref):
    i = pl.program_id(1)
    tb = MOBA_BLOCK
    nb = km_ref.shape[0]
    scale = MOBA_HEAD_DIM ** -0.5
    q = q_ref[...]

    gate = _dot_nt(q, km_ref[...].astype(BF16))
    col = lax.broadcasted_iota(jnp.int32, gate.shape, 1).astype(F32)
    gate = jnp.where(col < i.astype(F32), gate, -jnp.inf)
    sel = jnp.zeros(gate.shape, F32)
    for _ in range(MOBA_TOPK):
        m = jnp.max(gate, axis=-1, keepdims=True)
        first = jnp.min(jnp.where(gate == m, col, float(nb)), axis=-1, keepdims=True)
        pick = (col == first) & (m > -jnp.inf)
        sel = jnp.where(pick, 1.0, sel)
        gate = jnp.where(pick, -jnp.inf, gate)

    own = pl.multiple_of(i * tb, tb)
    s = _dot_nt(q, k_ref[pl.ds(own, tb), :]) * scale
    ri = lax.broadcasted_iota(jnp.int32, s.shape, 0)
    ci = lax.broadcasted_iota(jnp.int32, s.shape, 1)
    s = jnp.where(ci <= ri, s, -jnp.inf)
    m0 = jnp.max(s, axis=-1, keepdims=True)
    p = jnp.exp(s - m0)
    l0 = jnp.sum(p, axis=-1, keepdims=True)
    acc0 = _dot(p.astype(BF16), v_ref[pl.ds(own, tb), :])

    def body(j, carry):
        m, l, acc = carry
        start = pl.multiple_of(j * tb, tb)
        chosen = jnp.sum(jnp.where(col == j.astype(F32), sel, 0.0), axis=-1, keepdims=True)
        sj = _dot_nt(q, k_ref[pl.ds(start, tb), :]) * scale
        sj = jnp.where(chosen > 0.0, sj, -jnp.inf)
        m_new = jnp.maximum(m, jnp.max(sj, axis=-1, keepdims=True))
        alpha = jnp.exp(m - m_new)
        pj = jnp.exp(sj - m_new)
        l = alpha * l + jnp.sum(pj, axis=-1, keepdims=True)
        acc = alpha * acc + _dot(pj.astype(BF16), v_ref[pl.ds(start, tb), :])
        return m_new, l, acc

    _, l, acc = lax.fori_loop(0, i, body, (m0, l0, acc0))
    o_ref[...] = (acc / l).astype(o_ref.dtype)


def _moba_attn(q, k, v, kmean):
    s, w = q.shape
    tb = MOBA_BLOCK
    nb = s // tb
    hd = MOBA_HEAD_DIM
    return pl.pallas_call(
        _moba_attn_kernel,
        grid=(w // hd, nb),
        in_specs=[pl.BlockSpec((tb, hd), lambda h, i: (i, h)),
                  pl.BlockSpec((s, hd), lambda h, i: (0, h)),
                  pl.BlockSpec((s, hd), lambda h, i: (0, h)),
                  pl.BlockSpec((nb, hd), lambda h, i: (0, h))],
        out_specs=pl.BlockSpec((tb, hd), lambda h, i: (i, h)),
        out_shape=jax.ShapeDtypeStruct((s, w), BF16),
        compiler_params=_params(("parallel", "arbitrary")),
        name="moba_attn",
    )(q, k, v, kmean)


def _top_rows(x, n):
    rows = x.shape[0]
    rid = lax.broadcasted_iota(jnp.int32, x.shape, 0).astype(F32)
    vals, idxs = [], []
    for _ in range(n):
        m = jnp.max(x, axis=0, keepdims=True)
        first = jnp.min(jnp.where(x == m, rid, float(rows)), axis=0, keepdims=True)
        vals.append(m)
        idxs.append(first)
        x = jnp.where(rid == first, -jnp.inf, x)
    return jnp.concatenate(vals, axis=0), jnp.concatenate(idxs, axis=0)


def _peer_route_kernel(q_ref, keys_ref, e1_ref, e2_ref, g_ref):
    n = PEER_TOPK
    tm = q_ref.shape[0]
    e1s, e2s, gs = [], [], []
    for h in range(PEER_HEADS):
        tops = []
        for p in range(2):
            hp = 2 * h + p
            cols = slice(hp * PEER_HALF, (hp + 1) * PEER_HALF)
            rows = slice(hp * PEER_N_KEYS, (hp + 1) * PEER_N_KEYS)
            st = _dot_nt(keys_ref[rows, :], q_ref[:, cols].astype(BF16))
            tops.append(_top_rows(st, n))
        (s1, i1), (s2, i2) = tops
        cand = jnp.concatenate([s1[a:a + 1, :] + s2 for a in range(n)], axis=0)
        f_s, f_pos = _top_rows(cand, n)
        pa = jnp.floor(f_pos * (1.0 / n))
        pb = f_pos - pa * n
        e1 = jnp.zeros((n, tm), F32)
        e2 = jnp.zeros((n, tm), F32)
        for a in range(n):
            e1 = jnp.where(pa == float(a), i1[a:a + 1, :], e1)
            e2 = jnp.where(pb == float(a), i2[a:a + 1, :], e2)
        ex = jnp.exp(f_s - f_s[0:1, :])
        gs.append(ex / jnp.sum(ex, axis=0, keepdims=True))
        e1s.append(e1)
        e2s.append(e2)
    e1_ref[...] = jnp.concatenate(e1s, axis=0).T
    e2_ref[...] = jnp.concatenate(e2s, axis=0).T
    g_ref[...] = jnp.concatenate(gs, axis=0).T


def _peer_route(q, keys, tm=256):
    s, w = q.shape
    nsel = PEER_HEADS * PEER_TOPK
    tm = min(tm, s)
    out = jax.ShapeDtypeStruct((s, nsel), F32)
    ob = pl.BlockSpec((tm, nsel), lambda i: (i, 0))
    return pl.pallas_call(
        _peer_route_kernel,
        grid=(s // tm,),
        in_specs=[pl.BlockSpec((tm, w), lambda i: (i, 0)),
                  pl.BlockSpec(keys.shape, lambda i: (0, 0))],
        out_specs=[ob, ob, ob],
        out_shape=[out, out, out],
        compiler_params=_params(("parallel",)),
        name="peer_route",
    )(q, keys)


def _peer_expand_kernel(e1_ref, e2_ref, g_ref, w_ref):
    nk = PEER_N_KEYS
    nsel = e1_ref.shape[1]
    rid = lax.broadcasted_iota(jnp.int32, (nk, nsel), 0).astype(F32)

    def body(t, carry):
        e1 = e1_ref[pl.ds(t, 1), :]
        e2 = e2_ref[pl.ds(t, 1), :]
        g = g_ref[pl.ds(t, 1), :]
        left = jnp.where(rid == e1, g, 0.0).astype(BF16)
        right = jnp.where(rid == e2, 1.0, 0.0).astype(BF16)
        w_ref[t] = _dot_nt(left, right).astype(w_ref.dtype)
        return carry

    lax.fori_loop(0, e1_ref.shape[0], body, 0)


def _peer_expand(e1, e2, g, tm=128):
    s, nsel = e1.shape
    nk = PEER_N_KEYS
    tm = min(tm, s)
    ib = pl.BlockSpec((tm, nsel), lambda i: (i, 0))
    return pl.pallas_call(
        _peer_expand_kernel,
        grid=(s // tm,),
        in_specs=[ib, ib, ib],
        out_specs=pl.BlockSpec((tm, nk, nk), lambda i: (i, 0, 0)),
        out_shape=jax.ShapeDtypeStruct((s, nk, nk), BF16),
        compiler_params=_params(("parallel",)),
        name="peer_expand",
    )(e1, e2, g)


def _peer_dense_kernel(h_ref, u_ref, v_ref, w_ref, o_ref):
    @pl.when(pl.program_id(1) == 0)
    def _():
        o_ref[...] = jnp.zeros_like(o_ref)

    act = _dot_nt(h_ref[...], u_ref[...])
    gelu = 0.5 * act * (1.0 + lax.erf(act * (2.0 ** -0.5)))
    o_ref[...] += _dot((w_ref[...].astype(F32) * gelu).astype(BF16), v_ref[...])


def _peer_dense(h2, u, v, wdense, tm=512, te=512):
    s, d = h2.shape
    ne = u.shape[0]
    tm = min(tm, s)
    return pl.pallas_call(
        _peer_dense_kernel,
        grid=(s // tm, ne // te),
        in_specs=[pl.BlockSpec((tm, d), lambda i, e: (i, 0)),
                  pl.BlockSpec((te, d), lambda i, e: (e, 0)),
                  pl.BlockSpec((te, d), lambda i, e: (e, 0)),
                  pl.BlockSpec((tm, te), lambda i, e: (i, e))],
        out_specs=pl.BlockSpec((tm, d), lambda i, e: (i, 0)),
        out_shape=jax.ShapeDtypeStruct((s, d), F32),
        compiler_params=_params(("parallel", "arbitrary")),
        name="peer_dense",
    )(h2, u, v, wdense)


def _add_rmsnorm_kernel(x_ref, y_ref, g_ref, o_ref):
    x = x_ref[...] + y_ref[...]
    ms = jnp.mean(x * x, axis=-1, keepdims=True)
    o_ref[...] = x * lax.rsqrt(ms + NORM_EPS) * g_ref[...]


def _add_rmsnorm(x, y, g, tm=256):
    s, d = x.shape
    blk = pl.BlockSpec((tm, d), lambda i: (i, 0))
    return pl.pallas_call(
        _add_rmsnorm_kernel,
        grid=(s // tm,),
        in_specs=[blk, blk, pl.BlockSpec((1, d), lambda i: (0, 0))],
        out_specs=blk,
        out_shape=jax.ShapeDtypeStruct((s, d), F32),
        compiler_params=_params(("parallel",)),
        name="residual_final_norm",
    )(x, y, g.reshape(1, d))


def _rwkv_branch(proj_r, w0, w_up, a0, a_up, g_up, k_k, k_a, r_k, lnx_g, lnx_b):
    g_up_pad = jnp.pad(g_up, ((0, GATE_LORA_PAD - GATE_LORA), (0, 0))).astype(BF16)
    wpre, a, g = _rwkv_lora(proj_r, w0, w_up.astype(BF16), a0, a_up.astype(BF16), g_up_pad)
    rt, at, kh, bh, kb, bb, pc, bonus = _rwkv_prep(proj_r, wpre, a, k_k, k_a, r_k.reshape(-1))
    x, y, opre, qb2, bbt, gst, pcm = _rwkv_chunk(rt, at, kh, bh, kb, bb, pc, proj_r)
    o = _rwkv_scan(x, y, rt, opre, qb2, bbt, gst, pcm)
    return _rwkv_post(o, bonus, g, lnx_g, lnx_b)


def _rope_tables(s):
    half = MOBA_HEAD_DIM // 2
    inv_freq = ROPE_THETA ** (-jnp.arange(half, dtype=F32) / half)
    ang = jnp.arange(s, dtype=jnp.int32).astype(F32)[:, None] * inv_freq[None, :]
    cos = jnp.cos(ang)
    sin = jnp.sin(ang)
    return jnp.concatenate([cos, cos], axis=-1), jnp.concatenate([-sin, sin], axis=-1)


def _moba_branch(proj_m):
    s = proj_m.shape[0]
    cos, sin = _rope_tables(s)
    q, k, v, kmean = _moba_prep(proj_m, cos, sin)
    return _moba_attn(q, k, v, kmean.reshape(kmean.shape[0], kmean.shape[2]))


def _peer_layer(x1, norm2_g, w_q, sub_keys, expert_u, expert_v, final_g):
    s, d = x1.shape
    h2 = _rmsnorm(x1, norm2_g, BF16)
    q = _matmul(h2, w_q.astype(BF16), F32, tm=min(1024, s), tn=512, name="peer_query")
    keys = sub_keys.reshape(PEER_HEADS * 2 * PEER_N_KEYS, PEER_HALF).astype(BF16)
    e1, e2, g = _peer_route(q, keys)
    wdense = _peer_expand(e1, e2, g).reshape(s, PEER_N_KEYS * PEER_N_KEYS)
    peer = _peer_dense(h2, expert_u.astype(BF16), expert_v.astype(BF16), wdense)
    return _add_rmsnorm(x1, peer, final_g)


def kernel(x, norm1_g, w_in, rwkv_mu, rwkv_w0, rwkv_w_up, rwkv_a0, rwkv_a_up, rwkv_g_up, rwkv_k_k, rwkv_k_a, rwkv_r_k, rwkv_lnx_g, rwkv_lnx_b, w_branch_rwkv, w_branch_moba, w_out, norm2_g, peer_w_q, peer_sub_keys, peer_u, peer_v, final_g):
    b, s, d = x.shape
    depth = w_in.shape[0]
    assert b == 1 and depth == 1
    x2d = x.reshape(s, d)
    l = 0
    rw = rwkv_w0.shape[-1]
    shift_w = 3 * rw + DECAY_LORA + AAA_LORA + GATE_LORA
    shift_pad = shift_w + (GATE_LORA_PAD - GATE_LORA)
    mw = w_branch_moba.shape[1]
    tm = min(1024, s)

    h = _rmsnorm(x2d, norm1_g[l], BF16)
    w_r = jnp.pad(w_in[l][:, :shift_w], ((0, 0), (0, shift_pad - shift_w))).astype(BF16)
    mu = jnp.pad(rwkv_mu[l], (0, shift_pad - shift_w)).reshape(1, shift_pad)
    w_m = w_in[l][:, shift_w:shift_w + 3 * mw].astype(BF16)
    w_g = w_in[l][:, shift_w + 3 * mw:].astype(BF16)
    proj_r = _matmul_shift(h, w_r, mu, tm=tm, tn=768)
    proj_m = _matmul(h, w_m, F32, tm=tm, tn=512, name="in_proj_moba")
    gates = _matmul(h, w_g, F32, tm=tm, tn=512, name="in_proj_gates")

    y_a = _rwkv_branch(proj_r, rwkv_w0[l], rwkv_w_up[l], rwkv_a0[l], rwkv_a_up[l], rwkv_g_up[l],
                       rwkv_k_k[l], rwkv_k_a[l], rwkv_r_k[l], rwkv_lnx_g[l], rwkv_lnx_b[l])
    y_b = _moba_branch(proj_m)
    mixed = _gated_pair(y_a, w_branch_rwkv[l].astype(BF16), y_b, w_branch_moba[l].astype(BF16),
                        gates, tm=tm, tn=512)
    x1 = _matmul_residual(mixed, w_out[l].astype(BF16), x2d, tm=tm, tn=512)
    out = _peer_layer(x1, norm2_g[l], peer_w_q[l], peer_sub_keys[l], peer_u[l], peer_v[l], final_g)
    return out.reshape(b, s, d)
```

```python
import functools

import jax
import jax.numpy as jnp
from jax import lax
from jax.experimental import pallas as pl
from jax.experimental.pallas import tpu as pltpu

F32 = jnp.float32
BF16 = jnp.bfloat16

NORM_EPS = 1e-6
LANES = 128
RWKV_HEAD_DIM = 64
DECAY_LORA = 128
AAA_LORA = 128
GATE_LORA = 480
GATE_LORA_PAD = 512
LN_X_EPS = 64e-5
CHUNK = 128
MOBA_HEAD_DIM = 128
MOBA_BLOCK = 256
MOBA_TOPK = 3
ROPE_THETA = 10000.0
PEER_HEADS = 8
PEER_N_KEYS = 128
PEER_HALF = 128
PEER_TOPK = 16

VMEM_LIMIT = 56 * 1024 * 1024

_NT = (((1,), (1,)), ((), ()))


def _params(sem, vmem=VMEM_LIMIT):
    return pltpu.CompilerParams(dimension_semantics=sem, vmem_limit_bytes=vmem)


def _dot(a, b):
    return jnp.dot(a, b, preferred_element_type=F32)


def _dot_nt(a, b):
    return lax.dot_general(a, b, _NT, preferred_element_type=F32)


def _split3(x):
    hi = x.astype(BF16)
    r1 = x - hi.astype(F32)
    mid = r1.astype(BF16)
    lo = (r1 - mid.astype(F32)).astype(BF16)
    return hi, mid, lo


def _dot_l3(l_bf16, x):
    hi, mid, lo = _split3(x)
    return _dot(l_bf16, hi) + _dot(l_bf16, mid) + _dot(l_bf16, lo)


def _dot_r3(x, r_bf16):
    hi, mid, lo = _split3(x)
    return _dot(hi, r_bf16) + _dot(mid, r_bf16) + _dot(lo, r_bf16)


def _rmsnorm_kernel(x_ref, g_ref, o_ref):
    x = x_ref[...]
    ms = jnp.mean(x * x, axis=-1, keepdims=True)
    o_ref[...] = (x * lax.rsqrt(ms + NORM_EPS) * g_ref[...]).astype(o_ref.dtype)


def _rmsnorm(x, g, out_dtype, tm=256):
    s, d = x.shape
    return pl.pallas_call(
        _rmsnorm_kernel,
        grid=(s // tm,),
        in_specs=[pl.BlockSpec((tm, d), lambda i: (i, 0)),
                  pl.BlockSpec((1, d), lambda i: (0, 0))],
        out_specs=pl.BlockSpec((tm, d), lambda i: (i, 0)),
        out_shape=jax.ShapeDtypeStruct((s, d), out_dtype),
        compiler_params=_params(("parallel",)),
        name="rmsnorm",
    )(x, g.reshape(1, d))


def _mm_kernel(a_ref, b_ref, o_ref):
    o_ref[...] = _dot(a_ref[...], b_ref[...]).astype(o_ref.dtype)


def _matmul(a, b, out_dtype, tm, tn, name, col0=0, n=None):
    m, k = a.shape
    n = b.shape[1] if n is None else n
    off = col0 // tn
    return pl.pallas_call(
        _mm_kernel,
        grid=(m // tm, n // tn),
        in_specs=[pl.BlockSpec((tm, k), lambda i, j: (i, 0)),
                  pl.BlockSpec((k, tn), lambda i, j: (0, j + off))],
        out_specs=pl.BlockSpec((tm, tn), lambda i, j: (i, j)),
        out_shape=jax.ShapeDtypeStruct((m, n), out_dtype),
        compiler_params=_params(("parallel", "parallel")),
        name=name,
    )(a, b)


BF16_SUBLANES = 16


def _mm_shift_kernel(a_ref, ap_ref, b_ref, mu_ref, o_ref):
    i = pl.program_id(0)
    b = b_ref[...]
    z = _dot(a_ref[...], b)
    zp = _dot(ap_ref[...], b)
    prev = jnp.where(i > 0, zp[BF16_SUBLANES - 1:BF16_SUBLANES, :], 0.0)
    row = lax.broadcasted_iota(jnp.int32, z.shape, 0)
    z_prev = jnp.where(row == 0, prev, pltpu.roll(z, 1, axis=0))
    o_ref[...] = z + (z_prev - z) * mu_ref[...]


def _matmul_shift(a, b, mu, tm, tn):
    m, k = a.shape
    n = mu.shape[1]
    per = tm // BF16_SUBLANES
    return pl.pallas_call(
        _mm_shift_kernel,
        grid=(m // tm, n // tn),
        in_specs=[pl.BlockSpec((tm, k), lambda i, j: (i, 0)),
                  pl.BlockSpec((BF16_SUBLANES, k), lambda i, j: (jnp.maximum(i * per - 1, 0), 0)),
                  pl.BlockSpec((k, tn), lambda i, j: (0, j)),
                  pl.BlockSpec((1, tn), lambda i, j: (0, j))],
        out_specs=pl.BlockSpec((tm, tn), lambda i, j: (i, j)),
        out_shape=jax.ShapeDtypeStruct((m, n), F32),
        compiler_params=_params(("parallel", "parallel")),
        name="in_proj_rwkv",
    )(a, a, b, mu)


def _gated_pair_kernel(ya_ref, wa_ref, yb_ref, wb_ref, ga_ref, gb_ref, o_ref):
    pa = _dot(ya_ref[...], wa_ref[...])
    pb = _dot(yb_ref[...], wb_ref[...])
    mixed = jax.nn.sigmoid(ga_ref[...]) * pa + jax.nn.sigmoid(gb_ref[...]) * pb
    o_ref[...] = mixed.astype(o_ref.dtype)


def _gated_pair(ya, wa, yb, wb, gates, tm, tn):
    m, k = ya.shape
    n = wa.shape[1]
    nb = n // tn
    return pl.pallas_call(
        _gated_pair_kernel,
        grid=(m // tm, nb),
        in_specs=[pl.BlockSpec((tm, k), lambda i, j: (i, 0)),
                  pl.BlockSpec((k, tn), lambda i, j: (0, j)),
                  pl.BlockSpec((tm, k), lambda i, j: (i, 0)),
                  pl.BlockSpec((k, tn), lambda i, j: (0, j)),
                  pl.BlockSpec((tm, tn), lambda i, j: (i, j)),
                  pl.BlockSpec((tm, tn), lambda i, j: (i, j + nb))],
        out_specs=pl.BlockSpec((tm, tn), lambda i, j: (i, j)),
        out_shape=jax.ShapeDtypeStruct((m, n), BF16),
        compiler_params=_params(("parallel", "parallel")),
        name="branch_merge",
    )(ya, wa, yb, wb, gates, gates)


def _mm_res_kernel(a_ref, b_ref, r_ref, o_ref):
    o_ref[...] = r_ref[...] + _dot(a_ref[...], b_ref[...])


def _matmul_residual(a, b, res, tm, tn):
    m, k = a.shape
    n = b.shape[1]
    return pl.pallas_call(
        _mm_res_kernel,
        grid=(m // tm, n // tn),
        in_specs=[pl.BlockSpec((tm, k), lambda i, j: (i, 0)),
                  pl.BlockSpec((k, tn), lambda i, j: (0, j)),
                  pl.BlockSpec((tm, tn), lambda i, j: (i, j))],
        out_specs=pl.BlockSpec((tm, tn), lambda i, j: (i, j)),
        out_shape=jax.ShapeDtypeStruct((m, n), F32),
        compiler_params=_params(("parallel", "parallel")),
        name="out_proj",
    )(a, b, res)


def _rwkv_lora_kernel(wl_ref, al_ref, gl0_ref, gl1_ref, w0_ref, wup_ref, a0_ref, aup_ref,
                      gup_ref, wpre_ref, a_ref, g_ref):
    half = GATE_LORA_PAD // 2
    wpre_ref[...] = w0_ref[...] + _dot(jnp.tanh(wl_ref[...]).astype(BF16), wup_ref[...])
    a_ref[...] = jax.nn.sigmoid(a0_ref[...] + _dot(al_ref[...].astype(BF16), aup_ref[...]))
    g_ref[...] = (_dot(jax.nn.sigmoid(gl0_ref[...]).astype(BF16), gup_ref[:half, :])
                  + _dot(jax.nn.sigmoid(gl1_ref[...]).astype(BF16), gup_ref[half:, :]))


def _rwkv_lora(proj_r, w0, w_up, a0, a_up, g_up_pad, tm=256):
    s = proj_r.shape[0]
    w = w0.shape[-1]
    half = GATE_LORA_PAD // 2
    c_wl = 3 * w // DECAY_LORA
    c_al = (3 * w + DECAY_LORA) // AAA_LORA
    c_gl = (3 * w + DECAY_LORA + AAA_LORA) // half
    row = lambda i: (i, 0)
    const = lambda i: (0, 0)
    out = jax.ShapeDtypeStruct((s, w), F32)
    return pl.pallas_call(
        _rwkv_lora_kernel,
        grid=(s // tm,),
        in_specs=[pl.BlockSpec((tm, DECAY_LORA), lambda i: (i, c_wl)),
                  pl.BlockSpec((tm, AAA_LORA), lambda i: (i, c_al)),
                  pl.BlockSpec((tm, half), lambda i: (i, c_gl)),
                  pl.BlockSpec((tm, half), lambda i: (i, c_gl + 1)),
                  pl.BlockSpec((1, w), const),
                  pl.BlockSpec((DECAY_LORA, w), const),
                  pl.BlockSpec((1, w), const),
                  pl.BlockSpec((AAA_LORA, w), const),
                  pl.BlockSpec((GATE_LORA_PAD, w), const)],
        out_specs=[pl.BlockSpec((tm, w), row)] * 3,
        out_shape=[out, out, out],
        compiler_params=_params(("parallel",)),
        name="rwkv_lora",
    )(proj_r, proj_r, proj_r, proj_r, w0.reshape(1, w), w_up, a0.reshape(1, w), a_up, g_up_pad)


def _head_pair_ones():
    r = lax.broadcasted_iota(jnp.int32, (LANES, LANES), 0) // RWKV_HEAD_DIM
    c = lax.broadcasted_iota(jnp.int32, (LANES, LANES), 1) // RWKV_HEAD_DIM
    return r == c


def _rwkv_prep_kernel(r_ref, k_ref, v_ref, wpre_ref, a_ref, kk_ref, ka_ref, rk_ref,
                      rt_ref, at_ref, kh_ref, bh_ref, kb_ref, bb_ref, pc_ref, bonus_ref):
    tm = r_ref.shape[0]
    r = r_ref[...]
    k = k_ref[...]
    v = v_ref[...]
    a = a_ref[...]
    same_head = jnp.where(_head_pair_ones(), 1.0, 0.0).astype(BF16)

    x = -wpre_ref[...]
    softplus = jnp.maximum(x, 0.0) + jnp.log1p(jnp.exp(-jnp.abs(x)))
    lw = -jnp.exp(-softplus - 0.5)

    kk = k * kk_ref[...]
    ss = _dot_r3(kk * kk, same_head)
    kkn = kk / jnp.maximum(jnp.sqrt(ss), 1e-12)
    k2 = k * (1.0 + (a - 1.0) * ka_ref[...])
    bonus_ref[...] = _dot_r3(r * k2 * rk_ref[...], same_head) * v

    ri = lax.broadcasted_iota(jnp.int32, (tm, tm), 0)
    ci = lax.broadcasted_iota(jnp.int32, (tm, tm), 1)
    same_chunk = (ri // CHUNK) == (ci // CHUNK)
    tri = jnp.where(same_chunk & (ci <= ri), 1.0, 0.0).astype(BF16)
    allc = jnp.where(same_chunk, 1.0, 0.0).astype(BF16)
    cum = _dot_l3(tri, lw)
    tot = _dot_l3(allc, lw)

    p_in = jnp.exp(cum)
    p_inv = jnp.exp(-cum)
    p_tail = jnp.exp(tot - cum)
    b = kkn * a
    rt_ref[...] = r * p_in
    at_ref[...] = -kkn * jnp.exp(cum - lw)
    kh_ref[...] = k2 * p_inv
    bh_ref[...] = b * p_inv
    kb_ref[...] = k2 * p_tail
    bb_ref[...] = b * p_tail
    pc_ref[...] = jnp.exp(tot)


def _rwkv_prep(proj_r, wpre, a, k_k, k_a, r_k, tm=256):
    s, w = wpre.shape
    npair = w // LANES
    blk = lambda off: pl.BlockSpec((tm, LANES), lambda i, j, off=off: (i, j + off))
    vec = pl.BlockSpec((1, LANES), lambda i, j: (0, j))
    out = jax.ShapeDtypeStruct((s, w), F32)
    return pl.pallas_call(
        _rwkv_prep_kernel,
        grid=(s // tm, npair),
        in_specs=[blk(0), blk(npair), blk(2 * npair), blk(0), blk(0), vec, vec, vec],
        out_specs=[blk(0)] * 8,
        out_shape=[out] * 8,
        compiler_params=_params(("parallel", "parallel")),
        name="rwkv_prep",
    )(proj_r, proj_r, proj_r, wpre, a, k_k.reshape(1, w), k_a.reshape(1, w), r_k.reshape(1, w))


def _rwkv_chunk_kernel(rt_ref, at_ref, kh_ref, bh_ref, kb_ref, bb_ref, pc_ref, v_ref,
                       x_ref, y_ref, op_ref, qb_ref, bbt_ref, g_ref, pcm_ref):
    c = CHUNK
    nheads = LANES // RWKV_HEAD_DIM
    chunks = range(rt_ref.shape[0] // c)
    rows = [slice(q * c, (q + 1) * c) for q in chunks]
    probs = [(q, h) for q in chunks for h in range(nheads)]
    lane = lax.broadcasted_iota(jnp.int32, (c, LANES), 1)
    ri = lax.broadcasted_iota(jnp.int32, (c, c), 0)
    ci = lax.broadcasted_iota(jnp.int32, (c, c), 1)
    strict = ci < ri
    incl = ci <= ri
    eye = jnp.where(ri == ci, 1.0, 0.0)
    head_mask = [(lane // RWKV_HEAD_DIM) == h for h in range(nheads)]

    kh = [kh_ref[r, :].astype(BF16) for r in rows]
    bh = [bh_ref[r, :].astype(BF16) for r in rows]
    a_h = [jnp.where(head_mask[h], at_ref[rows[q], :], 0.0).astype(BF16) for q, h in probs]
    r_h = [jnp.where(head_mask[h], rt_ref[rows[q], :], 0.0).astype(BF16) for q, h in probs]
    v_h = [jnp.where(head_mask[h], v_ref[rows[q], :], 0.0).astype(BF16) for q, h in probs]
    n_ab = [jnp.where(strict, _dot_nt(a, bh[q]), 0.0) for a, (q, _) in zip(a_h, probs)]
    n_ak = [jnp.where(strict, _dot_nt(a, kh[q]), 0.0).astype(BF16) for a, (q, _) in zip(a_h, probs)]
    q_k = [jnp.where(incl, _dot_nt(r, kh[q]), 0.0).astype(BF16) for r, (q, _) in zip(r_h, probs)]
    q_b = [jnp.where(incl, _dot_nt(r, bh[q]), 0.0) for r, (q, _) in zip(r_h, probs)]
    for qb, (q, h) in zip(q_b, probs):
        qb_ref[rows[q], h * c:(h + 1) * c] = qb.astype(qb_ref.dtype)

    t = [eye + n for n in n_ab]
    p = n_ab
    for _ in range((c - 1).bit_length() - 1):
        pb = [x.astype(BF16) for x in p]
        p = [_dot(x, x) for x in pb]
        t = [ti + _dot(ti.astype(BF16), pi.astype(BF16)) for ti, pi in zip(t, p)]
    tb = [ti.astype(BF16) for ti in t]
    xs = [_dot(ti, a) for ti, a in zip(tb, a_h)]
    nv = [_dot(n, v).astype(BF16) for n, v in zip(n_ak, v_h)]
    ys = [_dot(ti, z) for ti, z in zip(tb, nv)]
    os = [_dot(qk, v) for qk, v in zip(q_k, v_h)]
    same_head = _head_pair_ones()
    for q in chunks:
        mine = [n for n, (qq, _) in enumerate(probs) if qq == q]
        x_ref[rows[q], :] = sum(xs[n] for n in mine).astype(x_ref.dtype)
        y_ref[rows[q], :] = sum(ys[n] for n in mine)
        op_ref[rows[q], :] = sum(os[n] for n in mine)
        kbt = kb_ref[rows[q], :].T.astype(BF16)
        g_ref[rows[q], :] = jnp.where(same_head, _dot(kbt, v_ref[rows[q], :].astype(BF16)), 0.0)
        bbt_ref[:, rows[q]] = bb_ref[rows[q], :].T.astype(bbt_ref.dtype)
        pcm_ref[rows[q], :] = pc_ref[rows[q], :].T


def _rwkv_chunk(rt, at, kh, bh, kb, bb, pc, proj_r, tc=2 * CHUNK):
    s, w = rt.shape
    npair = w // LANES
    c = CHUNK
    blk = lambda off: pl.BlockSpec((tc, LANES), lambda i, j, off=off: (i, j + off))
    f32o = jax.ShapeDtypeStruct((s, w), F32)
    return pl.pallas_call(
        _rwkv_chunk_kernel,
        grid=(s // tc, npair),
        in_specs=[blk(0)] * 7 + [blk(2 * npair)],
        out_specs=[blk(0), blk(0), blk(0),
                   pl.BlockSpec((tc, 2 * c), lambda i, j: (i, j)),
                   pl.BlockSpec((LANES, tc), lambda i, j: (j, i)),
                   blk(0), blk(0)],
        out_shape=[jax.ShapeDtypeStruct((s, w), BF16), f32o, f32o,
                   jax.ShapeDtypeStruct((s, 2 * w), BF16),
                   jax.ShapeDtypeStruct((w, s), BF16),
                   f32o, f32o],
        compiler_params=_params(("parallel", "parallel")),
        name="rwkv_chunk",
    )(rt, at, kh, bh, kb, bb, pc, proj_r)


def _rwkv_scan_kernel(x_ref, y_ref, rt_ref, op_ref, qb_ref, bbt_ref, g_ref, pcm_ref,
                      o_ref, h_ref):
    c = CHUNK

    @pl.when(pl.program_id(1) == 0)
    def _():
        h_ref[...] = jnp.zeros_like(h_ref)

    lane = lax.broadcasted_iota(jnp.int32, (c, LANES), 1)
    first = lane < RWKV_HEAD_DIM
    same_head = _head_pair_ones()
    for q in range(x_ref.shape[0] // c):
        rows = slice(q * c, (q + 1) * c)
        h = h_ref[...]
        hb = h.astype(BF16)
        u = _dot(x_ref[rows, :], hb) + y_ref[rows, :]
        u2 = jnp.concatenate([jnp.where(first, u, 0.0), jnp.where(first, 0.0, u)], axis=0).astype(BF16)
        o_ref[rows, :] = (_dot(rt_ref[rows, :].astype(BF16), hb) + op_ref[rows, :]
                          + _dot(qb_ref[rows, :], u2))
        upd = _dot(bbt_ref[:, rows], u.astype(BF16))
        h_ref[...] = pcm_ref[rows, :] * h + g_ref[rows, :] + jnp.where(same_head, upd, 0.0)


def _rwkv_scan(x, y, rt, opre, qb2, bbt, g, pcm, ts=512):
    s, w = y.shape
    npair = w // LANES
    ts = min(ts, s)
    blk = pl.BlockSpec((ts, LANES), lambda j, i: (i, j))
    return pl.pallas_call(
        _rwkv_scan_kernel,
        grid=(npair, s // ts),
        in_specs=[blk, blk, blk, blk,
                  pl.BlockSpec((ts, 2 * CHUNK), lambda j, i: (i, j)),
                  pl.BlockSpec((LANES, ts), lambda j, i: (j, i)),
                  blk, blk],
        out_specs=blk,
        out_shape=jax.ShapeDtypeStruct((s, w), F32),
        scratch_shapes=[pltpu.VMEM((LANES, LANES), F32)],
        compiler_params=_params(("parallel", "arbitrary")),
        name="rwkv_scan",
    )(x, y, rt, opre, qb2, bbt, g, pcm)


def _rwkv_post_kernel(o_ref, bonus_ref, g_ref, lng_ref, lnb_ref, y_ref):
    same_head = jnp.where(_head_pair_ones(), 1.0, 0.0).astype(BF16)
    o = o_ref[...]
    mu = _dot_r3(o, same_head) * (1.0 / RWKV_HEAD_DIM)
    d = o - mu
    var = _dot_r3(d * d, same_head) * (1.0 / RWKV_HEAD_DIM)
    y = d * lax.rsqrt(var + LN_X_EPS) * lng_ref[...] + lnb_ref[...]
    y_ref[...] = ((y + bonus_ref[...]) * g_ref[...]).astype(y_ref.dtype)


def _rwkv_post(o, bonus, g, lnx_g, lnx_b, tm=512):
    s, w = o.shape
    tm = min(tm, s)
    blk = pl.BlockSpec((tm, LANES), lambda i, j: (i, j))
    vec = pl.BlockSpec((1, LANES), lambda i, j: (0, j))
    return pl.pallas_call(
        _rwkv_post_kernel,
        grid=(s // tm, w // LANES),
        in_specs=[blk, blk, blk, vec, vec],
        out_specs=blk,
        out_shape=jax.ShapeDtypeStruct((s, w), BF16),
        compiler_params=_params(("parallel", "parallel")),
        name="rwkv_post",
    )(o, bonus, g, lnx_g.reshape(1, w), lnx_b.reshape(1, w))


V_ONES_ROWS = 16


def _moba_prep_kernel(q_ref, k_ref, v_ref, cos_ref, sin_ref, qo_ref, ko_ref, vt_ref, km_ref):
    cos = cos_ref[...]
    sin = sin_ref[...]
    hd = MOBA_HEAD_DIM
    half = hd // 2
    tb = q_ref.shape[0]
    for h in range(q_ref.shape[1] // hd):
        cols = slice(h * hd, (h + 1) * hd)
        q = q_ref[:, cols]
        k = k_ref[:, cols]
        qr = q * cos + pltpu.roll(q, half, axis=1) * sin
        kr = k * cos + pltpu.roll(k, half, axis=1) * sin
        qo_ref[:, cols] = qr.astype(qo_ref.dtype)
        ko_ref[:, cols] = kr.astype(ko_ref.dtype)
        km_ref[0, :, cols] = jnp.mean(kr, axis=0, keepdims=True)
        vt_ref[h, 0, :hd, :] = v_ref[:, cols].T.astype(vt_ref.dtype)
        vt_ref[h, 0, hd:, :] = jnp.ones((V_ONES_ROWS, tb), vt_ref.dtype)


def _moba_prep(proj_m, cos, sin):
    s = proj_m.shape[0]
    w = proj_m.shape[1] // 3
    tb = MOBA_BLOCK
    nb = s // tb
    nh = w // MOBA_HEAD_DIM
    vrows = MOBA_HEAD_DIM + V_ONES_ROWS
    blk = lambda off: pl.BlockSpec((tb, w), lambda i, off=off: (i, off))
    tab = pl.BlockSpec((tb, MOBA_HEAD_DIM), lambda i: (i, 0))
    bo = jax.ShapeDtypeStruct((s, w), BF16)
    return pl.pallas_call(
        _moba_prep_kernel,
        grid=(nb,),
        in_specs=[blk(0), blk(1), blk(2), tab, tab],
        out_specs=[blk(0), blk(0),
                   pl.BlockSpec((nh, 1, vrows, tb), lambda i: (0, i, 0, 0)),
                   pl.BlockSpec((1, 1, w), lambda i: (i, 0, 0))],
        out_shape=[bo, bo, jax.ShapeDtypeStruct((nh, nb, vrows, tb), BF16),
                   jax.ShapeDtypeStruct((nb, 1, w), F32)],
        compiler_params=_params(("parallel",)),
        name="moba_prep",
    )(proj_m, proj_m, proj_m, cos, sin)


LOG2_E = 1.4426950408889634
MOBA_CHAINS = 4


def _moba_attn_kernel(q_ref, k_ref, vt_ref, km_ref, o_ref, bias_ref):
    i = pl.program_id(1)
    tb = MOBA_BLOCK
    nb = km_ref.shape[0]
    hd = MOBA_HEAD_DIM
    c = (hd ** -0.5) * LOG2_E
    q = q_ref[...]

    gate = _dot_nt(km_ref[...].astype(BF16), q)
    rid = lax.broadcasted_iota(jnp.int32, gate.shape, 0).astype(F32)
    gate = jnp.where(rid < i.astype(F32), gate, -jnp.inf)
    bias = jnp.full(gate.shape, -jnp.inf, F32)
    for _ in range(MOBA_TOPK):
        m = jnp.max(gate, axis=0, keepdims=True)
        first = jnp.min(jnp.where(gate == m, rid, float(nb)), axis=0, keepdims=True)
        pick = (rid == first) & (m > -jnp.inf)
        bias = jnp.where(pick, 0.0, bias)
        gate = jnp.where(pick, -jnp.inf, gate)
    bias_ref[...] = bias

    own = pl.multiple_of(i * tb, tb)
    s = _dot_nt(k_ref[pl.ds(own, tb), :], q)
    ki = lax.broadcasted_iota(jnp.int32, s.shape, 0)
    qi = lax.broadcasted_iota(jnp.int32, s.shape, 1)
    s = jnp.where(ki <= qi, s, -jnp.inf)
    m0 = jnp.max(s, axis=0, keepdims=True)
    p = jnp.exp2((s - m0) * c)
    acc0 = _dot(vt_ref[i], p.astype(BF16))

    def body(t, carry):
        js = [t * MOBA_CHAINS + g for g in range(MOBA_CHAINS)]
        ss = [_dot_nt(k_ref[pl.ds(pl.multiple_of(j * tb, tb), tb), :], q) + bias_ref[pl.ds(j, 1), :]
              for j in js]
        ms = [jnp.maximum(m, jnp.max(sj, axis=0, keepdims=True)) for (m, _), sj in zip(carry, ss)]
        ps = [jnp.exp2((sj - mn) * c).astype(BF16) for sj, mn in zip(ss, ms)]
        return tuple((mn, jnp.exp2((m - mn) * c) * acc + _dot(vt_ref[j], pj))
                     for (m, acc), mn, pj, j in zip(carry, ms, ps, js))

    init = ((m0, acc0),) + ((m0, jnp.zeros_like(acc0)),) * (MOBA_CHAINS - 1)
    chains = lax.fori_loop(0, (i + MOBA_CHAINS - 1) // MOBA_CHAINS, body, init)
    m = chains[0][0]
    for mg, _ in chains[1:]:
        m = jnp.maximum(m, mg)
    acc = sum(jnp.exp2((mg - m) * c) * ag for mg, ag in chains)
    out = acc[:hd, :] / acc[hd:hd + 1, :]
    o_ref[...] = out.T.astype(o_ref.dtype)


def _moba_attn(q, k, vt, kmean):
    s, w = q.shape
    tb = MOBA_BLOCK
    nb = s // tb
    hd = MOBA_HEAD_DIM
    vrows = vt.shape[2]
    return pl.pallas_call(
        _moba_attn_kernel,
        grid=(w // hd, nb),
        in_specs=[pl.BlockSpec((tb, hd), lambda h, i: (i, h)),
                  pl.BlockSpec((s, hd), lambda h, i: (0, h)),
                  pl.BlockSpec((None, nb, vrows, tb), lambda h, i: (h, 0, 0, 0)),
                  pl.BlockSpec((nb, hd), lambda h, i: (0, h))],
        out_specs=pl.BlockSpec((tb, hd), lambda h, i: (i, h)),
        out_shape=jax.ShapeDtypeStruct((s, w), BF16),
        scratch_shapes=[pltpu.VMEM((nb, tb), F32)],
        compiler_params=_params(("parallel", "arbitrary")),
        name="moba_attn",
    )(q, k, vt, kmean)


def _top_rows(x, n):
    rows = x.shape[0]
    rid = lax.broadcasted_iota(jnp.int32, x.shape, 0).astype(F32)
    vals, idxs = [], []
    for _ in range(n):
        m = jnp.max(x, axis=0, keepdims=True)
        first = jnp.min(jnp.where(x == m, rid, float(rows)), axis=0, keepdims=True)
        vals.append(m)
        idxs.append(first)
        x = jnp.where(rid == first, -jnp.inf, x)
    return jnp.concatenate(vals, axis=0), jnp.concatenate(idxs, axis=0)


def _peer_route_kernel(q_ref, keys_ref, e1_ref, e2_ref, g_ref):
    n = PEER_TOPK
    tm = q_ref.shape[0]
    e1s, e2s, gs = [], [], []
    for h in range(PEER_HEADS):
        tops = []
        for p in range(2):
            hp = 2 * h + p
            cols = slice(hp * PEER_HALF, (hp + 1) * PEER_HALF)
            rows = slice(hp * PEER_N_KEYS, (hp + 1) * PEER_N_KEYS)
            st = _dot_nt(keys_ref[rows, :], q_ref[:, cols].astype(BF16))
            tops.append(_top_rows(st, n))
        (s1, i1), (s2, i2) = tops
        cand = jnp.concatenate([s1[a:a + 1, :] + s2 for a in range(n)], axis=0)
        f_s, f_pos = _top_rows(cand, n)
        pa = jnp.floor(f_pos * (1.0 / n))
        pb = f_pos - pa * n
        e1 = jnp.zeros((n, tm), F32)
        e2 = jnp.zeros((n, tm), F32)
        for a in range(n):
            e1 = jnp.where(pa == float(a), i1[a:a + 1, :], e1)
            e2 = jnp.where(pb == float(a), i2[a:a + 1, :], e2)
        ex = jnp.exp(f_s - f_s[0:1, :])
        gs.append(ex / jnp.sum(ex, axis=0, keepdims=True))
        e1s.append(e1)
        e2s.append(e2)
    e1_ref[...] = jnp.concatenate(e1s, axis=0).T
    e2_ref[...] = jnp.concatenate(e2s, axis=0).T
    g_ref[...] = jnp.concatenate(gs, axis=0).T


def _peer_route(q, keys, tm=256):
    s, w = q.shape
    nsel = PEER_HEADS * PEER_TOPK
    tm = min(tm, s)
    out = jax.ShapeDtypeStruct((s, nsel), F32)
    ob = pl.BlockSpec((tm, nsel), lambda i: (i, 0))
    return pl.pallas_call(
        _peer_route_kernel,
        grid=(s // tm,),
        in_specs=[pl.BlockSpec((tm, w), lambda i: (i, 0)),
                  pl.BlockSpec(keys.shape, lambda i: (0, 0))],
        out_specs=[ob, ob, ob],
        out_shape=[out, out, out],
        compiler_params=_params(("parallel",)),
        name="peer_route",
    )(q, keys)


EXPAND_UNROLL = 8


def _peer_expand_kernel(e1_ref, e2_ref, g_ref, w_ref):
    nk = PEER_N_KEYS
    nsel = e1_ref.shape[1]
    rid = lax.broadcasted_iota(jnp.int32, (nk, nsel), 0).astype(F32)

    un = EXPAND_UNROLL

    def body(tt, carry):
        base = pl.multiple_of(tt * un, un)
        e1 = e1_ref[pl.ds(base, un), :]
        e2 = e2_ref[pl.ds(base, un), :]
        g = g_ref[pl.ds(base, un), :]
        lefts = [jnp.where(rid == e1[u:u + 1, :], g[u:u + 1, :], 0.0).astype(BF16) for u in range(un)]
        rights = [jnp.where(rid == e2[u:u + 1, :], 1.0, 0.0).astype(BF16) for u in range(un)]
        ws = [_dot_nt(l, r) for l, r in zip(lefts, rights)]
        for u in range(un):
            w_ref[base + u] = ws[u].astype(w_ref.dtype)
        return carry

    lax.fori_loop(0, e1_ref.shape[0] // un, body, 0)


def _peer_expand(e1, e2, g, tm=128):
    s, nsel = e1.shape
    nk = PEER_N_KEYS
    tm = min(tm, s)
    ib = pl.BlockSpec((tm, nsel), lambda i: (i, 0))
    return pl.pallas_call(
        _peer_expand_kernel,
        grid=(s // tm,),
        in_specs=[ib, ib, ib],
        out_specs=pl.BlockSpec((tm, nk, nk), lambda i: (i, 0, 0)),
        out_shape=jax.ShapeDtypeStruct((s, nk, nk), BF16),
        compiler_params=_params(("parallel",)),
        name="peer_expand",
    )(e1, e2, g)


def _peer_dense_kernel(h_ref, u_ref, v_ref, w_ref, o_ref):
    @pl.when(pl.program_id(1) == 0)
    def _():
        o_ref[...] = jnp.zeros_like(o_ref)

    act = _dot_nt(h_ref[...], u_ref[...])
    gelu = 0.5 * act * (1.0 + lax.erf(act * (2.0 ** -0.5)))
    o_ref[...] += _dot((w_ref[...].astype(F32) * gelu).astype(BF16), v_ref[...])


def _peer_dense(h2, u, v, wdense, tm=512, te=512):
    s, d = h2.shape
    ne = u.shape[0]
    tm = min(tm, s)
    return pl.pallas_call(
        _peer_dense_kernel,
        grid=(s // tm, ne // te),
        in_specs=[pl.BlockSpec((tm, d), lambda i, e: (i, 0)),
                  pl.BlockSpec((te, d), lambda i, e: (e, 0)),
                  pl.BlockSpec((te, d), lambda i, e: (e, 0)),
                  pl.BlockSpec((tm, te), lambda i, e: (i, e))],
        out_specs=pl.BlockSpec((tm, d), lambda i, e: (i, 0)),
        out_shape=jax.ShapeDtypeStruct((s, d), F32),
        compiler_params=_params(("parallel", "arbitrary")),
        name="peer_dense",
    )(h2, u, v, wdense)


def _add_rmsnorm_kernel(x_ref, y_ref, g_ref, o_ref):
    x = x_ref[...] + y_ref[...]
    ms = jnp.mean(x * x, axis=-1, keepdims=True)
    o_ref[...] = x * lax.rsqrt(ms + NORM_EPS) * g_ref[...]


def _add_rmsnorm(x, y, g, tm=256):
    s, d = x.shape
    blk = pl.BlockSpec((tm, d), lambda i: (i, 0))
    return pl.pallas_call(
        _add_rmsnorm_kernel,
        grid=(s // tm,),
        in_specs=[blk, blk, pl.BlockSpec((1, d), lambda i: (0, 0))],
        out_specs=blk,
        out_shape=jax.ShapeDtypeStruct((s, d), F32),
        compiler_params=_params(("parallel",)),
        name="residual_final_norm",
    )(x, y, g.reshape(1, d))


def _rwkv_branch(proj_r, w0, w_up, a0, a_up, g_up, k_k, k_a, r_k, lnx_g, lnx_b):
    g_up_pad = jnp.pad(g_up, ((0, GATE_LORA_PAD - GATE_LORA), (0, 0))).astype(BF16)
    wpre, a, g = _rwkv_lora(proj_r, w0, w_up.astype(BF16), a0, a_up.astype(BF16), g_up_pad)
    rt, at, kh, bh, kb, bb, pc, bonus = _rwkv_prep(proj_r, wpre, a, k_k, k_a, r_k.reshape(-1))
    x, y, opre, qb2, bbt, gst, pcm = _rwkv_chunk(rt, at, kh, bh, kb, bb, pc, proj_r)
    o = _rwkv_scan(x, y, rt, opre, qb2, bbt, gst, pcm)
    return _rwkv_post(o, bonus, g, lnx_g, lnx_b)


def _rope_tables(s):
    half = MOBA_HEAD_DIM // 2
    inv_freq = ROPE_THETA ** (-jnp.arange(half, dtype=F32) / half)
    ang = jnp.arange(s, dtype=jnp.int32).astype(F32)[:, None] * inv_freq[None, :]
    cos = jnp.cos(ang)
    sin = jnp.sin(ang)
    return jnp.concatenate([cos, cos], axis=-1), jnp.concatenate([-sin, sin], axis=-1)


def _moba_branch(proj_m):
    s = proj_m.shape[0]
    cos, sin = _rope_tables(s)
    q, k, vt, kmean = _moba_prep(proj_m, cos, sin)
    return _moba_attn(q, k, vt, kmean.reshape(kmean.shape[0], kmean.shape[2]))


def _peer_layer(x1, norm2_g, w_q, sub_keys, expert_u, expert_v, final_g):
    s, d = x1.shape
    h2 = _rmsnorm(x1, norm2_g, BF16)
    q = _matmul(h2, w_q.astype(BF16), F32, tm=min(1024, s), tn=512, name="peer_query")
    keys = sub_keys.reshape(PEER_HEADS * 2 * PEER_N_KEYS, PEER_HALF).astype(BF16)
    e1, e2, g = _peer_route(q, keys)
    wdense = _peer_expand(e1, e2, g).reshape(s, PEER_N_KEYS * PEER_N_KEYS)
    peer = _peer_dense(h2, expert_u.astype(BF16), expert_v.astype(BF16), wdense)
    return _add_rmsnorm(x1, peer, final_g)


def kernel(x, norm1_g, w_in, rwkv_mu, rwkv_w0, rwkv_w_up, rwkv_a0, rwkv_a_up, rwkv_g_up, rwkv_k_k, rwkv_k_a, rwkv_r_k, rwkv_lnx_g, rwkv_lnx_b, w_branch_rwkv, w_branch_moba, w_out, norm2_g, peer_w_q, peer_sub_keys, peer_u, peer_v, final_g):
    b, s, d = x.shape
    depth = w_in.shape[0]
    assert b == 1 and depth == 1
    x2d = x.reshape(s, d)
    l = 0
    rw = rwkv_w0.shape[-1]
    shift_w = 3 * rw + DECAY_LORA + AAA_LORA + GATE_LORA
    tn = 512
    shift_pad = -(-shift_w // tn) * tn
    mw = w_branch_moba.shape[1]
    tm = min(1024, s)

    h = _rmsnorm(x2d, norm1_g[l], BF16)
    w_all = jnp.concatenate([w_in[l][:, :shift_w].astype(BF16),
                             jnp.zeros((d, shift_pad - shift_w), BF16),
                             w_in[l][:, shift_w:].astype(BF16)], axis=1)
    mu = jnp.pad(rwkv_mu[l], (0, shift_pad - shift_w)).reshape(1, shift_pad)
    proj_r = _matmul_shift(h, w_all, mu, tm=tm, tn=tn)
    proj_m = _matmul(h, w_all, F32, tm=tm, tn=tn, name="in_proj_moba", col0=shift_pad, n=3 * mw)
    gates = _matmul(h, w_all, F32, tm=tm, tn=tn, name="in_proj_gates", col0=shift_pad + 3 * mw, n=2 * d)

    y_a = _rwkv_branch(proj_r, rwkv_w0[l], rwkv_w_up[l], rwkv_a0[l], rwkv_a_up[l], rwkv_g_up[l],
                       rwkv_k_k[l], rwkv_k_a[l], rwkv_r_k[l], rwkv_lnx_g[l], rwkv_lnx_b[l])
    y_b = _moba_branch(proj_m)
    mixed = _gated_pair(y_a, w_branch_rwkv[l].astype(BF16), y_b, w_branch_moba[l].astype(BF16),
                        gates, tm=tm, tn=512)
    x1 = _matmul_residual(mixed, w_out[l].astype(BF16), x2d, tm=tm, tn=512)
    out = _peer_layer(x1, norm2_g[l], peer_w_q[l], peer_sub_keys[l], peer_u[l], peer_v[l], final_g)
    return out.reshape(b, s, d)
```

```python
import functools

import jax
import jax.numpy as jnp
from jax import lax
from jax.experimental import pallas as pl
from jax.experimental.pallas import tpu as pltpu

F32 = jnp.float32
BF16 = jnp.bfloat16

NORM_EPS = 1e-6
LANES = 128
RWKV_HEAD_DIM = 64
DECAY_LORA = 128
AAA_LORA = 128
GATE_LORA = 480
GATE_LORA_PAD = 512
LN_X_EPS = 64e-5
CHUNK = 128
MOBA_HEAD_DIM = 128
MOBA_BLOCK = 256
MOBA_TOPK = 3
ROPE_THETA = 10000.0
PEER_HEADS = 8
PEER_N_KEYS = 128
PEER_HALF = 128
PEER_TOPK = 16

VMEM_LIMIT = 56 * 1024 * 1024

_NT = (((1,), (1,)), ((), ()))


def _params(sem, vmem=VMEM_LIMIT):
    return pltpu.CompilerParams(dimension_semantics=sem, vmem_limit_bytes=vmem)


def _dot(a, b):
    return jnp.dot(a, b, preferred_element_type=F32)


def _dot_nt(a, b):
    return lax.dot_general(a, b, _NT, preferred_element_type=F32)


def _split3(x):
    hi = x.astype(BF16)
    r1 = x - hi.astype(F32)
    mid = r1.astype(BF16)
    lo = (r1 - mid.astype(F32)).astype(BF16)
    return hi, mid, lo


def _dot_l3(l_bf16, x):
    hi, mid, lo = _split3(x)
    return _dot(l_bf16, hi) + _dot(l_bf16, mid) + _dot(l_bf16, lo)


def _dot_r3(x, r_bf16):
    hi, mid, lo = _split3(x)
    return _dot(hi, r_bf16) + _dot(mid, r_bf16) + _dot(lo, r_bf16)


def _rmsnorm_kernel(x_ref, g_ref, o_ref):
    x = x_ref[...]
    ms = jnp.mean(x * x, axis=-1, keepdims=True)
    o_ref[...] = (x * lax.rsqrt(ms + NORM_EPS) * g_ref[...]).astype(o_ref.dtype)


def _rmsnorm(x, g, out_dtype, tm=256):
    s, d = x.shape
    return pl.pallas_call(
        _rmsnorm_kernel,
        grid=(s // tm,),
        in_specs=[pl.BlockSpec((tm, d), lambda i: (i, 0)),
                  pl.BlockSpec((1, d), lambda i: (0, 0))],
        out_specs=pl.BlockSpec((tm, d), lambda i: (i, 0)),
        out_shape=jax.ShapeDtypeStruct((s, d), out_dtype),
        compiler_params=_params(("parallel",)),
        name="rmsnorm",
    )(x, g.reshape(1, d))


def _mm_kernel(a_ref, b_ref, o_ref):
    o_ref[...] = _dot(a_ref[...], b_ref[...]).astype(o_ref.dtype)


def _matmul(a, b, out_dtype, tm, tn, name, col0=0, n=None):
    m, k = a.shape
    n = b.shape[1] if n is None else n
    off = col0 // tn
    return pl.pallas_call(
        _mm_kernel,
        grid=(m // tm, n // tn),
        in_specs=[pl.BlockSpec((tm, k), lambda i, j: (i, 0)),
                  pl.BlockSpec((k, tn), lambda i, j: (0, j + off))],
        out_specs=pl.BlockSpec((tm, tn), lambda i, j: (i, j)),
        out_shape=jax.ShapeDtypeStruct((m, n), out_dtype),
        compiler_params=_params(("parallel", "parallel")),
        name=name,
    )(a, b)


BF16_SUBLANES = 16


def _mm_shift_kernel(a_ref, ap_ref, b_ref, mu_ref, o_ref):
    i = pl.program_id(0)
    b = b_ref[...].astype(BF16)
    z = _dot(a_ref[...], b)
    zp = _dot(ap_ref[...], b)
    prev = jnp.where(i > 0, zp[BF16_SUBLANES - 1:BF16_SUBLANES, :], 0.0)
    row = lax.broadcasted_iota(jnp.int32, z.shape, 0)
    z_prev = jnp.where(row == 0, prev, pltpu.roll(z, 1, axis=0))
    o_ref[...] = z + (z_prev - z) * mu_ref[...]


def _matmul_shift(a, b, mu, tm, tn):
    m, k = a.shape
    n = mu.shape[1]
    per = tm // BF16_SUBLANES
    return pl.pallas_call(
        _mm_shift_kernel,
        grid=(m // tm, n // tn),
        in_specs=[pl.BlockSpec((tm, k), lambda i, j: (i, 0)),
                  pl.BlockSpec((BF16_SUBLANES, k), lambda i, j: (jnp.maximum(i * per - 1, 0), 0)),
                  pl.BlockSpec((k, tn), lambda i, j: (0, j)),
                  pl.BlockSpec((1, tn), lambda i, j: (0, j))],
        out_specs=pl.BlockSpec((tm, tn), lambda i, j: (i, j)),
        out_shape=jax.ShapeDtypeStruct((m, n), F32),
        compiler_params=_params(("parallel", "parallel")),
        name="in_proj_rwkv",
    )(a, a, b, mu)


def _gated_pair_kernel(ya_ref, wa_ref, yb_ref, wb_ref, ga_ref, gb_ref, o_ref):
    pa = _dot(ya_ref[...], wa_ref[...])
    pb = _dot(yb_ref[...], wb_ref[...])
    mixed = jax.nn.sigmoid(ga_ref[...]) * pa + jax.nn.sigmoid(gb_ref[...]) * pb
    o_ref[...] = mixed.astype(o_ref.dtype)


def _gated_pair(ya, wa, yb, wb, gates, tm, tn):
    m, k = ya.shape
    n = wa.shape[1]
    nb = n // tn
    return pl.pallas_call(
        _gated_pair_kernel,
        grid=(m // tm, nb),
        in_specs=[pl.BlockSpec((tm, k), lambda i, j: (i, 0)),
                  pl.BlockSpec((k, tn), lambda i, j: (0, j)),
                  pl.BlockSpec((tm, k), lambda i, j: (i, 0)),
                  pl.BlockSpec((k, tn), lambda i, j: (0, j)),
                  pl.BlockSpec((tm, tn), lambda i, j: (i, j)),
                  pl.BlockSpec((tm, tn), lambda i, j: (i, j + nb))],
        out_specs=pl.BlockSpec((tm, tn), lambda i, j: (i, j)),
        out_shape=jax.ShapeDtypeStruct((m, n), BF16),
        compiler_params=_params(("parallel", "parallel")),
        name="branch_merge",
    )(ya, wa, yb, wb, gates, gates)


def _mm_res_kernel(a_ref, b_ref, r_ref, o_ref):
    o_ref[...] = r_ref[...] + _dot(a_ref[...], b_ref[...])


def _matmul_residual(a, b, res, tm, tn):
    m, k = a.shape
    n = b.shape[1]
    return pl.pallas_call(
        _mm_res_kernel,
        grid=(m // tm, n // tn),
        in_specs=[pl.BlockSpec((tm, k), lambda i, j: (i, 0)),
                  pl.BlockSpec((k, tn), lambda i, j: (0, j)),
                  pl.BlockSpec((tm, tn), lambda i, j: (i, j))],
        out_specs=pl.BlockSpec((tm, tn), lambda i, j: (i, j)),
        out_shape=jax.ShapeDtypeStruct((m, n), F32),
        compiler_params=_params(("parallel", "parallel")),
        name="out_proj",
    )(a, b, res)


def _rwkv_lora_kernel(wl_ref, al_ref, gl0_ref, gl1_ref, w0_ref, wup_ref, a0_ref, aup_ref,
                      gup_ref, wpre_ref, a_ref, g_ref):
    half = GATE_LORA_PAD // 2
    wpre_ref[...] = w0_ref[...] + _dot(jnp.tanh(wl_ref[...]).astype(BF16), wup_ref[...])
    a_ref[...] = jax.nn.sigmoid(a0_ref[...] + _dot(al_ref[...].astype(BF16), aup_ref[...]))
    g_ref[...] = (_dot(jax.nn.sigmoid(gl0_ref[...]).astype(BF16), gup_ref[:half, :])
                  + _dot(jax.nn.sigmoid(gl1_ref[...]).astype(BF16), gup_ref[half:, :]))


def _rwkv_lora(proj_r, w0, w_up, a0, a_up, g_up_pad, tm=256):
    s = proj_r.shape[0]
    w = w0.shape[-1]
    half = GATE_LORA_PAD // 2
    c_wl = 3 * w // DECAY_LORA
    c_al = (3 * w + DECAY_LORA) // AAA_LORA
    c_gl = (3 * w + DECAY_LORA + AAA_LORA) // half
    row = lambda i: (i, 0)
    const = lambda i: (0, 0)
    out = jax.ShapeDtypeStruct((s, w), F32)
    return pl.pallas_call(
        _rwkv_lora_kernel,
        grid=(s // tm,),
        in_specs=[pl.BlockSpec((tm, DECAY_LORA), lambda i: (i, c_wl)),
                  pl.BlockSpec((tm, AAA_LORA), lambda i: (i, c_al)),
                  pl.BlockSpec((tm, half), lambda i: (i, c_gl)),
                  pl.BlockSpec((tm, half), lambda i: (i, c_gl + 1)),
                  pl.BlockSpec((1, w), const),
                  pl.BlockSpec((DECAY_LORA, w), const),
                  pl.BlockSpec((1, w), const),
                  pl.BlockSpec((AAA_LORA, w), const),
                  pl.BlockSpec((GATE_LORA_PAD, w), const)],
        out_specs=[pl.BlockSpec((tm, w), row)] * 3,
        out_shape=[out, out, out],
        compiler_params=_params(("parallel",)),
        name="rwkv_lora",
    )(proj_r, proj_r, proj_r, proj_r, w0.reshape(1, w), w_up, a0.reshape(1, w), a_up, g_up_pad)


def _head_pair_ones():
    r = lax.broadcasted_iota(jnp.int32, (LANES, LANES), 0) // RWKV_HEAD_DIM
    c = lax.broadcasted_iota(jnp.int32, (LANES, LANES), 1) // RWKV_HEAD_DIM
    return r == c


def _rwkv_prep_kernel(r_ref, k_ref, v_ref, wpre_ref, a_ref, kk_ref, ka_ref, rk_ref,
                      rt_ref, at_ref, kh_ref, bh_ref, kb_ref, bb_ref, pc_ref, bonus_ref):
    tm = r_ref.shape[0]
    r = r_ref[...]
    k = k_ref[...]
    v = v_ref[...]
    a = a_ref[...]
    same_head = jnp.where(_head_pair_ones(), 1.0, 0.0).astype(BF16)

    x = -wpre_ref[...]
    softplus = jnp.maximum(x, 0.0) + jnp.log1p(jnp.exp(-jnp.abs(x)))
    lw = -jnp.exp(-softplus - 0.5)

    kk = k * kk_ref[...]
    ss = _dot_r3(kk * kk, same_head)
    kkn = kk / jnp.maximum(jnp.sqrt(ss), 1e-12)
    k2 = k * (1.0 + (a - 1.0) * ka_ref[...])
    bonus_ref[...] = _dot_r3(r * k2 * rk_ref[...], same_head) * v

    ri = lax.broadcasted_iota(jnp.int32, (tm, tm), 0)
    ci = lax.broadcasted_iota(jnp.int32, (tm, tm), 1)
    same_chunk = (ri // CHUNK) == (ci // CHUNK)
    tri = jnp.where(same_chunk & (ci <= ri), 1.0, 0.0).astype(BF16)
    allc = jnp.where(same_chunk, 1.0, 0.0).astype(BF16)
    cum = _dot_l3(tri, lw)
    tot = _dot_l3(allc, lw)

    p_in = jnp.exp(cum)
    p_inv = jnp.exp(-cum)
    p_tail = jnp.exp(tot - cum)
    b = kkn * a
    rt_ref[...] = r * p_in
    at_ref[...] = -kkn * jnp.exp(cum - lw)
    kh_ref[...] = k2 * p_inv
    bh_ref[...] = b * p_inv
    kb_ref[...] = k2 * p_tail
    bb_ref[...] = b * p_tail
    pc_ref[...] = jnp.exp(tot)


def _rwkv_prep(proj_r, wpre, a, k_k, k_a, r_k, tm=256):
    s, w = wpre.shape
    npair = w // LANES
    blk = lambda off: pl.BlockSpec((tm, LANES), lambda i, j, off=off: (i, j + off))
    vec = pl.BlockSpec((1, LANES), lambda i, j: (0, j))
    out = jax.ShapeDtypeStruct((s, w), F32)
    return pl.pallas_call(
        _rwkv_prep_kernel,
        grid=(s // tm, npair),
        in_specs=[blk(0), blk(npair), blk(2 * npair), blk(0), blk(0), vec, vec, vec],
        out_specs=[blk(0)] * 8,
        out_shape=[out] * 8,
        compiler_params=_params(("parallel", "parallel")),
        name="rwkv_prep",
    )(proj_r, proj_r, proj_r, wpre, a, k_k.reshape(1, w), k_a.reshape(1, w), r_k.reshape(1, w))


def _rwkv_chunk_kernel(rt_ref, at_ref, kh_ref, bh_ref, kb_ref, bb_ref, pc_ref, v_ref,
                       x_ref, y_ref, op_ref, qb_ref, bbt_ref, g_ref, pcm_ref):
    c = CHUNK
    nheads = LANES // RWKV_HEAD_DIM
    chunks = range(rt_ref.shape[0] // c)
    rows = [slice(q * c, (q + 1) * c) for q in chunks]
    probs = [(q, h) for q in chunks for h in range(nheads)]
    lane = lax.broadcasted_iota(jnp.int32, (c, LANES), 1)
    ri = lax.broadcasted_iota(jnp.int32, (c, c), 0)
    ci = lax.broadcasted_iota(jnp.int32, (c, c), 1)
    strict = ci < ri
    incl = ci <= ri
    eye = jnp.where(ri == ci, 1.0, 0.0)
    head_mask = [(lane // RWKV_HEAD_DIM) == h for h in range(nheads)]

    kh = [kh_ref[r, :].astype(BF16) for r in rows]
    bh = [bh_ref[r, :].astype(BF16) for r in rows]
    a_h = [jnp.where(head_mask[h], at_ref[rows[q], :], 0.0).astype(BF16) for q, h in probs]
    r_h = [jnp.where(head_mask[h], rt_ref[rows[q], :], 0.0).astype(BF16) for q, h in probs]
    v_h = [jnp.where(head_mask[h], v_ref[rows[q], :], 0.0).astype(BF16) for q, h in probs]
    n_ab = [jnp.where(strict, _dot_nt(a, bh[q]), 0.0) for a, (q, _) in zip(a_h, probs)]
    n_ak = [jnp.where(strict, _dot_nt(a, kh[q]), 0.0).astype(BF16) for a, (q, _) in zip(a_h, probs)]
    q_k = [jnp.where(incl, _dot_nt(r, kh[q]), 0.0).astype(BF16) for r, (q, _) in zip(r_h, probs)]
    q_b = [jnp.where(incl, _dot_nt(r, bh[q]), 0.0) for r, (q, _) in zip(r_h, probs)]
    for qb, (q, h) in zip(q_b, probs):
        qb_ref[rows[q], h * c:(h + 1) * c] = qb.astype(qb_ref.dtype)

    t = [eye + n for n in n_ab]
    p = n_ab
    for _ in range((c - 1).bit_length() - 1):
        pb = [x.astype(BF16) for x in p]
        p = [_dot(x, x) for x in pb]
        t = [ti + _dot(ti.astype(BF16), pi.astype(BF16)) for ti, pi in zip(t, p)]
    tb = [ti.astype(BF16) for ti in t]
    xs = [_dot(ti, a) for ti, a in zip(tb, a_h)]
    nv = [_dot(n, v).astype(BF16) for n, v in zip(n_ak, v_h)]
    ys = [_dot(ti, z) for ti, z in zip(tb, nv)]
    os = [_dot(qk, v) for qk, v in zip(q_k, v_h)]
    same_head = _head_pair_ones()
    for q in chunks:
        mine = [n for n, (qq, _) in enumerate(probs) if qq == q]
        x_ref[rows[q], :] = sum(xs[n] for n in mine).astype(x_ref.dtype)
        y_ref[rows[q], :] = sum(ys[n] for n in mine)
        op_ref[rows[q], :] = sum(os[n] for n in mine)
        kbt = kb_ref[rows[q], :].T.astype(BF16)
        g_ref[rows[q], :] = jnp.where(same_head, _dot(kbt, v_ref[rows[q], :].astype(BF16)), 0.0)
        bbt_ref[:, rows[q]] = bb_ref[rows[q], :].T.astype(bbt_ref.dtype)
        pcm_ref[rows[q], :] = pc_ref[rows[q], :].T


def _rwkv_chunk(rt, at, kh, bh, kb, bb, pc, proj_r, tc=4 * CHUNK):
    s, w = rt.shape
    npair = w // LANES
    c = CHUNK
    blk = lambda off: pl.BlockSpec((tc, LANES), lambda i, j, off=off: (i, j + off))
    f32o = jax.ShapeDtypeStruct((s, w), F32)
    return pl.pallas_call(
        _rwkv_chunk_kernel,
        grid=(s // tc, npair),
        in_specs=[blk(0)] * 7 + [blk(2 * npair)],
        out_specs=[blk(0), blk(0), blk(0),
                   pl.BlockSpec((tc, 2 * c), lambda i, j: (i, j)),
                   pl.BlockSpec((LANES, tc), lambda i, j: (j, i)),
                   blk(0), blk(0)],
        out_shape=[jax.ShapeDtypeStruct((s, w), BF16), f32o, f32o,
                   jax.ShapeDtypeStruct((s, 2 * w), BF16),
                   jax.ShapeDtypeStruct((w, s), BF16),
                   f32o, f32o],
        compiler_params=_params(("parallel", "parallel")),
        name="rwkv_chunk",
    )(rt, at, kh, bh, kb, bb, pc, proj_r)


def _rwkv_scan_kernel(x_ref, y_ref, rt_ref, op_ref, qb_ref, bbt_ref, g_ref, pcm_ref,
                      o_ref, h_ref):
    c = CHUNK

    @pl.when(pl.program_id(1) == 0)
    def _():
        h_ref[...] = jnp.zeros_like(h_ref)

    lane = lax.broadcasted_iota(jnp.int32, (c, LANES), 1)
    first = lane < RWKV_HEAD_DIM
    same_head = _head_pair_ones()
    for q in range(x_ref.shape[0] // c):
        rows = slice(q * c, (q + 1) * c)
        h = h_ref[...]
        hb = h.astype(BF16)
        u = _dot(x_ref[rows, :], hb) + y_ref[rows, :]
        u2 = jnp.concatenate([jnp.where(first, u, 0.0), jnp.where(first, 0.0, u)], axis=0).astype(BF16)
        o_ref[rows, :] = (_dot(rt_ref[rows, :].astype(BF16), hb) + op_ref[rows, :]
                          + _dot(qb_ref[rows, :], u2))
        upd = _dot(bbt_ref[:, rows], u.astype(BF16))
        h_ref[...] = pcm_ref[rows, :] * h + g_ref[rows, :] + jnp.where(same_head, upd, 0.0)


def _rwkv_scan(x, y, rt, opre, qb2, bbt, g, pcm, ts=512):
    s, w = y.shape
    npair = w // LANES
    ts = min(ts, s)
    blk = pl.BlockSpec((ts, LANES), lambda j, i: (i, j))
    return pl.pallas_call(
        _rwkv_scan_kernel,
        grid=(npair, s // ts),
        in_specs=[blk, blk, blk, blk,
                  pl.BlockSpec((ts, 2 * CHUNK), lambda j, i: (i, j)),
                  pl.BlockSpec((LANES, ts), lambda j, i: (j, i)),
                  blk, blk],
        out_specs=blk,
        out_shape=jax.ShapeDtypeStruct((s, w), F32),
        scratch_shapes=[pltpu.VMEM((LANES, LANES), F32)],
        compiler_params=_params(("parallel", "arbitrary")),
        name="rwkv_scan",
    )(x, y, rt, opre, qb2, bbt, g, pcm)


def _rwkv_post_kernel(o_ref, bonus_ref, g_ref, lng_ref, lnb_ref, y_ref):
    same_head = jnp.where(_head_pair_ones(), 1.0, 0.0).astype(BF16)
    o = o_ref[...]
    mu = _dot_r3(o, same_head) * (1.0 / RWKV_HEAD_DIM)
    d = o - mu
    var = _dot_r3(d * d, same_head) * (1.0 / RWKV_HEAD_DIM)
    y = d * lax.rsqrt(var + LN_X_EPS) * lng_ref[...] + lnb_ref[...]
    y_ref[...] = ((y + bonus_ref[...]) * g_ref[...]).astype(y_ref.dtype)


def _rwkv_post(o, bonus, g, lnx_g, lnx_b, tm=512):
    s, w = o.shape
    tm = min(tm, s)
    blk = pl.BlockSpec((tm, LANES), lambda i, j: (i, j))
    vec = pl.BlockSpec((1, LANES), lambda i, j: (0, j))
    return pl.pallas_call(
        _rwkv_post_kernel,
        grid=(s // tm, w // LANES),
        in_specs=[blk, blk, blk, vec, vec],
        out_specs=blk,
        out_shape=jax.ShapeDtypeStruct((s, w), BF16),
        compiler_params=_params(("parallel", "parallel")),
        name="rwkv_post",
    )(o, bonus, g, lnx_g.reshape(1, w), lnx_b.reshape(1, w))


V_ONES_ROWS = 16


LOG2_E = 1.4426950408889634
MOBA_LOG2_SCALE = (MOBA_HEAD_DIM ** -0.5) * LOG2_E


def _moba_prep_kernel(q_ref, k_ref, v_ref, cos_ref, sin_ref, qo_ref, qs_ref, ko_ref, vt_ref, km_ref):
    cos = cos_ref[...]
    sin = sin_ref[...]
    hd = MOBA_HEAD_DIM
    half = hd // 2
    tb = q_ref.shape[0]
    for h in range(q_ref.shape[1] // hd):
        cols = slice(h * hd, (h + 1) * hd)
        q = q_ref[:, cols]
        k = k_ref[:, cols]
        qr = q * cos + pltpu.roll(q, half, axis=1) * sin
        kr = k * cos + pltpu.roll(k, half, axis=1) * sin
        qo_ref[:, cols] = qr.astype(qo_ref.dtype)
        qs_ref[:, cols] = (qr * MOBA_LOG2_SCALE).astype(qs_ref.dtype)
        ko_ref[:, cols] = kr.astype(ko_ref.dtype)
        km_ref[0, :, cols] = jnp.mean(kr, axis=0, keepdims=True)
        vt_ref[h, 0, :hd, :] = v_ref[:, cols].T.astype(vt_ref.dtype)
        vt_ref[h, 0, hd:, :] = jnp.ones((V_ONES_ROWS, tb), vt_ref.dtype)


def _moba_prep(proj_m, cos, sin):
    s = proj_m.shape[0]
    w = proj_m.shape[1] // 3
    tb = MOBA_BLOCK
    nb = s // tb
    nh = w // MOBA_HEAD_DIM
    vrows = MOBA_HEAD_DIM + V_ONES_ROWS
    blk = lambda off: pl.BlockSpec((tb, w), lambda i, off=off: (i, off))
    tab = pl.BlockSpec((tb, MOBA_HEAD_DIM), lambda i: (i, 0))
    bo = jax.ShapeDtypeStruct((s, w), BF16)
    return pl.pallas_call(
        _moba_prep_kernel,
        grid=(nb,),
        in_specs=[blk(0), blk(1), blk(2), tab, tab],
        out_specs=[blk(0), blk(0), blk(0),
                   pl.BlockSpec((nh, 1, vrows, tb), lambda i: (0, i, 0, 0)),
                   pl.BlockSpec((1, 1, w), lambda i: (i, 0, 0))],
        out_shape=[bo, bo, bo, jax.ShapeDtypeStruct((nh, nb, vrows, tb), BF16),
                   jax.ShapeDtypeStruct((nb, 1, w), F32)],
        compiler_params=_params(("parallel",)),
        name="moba_prep",
    )(proj_m, proj_m, proj_m, cos, sin)


MOBA_CHAINS = 4


def _moba_attn_kernel(q_ref, qs_ref, k_ref, vt_ref, km_ref, o_ref, bias_ref):
    i = pl.program_id(1)
    tb = MOBA_BLOCK
    nb = km_ref.shape[0]
    hd = MOBA_HEAD_DIM
    q = qs_ref[...]

    gate = _dot_nt(km_ref[...].astype(BF16), q_ref[...])
    rid = lax.broadcasted_iota(jnp.int32, gate.shape, 0).astype(F32)
    gate = jnp.where(rid < i.astype(F32), gate, -jnp.inf)
    bias = jnp.full(gate.shape, -jnp.inf, F32)
    for _ in range(MOBA_TOPK):
        m = jnp.max(gate, axis=0, keepdims=True)
        first = jnp.min(jnp.where(gate == m, rid, float(nb)), axis=0, keepdims=True)
        pick = (rid == first) & (m > -jnp.inf)
        bias = jnp.where(pick, 0.0, bias)
        gate = jnp.where(pick, -jnp.inf, gate)
    bias_ref[...] = bias

    own = pl.multiple_of(i * tb, tb)
    s = _dot_nt(k_ref[pl.ds(own, tb), :], q)
    ki = lax.broadcasted_iota(jnp.int32, s.shape, 0)
    qi = lax.broadcasted_iota(jnp.int32, s.shape, 1)
    s = jnp.where(ki <= qi, s, -jnp.inf).astype(BF16)
    m0 = jnp.max(s, axis=0, keepdims=True)
    acc0 = _dot(vt_ref[i], jnp.exp2(s - m0))
    m0 = m0.astype(F32)

    def scores(t):
        js = [jnp.minimum(t * MOBA_CHAINS + g, nb - 1) for g in range(MOBA_CHAINS)]
        return [_dot_nt(k_ref[pl.ds(pl.multiple_of(j * tb, tb), tb), :], q).astype(BF16)
                + bias_ref[pl.ds(j, 1), :].astype(BF16) for j in js]

    def body(t, carry):
        chains, ss = carry
        ss_next = scores(t + 1)
        js = [t * MOBA_CHAINS + g for g in range(MOBA_CHAINS)]
        ms = [jnp.maximum(m, jnp.max(sj, axis=0, keepdims=True).astype(F32))
              for (m, _), sj in zip(chains, ss)]
        ps = [jnp.exp2(sj - mn.astype(BF16)) for sj, mn in zip(ss, ms)]
        chains = tuple((mn, jnp.exp2(m - mn) * acc + _dot(vt_ref[j], pj))
                       for (m, acc), mn, pj, j in zip(chains, ms, ps, js))
        return chains, ss_next

    init = ((m0, acc0),) + ((m0, jnp.zeros_like(acc0)),) * (MOBA_CHAINS - 1)
    chains, _ = lax.fori_loop(0, (i + MOBA_CHAINS - 1) // MOBA_CHAINS, body, (init, scores(0)))
    m = chains[0][0]
    for mg, _ in chains[1:]:
        m = jnp.maximum(m, mg)
    acc = sum(jnp.exp2(mg - m) * ag for mg, ag in chains)
    out = acc[:hd, :] / acc[hd:hd + 1, :]
    o_ref[...] = out.T.astype(o_ref.dtype)


def _moba_attn(q, qs, k, vt, kmean):
    s, w = q.shape
    tb = MOBA_BLOCK
    nb = s // tb
    hd = MOBA_HEAD_DIM
    vrows = vt.shape[2]
    return pl.pallas_call(
        _moba_attn_kernel,
        grid=(w // hd, nb),
        in_specs=[pl.BlockSpec((tb, hd), lambda h, i: (i, h)),
                  pl.BlockSpec((tb, hd), lambda h, i: (i, h)),
                  pl.BlockSpec((s, hd), lambda h, i: (0, h)),
                  pl.BlockSpec((None, nb, vrows, tb), lambda h, i: (h, 0, 0, 0)),
                  pl.BlockSpec((nb, hd), lambda h, i: (0, h))],
        out_specs=pl.BlockSpec((tb, hd), lambda h, i: (i, h)),
        out_shape=jax.ShapeDtypeStruct((s, w), BF16),
        scratch_shapes=[pltpu.VMEM((nb, tb), F32)],
        compiler_params=_params(("parallel", "arbitrary")),
        name="moba_attn",
    )(q, qs, k, vt, kmean)


def _top_rows(x, n):
    rows = x.shape[0]
    rid = lax.broadcasted_iota(jnp.int32, x.shape, 0).astype(F32)
    vals, idxs = [], []
    for _ in range(n):
        m = jnp.max(x, axis=0, keepdims=True)
        first = jnp.min(jnp.where(x == m, rid, float(rows)), axis=0, keepdims=True)
        vals.append(m)
        idxs.append(first)
        x = jnp.where(rid == first, -jnp.inf, x)
    return jnp.concatenate(vals, axis=0), jnp.concatenate(idxs, axis=0)


def _peer_route_kernel(q_ref, keys_ref, e1_ref, e2_ref, g_ref):
    n = PEER_TOPK
    tm = q_ref.shape[0]
    e1s, e2s, gs = [], [], []
    for h in range(PEER_HEADS):
        tops = []
        for p in range(2):
            hp = 2 * h + p
            cols = slice(hp * PEER_HALF, (hp + 1) * PEER_HALF)
            rows = slice(hp * PEER_N_KEYS, (hp + 1) * PEER_N_KEYS)
            st = _dot_nt(keys_ref[rows, :], q_ref[:, cols].astype(BF16))
            tops.append(_top_rows(st, n))
        (s1, i1), (s2, i2) = tops
        hn = n // 2
        cand = jnp.concatenate([s1[0:1, :] + s2]
                               + [s1[a:a + 1, :] + s2[:hn, :] for a in range(1, hn)]
                               + [s1[hn:, :] + s2[0:1, :]], axis=0)
        f_s, f_pos = _top_rows(cand, n)
        mid = jnp.floor((f_pos - n) * (1.0 / hn))
        tail0 = float(n + (hn - 1) * hn)
        pa = jnp.where(f_pos < n, 0.0, jnp.where(f_pos < tail0, 1.0 + mid, f_pos - tail0 + hn))
        pb = jnp.where(f_pos < n, f_pos, jnp.where(f_pos < tail0, f_pos - n - mid * hn, 0.0))
        e1 = jnp.zeros((n, tm), F32)
        e2 = jnp.zeros((n, tm), F32)
        for a in range(n):
            e1 = jnp.where(pa == float(a), i1[a:a + 1, :], e1)
            e2 = jnp.where(pb == float(a), i2[a:a + 1, :], e2)
        ex = jnp.exp(f_s - f_s[0:1, :])
        gs.append(ex / jnp.sum(ex, axis=0, keepdims=True))
        e1s.append(e1)
        e2s.append(e2)
    e1_ref[...] = jnp.concatenate(e1s, axis=0).T
    e2_ref[...] = jnp.concatenate(e2s, axis=0).T
    g_ref[...] = jnp.concatenate(gs, axis=0).T


def _peer_route(q, keys, tm=256):
    s, w = q.shape
    nsel = PEER_HEADS * PEER_TOPK
    tm = min(tm, s)
    out = jax.ShapeDtypeStruct((s, nsel), F32)
    ob = pl.BlockSpec((tm, nsel), lambda i: (i, 0))
    return pl.pallas_call(
        _peer_route_kernel,
        grid=(s // tm,),
        in_specs=[pl.BlockSpec((tm, w), lambda i: (i, 0)),
                  pl.BlockSpec(keys.shape, lambda i: (0, 0))],
        out_specs=[ob, ob, ob],
        out_shape=[out, out, out],
        compiler_params=_params(("parallel",)),
        name="peer_route",
    )(q, keys)


EXPAND_UNROLL = 16
EXPAND_GROUP = 4


def _peer_expand_kernel(e1_ref, e2_ref, g_ref, w_ref, stage_ref):
    nk = PEER_N_KEYS
    nsel = e1_ref.shape[1]
    rid = lax.broadcasted_iota(jnp.int32, (nk, nsel), 0).astype(F32)
    un = EXPAND_UNROLL

    def body(tt, carry):
        base = pl.multiple_of(tt * un, un)
        e1 = e1_ref[pl.ds(base, un), :]
        e2 = e2_ref[pl.ds(base, un), :]
        g = g_ref[pl.ds(base, un), :]
        for u0 in range(0, un, EXPAND_GROUP):
            us = range(u0, u0 + EXPAND_GROUP)
            lefts = [jnp.where(rid == e1[u:u + 1, :], g[u:u + 1, :], 0.0).astype(BF16) for u in us]
            rights = [jnp.where(rid == e2[u:u + 1, :], 1.0, 0.0).astype(BF16) for u in us]
            for u, l, r in zip(us, lefts, rights):
                stage_ref[u * nk:(u + 1) * nk, :] = _dot_nt(l, r)
        for a in range(nk):
            rows = stage_ref[pl.ds(a, un, stride=nk), :]
            w_ref[pl.ds(base, un), a * nk:(a + 1) * nk] = rows.astype(w_ref.dtype)
        return carry

    lax.fori_loop(0, e1_ref.shape[0] // un, body, 0)


def _peer_expand(e1, e2, g, tm=128):
    s, nsel = e1.shape
    nk = PEER_N_KEYS
    tm = min(tm, s)
    ib = pl.BlockSpec((tm, nsel), lambda i: (i, 0))
    return pl.pallas_call(
        _peer_expand_kernel,
        grid=(s // tm,),
        in_specs=[ib, ib, ib],
        out_specs=pl.BlockSpec((tm, nk * nk), lambda i: (i, 0)),
        out_shape=jax.ShapeDtypeStruct((s, nk * nk), BF16),
        scratch_shapes=[pltpu.VMEM((EXPAND_UNROLL * nk, nk), F32)],
        compiler_params=_params(("parallel",)),
        name="peer_expand",
    )(e1, e2, g)


def _peer_dense_kernel(h_ref, u_ref, v_ref, w_ref, o_ref):
    @pl.when(pl.program_id(1) == 0)
    def _():
        o_ref[...] = jnp.zeros_like(o_ref)

    act = _dot_nt(h_ref[...], u_ref[...])
    gelu = 0.5 * act * (1.0 + lax.erf(act * (2.0 ** -0.5)))
    o_ref[...] += _dot((w_ref[...].astype(F32) * gelu).astype(BF16), v_ref[...])


def _peer_dense(h2, u, v, wdense, tm=512, te=512):
    s, d = h2.shape
    ne = u.shape[0]
    tm = min(tm, s)
    return pl.pallas_call(
        _peer_dense_kernel,
        grid=(s // tm, ne // te),
        in_specs=[pl.BlockSpec((tm, d), lambda i, e: (i, 0)),
                  pl.BlockSpec((te, d), lambda i, e: (e, 0)),
                  pl.BlockSpec((te, d), lambda i, e: (e, 0)),
                  pl.BlockSpec((tm, te), lambda i, e: (i, e))],
        out_specs=pl.BlockSpec((tm, d), lambda i, e: (i, 0)),
        out_shape=jax.ShapeDtypeStruct((s, d), F32),
        compiler_params=_params(("parallel", "arbitrary")),
        name="peer_dense",
    )(h2, u, v, wdense)


def _add_rmsnorm_kernel(x_ref, y_ref, g_ref, o_ref):
    x = x_ref[...] + y_ref[...]
    ms = jnp.mean(x * x, axis=-1, keepdims=True)
    o_ref[...] = x * lax.rsqrt(ms + NORM_EPS) * g_ref[...]


def _add_rmsnorm(x, y, g, tm=256):
    s, d = x.shape
    blk = pl.BlockSpec((tm, d), lambda i: (i, 0))
    return pl.pallas_call(
        _add_rmsnorm_kernel,
        grid=(s // tm,),
        in_specs=[blk, blk, pl.BlockSpec((1, d), lambda i: (0, 0))],
        out_specs=blk,
        out_shape=jax.ShapeDtypeStruct((s, d), F32),
        compiler_params=_params(("parallel",)),
        name="residual_final_norm",
    )(x, y, g.reshape(1, d))


def _rwkv_branch(proj_r, w0, w_up, a0, a_up, g_up, k_k, k_a, r_k, lnx_g, lnx_b):
    g_up_pad = jnp.pad(g_up, ((0, GATE_LORA_PAD - GATE_LORA), (0, 0))).astype(BF16)
    wpre, a, g = _rwkv_lora(proj_r, w0, w_up.astype(BF16), a0, a_up.astype(BF16), g_up_pad)
    rt, at, kh, bh, kb, bb, pc, bonus = _rwkv_prep(proj_r, wpre, a, k_k, k_a, r_k.reshape(-1))
    x, y, opre, qb2, bbt, gst, pcm = _rwkv_chunk(rt, at, kh, bh, kb, bb, pc, proj_r)
    o = _rwkv_scan(x, y, rt, opre, qb2, bbt, gst, pcm)
    return _rwkv_post(o, bonus, g, lnx_g, lnx_b)


def _rope_tables(s):
    half = MOBA_HEAD_DIM // 2
    inv_freq = ROPE_THETA ** (-jnp.arange(half, dtype=F32) / half)
    ang = jnp.arange(s, dtype=jnp.int32).astype(F32)[:, None] * inv_freq[None, :]
    cos = jnp.cos(ang)
    sin = jnp.sin(ang)
    return jnp.concatenate([cos, cos], axis=-1), jnp.concatenate([-sin, sin], axis=-1)


def _moba_branch(proj_m):
    s = proj_m.shape[0]
    cos, sin = _rope_tables(s)
    q, qs, k, vt, kmean = _moba_prep(proj_m, cos, sin)
    return _moba_attn(q, qs, k, vt, kmean.reshape(kmean.shape[0], kmean.shape[2]))


def _peer_layer(x1, norm2_g, w_q, sub_keys, expert_u, expert_v, final_g):
    s, d = x1.shape
    h2 = _rmsnorm(x1, norm2_g, BF16)
    q = _matmul(h2, w_q.astype(BF16), F32, tm=min(1024, s), tn=512, name="peer_query")
    keys = sub_keys.reshape(PEER_HEADS * 2 * PEER_N_KEYS, PEER_HALF).astype(BF16)
    e1, e2, g = _peer_route(q, keys)
    wdense = _peer_expand(e1, e2, g)
    peer = _peer_dense(h2, expert_u.astype(BF16), expert_v.astype(BF16), wdense)
    return _add_rmsnorm(x1, peer, final_g)


def kernel(x, norm1_g, w_in, rwkv_mu, rwkv_w0, rwkv_w_up, rwkv_a0, rwkv_a_up, rwkv_g_up, rwkv_k_k, rwkv_k_a, rwkv_r_k, rwkv_lnx_g, rwkv_lnx_b, w_branch_rwkv, w_branch_moba, w_out, norm2_g, peer_w_q, peer_sub_keys, peer_u, peer_v, final_g):
    b, s, d = x.shape
    depth = w_in.shape[0]
    assert b == 1 and depth == 1
    x2d = x.reshape(s, d)
    l = 0
    rw = rwkv_w0.shape[-1]
    shift_w = 3 * rw + DECAY_LORA + AAA_LORA + GATE_LORA
    tn = 512
    shift_pad = -(-shift_w // tn) * tn
    mw = w_branch_moba.shape[1]
    tm = min(1024, s)

    h = _rmsnorm(x2d, norm1_g[l], BF16)
    mu = jnp.pad(rwkv_mu[l], (0, shift_pad - shift_w)).reshape(1, shift_pad)
    proj_r = _matmul_shift(h, w_in[l], mu, tm=tm, tn=tn)
    w_mg = w_in[l][:, shift_w:].astype(BF16)
    proj_m = _matmul(h, w_mg, F32, tm=tm, tn=tn, name="in_proj_moba", col0=0, n=3 * mw)
    gates = _matmul(h, w_mg, F32, tm=tm, tn=tn, name="in_proj_gates", col0=3 * mw, n=2 * d)

    y_a = _rwkv_branch(proj_r, rwkv_w0[l], rwkv_w_up[l], rwkv_a0[l], rwkv_a_up[l], rwkv_g_up[l],
                       rwkv_k_k[l], rwkv_k_a[l], rwkv_r_k[l], rwkv_lnx_g[l], rwkv_lnx_b[l])
    y_b = _moba_branch(proj_m)
    mixed = _gated_pair(y_a, w_branch_rwkv[l].astype(BF16), y_b, w_branch_moba[l].astype(BF16),
                        gates, tm=tm, tn=512)
    x1 = _matmul_residual(mixed, w_out[l].astype(BF16), x2d, tm=tm, tn=512)
    out = _peer_layer(x1, norm2_g[l], peer_w_q[l], peer_sub_keys[l], peer_u[l], peer_v[l], final_g)
    return out.reshape(b, s, d)
```

```python
import functools

import jax
import jax.numpy as jnp
from jax import lax
from jax.experimental import pallas as pl
from jax.experimental.pallas import tpu as pltpu

F32 = jnp.float32
BF16 = jnp.bfloat16

NORM_EPS = 1e-6
LANES = 128
RWKV_HEAD_DIM = 64
DECAY_LORA = 128
AAA_LORA = 128
GATE_LORA = 480
GATE_LORA_PAD = 512
LN_X_EPS = 64e-5
CHUNK = 128
MOBA_HEAD_DIM = 128
MOBA_BLOCK = 256
MOBA_TOPK = 3
ROPE_THETA = 10000.0
PEER_HEADS = 8
PEER_N_KEYS = 128
PEER_HALF = 128
PEER_TOPK = 16

VMEM_LIMIT = 56 * 1024 * 1024

_NT = (((1,), (1,)), ((), ()))


def _params(sem, vmem=VMEM_LIMIT):
    return pltpu.CompilerParams(dimension_semantics=sem, vmem_limit_bytes=vmem)


def _dot(a, b):
    return jnp.dot(a, b, preferred_element_type=F32)


def _dot_nt(a, b):
    return lax.dot_general(a, b, _NT, preferred_element_type=F32)


def _split3(x):
    hi = x.astype(BF16)
    r1 = x - hi.astype(F32)
    mid = r1.astype(BF16)
    lo = (r1 - mid.astype(F32)).astype(BF16)
    return hi, mid, lo


def _dot_l3(l_bf16, x):
    hi, mid, lo = _split3(x)
    return _dot(l_bf16, hi) + _dot(l_bf16, mid) + _dot(l_bf16, lo)


def _dot_r3(x, r_bf16):
    hi, mid, lo = _split3(x)
    return _dot(hi, r_bf16) + _dot(mid, r_bf16) + _dot(lo, r_bf16)


def _rmsnorm_kernel(x_ref, g_ref, o_ref):
    x = x_ref[...]
    ms = jnp.mean(x * x, axis=-1, keepdims=True)
    o_ref[...] = (x * lax.rsqrt(ms + NORM_EPS) * g_ref[...]).astype(o_ref.dtype)


def _rmsnorm(x, g, out_dtype, tm=256):
    s, d = x.shape
    return pl.pallas_call(
        _rmsnorm_kernel,
        grid=(s // tm,),
        in_specs=[pl.BlockSpec((tm, d), lambda i: (i, 0)),
                  pl.BlockSpec((1, d), lambda i: (0, 0))],
        out_specs=pl.BlockSpec((tm, d), lambda i: (i, 0)),
        out_shape=jax.ShapeDtypeStruct((s, d), out_dtype),
        compiler_params=_params(("parallel",)),
        name="rmsnorm",
    )(x, g.reshape(1, d))


def _mm_kernel(a_ref, b_ref, o_ref):
    o_ref[...] = _dot(a_ref[...], b_ref[...]).astype(o_ref.dtype)


def _matmul(a, b, out_dtype, tm, tn, name, col0=0, n=None):
    m, k = a.shape
    n = b.shape[1] if n is None else n
    off = col0 // tn
    return pl.pallas_call(
        _mm_kernel,
        grid=(m // tm, n // tn),
        in_specs=[pl.BlockSpec((tm, k), lambda i, j: (i, 0)),
                  pl.BlockSpec((k, tn), lambda i, j: (0, j + off))],
        out_specs=pl.BlockSpec((tm, tn), lambda i, j: (i, j)),
        out_shape=jax.ShapeDtypeStruct((m, n), out_dtype),
        compiler_params=_params(("parallel", "parallel")),
        name=name,
    )(a, b)


BF16_SUBLANES = 16


def _mm_shift_kernel(a_ref, ap_ref, b_ref, mu_ref, o_ref):
    i = pl.program_id(0)
    b = b_ref[...].astype(BF16)
    z = _dot(a_ref[...], b)
    zp = _dot(ap_ref[...], b)
    prev = jnp.where(i > 0, zp[BF16_SUBLANES - 1:BF16_SUBLANES, :], 0.0)
    row = lax.broadcasted_iota(jnp.int32, z.shape, 0)
    z_prev = jnp.where(row == 0, prev, pltpu.roll(z, 1, axis=0))
    o_ref[...] = z + (z_prev - z) * mu_ref[...]


def _matmul_shift(a, b, mu, tm, tn):
    m, k = a.shape
    n = mu.shape[1]
    per = tm // BF16_SUBLANES
    return pl.pallas_call(
        _mm_shift_kernel,
        grid=(m // tm, n // tn),
        in_specs=[pl.BlockSpec((tm, k), lambda i, j: (i, 0)),
                  pl.BlockSpec((BF16_SUBLANES, k), lambda i, j: (jnp.maximum(i * per - 1, 0), 0)),
                  pl.BlockSpec((k, tn), lambda i, j: (0, j)),
                  pl.BlockSpec((1, tn), lambda i, j: (0, j))],
        out_specs=pl.BlockSpec((tm, tn), lambda i, j: (i, j)),
        out_shape=jax.ShapeDtypeStruct((m, n), F32),
        compiler_params=_params(("parallel", "parallel")),
        name="in_proj_rwkv",
    )(a, a, b, mu)


def _gated_pair_kernel(ya_ref, wa_ref, yb_ref, wb_ref, ga_ref, gb_ref, o_ref):
    pa = _dot(ya_ref[...], wa_ref[...])
    pb = _dot(yb_ref[...], wb_ref[...])
    mixed = jax.nn.sigmoid(ga_ref[...]) * pa + jax.nn.sigmoid(gb_ref[...]) * pb
    o_ref[...] = mixed.astype(o_ref.dtype)


def _gated_pair(ya, wa, yb, wb, gates, tm, tn):
    m, k = ya.shape
    n = wa.shape[1]
    nb = n // tn
    return pl.pallas_call(
        _gated_pair_kernel,
        grid=(m // tm, nb),
        in_specs=[pl.BlockSpec((tm, k), lambda i, j: (i, 0)),
                  pl.BlockSpec((k, tn), lambda i, j: (0, j)),
                  pl.BlockSpec((tm, k), lambda i, j: (i, 0)),
                  pl.BlockSpec((k, tn), lambda i, j: (0, j)),
                  pl.BlockSpec((tm, tn), lambda i, j: (i, j)),
                  pl.BlockSpec((tm, tn), lambda i, j: (i, j + nb))],
        out_specs=pl.BlockSpec((tm, tn), lambda i, j: (i, j)),
        out_shape=jax.ShapeDtypeStruct((m, n), BF16),
        compiler_params=_params(("parallel", "parallel")),
        name="branch_merge",
    )(ya, wa, yb, wb, gates, gates)


def _mm_res_kernel(a_ref, b_ref, r_ref, o_ref):
    o_ref[...] = r_ref[...] + _dot(a_ref[...], b_ref[...])


def _matmul_residual(a, b, res, tm, tn):
    m, k = a.shape
    n = b.shape[1]
    return pl.pallas_call(
        _mm_res_kernel,
        grid=(m // tm, n // tn),
        in_specs=[pl.BlockSpec((tm, k), lambda i, j: (i, 0)),
                  pl.BlockSpec((k, tn), lambda i, j: (0, j)),
                  pl.BlockSpec((tm, tn), lambda i, j: (i, j))],
        out_specs=pl.BlockSpec((tm, tn), lambda i, j: (i, j)),
        out_shape=jax.ShapeDtypeStruct((m, n), F32),
        compiler_params=_params(("parallel", "parallel")),
        name="out_proj",
    )(a, b, res)


def _rwkv_lora_kernel(wl_ref, al_ref, gl0_ref, gl1_ref, w0_ref, wup_ref, a0_ref, aup_ref,
                      gup_ref, wpre_ref, a_ref, g_ref):
    half = GATE_LORA_PAD // 2
    wpre_ref[...] = w0_ref[...] + _dot(jnp.tanh(wl_ref[...]).astype(BF16), wup_ref[...])
    a_ref[...] = jax.nn.sigmoid(a0_ref[...] + _dot(al_ref[...].astype(BF16), aup_ref[...]))
    g_ref[...] = (_dot(jax.nn.sigmoid(gl0_ref[...]).astype(BF16), gup_ref[:half, :])
                  + _dot(jax.nn.sigmoid(gl1_ref[...]).astype(BF16), gup_ref[half:, :]))


def _rwkv_lora(proj_r, w0, w_up, a0, a_up, g_up_pad, tm=256):
    s = proj_r.shape[0]
    w = w0.shape[-1]
    half = GATE_LORA_PAD // 2
    c_wl = 3 * w // DECAY_LORA
    c_al = (3 * w + DECAY_LORA) // AAA_LORA
    c_gl = (3 * w + DECAY_LORA + AAA_LORA) // half
    row = lambda i: (i, 0)
    const = lambda i: (0, 0)
    out = jax.ShapeDtypeStruct((s, w), F32)
    return pl.pallas_call(
        _rwkv_lora_kernel,
        grid=(s // tm,),
        in_specs=[pl.BlockSpec((tm, DECAY_LORA), lambda i: (i, c_wl)),
                  pl.BlockSpec((tm, AAA_LORA), lambda i: (i, c_al)),
                  pl.BlockSpec((tm, half), lambda i: (i, c_gl)),
                  pl.BlockSpec((tm, half), lambda i: (i, c_gl + 1)),
                  pl.BlockSpec((1, w), const),
                  pl.BlockSpec((DECAY_LORA, w), const),
                  pl.BlockSpec((1, w), const),
                  pl.BlockSpec((AAA_LORA, w), const),
                  pl.BlockSpec((GATE_LORA_PAD, w), const)],
        out_specs=[pl.BlockSpec((tm, w), row)] * 3,
        out_shape=[out, out, out],
        compiler_params=_params(("parallel",)),
        name="rwkv_lora",
    )(proj_r, proj_r, proj_r, proj_r, w0.reshape(1, w), w_up, a0.reshape(1, w), a_up, g_up_pad)


def _head_pair_ones():
    r = lax.broadcasted_iota(jnp.int32, (LANES, LANES), 0) // RWKV_HEAD_DIM
    c = lax.broadcasted_iota(jnp.int32, (LANES, LANES), 1) // RWKV_HEAD_DIM
    return r == c


def _rwkv_prep_kernel(r_ref, k_ref, v_ref, wpre_ref, a_ref, kk_ref, ka_ref, rk_ref,
                      rt_ref, at_ref, kh_ref, bh_ref, kb_ref, bb_ref, pc_ref, bonus_ref):
    tm = r_ref.shape[0]
    r = r_ref[...]
    k = k_ref[...]
    v = v_ref[...]
    a = a_ref[...]
    same_head = jnp.where(_head_pair_ones(), 1.0, 0.0).astype(BF16)

    x = -wpre_ref[...]
    softplus = jnp.maximum(x, 0.0) + jnp.log1p(jnp.exp(-jnp.abs(x)))
    lw = -jnp.exp(-softplus - 0.5)

    kk = k * kk_ref[...]
    ss = _dot_r3(kk * kk, same_head)
    kkn = kk / jnp.maximum(jnp.sqrt(ss), 1e-12)
    k2 = k * (1.0 + (a - 1.0) * ka_ref[...])
    bonus_ref[...] = _dot_r3(r * k2 * rk_ref[...], same_head) * v

    ri = lax.broadcasted_iota(jnp.int32, (tm, tm), 0)
    ci = lax.broadcasted_iota(jnp.int32, (tm, tm), 1)
    same_chunk = (ri // CHUNK) == (ci // CHUNK)
    tri = jnp.where(same_chunk & (ci <= ri), 1.0, 0.0).astype(BF16)
    allc = jnp.where(same_chunk, 1.0, 0.0).astype(BF16)
    cum = _dot_l3(tri, lw)
    tot = _dot_l3(allc, lw)

    p_in = jnp.exp(cum)
    p_inv = jnp.exp(-cum)
    p_tail = jnp.exp(tot - cum)
    b = kkn * a
    rt_ref[...] = r * p_in
    at_ref[...] = -kkn * jnp.exp(cum - lw)
    kh_ref[...] = k2 * p_inv
    bh_ref[...] = b * p_inv
    kb_ref[...] = k2 * p_tail
    bb_ref[...] = b * p_tail
    pc_ref[...] = jnp.exp(tot)


def _rwkv_prep(proj_r, wpre, a, k_k, k_a, r_k, tm=256):
    s, w = wpre.shape
    npair = w // LANES
    blk = lambda off: pl.BlockSpec((tm, LANES), lambda i, j, off=off: (i, j + off))
    vec = pl.BlockSpec((1, LANES), lambda i, j: (0, j))
    out = jax.ShapeDtypeStruct((s, w), F32)
    return pl.pallas_call(
        _rwkv_prep_kernel,
        grid=(s // tm, npair),
        in_specs=[blk(0), blk(npair), blk(2 * npair), blk(0), blk(0), vec, vec, vec],
        out_specs=[blk(0)] * 8,
        out_shape=[out] * 8,
        compiler_params=_params(("parallel", "parallel")),
        name="rwkv_prep",
    )(proj_r, proj_r, proj_r, wpre, a, k_k.reshape(1, w), k_a.reshape(1, w), r_k.reshape(1, w))


def _rwkv_chunk_kernel(rt_ref, at_ref, kh_ref, bh_ref, kb_ref, bb_ref, pc_ref, v_ref,
                       x_ref, y_ref, op_ref, qb_ref, bbt_ref, g_ref, pcm_ref):
    c = CHUNK
    nheads = LANES // RWKV_HEAD_DIM
    chunks = range(rt_ref.shape[0] // c)
    rows = [slice(q * c, (q + 1) * c) for q in chunks]
    probs = [(q, h) for q in chunks for h in range(nheads)]
    lane = lax.broadcasted_iota(jnp.int32, (c, LANES), 1)
    ri = lax.broadcasted_iota(jnp.int32, (c, c), 0)
    ci = lax.broadcasted_iota(jnp.int32, (c, c), 1)
    strict = ci < ri
    incl = ci <= ri
    eye = jnp.where(ri == ci, 1.0, 0.0)
    head_mask = [(lane // RWKV_HEAD_DIM) == h for h in range(nheads)]

    kh = [kh_ref[r, :].astype(BF16) for r in rows]
    bh = [bh_ref[r, :].astype(BF16) for r in rows]
    a_h = [jnp.where(head_mask[h], at_ref[rows[q], :], 0.0).astype(BF16) for q, h in probs]
    r_h = [jnp.where(head_mask[h], rt_ref[rows[q], :], 0.0).astype(BF16) for q, h in probs]
    v_h = [jnp.where(head_mask[h], v_ref[rows[q], :], 0.0).astype(BF16) for q, h in probs]
    n_ab = [jnp.where(strict, _dot_nt(a, bh[q]), 0.0) for a, (q, _) in zip(a_h, probs)]
    n_ak = [jnp.where(strict, _dot_nt(a, kh[q]), 0.0).astype(BF16) for a, (q, _) in zip(a_h, probs)]
    q_k = [jnp.where(incl, _dot_nt(r, kh[q]), 0.0).astype(BF16) for r, (q, _) in zip(r_h, probs)]
    q_b = [jnp.where(incl, _dot_nt(r, bh[q]), 0.0) for r, (q, _) in zip(r_h, probs)]
    for qb, (q, h) in zip(q_b, probs):
        qb_ref[rows[q], h * c:(h + 1) * c] = qb.astype(qb_ref.dtype)

    t = [eye + n for n in n_ab]
    p = n_ab
    for _ in range((c - 1).bit_length() - 1):
        pb = [x.astype(BF16) for x in p]
        p = [_dot(x, x) for x in pb]
        t = [ti + _dot(ti.astype(BF16), pi.astype(BF16)) for ti, pi in zip(t, p)]
    tb = [ti.astype(BF16) for ti in t]
    xs = [_dot(ti, a) for ti, a in zip(tb, a_h)]
    nv = [_dot(n, v).astype(BF16) for n, v in zip(n_ak, v_h)]
    ys = [_dot(ti, z) for ti, z in zip(tb, nv)]
    os = [_dot(qk, v) for qk, v in zip(q_k, v_h)]
    same_head = _head_pair_ones()
    for q in chunks:
        mine = [n for n, (qq, _) in enumerate(probs) if qq == q]
        x_ref[rows[q], :] = sum(xs[n] for n in mine).astype(x_ref.dtype)
        y_ref[rows[q], :] = sum(ys[n] for n in mine)
        op_ref[rows[q], :] = sum(os[n] for n in mine)
        kbt = kb_ref[rows[q], :].T.astype(BF16)
        g_ref[rows[q], :] = jnp.where(same_head, _dot(kbt, v_ref[rows[q], :].astype(BF16)), 0.0)
        bbt_ref[:, rows[q]] = bb_ref[rows[q], :].T.astype(bbt_ref.dtype)
        pcm_ref[rows[q], :] = pc_ref[rows[q], :].T


def _rwkv_chunk(rt, at, kh, bh, kb, bb, pc, proj_r, tc=4 * CHUNK):
    s, w = rt.shape
    npair = w // LANES
    c = CHUNK
    blk = lambda off: pl.BlockSpec((tc, LANES), lambda i, j, off=off: (i, j + off))
    f32o = jax.ShapeDtypeStruct((s, w), F32)
    return pl.pallas_call(
        _rwkv_chunk_kernel,
        grid=(s // tc, npair),
        in_specs=[blk(0)] * 7 + [blk(2 * npair)],
        out_specs=[blk(0), blk(0), blk(0),
                   pl.BlockSpec((tc, 2 * c), lambda i, j: (i, j)),
                   pl.BlockSpec((LANES, tc), lambda i, j: (j, i)),
                   blk(0), blk(0)],
        out_shape=[jax.ShapeDtypeStruct((s, w), BF16), f32o, f32o,
                   jax.ShapeDtypeStruct((s, 2 * w), BF16),
                   jax.ShapeDtypeStruct((w, s), BF16),
                   f32o, f32o],
        compiler_params=_params(("parallel", "parallel")),
        name="rwkv_chunk",
    )(rt, at, kh, bh, kb, bb, pc, proj_r)


def _rwkv_scan_kernel(x_ref, y_ref, rt_ref, op_ref, qb_ref, bbt_ref, g_ref, pcm_ref,
                      o_ref, h_ref):
    c = CHUNK

    @pl.when(pl.program_id(1) == 0)
    def _():
        h_ref[...] = jnp.zeros_like(h_ref)

    lane = lax.broadcasted_iota(jnp.int32, (c, LANES), 1)
    first = lane < RWKV_HEAD_DIM
    same_head = _head_pair_ones()
    for q in range(x_ref.shape[0] // c):
        rows = slice(q * c, (q + 1) * c)
        h = h_ref[...]
        hb = h.astype(BF16)
        u = _dot(x_ref[rows, :], hb) + y_ref[rows, :]
        u2 = jnp.concatenate([jnp.where(first, u, 0.0), jnp.where(first, 0.0, u)], axis=0).astype(BF16)
        o_ref[rows, :] = (_dot(rt_ref[rows, :].astype(BF16), hb) + op_ref[rows, :]
                          + _dot(qb_ref[rows, :], u2))
        upd = _dot(bbt_ref[:, rows], u.astype(BF16))
        h_ref[...] = pcm_ref[rows, :] * h + g_ref[rows, :] + jnp.where(same_head, upd, 0.0)


def _rwkv_scan(x, y, rt, opre, qb2, bbt, g, pcm, ts=512):
    s, w = y.shape
    npair = w // LANES
    ts = min(ts, s)
    blk = pl.BlockSpec((ts, LANES), lambda j, i: (i, j))
    return pl.pallas_call(
        _rwkv_scan_kernel,
        grid=(npair, s // ts),
        in_specs=[blk, blk, blk, blk,
                  pl.BlockSpec((ts, 2 * CHUNK), lambda j, i: (i, j)),
                  pl.BlockSpec((LANES, ts), lambda j, i: (j, i)),
                  blk, blk],
        out_specs=blk,
        out_shape=jax.ShapeDtypeStruct((s, w), F32),
        scratch_shapes=[pltpu.VMEM((LANES, LANES), F32)],
        compiler_params=_params(("parallel", "arbitrary")),
        name="rwkv_scan",
    )(x, y, rt, opre, qb2, bbt, g, pcm)


def _rwkv_post_kernel(o_ref, bonus_ref, g_ref, lng_ref, lnb_ref, y_ref):
    same_head = jnp.where(_head_pair_ones(), 1.0, 0.0).astype(BF16)
    o = o_ref[...]
    mu = _dot_r3(o, same_head) * (1.0 / RWKV_HEAD_DIM)
    d = o - mu
    var = _dot_r3(d * d, same_head) * (1.0 / RWKV_HEAD_DIM)
    y = d * lax.rsqrt(var + LN_X_EPS) * lng_ref[...] + lnb_ref[...]
    y_ref[...] = ((y + bonus_ref[...]) * g_ref[...]).astype(y_ref.dtype)


def _rwkv_post(o, bonus, g, lnx_g, lnx_b, tm=512):
    s, w = o.shape
    tm = min(tm, s)
    blk = pl.BlockSpec((tm, LANES), lambda i, j: (i, j))
    vec = pl.BlockSpec((1, LANES), lambda i, j: (0, j))
    return pl.pallas_call(
        _rwkv_post_kernel,
        grid=(s // tm, w // LANES),
        in_specs=[blk, blk, blk, vec, vec],
        out_specs=blk,
        out_shape=jax.ShapeDtypeStruct((s, w), BF16),
        compiler_params=_params(("parallel", "parallel")),
        name="rwkv_post",
    )(o, bonus, g, lnx_g.reshape(1, w), lnx_b.reshape(1, w))


V_ONES_ROWS = 16


LOG2_E = 1.4426950408889634
MOBA_LOG2_SCALE = (MOBA_HEAD_DIM ** -0.5) * LOG2_E


def _moba_prep_kernel(q_ref, k_ref, v_ref, cos_ref, sin_ref, qo_ref, qs_ref, ko_ref, vt_ref, km_ref):
    cos = cos_ref[...]
    sin = sin_ref[...]
    hd = MOBA_HEAD_DIM
    half = hd // 2
    tb = q_ref.shape[0]
    for h in range(q_ref.shape[1] // hd):
        cols = slice(h * hd, (h + 1) * hd)
        q = q_ref[:, cols]
        k = k_ref[:, cols]
        qr = q * cos + pltpu.roll(q, half, axis=1) * sin
        kr = k * cos + pltpu.roll(k, half, axis=1) * sin
        qo_ref[:, cols] = qr.astype(qo_ref.dtype)
        qs_ref[:, cols] = (qr * MOBA_LOG2_SCALE).astype(qs_ref.dtype)
        ko_ref[:, cols] = kr.astype(ko_ref.dtype)
        km_ref[0, :, cols] = jnp.mean(kr, axis=0, keepdims=True)
        vt_ref[h, 0, :hd, :] = v_ref[:, cols].T.astype(vt_ref.dtype)
        vt_ref[h, 0, hd:, :] = jnp.ones((V_ONES_ROWS, tb), vt_ref.dtype)


def _moba_prep(proj_m, cos, sin):
    s = proj_m.shape[0]
    w = proj_m.shape[1] // 3
    tb = MOBA_BLOCK
    nb = s // tb
    nh = w // MOBA_HEAD_DIM
    vrows = MOBA_HEAD_DIM + V_ONES_ROWS
    blk = lambda off: pl.BlockSpec((tb, w), lambda i, off=off: (i, off))
    tab = pl.BlockSpec((tb, MOBA_HEAD_DIM), lambda i: (i, 0))
    bo = jax.ShapeDtypeStruct((s, w), BF16)
    return pl.pallas_call(
        _moba_prep_kernel,
        grid=(nb,),
        in_specs=[blk(0), blk(1), blk(2), tab, tab],
        out_specs=[blk(0), blk(0), blk(0),
                   pl.BlockSpec((nh, 1, vrows, tb), lambda i: (0, i, 0, 0)),
                   pl.BlockSpec((1, 1, w), lambda i: (i, 0, 0))],
        out_shape=[bo, bo, bo, jax.ShapeDtypeStruct((nh, nb, vrows, tb), BF16),
                   jax.ShapeDtypeStruct((nb, 1, w), F32)],
        compiler_params=_params(("parallel",)),
        name="moba_prep",
    )(proj_m, proj_m, proj_m, cos, sin)


MOBA_CHAINS = 4


def _moba_attn_kernel(q_ref, qs_ref, k_ref, vt_ref, km_ref, o_ref, bias_ref):
    i = pl.program_id(1)
    tb = MOBA_BLOCK
    nb = km_ref.shape[0]
    hd = MOBA_HEAD_DIM
    q = qs_ref[...]

    gate = _dot_nt(km_ref[...].astype(BF16), q_ref[...])
    rid = lax.broadcasted_iota(jnp.int32, gate.shape, 0).astype(F32)
    gate = jnp.where(rid < i.astype(F32), gate, -jnp.inf)
    bias = jnp.full(gate.shape, -jnp.inf, F32)
    for _ in range(MOBA_TOPK):
        m = jnp.max(gate, axis=0, keepdims=True)
        first = jnp.min(jnp.where(gate == m, rid, float(nb)), axis=0, keepdims=True)
        pick = (rid == first) & (m > -jnp.inf)
        bias = jnp.where(pick, 0.0, bias)
        gate = jnp.where(pick, -jnp.inf, gate)
    bias_ref[...] = bias

    own = pl.multiple_of(i * tb, tb)
    s = _dot_nt(k_ref[pl.ds(own, tb), :], q)
    ki = lax.broadcasted_iota(jnp.int32, s.shape, 0)
    qi = lax.broadcasted_iota(jnp.int32, s.shape, 1)
    s = jnp.where(ki <= qi, s, -jnp.inf).astype(BF16)
    m0 = jnp.max(s, axis=0, keepdims=True)
    acc0 = _dot(vt_ref[i], jnp.exp2(s - m0))
    m0 = m0.astype(F32)

    def scores(t):
        js = [jnp.minimum(t * MOBA_CHAINS + g, nb - 1) for g in range(MOBA_CHAINS)]
        return [_dot_nt(k_ref[pl.ds(pl.multiple_of(j * tb, tb), tb), :], q).astype(BF16)
                + bias_ref[pl.ds(j, 1), :].astype(BF16) for j in js]

    def body(t, carry):
        chains, ss = carry
        ss_next = scores(t + 1)
        js = [t * MOBA_CHAINS + g for g in range(MOBA_CHAINS)]
        ms = [jnp.maximum(m, jnp.max(sj, axis=0, keepdims=True).astype(F32))
              for (m, _), sj in zip(chains, ss)]
        ps = [jnp.exp2(sj - mn.astype(BF16)) for sj, mn in zip(ss, ms)]
        chains = tuple((mn, jnp.exp2(m - mn) * acc + _dot(vt_ref[j], pj))
                       for (m, acc), mn, pj, j in zip(chains, ms, ps, js))
        return chains, ss_next

    init = ((m0, acc0),) + ((m0, jnp.zeros_like(acc0)),) * (MOBA_CHAINS - 1)
    chains, _ = lax.fori_loop(0, (i + MOBA_CHAINS - 1) // MOBA_CHAINS, body, (init, scores(0)))
    m = chains[0][0]
    for mg, _ in chains[1:]:
        m = jnp.maximum(m, mg)
    acc = sum(jnp.exp2(mg - m) * ag for mg, ag in chains)
    out = acc[:hd, :] / acc[hd:hd + 1, :]
    o_ref[...] = out.T.astype(o_ref.dtype)


def _moba_attn(q, qs, k, vt, kmean):
    s, w = q.shape
    tb = MOBA_BLOCK
    nb = s // tb
    hd = MOBA_HEAD_DIM
    vrows = vt.shape[2]
    return pl.pallas_call(
        _moba_attn_kernel,
        grid=(w // hd, nb),
        in_specs=[pl.BlockSpec((tb, hd), lambda h, i: (i, h)),
                  pl.BlockSpec((tb, hd), lambda h, i: (i, h)),
                  pl.BlockSpec((s, hd), lambda h, i: (0, h)),
                  pl.BlockSpec((None, nb, vrows, tb), lambda h, i: (h, 0, 0, 0)),
                  pl.BlockSpec((nb, hd), lambda h, i: (0, h))],
        out_specs=pl.BlockSpec((tb, hd), lambda h, i: (i, h)),
        out_shape=jax.ShapeDtypeStruct((s, w), BF16),
        scratch_shapes=[pltpu.VMEM((nb, tb), F32)],
        compiler_params=_params(("parallel", "arbitrary")),
        name="moba_attn",
    )(q, qs, k, vt, kmean)


def _top_rows(x, n):
    rows = x.shape[0]
    rid = lax.broadcasted_iota(jnp.int32, x.shape, 0).astype(F32)
    vals, idxs = [], []
    for _ in range(n):
        m = jnp.max(x, axis=0, keepdims=True)
        first = jnp.min(jnp.where(x == m, rid, float(rows)), axis=0, keepdims=True)
        vals.append(m)
        idxs.append(first)
        x = jnp.where(rid == first, -jnp.inf, x)
    return jnp.concatenate(vals, axis=0), jnp.concatenate(idxs, axis=0)


def _peer_route_kernel(q_ref, keys_ref, e1_ref, e2_ref, g_ref):
    n = PEER_TOPK
    tm = q_ref.shape[0]
    e1s, e2s, gs = [], [], []
    for h in range(PEER_HEADS):
        tops = []
        for p in range(2):
            hp = 2 * h + p
            cols = slice(hp * PEER_HALF, (hp + 1) * PEER_HALF)
            rows = slice(hp * PEER_N_KEYS, (hp + 1) * PEER_N_KEYS)
            st = _dot_nt(keys_ref[rows, :], q_ref[:, cols].astype(BF16))
            tops.append(_top_rows(st, n))
        (s1, i1), (s2, i2) = tops
        hn = n // 2
        cand = jnp.concatenate([s1[0:1, :] + s2]
                               + [s1[a:a + 1, :] + s2[:hn, :] for a in range(1, hn)]
                               + [s1[hn:, :] + s2[0:1, :]], axis=0)
        f_s, f_pos = _top_rows(cand, n)
        mid = jnp.floor((f_pos - n) * (1.0 / hn))
        tail0 = float(n + (hn - 1) * hn)
        pa = jnp.where(f_pos < n, 0.0, jnp.where(f_pos < tail0, 1.0 + mid, f_pos - tail0 + hn))
        pb = jnp.where(f_pos < n, f_pos, jnp.where(f_pos < tail0, f_pos - n - mid * hn, 0.0))
        e1 = jnp.zeros((n, tm), F32)
        e2 = jnp.zeros((n, tm), F32)
        for a in range(n):
            e1 = jnp.where(pa == float(a), i1[a:a + 1, :], e1)
            e2 = jnp.where(pb == float(a), i2[a:a + 1, :], e2)
        ex = jnp.exp(f_s - f_s[0:1, :])
        gs.append(ex / jnp.sum(ex, axis=0, keepdims=True))
        e1s.append(e1)
        e2s.append(e2)
    e1_ref[...] = jnp.concatenate(e1s, axis=0).T
    e2_ref[...] = jnp.concatenate(e2s, axis=0).T
    g_ref[...] = jnp.concatenate(gs, axis=0).T


def _peer_route(q, keys, tm=256):
    s, w = q.shape
    nsel = PEER_HEADS * PEER_TOPK
    tm = min(tm, s)
    out = jax.ShapeDtypeStruct((s, nsel), F32)
    ob = pl.BlockSpec((tm, nsel), lambda i: (i, 0))
    return pl.pallas_call(
        _peer_route_kernel,
        grid=(s // tm,),
        in_specs=[pl.BlockSpec((tm, w), lambda i: (i, 0)),
                  pl.BlockSpec(keys.shape, lambda i: (0, 0))],
        out_specs=[ob, ob, ob],
        out_shape=[out, out, out],
        compiler_params=_params(("parallel",)),
        name="peer_route",
    )(q, keys)


EXPAND_UNROLL = 16
EXPAND_GROUP = 4


def _peer_expand_kernel(e1_ref, e2_ref, g_ref, w_ref, stage_ref):
    nk = PEER_N_KEYS
    nsel = e1_ref.shape[1]
    rid = lax.broadcasted_iota(jnp.int32, (nk, nsel), 0).astype(F32)
    un = EXPAND_UNROLL

    def body(tt, carry):
        base = pl.multiple_of(tt * un, un)
        e1 = e1_ref[pl.ds(base, un), :]
        e2 = e2_ref[pl.ds(base, un), :]
        g = g_ref[pl.ds(base, un), :]
        for u0 in range(0, un, EXPAND_GROUP):
            us = range(u0, u0 + EXPAND_GROUP)
            lefts = [jnp.where(rid == e1[u:u + 1, :], g[u:u + 1, :], 0.0).astype(BF16) for u in us]
            rights = [jnp.where(rid == e2[u:u + 1, :], 1.0, 0.0).astype(BF16) for u in us]
            for u, l, r in zip(us, lefts, rights):
                stage_ref[u * nk:(u + 1) * nk, :] = _dot_nt(l, r)
        for a in range(nk):
            rows = stage_ref[pl.ds(a, un, stride=nk), :]
            w_ref[pl.ds(base, un), a * nk:(a + 1) * nk] = rows.astype(w_ref.dtype)
        return carry

    lax.fori_loop(0, e1_ref.shape[0] // un, body, 0)


def _peer_expand(e1, e2, g, tm=128):
    s, nsel = e1.shape
    nk = PEER_N_KEYS
    tm = min(tm, s)
    ib = pl.BlockSpec((tm, nsel), lambda i: (i, 0))
    return pl.pallas_call(
        _peer_expand_kernel,
        grid=(s // tm,),
        in_specs=[ib, ib, ib],
        out_specs=pl.BlockSpec((tm, nk * nk), lambda i: (i, 0)),
        out_shape=jax.ShapeDtypeStruct((s, nk * nk), BF16),
        scratch_shapes=[pltpu.VMEM((EXPAND_UNROLL * nk, nk), F32)],
        compiler_params=_params(("parallel",)),
        name="peer_expand",
    )(e1, e2, g)


FP8 = jnp.float8_e4m3fn
FP8_MAX = 448.0
SCALE_ROWS = 8


def _rmsnorm_fp8_kernel(x_ref, g_ref, o_ref, o8_ref, s_ref):
    x = x_ref[...]
    ms = jnp.mean(x * x, axis=-1, keepdims=True)
    y = x * lax.rsqrt(ms + NORM_EPS) * g_ref[...]
    o_ref[...] = y.astype(o_ref.dtype)
    scale = jnp.maximum(jnp.max(jnp.abs(y), axis=-1, keepdims=True), 1e-30) * (1.0 / FP8_MAX)
    o8_ref[...] = (y / scale).astype(o8_ref.dtype)
    s_ref[...] = jnp.broadcast_to(scale, s_ref.shape)


def _rmsnorm_fp8(x, g, tm=256):
    s, d = x.shape
    blk = pl.BlockSpec((tm, d), lambda i: (i, 0))
    return pl.pallas_call(
        _rmsnorm_fp8_kernel,
        grid=(s // tm,),
        in_specs=[blk, pl.BlockSpec((1, d), lambda i: (0, 0))],
        out_specs=[blk, blk, pl.BlockSpec((tm, LANES), lambda i: (i, 0))],
        out_shape=[jax.ShapeDtypeStruct((s, d), BF16), jax.ShapeDtypeStruct((s, d), FP8),
                   jax.ShapeDtypeStruct((s, LANES), F32)],
        compiler_params=_params(("parallel",)),
        name="rmsnorm_fp8",
    )(x, g.reshape(1, d))


def _quant_rows_kernel(u_ref, u8_ref, s_ref):
    u = u_ref[...]
    scale = jnp.maximum(jnp.max(jnp.abs(u), axis=-1, keepdims=True), 1e-30) * (1.0 / FP8_MAX)
    u8_ref[...] = (u / scale).astype(u8_ref.dtype)
    s_ref[...] = jnp.broadcast_to(scale, (u.shape[0], LANES)).T[:SCALE_ROWS, :]


def _quant_rows(u, te=512):
    ne, d = u.shape
    return pl.pallas_call(
        _quant_rows_kernel,
        grid=(ne // te,),
        in_specs=[pl.BlockSpec((te, d), lambda e: (e, 0))],
        out_specs=[pl.BlockSpec((te, d), lambda e: (e, 0)),
                   pl.BlockSpec((SCALE_ROWS, te), lambda e: (0, e))],
        out_shape=[jax.ShapeDtypeStruct((ne, d), FP8), jax.ShapeDtypeStruct((SCALE_ROWS, ne), F32)],
        compiler_params=_params(("parallel",)),
        name="peer_quant_u",
    )(u)


def _peer_dense_kernel(h_ref, hs_ref, u_ref, us_ref, v_ref, vs_ref, w_ref, o_ref):
    @pl.when(pl.program_id(1) == 0)
    def _():
        o_ref[...] = jnp.zeros_like(o_ref)

    act = _dot_nt(h_ref[...], u_ref[...]) * hs_ref[:, 0:1] * us_ref[0:1, :]
    gelu = 0.5 * act * (1.0 + lax.erf(act * (2.0 ** -0.5)))
    mix = w_ref[...].astype(F32) * gelu * vs_ref[0:1, :]
    scale = jnp.maximum(jnp.max(jnp.abs(mix), axis=-1, keepdims=True), 1e-30) * (1.0 / FP8_MAX)
    o_ref[...] += _dot((mix / scale).astype(FP8), v_ref[...]) * scale


def _peer_dense(h8, hs, u8, us, v8, vs, wdense, tm=512, te=1024):
    s, d = h8.shape
    ne = u8.shape[0]
    tm = min(tm, s)
    return pl.pallas_call(
        _peer_dense_kernel,
        grid=(s // tm, ne // te),
        in_specs=[pl.BlockSpec((tm, d), lambda i, e: (i, 0)),
                  pl.BlockSpec((tm, LANES), lambda i, e: (i, 0)),
                  pl.BlockSpec((te, d), lambda i, e: (e, 0)),
                  pl.BlockSpec((SCALE_ROWS, te), lambda i, e: (0, e)),
                  pl.BlockSpec((te, d), lambda i, e: (e, 0)),
                  pl.BlockSpec((SCALE_ROWS, te), lambda i, e: (0, e)),
                  pl.BlockSpec((tm, te), lambda i, e: (i, e))],
        out_specs=pl.BlockSpec((tm, d), lambda i, e: (i, 0)),
        out_shape=jax.ShapeDtypeStruct((s, d), F32),
        compiler_params=_params(("parallel", "arbitrary")),
        name="peer_dense",
    )(h8, hs, u8, us, v8, vs, wdense)


def _add_rmsnorm_kernel(x_ref, y_ref, g_ref, o_ref):
    x = x_ref[...] + y_ref[...]
    ms = jnp.mean(x * x, axis=-1, keepdims=True)
    o_ref[...] = x * lax.rsqrt(ms + NORM_EPS) * g_ref[...]


def _add_rmsnorm(x, y, g, tm=256):
    s, d = x.shape
    blk = pl.BlockSpec((tm, d), lambda i: (i, 0))
    return pl.pallas_call(
        _add_rmsnorm_kernel,
        grid=(s // tm,),
        in_specs=[blk, blk, pl.BlockSpec((1, d), lambda i: (0, 0))],
        out_specs=blk,
        out_shape=jax.ShapeDtypeStruct((s, d), F32),
        compiler_params=_params(("parallel",)),
        name="residual_final_norm",
    )(x, y, g.reshape(1, d))


def _rwkv_branch(proj_r, w0, w_up, a0, a_up, g_up, k_k, k_a, r_k, lnx_g, lnx_b):
    g_up_pad = jnp.pad(g_up, ((0, GATE_LORA_PAD - GATE_LORA), (0, 0))).astype(BF16)
    wpre, a, g = _rwkv_lora(proj_r, w0, w_up.astype(BF16), a0, a_up.astype(BF16), g_up_pad)
    rt, at, kh, bh, kb, bb, pc, bonus = _rwkv_prep(proj_r, wpre, a, k_k, k_a, r_k.reshape(-1))
    x, y, opre, qb2, bbt, gst, pcm = _rwkv_chunk(rt, at, kh, bh, kb, bb, pc, proj_r)
    o = _rwkv_scan(x, y, rt, opre, qb2, bbt, gst, pcm)
    return _rwkv_post(o, bonus, g, lnx_g, lnx_b)


def _rope_tables(s):
    half = MOBA_HEAD_DIM // 2
    inv_freq = ROPE_THETA ** (-jnp.arange(half, dtype=F32) / half)
    ang = jnp.arange(s, dtype=jnp.int32).astype(F32)[:, None] * inv_freq[None, :]
    cos = jnp.cos(ang)
    sin = jnp.sin(ang)
    return jnp.concatenate([cos, cos], axis=-1), jnp.concatenate([-sin, sin], axis=-1)


def _moba_branch(proj_m):
    s = proj_m.shape[0]
    cos, sin = _rope_tables(s)
    q, qs, k, vt, kmean = _moba_prep(proj_m, cos, sin)
    return _moba_attn(q, qs, k, vt, kmean.reshape(kmean.shape[0], kmean.shape[2]))


def _peer_layer(x1, norm2_g, w_q, sub_keys, expert_u, expert_v, final_g):
    s, d = x1.shape
    h2, h8, hs = _rmsnorm_fp8(x1, norm2_g)
    q = _matmul(h2, w_q.astype(BF16), F32, tm=min(1024, s), tn=512, name="peer_query")
    keys = sub_keys.reshape(PEER_HEADS * 2 * PEER_N_KEYS, PEER_HALF).astype(BF16)
    e1, e2, g = _peer_route(q, keys)
    wdense = _peer_expand(e1, e2, g)
    u8, us = _quant_rows(expert_u)
    v8, vs = _quant_rows(expert_v)
    peer = _peer_dense(h8, hs, u8, us, v8, vs, wdense)
    return _add_rmsnorm(x1, peer, final_g)


def kernel(x, norm1_g, w_in, rwkv_mu, rwkv_w0, rwkv_w_up, rwkv_a0, rwkv_a_up, rwkv_g_up, rwkv_k_k, rwkv_k_a, rwkv_r_k, rwkv_lnx_g, rwkv_lnx_b, w_branch_rwkv, w_branch_moba, w_out, norm2_g, peer_w_q, peer_sub_keys, peer_u, peer_v, final_g):
    b, s, d = x.shape
    depth = w_in.shape[0]
    assert b == 1 and depth == 1
    x2d = x.reshape(s, d)
    l = 0
    rw = rwkv_w0.shape[-1]
    shift_w = 3 * rw + DECAY_LORA + AAA_LORA + GATE_LORA
    tn = 512
    shift_pad = -(-shift_w // tn) * tn
    mw = w_branch_moba.shape[1]
    tm = min(1024, s)

    h = _rmsnorm(x2d, norm1_g[l], BF16)
    mu = jnp.pad(rwkv_mu[l], (0, shift_pad - shift_w)).reshape(1, shift_pad)
    proj_r = _matmul_shift(h, w_in[l], mu, tm=tm, tn=tn)
    w_mg = w_in[l][:, shift_w:].astype(BF16)
    proj_m = _matmul(h, w_mg, F32, tm=tm, tn=tn, name="in_proj_moba", col0=0, n=3 * mw)
    gates = _matmul(h, w_mg, F32, tm=tm, tn=tn, name="in_proj_gates", col0=3 * mw, n=2 * d)

    y_a = _rwkv_branch(proj_r, rwkv_w0[l], rwkv_w_up[l], rwkv_a0[l], rwkv_a_up[l], rwkv_g_up[l],
                       rwkv_k_k[l], rwkv_k_a[l], rwkv_r_k[l], rwkv_lnx_g[l], rwkv_lnx_b[l])
    y_b = _moba_branch(proj_m)
    mixed = _gated_pair(y_a, w_branch_rwkv[l].astype(BF16), y_b, w_branch_moba[l].astype(BF16),
                        gates, tm=tm, tn=512)
    x1 = _matmul_residual(mixed, w_out[l].astype(BF16), x2d, tm=tm, tn=512)
    out = _peer_layer(x1, norm2_g[l], peer_w_q[l], peer_sub_keys[l], peer_u[l], peer_v[l], final_g)
    return out.reshape(b, s, d)
```

```python
import functools

import jax
import jax.numpy as jnp
from jax import lax
from jax.experimental import pallas as pl
from jax.experimental.pallas import tpu as pltpu

F32 = jnp.float32
BF16 = jnp.bfloat16

NORM_EPS = 1e-6
LANES = 128
RWKV_HEAD_DIM = 64
DECAY_LORA = 128
AAA_LORA = 128
GATE_LORA = 480
GATE_LORA_PAD = 512
LN_X_EPS = 64e-5
CHUNK = 128
MOBA_HEAD_DIM = 128
MOBA_BLOCK = 256
MOBA_TOPK = 3
ROPE_THETA = 10000.0
PEER_HEADS = 8
PEER_N_KEYS = 128
PEER_HALF = 128
PEER_TOPK = 16

VMEM_LIMIT = 56 * 1024 * 1024

_NT = (((1,), (1,)), ((), ()))


def _params(sem, vmem=VMEM_LIMIT):
    return pltpu.CompilerParams(dimension_semantics=sem, vmem_limit_bytes=vmem)


def _dot(a, b):
    return jnp.dot(a, b, preferred_element_type=F32)


def _dot_nt(a, b):
    return lax.dot_general(a, b, _NT, preferred_element_type=F32)


def _split3(x):
    hi = x.astype(BF16)
    r1 = x - hi.astype(F32)
    mid = r1.astype(BF16)
    lo = (r1 - mid.astype(F32)).astype(BF16)
    return hi, mid, lo


def _dot_l3(l_bf16, x):
    hi, mid, lo = _split3(x)
    return _dot(l_bf16, hi) + _dot(l_bf16, mid) + _dot(l_bf16, lo)


def _dot_r3(x, r_bf16):
    hi, mid, lo = _split3(x)
    return _dot(hi, r_bf16) + _dot(mid, r_bf16) + _dot(lo, r_bf16)


def _rmsnorm_kernel(x_ref, g_ref, o_ref):
    x = x_ref[...]
    ms = jnp.mean(x * x, axis=-1, keepdims=True)
    o_ref[...] = (x * lax.rsqrt(ms + NORM_EPS) * g_ref[...]).astype(o_ref.dtype)


def _rmsnorm(x, g, out_dtype, tm=256):
    s, d = x.shape
    return pl.pallas_call(
        _rmsnorm_kernel,
        grid=(s // tm,),
        in_specs=[pl.BlockSpec((tm, d), lambda i: (i, 0)),
                  pl.BlockSpec((1, d), lambda i: (0, 0))],
        out_specs=pl.BlockSpec((tm, d), lambda i: (i, 0)),
        out_shape=jax.ShapeDtypeStruct((s, d), out_dtype),
        compiler_params=_params(("parallel",)),
        name="rmsnorm",
    )(x, g.reshape(1, d))


def _mm_kernel(a_ref, b_ref, o_ref):
    o_ref[...] = _dot(a_ref[...], b_ref[...]).astype(o_ref.dtype)


def _matmul(a, b, out_dtype, tm, tn, name, col0=0, n=None):
    m, k = a.shape
    n = b.shape[1] if n is None else n
    off = col0 // tn
    return pl.pallas_call(
        _mm_kernel,
        grid=(m // tm, n // tn),
        in_specs=[pl.BlockSpec((tm, k), lambda i, j: (i, 0)),
                  pl.BlockSpec((k, tn), lambda i, j: (0, j + off))],
        out_specs=pl.BlockSpec((tm, tn), lambda i, j: (i, j)),
        out_shape=jax.ShapeDtypeStruct((m, n), out_dtype),
        compiler_params=_params(("parallel", "parallel")),
        name=name,
    )(a, b)


BF16_SUBLANES = 16


def _mm_shift_kernel(a_ref, ap_ref, b_ref, mu_ref, o_ref):
    i = pl.program_id(0)
    b = b_ref[...].astype(BF16)
    z = _dot(a_ref[...], b)
    zp = _dot(ap_ref[...], b)
    prev = jnp.where(i > 0, zp[BF16_SUBLANES - 1:BF16_SUBLANES, :], 0.0)
    row = lax.broadcasted_iota(jnp.int32, z.shape, 0)
    z_prev = jnp.where(row == 0, prev, pltpu.roll(z, 1, axis=0))
    o_ref[...] = z + (z_prev - z) * mu_ref[...]


def _matmul_shift(a, b, mu, tm, tn):
    m, k = a.shape
    n = mu.shape[1]
    per = tm // BF16_SUBLANES
    return pl.pallas_call(
        _mm_shift_kernel,
        grid=(m // tm, n // tn),
        in_specs=[pl.BlockSpec((tm, k), lambda i, j: (i, 0)),
                  pl.BlockSpec((BF16_SUBLANES, k), lambda i, j: (jnp.maximum(i * per - 1, 0), 0)),
                  pl.BlockSpec((k, tn), lambda i, j: (0, j)),
                  pl.BlockSpec((1, tn), lambda i, j: (0, j))],
        out_specs=pl.BlockSpec((tm, tn), lambda i, j: (i, j)),
        out_shape=jax.ShapeDtypeStruct((m, n), F32),
        compiler_params=_params(("parallel", "parallel")),
        name="in_proj_rwkv",
    )(a, a, b, mu)


def _repack_kernel(a_ref, b_ref, o_ref, *, off):
    o_ref[...] = jnp.concatenate([a_ref[:, off:], b_ref[:, :off]], axis=1).astype(o_ref.dtype)


def _repack_columns(w, col0, n, tk=512, tn=512):
    k = w.shape[0]
    first, off = divmod(col0, tn)
    return pl.pallas_call(
        functools.partial(_repack_kernel, off=off),
        grid=(k // tk, n // tn),
        in_specs=[pl.BlockSpec((tk, tn), lambda i, j: (i, j + first)),
                  pl.BlockSpec((tk, tn), lambda i, j: (i, j + first + 1))],
        out_specs=pl.BlockSpec((tk, tn), lambda i, j: (i, j)),
        out_shape=jax.ShapeDtypeStruct((k, n), BF16),
        compiler_params=_params(("parallel", "parallel")),
        name="repack_w_in",
    )(w, w)


def _gated_pair_kernel(ya_ref, wa_ref, yb_ref, wb_ref, ga_ref, gb_ref, o_ref):
    pa = _dot(ya_ref[...], wa_ref[...])
    pb = _dot(yb_ref[...], wb_ref[...])
    mixed = jax.nn.sigmoid(ga_ref[...]) * pa + jax.nn.sigmoid(gb_ref[...]) * pb
    o_ref[...] = mixed.astype(o_ref.dtype)


def _gated_pair(ya, wa, yb, wb, gates, tm, tn):
    m, k = ya.shape
    n = wa.shape[1]
    nb = n // tn
    return pl.pallas_call(
        _gated_pair_kernel,
        grid=(m // tm, nb),
        in_specs=[pl.BlockSpec((tm, k), lambda i, j: (i, 0)),
                  pl.BlockSpec((k, tn), lambda i, j: (0, j)),
                  pl.BlockSpec((tm, k), lambda i, j: (i, 0)),
                  pl.BlockSpec((k, tn), lambda i, j: (0, j)),
                  pl.BlockSpec((tm, tn), lambda i, j: (i, j)),
                  pl.BlockSpec((tm, tn), lambda i, j: (i, j + nb))],
        out_specs=pl.BlockSpec((tm, tn), lambda i, j: (i, j)),
        out_shape=jax.ShapeDtypeStruct((m, n), BF16),
        compiler_params=_params(("parallel", "parallel")),
        name="branch_merge",
    )(ya, wa, yb, wb, gates, gates)


def _mm_res_kernel(a_ref, b_ref, r_ref, o_ref):
    o_ref[...] = r_ref[...] + _dot(a_ref[...], b_ref[...])


def _matmul_residual(a, b, res, tm, tn):
    m, k = a.shape
    n = b.shape[1]
    return pl.pallas_call(
        _mm_res_kernel,
        grid=(m // tm, n // tn),
        in_specs=[pl.BlockSpec((tm, k), lambda i, j: (i, 0)),
                  pl.BlockSpec((k, tn), lambda i, j: (0, j)),
                  pl.BlockSpec((tm, tn), lambda i, j: (i, j))],
        out_specs=pl.BlockSpec((tm, tn), lambda i, j: (i, j)),
        out_shape=jax.ShapeDtypeStruct((m, n), F32),
        compiler_params=_params(("parallel", "parallel")),
        name="out_proj",
    )(a, b, res)


def _rwkv_lora_kernel(wl_ref, al_ref, gl0_ref, gl1_ref, w0_ref, wup_ref, a0_ref, aup_ref,
                      gup_ref, wpre_ref, a_ref, g_ref):
    half = GATE_LORA_PAD // 2
    wpre_ref[...] = w0_ref[...] + _dot(jnp.tanh(wl_ref[...]).astype(BF16), wup_ref[...])
    a_ref[...] = jax.nn.sigmoid(a0_ref[...] + _dot(al_ref[...].astype(BF16), aup_ref[...]))
    g_ref[...] = (_dot(jax.nn.sigmoid(gl0_ref[...]).astype(BF16), gup_ref[:half, :])
                  + _dot(jax.nn.sigmoid(gl1_ref[...]).astype(BF16), gup_ref[half:, :]))


def _rwkv_lora(proj_r, w0, w_up, a0, a_up, g_up_pad, tm=256):
    s = proj_r.shape[0]
    w = w0.shape[-1]
    half = GATE_LORA_PAD // 2
    c_wl = 3 * w // DECAY_LORA
    c_al = (3 * w + DECAY_LORA) // AAA_LORA
    c_gl = (3 * w + DECAY_LORA + AAA_LORA) // half
    row = lambda i: (i, 0)
    const = lambda i: (0, 0)
    out = jax.ShapeDtypeStruct((s, w), F32)
    return pl.pallas_call(
        _rwkv_lora_kernel,
        grid=(s // tm,),
        in_specs=[pl.BlockSpec((tm, DECAY_LORA), lambda i: (i, c_wl)),
                  pl.BlockSpec((tm, AAA_LORA), lambda i: (i, c_al)),
                  pl.BlockSpec((tm, half), lambda i: (i, c_gl)),
                  pl.BlockSpec((tm, half), lambda i: (i, c_gl + 1)),
                  pl.BlockSpec((1, w), const),
                  pl.BlockSpec((DECAY_LORA, w), const),
                  pl.BlockSpec((1, w), const),
                  pl.BlockSpec((AAA_LORA, w), const),
                  pl.BlockSpec((GATE_LORA_PAD, w), const)],
        out_specs=[pl.BlockSpec((tm, w), row)] * 3,
        out_shape=[out, out, out],
        compiler_params=_params(("parallel",)),
        name="rwkv_lora",
    )(proj_r, proj_r, proj_r, proj_r, w0.reshape(1, w), w_up, a0.reshape(1, w), a_up, g_up_pad)


def _head_pair_ones(width=LANES):
    r = lax.broadcasted_iota(jnp.int32, (width, width), 0) // RWKV_HEAD_DIM
    c = lax.broadcasted_iota(jnp.int32, (width, width), 1) // RWKV_HEAD_DIM
    return r == c


def _rwkv_prep_kernel(r_ref, k_ref, v_ref, wpre_ref, a_ref, kk_ref, ka_ref, rk_ref,
                      rt_ref, at_ref, kh_ref, bh_ref, kb_ref, bb_ref, pc_ref, bonus_ref):
    tm = r_ref.shape[0]
    r = r_ref[...]
    k = k_ref[...]
    v = v_ref[...]
    a = a_ref[...]
    same_head = jnp.where(_head_pair_ones(r.shape[1]), 1.0, 0.0).astype(BF16)

    x = -wpre_ref[...]
    softplus = jnp.maximum(x, 0.0) + jnp.log1p(jnp.exp(-jnp.abs(x)))
    lw = -jnp.exp(-softplus - 0.5)

    kk = k * kk_ref[...]
    ss = _dot_r3(kk * kk, same_head)
    kkn = kk / jnp.maximum(jnp.sqrt(ss), 1e-12)
    k2 = k * (1.0 + (a - 1.0) * ka_ref[...])
    bonus_ref[...] = _dot_r3(r * k2 * rk_ref[...], same_head) * v

    ri = lax.broadcasted_iota(jnp.int32, (tm, tm), 0)
    ci = lax.broadcasted_iota(jnp.int32, (tm, tm), 1)
    same_chunk = (ri // CHUNK) == (ci // CHUNK)
    tri = jnp.where(same_chunk & (ci <= ri), 1.0, 0.0).astype(BF16)
    allc = jnp.where(same_chunk, 1.0, 0.0).astype(BF16)
    cum = _dot_l3(tri, lw)
    tot = _dot_l3(allc, lw)

    p_in = jnp.exp(cum)
    p_inv = jnp.exp(-cum)
    p_tail = jnp.exp(tot - cum)
    b = kkn * a
    rt_ref[...] = r * p_in
    at_ref[...] = -kkn * jnp.exp(cum - lw)
    kh_ref[...] = k2 * p_inv
    bh_ref[...] = b * p_inv
    kb_ref[...] = k2 * p_tail
    bb_ref[...] = b * p_tail
    pc_ref[...] = jnp.exp(tot)


def _rwkv_prep(proj_r, wpre, a, k_k, k_a, r_k, tm=256, tw=512):
    s, w = wpre.shape
    ncol = w // tw
    blk = lambda off: pl.BlockSpec((tm, tw), lambda i, j, off=off: (i, j + off))
    vec = pl.BlockSpec((1, tw), lambda i, j: (0, j))
    out = jax.ShapeDtypeStruct((s, w), F32)
    return pl.pallas_call(
        _rwkv_prep_kernel,
        grid=(s // tm, ncol),
        in_specs=[blk(0), blk(ncol), blk(2 * ncol), blk(0), blk(0), vec, vec, vec],
        out_specs=[blk(0)] * 8,
        out_shape=[out] * 8,
        compiler_params=_params(("parallel", "parallel")),
        name="rwkv_prep",
    )(proj_r, proj_r, proj_r, wpre, a, k_k.reshape(1, w), k_a.reshape(1, w), r_k.reshape(1, w))


def _rwkv_chunk_kernel(rt_ref, at_ref, kh_ref, bh_ref, kb_ref, bb_ref, pc_ref, v_ref,
                       x_ref, y_ref, op_ref, qb_ref, bbt_ref, g_ref, pcm_ref):
    c = CHUNK
    nheads = LANES // RWKV_HEAD_DIM
    chunks = range(rt_ref.shape[0] // c)
    pairs = range(rt_ref.shape[1] // LANES)
    rows = [slice(q * c, (q + 1) * c) for q in chunks]
    lanes = [slice(p * LANES, (p + 1) * LANES) for p in pairs]
    tiles = [(q, p) for q in chunks for p in pairs]
    probs = [(n, h) for n in range(len(tiles)) for h in range(nheads)]
    lane = lax.broadcasted_iota(jnp.int32, (c, LANES), 1)
    ri = lax.broadcasted_iota(jnp.int32, (c, c), 0)
    ci = lax.broadcasted_iota(jnp.int32, (c, c), 1)
    strict = ci < ri
    incl = ci <= ri
    eye = jnp.where(ri == ci, 1.0, 0.0)
    head_mask = [(lane // RWKV_HEAD_DIM) == h for h in range(nheads)]
    tile = lambda ref, n: ref[rows[tiles[n][0]], lanes[tiles[n][1]]]

    kh = [tile(kh_ref, n).astype(BF16) for n in range(len(tiles))]
    bh = [tile(bh_ref, n).astype(BF16) for n in range(len(tiles))]
    a_h = [jnp.where(head_mask[h], tile(at_ref, n), 0.0).astype(BF16) for n, h in probs]
    r_h = [jnp.where(head_mask[h], tile(rt_ref, n), 0.0).astype(BF16) for n, h in probs]
    v_h = [jnp.where(head_mask[h], tile(v_ref, n), 0.0).astype(BF16) for n, h in probs]
    n_ab = [jnp.where(strict, _dot_nt(a, bh[n]), 0.0) for a, (n, _) in zip(a_h, probs)]
    n_ak = [jnp.where(strict, _dot_nt(a, kh[n]), 0.0).astype(BF16) for a, (n, _) in zip(a_h, probs)]
    q_k = [jnp.where(incl, _dot_nt(r, kh[n]), 0.0).astype(BF16) for r, (n, _) in zip(r_h, probs)]
    q_b = [jnp.where(incl, _dot_nt(r, bh[n]), 0.0) for r, (n, _) in zip(r_h, probs)]
    for qb, (n, h) in zip(q_b, probs):
        q, p = tiles[n]
        col = (p * nheads + h) * c
        qb_ref[rows[q], col:col + c] = qb.astype(qb_ref.dtype)

    t = [eye + n for n in n_ab]
    pw = n_ab
    for _ in range((c - 1).bit_length() - 1):
        pb = [x.astype(BF16) for x in pw]
        pw = [_dot(x, x) for x in pb]
        t = [ti + _dot(ti.astype(BF16), pi.astype(BF16)) for ti, pi in zip(t, pw)]
    tb = [ti.astype(BF16) for ti in t]
    xs = [_dot(ti, a) for ti, a in zip(tb, a_h)]
    nv = [_dot(n, v).astype(BF16) for n, v in zip(n_ak, v_h)]
    ys = [_dot(ti, z) for ti, z in zip(tb, nv)]
    os = [_dot(qk, v) for qk, v in zip(q_k, v_h)]
    same_head = _head_pair_ones()
    for n, (q, p) in enumerate(tiles):
        mine = [m for m, (nn, _) in enumerate(probs) if nn == n]
        x_ref[rows[q], lanes[p]] = sum(xs[m] for m in mine).astype(x_ref.dtype)
        y_ref[rows[q], lanes[p]] = sum(ys[m] for m in mine)
        op_ref[rows[q], lanes[p]] = sum(os[m] for m in mine)
        kbt = tile(kb_ref, n).T.astype(BF16)
        g_ref[rows[q], lanes[p]] = jnp.where(same_head, _dot(kbt, tile(v_ref, n).astype(BF16)), 0.0)
        bbt_ref[lanes[p], rows[q]] = tile(bb_ref, n).T.astype(bbt_ref.dtype)
        pcm_ref[rows[q], lanes[p]] = tile(pc_ref, n).T


def _rwkv_chunk(rt, at, kh, bh, kb, bb, pc, proj_r, tc=2 * CHUNK, tw=2 * LANES):
    s, w = rt.shape
    ncol = w // tw
    c = CHUNK
    blk = lambda off: pl.BlockSpec((tc, tw), lambda i, j, off=off: (i, j + off))
    f32o = jax.ShapeDtypeStruct((s, w), F32)
    return pl.pallas_call(
        _rwkv_chunk_kernel,
        grid=(s // tc, ncol),
        in_specs=[blk(0)] * 7 + [blk(2 * ncol)],
        out_specs=[blk(0), blk(0), blk(0),
                   pl.BlockSpec((tc, 2 * tw), lambda i, j: (i, j)),
                   pl.BlockSpec((tw, tc), lambda i, j: (j, i)),
                   blk(0), blk(0)],
        out_shape=[jax.ShapeDtypeStruct((s, w), BF16), f32o, f32o,
                   jax.ShapeDtypeStruct((s, 2 * w), BF16),
                   jax.ShapeDtypeStruct((w, s), BF16),
                   f32o, f32o],
        compiler_params=_params(("parallel", "parallel")),
        name="rwkv_chunk",
    )(rt, at, kh, bh, kb, bb, pc, proj_r)


def _rwkv_scan_kernel(x_ref, y_ref, rt_ref, op_ref, qb_ref, bbt_ref, g_ref, pcm_ref,
                      bonus_ref, gate_ref, lng_ref, lnb_ref, ya_ref, h_ref):
    c = CHUNK
    pairs = range(x_ref.shape[1] // LANES)
    lanes = [slice(p * LANES, (p + 1) * LANES) for p in pairs]

    @pl.when(pl.program_id(1) == 0)
    def _():
        h_ref[...] = jnp.zeros_like(h_ref)

    lane = lax.broadcasted_iota(jnp.int32, (c, LANES), 1)
    first = lane < RWKV_HEAD_DIM
    same_head = _head_pair_ones()
    ones_head = jnp.where(same_head, 1.0, 0.0).astype(BF16)
    inv_n = 1.0 / RWKV_HEAD_DIM
    for q in range(x_ref.shape[0] // c):
        rows = slice(q * c, (q + 1) * c)
        hs = [h_ref[p] for p in pairs]
        hb = [h.astype(BF16) for h in hs]
        us = [_dot(x_ref[rows, lanes[p]], hb[p]) + y_ref[rows, lanes[p]] for p in pairs]
        u2 = [jnp.concatenate([jnp.where(first, u, 0.0), jnp.where(first, 0.0, u)], axis=0).astype(BF16)
              for u in us]
        os = [_dot(rt_ref[rows, lanes[p]].astype(BF16), hb[p]) + op_ref[rows, lanes[p]]
              + _dot(qb_ref[rows, p * 2 * c:(p + 1) * 2 * c], u2[p]) for p in pairs]
        upd = [_dot(bbt_ref[lanes[p], rows], us[p].astype(BF16)) for p in pairs]
        for p in pairs:
            h_ref[p] = (pcm_ref[rows, lanes[p]] * hs[p] + g_ref[rows, lanes[p]]
                        + jnp.where(same_head, upd[p], 0.0))
        mus = [_dot_r3(o, ones_head) * inv_n for o in os]
        ds = [o - mu for o, mu in zip(os, mus)]
        vs = [_dot_r3(d * d, ones_head) * inv_n for d in ds]
        for p in pairs:
            y = ds[p] * lax.rsqrt(vs[p] + LN_X_EPS) * lng_ref[:, lanes[p]] + lnb_ref[:, lanes[p]]
            ya_ref[rows, lanes[p]] = ((y + bonus_ref[rows, lanes[p]])
                                      * gate_ref[rows, lanes[p]]).astype(ya_ref.dtype)


def _rwkv_scan(x, y, rt, opre, qb2, bbt, g, pcm, bonus, gate, lnx_g, lnx_b, ts=512, tw=2 * LANES):
    s, w = y.shape
    ts = min(ts, s)
    blk = pl.BlockSpec((ts, tw), lambda j, i: (i, j))
    vec = pl.BlockSpec((1, tw), lambda j, i: (0, j))
    return pl.pallas_call(
        _rwkv_scan_kernel,
        grid=(w // tw, s // ts),
        in_specs=[blk, blk, blk, blk,
                  pl.BlockSpec((ts, 2 * tw), lambda j, i: (i, j)),
                  pl.BlockSpec((tw, ts), lambda j, i: (j, i)),
                  blk, blk, blk, blk, vec, vec],
        out_specs=blk,
        out_shape=jax.ShapeDtypeStruct((s, w), BF16),
        scratch_shapes=[pltpu.VMEM((tw // LANES, LANES, LANES), F32)],
        compiler_params=_params(("parallel", "arbitrary")),
        name="rwkv_scan",
    )(x, y, rt, opre, qb2, bbt, g, pcm, bonus, gate, lnx_g.reshape(1, w), lnx_b.reshape(1, w))


V_ONES_ROWS = 16


LOG2_E = 1.4426950408889634
MOBA_LOG2_SCALE = (MOBA_HEAD_DIM ** -0.5) * LOG2_E


def _moba_prep_kernel(q_ref, k_ref, v_ref, cos_ref, sin_ref, qo_ref, qs_ref, ko_ref, vt_ref, km_ref):
    cos = cos_ref[...]
    sin = sin_ref[...]
    hd = MOBA_HEAD_DIM
    half = hd // 2
    tb = q_ref.shape[0]
    for h in range(q_ref.shape[1] // hd):
        cols = slice(h * hd, (h + 1) * hd)
        q = q_ref[:, cols]
        k = k_ref[:, cols]
        qr = q * cos + pltpu.roll(q, half, axis=1) * sin
        kr = k * cos + pltpu.roll(k, half, axis=1) * sin
        qo_ref[:, cols] = qr.astype(qo_ref.dtype)
        qs_ref[:, cols] = (qr * MOBA_LOG2_SCALE).astype(qs_ref.dtype)
        ko_ref[:, cols] = kr.astype(ko_ref.dtype)
        km_ref[0, :, cols] = jnp.mean(kr, axis=0, keepdims=True)
        vt_ref[h, 0, :hd, :] = v_ref[:, cols].T.astype(vt_ref.dtype)
        vt_ref[h, 0, hd:, :] = jnp.ones((V_ONES_ROWS, tb), vt_ref.dtype)


def _moba_prep(proj_m, cos, sin):
    s = proj_m.shape[0]
    w = proj_m.shape[1] // 3
    tb = MOBA_BLOCK
    nb = s // tb
    nh = w // MOBA_HEAD_DIM
    vrows = MOBA_HEAD_DIM + V_ONES_ROWS
    blk = lambda off: pl.BlockSpec((tb, w), lambda i, off=off: (i, off))
    tab = pl.BlockSpec((tb, MOBA_HEAD_DIM), lambda i: (i, 0))
    bo = jax.ShapeDtypeStruct((s, w), BF16)
    return pl.pallas_call(
        _moba_prep_kernel,
        grid=(nb,),
        in_specs=[blk(0), blk(1), blk(2), tab, tab],
        out_specs=[blk(0), blk(0), blk(0),
                   pl.BlockSpec((nh, 1, vrows, tb), lambda i: (0, i, 0, 0)),
                   pl.BlockSpec((1, 1, w), lambda i: (i, 0, 0))],
        out_shape=[bo, bo, bo, jax.ShapeDtypeStruct((nh, nb, vrows, tb), BF16),
                   jax.ShapeDtypeStruct((nb, 1, w), F32)],
        compiler_params=_params(("parallel",)),
        name="moba_prep",
    )(proj_m, proj_m, proj_m, cos, sin)


MOBA_CHAINS = 4


def _moba_attn_kernel(q_ref, qs_ref, k_ref, vt_ref, km_ref, o_ref, bias_ref):
    i = pl.program_id(1)
    tb = MOBA_BLOCK
    nb = km_ref.shape[0]
    hd = MOBA_HEAD_DIM
    q = qs_ref[...]

    own = pl.multiple_of(i * tb, tb)
    s = _dot_nt(k_ref[pl.ds(own, tb), :], q)
    raw0 = [_dot_nt(k_ref[g * tb:(g + 1) * tb, :], q) for g in range(MOBA_CHAINS)]

    gate = _dot_nt(km_ref[...].astype(BF16), q_ref[...])
    rid = lax.broadcasted_iota(jnp.int32, gate.shape, 0).astype(F32)
    gate = jnp.where(rid < i.astype(F32), gate, -jnp.inf)
    bias = jnp.full(gate.shape, -jnp.inf, F32)
    for _ in range(MOBA_TOPK):
        m = jnp.max(gate, axis=0, keepdims=True)
        first = jnp.min(jnp.where(gate == m, rid, float(nb)), axis=0, keepdims=True)
        pick = (rid == first) & (m > -jnp.inf)
        bias = jnp.where(pick, 0.0, bias)
        gate = jnp.where(pick, -jnp.inf, gate)
    bias_ref[...] = bias

    ss0 = [r.astype(BF16) + bias[g:g + 1, :].astype(BF16) for g, r in enumerate(raw0)]

    ki = lax.broadcasted_iota(jnp.int32, s.shape, 0)
    qi = lax.broadcasted_iota(jnp.int32, s.shape, 1)
    s = jnp.where(ki <= qi, s, -jnp.inf).astype(BF16)
    m0 = jnp.max(s, axis=0, keepdims=True)
    acc0 = _dot(vt_ref[i], jnp.exp2(s - m0))
    m0 = m0.astype(F32)

    def scores(t):
        js = [jnp.minimum(t * MOBA_CHAINS + g, nb - 1) for g in range(MOBA_CHAINS)]
        return [_dot_nt(k_ref[pl.ds(pl.multiple_of(j * tb, tb), tb), :], q).astype(BF16)
                + bias_ref[pl.ds(j, 1), :].astype(BF16) for j in js]

    def body(t, carry):
        chains, ss = carry
        ss_next = scores(t + 1)
        js = [t * MOBA_CHAINS + g for g in range(MOBA_CHAINS)]
        ms = [jnp.maximum(m, jnp.max(sj, axis=0, keepdims=True).astype(F32))
              for (m, _), sj in zip(chains, ss)]
        ps = [jnp.exp2(sj - mn.astype(BF16)) for sj, mn in zip(ss, ms)]
        chains = tuple((mn, jnp.exp2(m - mn) * acc + _dot(vt_ref[j], pj))
                       for (m, acc), mn, pj, j in zip(chains, ms, ps, js))
        return chains, ss_next

    init = ((m0, acc0),) + ((m0, jnp.zeros_like(acc0)),) * (MOBA_CHAINS - 1)
    chains, _ = lax.fori_loop(0, (i + MOBA_CHAINS - 1) // MOBA_CHAINS, body, (init, ss0))
    m = chains[0][0]
    for mg, _ in chains[1:]:
        m = jnp.maximum(m, mg)
    acc = sum(jnp.exp2(mg - m) * ag for mg, ag in chains)
    out = acc[:hd, :] / acc[hd:hd + 1, :]
    o_ref[...] = out.T.astype(o_ref.dtype)


def _moba_attn(q, qs, k, vt, kmean):
    s, w = q.shape
    tb = MOBA_BLOCK
    nb = s // tb
    hd = MOBA_HEAD_DIM
    vrows = vt.shape[2]
    return pl.pallas_call(
        _moba_attn_kernel,
        grid=(w // hd, nb),
        in_specs=[pl.BlockSpec((tb, hd), lambda h, i: (i, h)),
                  pl.BlockSpec((tb, hd), lambda h, i: (i, h)),
                  pl.BlockSpec((s, hd), lambda h, i: (0, h)),
                  pl.BlockSpec((None, nb, vrows, tb), lambda h, i: (h, 0, 0, 0)),
                  pl.BlockSpec((nb, hd), lambda h, i: (0, h))],
        out_specs=pl.BlockSpec((tb, hd), lambda h, i: (i, h)),
        out_shape=jax.ShapeDtypeStruct((s, w), BF16),
        scratch_shapes=[pltpu.VMEM((nb, tb), F32)],
        compiler_params=_params(("parallel", "arbitrary")),
        name="moba_attn",
    )(q, qs, k, vt, kmean)


def _top_rows(x, n):
    rows = x.shape[0]
    rid = lax.broadcasted_iota(jnp.int32, x.shape, 0).astype(F32)
    vals, idxs = [], []
    for _ in range(n):
        m = jnp.max(x, axis=0, keepdims=True)
        first = jnp.min(jnp.where(x == m, rid, float(rows)), axis=0, keepdims=True)
        vals.append(m)
        idxs.append(first)
        x = jnp.where(rid == first, -jnp.inf, x)
    return jnp.concatenate(vals, axis=0), jnp.concatenate(idxs, axis=0)


def _peer_route_kernel(q_ref, keys_ref, e1_ref, e2_ref, g_ref):
    n = PEER_TOPK
    tm = q_ref.shape[0]
    e1s, e2s, gs = [], [], []
    for h in range(PEER_HEADS):
        tops = []
        for p in range(2):
            hp = 2 * h + p
            cols = slice(hp * PEER_HALF, (hp + 1) * PEER_HALF)
            rows = slice(hp * PEER_N_KEYS, (hp + 1) * PEER_N_KEYS)
            st = _dot_nt(keys_ref[rows, :], q_ref[:, cols].astype(BF16))
            tops.append(_top_rows(st, n))
        (s1, i1), (s2, i2) = tops
        hn = n // 2
        cand = jnp.concatenate([s1[0:1, :] + s2]
                               + [s1[a:a + 1, :] + s2[:hn, :] for a in range(1, hn)]
                               + [s1[hn:, :] + s2[0:1, :]], axis=0)
        f_s, f_pos = _top_rows(cand, n)
        mid = jnp.floor((f_pos - n) * (1.0 / hn))
        tail0 = float(n + (hn - 1) * hn)
        pa = jnp.where(f_pos < n, 0.0, jnp.where(f_pos < tail0, 1.0 + mid, f_pos - tail0 + hn))
        pb = jnp.where(f_pos < n, f_pos, jnp.where(f_pos < tail0, f_pos - n - mid * hn, 0.0))
        e1 = jnp.zeros((n, tm), F32)
        e2 = jnp.zeros((n, tm), F32)
        for a in range(n):
            e1 = jnp.where(pa == float(a), i1[a:a + 1, :], e1)
            e2 = jnp.where(pb == float(a), i2[a:a + 1, :], e2)
        ex = jnp.exp(f_s - f_s[0:1, :])
        gs.append(ex / jnp.sum(ex, axis=0, keepdims=True))
        e1s.append(e1)
        e2s.append(e2)
    e1_ref[...] = jnp.concatenate(e1s, axis=0).T
    e2_ref[...] = jnp.concatenate(e2s, axis=0).T
    g_ref[...] = jnp.concatenate(gs, axis=0).T


def _peer_route(q, keys, tm=256):
    s, w = q.shape
    nsel = PEER_HEADS * PEER_TOPK
    tm = min(tm, s)
    out = jax.ShapeDtypeStruct((s, nsel), F32)
    ob = pl.BlockSpec((tm, nsel), lambda i: (i, 0))
    return pl.pallas_call(
        _peer_route_kernel,
        grid=(s // tm,),
        in_specs=[pl.BlockSpec((tm, w), lambda i: (i, 0)),
                  pl.BlockSpec(keys.shape, lambda i: (0, 0))],
        out_specs=[ob, ob, ob],
        out_shape=[out, out, out],
        compiler_params=_params(("parallel",)),
        name="peer_route",
    )(q, keys)


EXPAND_UNROLL = 16
EXPAND_GROUP = 4


def _peer_expand_kernel(e1_ref, e2_ref, g_ref, w_ref, stage_ref):
    nk = PEER_N_KEYS
    nsel = e1_ref.shape[1]
    rid = lax.broadcasted_iota(jnp.int32, (nk, nsel), 0).astype(F32)
    un = EXPAND_UNROLL

    def body(tt, carry):
        base = pl.multiple_of(tt * un, un)
        e1 = e1_ref[pl.ds(base, un), :]
        e2 = e2_ref[pl.ds(base, un), :]
        g = g_ref[pl.ds(base, un), :]
        for u0 in range(0, un, EXPAND_GROUP):
            us = range(u0, u0 + EXPAND_GROUP)
            lefts = [jnp.where(rid == e1[u:u + 1, :], g[u:u + 1, :], 0.0).astype(BF16) for u in us]
            rights = [jnp.where(rid == e2[u:u + 1, :], 1.0, 0.0).astype(BF16) for u in us]
            for u, l, r in zip(us, lefts, rights):
                stage_ref[u * nk:(u + 1) * nk, :] = _dot_nt(l, r)
        for a in range(nk):
            rows = stage_ref[pl.ds(a, un, stride=nk), :]
            w_ref[pl.ds(base, un), a * nk:(a + 1) * nk] = rows.astype(w_ref.dtype)
        return carry

    lax.fori_loop(0, e1_ref.shape[0] // un, body, 0)


def _peer_expand(e1, e2, g, tm=128):
    s, nsel = e1.shape
    nk = PEER_N_KEYS
    tm = min(tm, s)
    ib = pl.BlockSpec((tm, nsel), lambda i: (i, 0))
    return pl.pallas_call(
        _peer_expand_kernel,
        grid=(s // tm,),
        in_specs=[ib, ib, ib],
        out_specs=pl.BlockSpec((tm, nk * nk), lambda i: (i, 0)),
        out_shape=jax.ShapeDtypeStruct((s, nk * nk), BF16),
        scratch_shapes=[pltpu.VMEM((EXPAND_UNROLL * nk, nk), F32)],
        compiler_params=_params(("parallel",)),
        name="peer_expand",
    )(e1, e2, g)


FP8 = jnp.float8_e4m3fn
FP8_MAX = 448.0
SCALE_ROWS = 8


def _rmsnorm_fp8_kernel(x_ref, g_ref, o_ref, o8_ref, s_ref):
    x = x_ref[...]
    ms = jnp.mean(x * x, axis=-1, keepdims=True)
    y = x * lax.rsqrt(ms + NORM_EPS) * g_ref[...]
    o_ref[...] = y.astype(o_ref.dtype)
    scale = jnp.maximum(jnp.max(jnp.abs(y), axis=-1, keepdims=True), 1e-30) * (1.0 / FP8_MAX)
    o8_ref[...] = (y / scale).astype(o8_ref.dtype)
    s_ref[...] = jnp.broadcast_to(scale, s_ref.shape)


def _rmsnorm_fp8(x, g, tm=256):
    s, d = x.shape
    blk = pl.BlockSpec((tm, d), lambda i: (i, 0))
    return pl.pallas_call(
        _rmsnorm_fp8_kernel,
        grid=(s // tm,),
        in_specs=[blk, pl.BlockSpec((1, d), lambda i: (0, 0))],
        out_specs=[blk, blk, pl.BlockSpec((tm, LANES), lambda i: (i, 0))],
        out_shape=[jax.ShapeDtypeStruct((s, d), BF16), jax.ShapeDtypeStruct((s, d), FP8),
                   jax.ShapeDtypeStruct((s, LANES), F32)],
        compiler_params=_params(("parallel",)),
        name="rmsnorm_fp8",
    )(x, g.reshape(1, d))


def _quant_rows_kernel(u_ref, u8_ref, s_ref):
    u = u_ref[...]
    scale = jnp.maximum(jnp.max(jnp.abs(u), axis=-1, keepdims=True), 1e-30) * (1.0 / FP8_MAX)
    u8_ref[...] = (u / scale).astype(u8_ref.dtype)
    s_ref[...] = jnp.broadcast_to(scale, (u.shape[0], LANES)).T[:SCALE_ROWS, :]


def _quant_rows(u, te=512):
    ne, d = u.shape
    return pl.pallas_call(
        _quant_rows_kernel,
        grid=(ne // te,),
        in_specs=[pl.BlockSpec((te, d), lambda e: (e, 0))],
        out_specs=[pl.BlockSpec((te, d), lambda e: (e, 0)),
                   pl.BlockSpec((SCALE_ROWS, te), lambda e: (0, e))],
        out_shape=[jax.ShapeDtypeStruct((ne, d), FP8), jax.ShapeDtypeStruct((SCALE_ROWS, ne), F32)],
        compiler_params=_params(("parallel",)),
        name="peer_quant_u",
    )(u)


def _peer_dense_kernel(h_ref, hs_ref, u_ref, us_ref, v_ref, vs_ref, w_ref, o_ref):
    @pl.when(pl.program_id(1) == 0)
    def _():
        o_ref[...] = jnp.zeros_like(o_ref)

    act = _dot_nt(h_ref[...], u_ref[...]) * hs_ref[:, 0:1] * us_ref[0:1, :]
    gelu = 0.5 * act * (1.0 + lax.erf(act * (2.0 ** -0.5)))
    mix = w_ref[...].astype(F32) * gelu * vs_ref[0:1, :]
    scale = jnp.maximum(jnp.max(jnp.abs(mix), axis=-1, keepdims=True), 1e-30) * (1.0 / FP8_MAX)
    o_ref[...] += _dot((mix / scale).astype(FP8), v_ref[...]) * scale


def _peer_dense(h8, hs, u8, us, v8, vs, wdense, tm=512, te=1024):
    s, d = h8.shape
    ne = u8.shape[0]
    tm = min(tm, s)
    return pl.pallas_call(
        _peer_dense_kernel,
        grid=(s // tm, ne // te),
        in_specs=[pl.BlockSpec((tm, d), lambda i, e: (i, 0)),
                  pl.BlockSpec((tm, LANES), lambda i, e: (i, 0)),
                  pl.BlockSpec((te, d), lambda i, e: (e, 0)),
                  pl.BlockSpec((SCALE_ROWS, te), lambda i, e: (0, e)),
                  pl.BlockSpec((te, d), lambda i, e: (e, 0)),
                  pl.BlockSpec((SCALE_ROWS, te), lambda i, e: (0, e)),
                  pl.BlockSpec((tm, te), lambda i, e: (i, e))],
        out_specs=pl.BlockSpec((tm, d), lambda i, e: (i, 0)),
        out_shape=jax.ShapeDtypeStruct((s, d), F32),
        compiler_params=_params(("parallel", "arbitrary")),
        name="peer_dense",
    )(h8, hs, u8, us, v8, vs, wdense)


def _add_rmsnorm_kernel(x_ref, y_ref, g_ref, o_ref):
    x = x_ref[...] + y_ref[...]
    ms = jnp.mean(x * x, axis=-1, keepdims=True)
    o_ref[...] = x * lax.rsqrt(ms + NORM_EPS) * g_ref[...]


def _add_rmsnorm(x, y, g, tm=256):
    s, d = x.shape
    blk = pl.BlockSpec((tm, d), lambda i: (i, 0))
    return pl.pallas_call(
        _add_rmsnorm_kernel,
        grid=(s // tm,),
        in_specs=[blk, blk, pl.BlockSpec((1, d), lambda i: (0, 0))],
        out_specs=blk,
        out_shape=jax.ShapeDtypeStruct((s, d), F32),
        compiler_params=_params(("parallel",)),
        name="residual_final_norm",
    )(x, y, g.reshape(1, d))


def _rwkv_branch(proj_r, w0, w_up, a0, a_up, g_up, k_k, k_a, r_k, lnx_g, lnx_b):
    g_up_pad = jnp.pad(g_up, ((0, GATE_LORA_PAD - GATE_LORA), (0, 0))).astype(BF16)
    wpre, a, g = _rwkv_lora(proj_r, w0, w_up.astype(BF16), a0, a_up.astype(BF16), g_up_pad)
    rt, at, kh, bh, kb, bb, pc, bonus = _rwkv_prep(proj_r, wpre, a, k_k, k_a, r_k.reshape(-1))
    x, y, opre, qb2, bbt, gst, pcm = _rwkv_chunk(rt, at, kh, bh, kb, bb, pc, proj_r)
    return _rwkv_scan(x, y, rt, opre, qb2, bbt, gst, pcm, bonus, g, lnx_g, lnx_b)


def _rope_tables(s):
    half = MOBA_HEAD_DIM // 2
    inv_freq = ROPE_THETA ** (-jnp.arange(half, dtype=F32) / half)
    ang = jnp.arange(s, dtype=jnp.int32).astype(F32)[:, None] * inv_freq[None, :]
    cos = jnp.cos(ang)
    sin = jnp.sin(ang)
    return jnp.concatenate([cos, cos], axis=-1), jnp.concatenate([-sin, sin], axis=-1)


def _moba_branch(proj_m):
    s = proj_m.shape[0]
    cos, sin = _rope_tables(s)
    q, qs, k, vt, kmean = _moba_prep(proj_m, cos, sin)
    return _moba_attn(q, qs, k, vt, kmean.reshape(kmean.shape[0], kmean.shape[2]))


def _peer_layer(x1, norm2_g, w_q, sub_keys, expert_u, expert_v, final_g):
    s, d = x1.shape
    h2, h8, hs = _rmsnorm_fp8(x1, norm2_g)
    q = _matmul(h2, w_q.astype(BF16), F32, tm=min(1024, s), tn=512, name="peer_query")
    keys = sub_keys.reshape(PEER_HEADS * 2 * PEER_N_KEYS, PEER_HALF).astype(BF16)
    e1, e2, g = _peer_route(q, keys)
    wdense = _peer_expand(e1, e2, g)
    u8, us = _quant_rows(expert_u)
    v8, vs = _quant_rows(expert_v)
    peer = _peer_dense(h8, hs, u8, us, v8, vs, wdense)
    return _add_rmsnorm(x1, peer, final_g)


def kernel(x, norm1_g, w_in, rwkv_mu, rwkv_w0, rwkv_w_up, rwkv_a0, rwkv_a_up, rwkv_g_up, rwkv_k_k, rwkv_k_a, rwkv_r_k, rwkv_lnx_g, rwkv_lnx_b, w_branch_rwkv, w_branch_moba, w_out, norm2_g, peer_w_q, peer_sub_keys, peer_u, peer_v, final_g):
    b, s, d = x.shape
    depth = w_in.shape[0]
    assert b == 1 and depth == 1
    x2d = x.reshape(s, d)
    l = 0
    rw = rwkv_w0.shape[-1]
    shift_w = 3 * rw + DECAY_LORA + AAA_LORA + GATE_LORA
    tn = 512
    shift_pad = -(-shift_w // tn) * tn
    mw = w_branch_moba.shape[1]
    tm = min(1024, s)

    h = _rmsnorm(x2d, norm1_g[l], BF16)
    mu = jnp.pad(rwkv_mu[l], (0, shift_pad - shift_w)).reshape(1, shift_pad)
    proj_r = _matmul_shift(h, w_in[l], mu, tm=tm, tn=tn)
    w_mg = _repack_columns(w_in[l], shift_w, 3 * mw + 2 * d)
    proj_m = _matmul(h, w_mg, F32, tm=tm, tn=tn, name="in_proj_moba", col0=0, n=3 * mw)
    gates = _matmul(h, w_mg, F32, tm=tm, tn=tn, name="in_proj_gates", col0=3 * mw, n=2 * d)

    y_a = _rwkv_branch(proj_r, rwkv_w0[l], rwkv_w_up[l], rwkv_a0[l], rwkv_a_up[l], rwkv_g_up[l],
                       rwkv_k_k[l], rwkv_k_a[l], rwkv_r_k[l], rwkv_lnx_g[l], rwkv_lnx_b[l])
    y_b = _moba_branch(proj_m)
    mixed = _gated_pair(y_a, w_branch_rwkv[l].astype(BF16), y_b, w_branch_moba[l].astype(BF16),
                        gates, tm=tm, tn=512)
    x1 = _matmul_residual(mixed, w_out[l].astype(BF16), x2d, tm=tm, tn=512)
    out = _peer_layer(x1, norm2_g[l], peer_w_q[l], peer_sub_keys[l], peer_u[l], peer_v[l], final_g)
    return out.reshape(b, s, d)
```

```python
import functools

import jax
import jax.numpy as jnp
from jax import lax
from jax.experimental import pallas as pl
from jax.experimental.pallas import tpu as pltpu

F32 = jnp.float32
BF16 = jnp.bfloat16

NORM_EPS = 1e-6
LANES = 128
RWKV_HEAD_DIM = 64
DECAY_LORA = 128
AAA_LORA = 128
GATE_LORA = 480
GATE_LORA_PAD = 512
LN_X_EPS = 64e-5
CHUNK = 128
MOBA_HEAD_DIM = 128
MOBA_BLOCK = 256
MOBA_TOPK = 3
ROPE_THETA = 10000.0
PEER_HEADS = 8
PEER_N_KEYS = 128
PEER_HALF = 128
PEER_TOPK = 16

VMEM_LIMIT = 56 * 1024 * 1024

_NT = (((1,), (1,)), ((), ()))


def _params(sem, vmem=VMEM_LIMIT):
    return pltpu.CompilerParams(dimension_semantics=sem, vmem_limit_bytes=vmem)


def _dot(a, b):
    return jnp.dot(a, b, preferred_element_type=F32)


def _dot_nt(a, b):
    return lax.dot_general(a, b, _NT, preferred_element_type=F32)


def _split3(x):
    hi = x.astype(BF16)
    r1 = x - hi.astype(F32)
    mid = r1.astype(BF16)
    lo = (r1 - mid.astype(F32)).astype(BF16)
    return hi, mid, lo


def _dot_l3(l_bf16, x):
    hi, mid, lo = _split3(x)
    return _dot(l_bf16, hi) + _dot(l_bf16, mid) + _dot(l_bf16, lo)


def _dot_r3(x, r_bf16):
    hi, mid, lo = _split3(x)
    return _dot(hi, r_bf16) + _dot(mid, r_bf16) + _dot(lo, r_bf16)


def _rmsnorm_kernel(x_ref, g_ref, o_ref):
    x = x_ref[...]
    ms = jnp.mean(x * x, axis=-1, keepdims=True)
    o_ref[...] = (x * lax.rsqrt(ms + NORM_EPS) * g_ref[...]).astype(o_ref.dtype)


def _rmsnorm(x, g, out_dtype, tm=256):
    s, d = x.shape
    return pl.pallas_call(
        _rmsnorm_kernel,
        grid=(s // tm,),
        in_specs=[pl.BlockSpec((tm, d), lambda i: (i, 0)),
                  pl.BlockSpec((1, d), lambda i: (0, 0))],
        out_specs=pl.BlockSpec((tm, d), lambda i: (i, 0)),
        out_shape=jax.ShapeDtypeStruct((s, d), out_dtype),
        compiler_params=_params(("parallel",)),
        name="rmsnorm",
    )(x, g.reshape(1, d))


def _mm_kernel(a_ref, b_ref, o_ref):
    o_ref[...] = _dot(a_ref[...], b_ref[...]).astype(o_ref.dtype)


def _matmul(a, b, out_dtype, tm, tn, name, col0=0, n=None):
    m, k = a.shape
    n = b.shape[1] if n is None else n
    off = col0 // tn
    return pl.pallas_call(
        _mm_kernel,
        grid=(m // tm, n // tn),
        in_specs=[pl.BlockSpec((tm, k), lambda i, j: (i, 0)),
                  pl.BlockSpec((k, tn), lambda i, j: (0, j + off))],
        out_specs=pl.BlockSpec((tm, tn), lambda i, j: (i, j)),
        out_shape=jax.ShapeDtypeStruct((m, n), out_dtype),
        compiler_params=_params(("parallel", "parallel")),
        name=name,
    )(a, b)


BF16_SUBLANES = 16


def _mm_nt_kernel(a_ref, bt_ref, o_ref):
    o_ref[...] = _dot_nt(a_ref[...], bt_ref[...]).astype(o_ref.dtype)


def _matmul_nt(a, bt, out_dtype, tm, tn, name, row0=0, n=None):
    m, k = a.shape
    n = bt.shape[0] if n is None else n
    off = row0 // tn
    return pl.pallas_call(
        _mm_nt_kernel,
        grid=(m // tm, n // tn),
        in_specs=[pl.BlockSpec((tm, k), lambda i, j: (i, 0)),
                  pl.BlockSpec((tn, k), lambda i, j: (j + off, 0))],
        out_specs=pl.BlockSpec((tm, tn), lambda i, j: (i, j)),
        out_shape=jax.ShapeDtypeStruct((m, n), out_dtype),
        compiler_params=_params(("parallel", "parallel")),
        name=name,
    )(a, bt)


def _mm_shift_kernel(a_ref, ap_ref, bt_ref, mu_ref, o_ref):
    i = pl.program_id(0)
    b = bt_ref[...].astype(BF16)
    z = _dot_nt(a_ref[...], b)
    zp = _dot_nt(ap_ref[...], b)
    prev = jnp.where(i > 0, zp[BF16_SUBLANES - 1:BF16_SUBLANES, :], 0.0)
    row = lax.broadcasted_iota(jnp.int32, z.shape, 0)
    z_prev = jnp.where(row == 0, prev, pltpu.roll(z, 1, axis=0))
    o_ref[...] = z + (z_prev - z) * mu_ref[...]


def _matmul_shift(a, bt, mu, tm, tn):
    m, k = a.shape
    n = mu.shape[1]
    per = tm // BF16_SUBLANES
    return pl.pallas_call(
        _mm_shift_kernel,
        grid=(m // tm, n // tn),
        in_specs=[pl.BlockSpec((tm, k), lambda i, j: (i, 0)),
                  pl.BlockSpec((BF16_SUBLANES, k), lambda i, j: (jnp.maximum(i * per - 1, 0), 0)),
                  pl.BlockSpec((tn, k), lambda i, j: (j, 0)),
                  pl.BlockSpec((1, tn), lambda i, j: (0, j))],
        out_specs=pl.BlockSpec((tm, tn), lambda i, j: (i, j)),
        out_shape=jax.ShapeDtypeStruct((m, n), F32),
        compiler_params=_params(("parallel", "parallel")),
        name="in_proj_rwkv",
    )(a, a, bt, mu)


def _repack_kernel(a_ref, b_ref, o_ref, *, off):
    o_ref[...] = jnp.concatenate([a_ref[off:, :], b_ref[:off, :]], axis=0).astype(o_ref.dtype)


def _repack_rows(wt, row0, n, tr=512, tk=1024):
    k = wt.shape[1]
    first, off = divmod(row0, tr)
    assert off % BF16_SUBLANES == 0
    return pl.pallas_call(
        functools.partial(_repack_kernel, off=off),
        grid=(n // tr, k // tk),
        in_specs=[pl.BlockSpec((tr, tk), lambda i, j: (i + first, j)),
                  pl.BlockSpec((tr, tk), lambda i, j: (i + first + 1, j))],
        out_specs=pl.BlockSpec((tr, tk), lambda i, j: (i, j)),
        out_shape=jax.ShapeDtypeStruct((n, k), BF16),
        compiler_params=_params(("parallel", "parallel")),
        name="repack_w_in",
    )(wt, wt)


def _gated_pair_kernel(ya_ref, wa_ref, yb_ref, wb_ref, ga_ref, gb_ref, o_ref):
    pa = _dot(ya_ref[...], wa_ref[...])
    pb = _dot(yb_ref[...], wb_ref[...])
    mixed = jax.nn.sigmoid(ga_ref[...]) * pa + jax.nn.sigmoid(gb_ref[...]) * pb
    o_ref[...] = mixed.astype(o_ref.dtype)


def _gated_pair(ya, wa, yb, wb, gates, tm, tn):
    m, k = ya.shape
    n = wa.shape[1]
    nb = n // tn
    return pl.pallas_call(
        _gated_pair_kernel,
        grid=(m // tm, nb),
        in_specs=[pl.BlockSpec((tm, k), lambda i, j: (i, 0)),
                  pl.BlockSpec((k, tn), lambda i, j: (0, j)),
                  pl.BlockSpec((tm, k), lambda i, j: (i, 0)),
                  pl.BlockSpec((k, tn), lambda i, j: (0, j)),
                  pl.BlockSpec((tm, tn), lambda i, j: (i, j)),
                  pl.BlockSpec((tm, tn), lambda i, j: (i, j + nb))],
        out_specs=pl.BlockSpec((tm, tn), lambda i, j: (i, j)),
        out_shape=jax.ShapeDtypeStruct((m, n), BF16),
        compiler_params=_params(("parallel", "parallel")),
        name="branch_merge",
    )(ya, wa, yb, wb, gates, gates)


def _mm_res_kernel(a_ref, b_ref, r_ref, o_ref):
    o_ref[...] = r_ref[...] + _dot(a_ref[...], b_ref[...])


def _matmul_residual(a, b, res, tm, tn):
    m, k = a.shape
    n = b.shape[1]
    return pl.pallas_call(
        _mm_res_kernel,
        grid=(m // tm, n // tn),
        in_specs=[pl.BlockSpec((tm, k), lambda i, j: (i, 0)),
                  pl.BlockSpec((k, tn), lambda i, j: (0, j)),
                  pl.BlockSpec((tm, tn), lambda i, j: (i, j))],
        out_specs=pl.BlockSpec((tm, tn), lambda i, j: (i, j)),
        out_shape=jax.ShapeDtypeStruct((m, n), F32),
        compiler_params=_params(("parallel", "parallel")),
        name="out_proj",
    )(a, b, res)


def _rwkv_lora_kernel(wl_ref, al_ref, gl0_ref, gl1_ref, w0_ref, wup_ref, a0_ref, aup_ref,
                      gup_ref, wpre_ref, a_ref, g_ref):
    half = GATE_LORA_PAD // 2
    wpre_ref[...] = w0_ref[...] + _dot(jnp.tanh(wl_ref[...]).astype(BF16), wup_ref[...])
    a_ref[...] = jax.nn.sigmoid(a0_ref[...] + _dot(al_ref[...].astype(BF16), aup_ref[...]))
    g_ref[...] = (_dot(jax.nn.sigmoid(gl0_ref[...]).astype(BF16), gup_ref[:half, :])
                  + _dot(jax.nn.sigmoid(gl1_ref[...]).astype(BF16), gup_ref[half:, :]))


def _rwkv_lora(proj_r, w0, w_up, a0, a_up, g_up_pad, tm=256):
    s = proj_r.shape[0]
    w = w0.shape[-1]
    half = GATE_LORA_PAD // 2
    c_wl = 3 * w // DECAY_LORA
    c_al = (3 * w + DECAY_LORA) // AAA_LORA
    c_gl = (3 * w + DECAY_LORA + AAA_LORA) // half
    row = lambda i: (i, 0)
    const = lambda i: (0, 0)
    out = jax.ShapeDtypeStruct((s, w), F32)
    return pl.pallas_call(
        _rwkv_lora_kernel,
        grid=(s // tm,),
        in_specs=[pl.BlockSpec((tm, DECAY_LORA), lambda i: (i, c_wl)),
                  pl.BlockSpec((tm, AAA_LORA), lambda i: (i, c_al)),
                  pl.BlockSpec((tm, half), lambda i: (i, c_gl)),
                  pl.BlockSpec((tm, half), lambda i: (i, c_gl + 1)),
                  pl.BlockSpec((1, w), const),
                  pl.BlockSpec((DECAY_LORA, w), const),
                  pl.BlockSpec((1, w), const),
                  pl.BlockSpec((AAA_LORA, w), const),
                  pl.BlockSpec((GATE_LORA_PAD, w), const)],
        out_specs=[pl.BlockSpec((tm, w), row)] * 3,
        out_shape=[out, out, out],
        compiler_params=_params(("parallel",)),
        name="rwkv_lora",
    )(proj_r, proj_r, proj_r, proj_r, w0.reshape(1, w), w_up, a0.reshape(1, w), a_up, g_up_pad)


def _head_pair_ones(width=LANES):
    r = lax.broadcasted_iota(jnp.int32, (width, width), 0) // RWKV_HEAD_DIM
    c = lax.broadcasted_iota(jnp.int32, (width, width), 1) // RWKV_HEAD_DIM
    return r == c


def _rwkv_prep_kernel(r_ref, k_ref, v_ref, wpre_ref, a_ref, kk_ref, ka_ref, rk_ref,
                      rt_ref, at_ref, kh_ref, bh_ref, kb_ref, bb_ref, pc_ref, bonus_ref):
    tm = r_ref.shape[0]
    r = r_ref[...]
    k = k_ref[...]
    v = v_ref[...]
    a = a_ref[...]
    same_head = jnp.where(_head_pair_ones(r.shape[1]), 1.0, 0.0).astype(BF16)

    x = -wpre_ref[...]
    softplus = jnp.maximum(x, 0.0) + jnp.log1p(jnp.exp(-jnp.abs(x)))
    lw = -jnp.exp(-softplus - 0.5)

    kk = k * kk_ref[...]
    ss = _dot_r3(kk * kk, same_head)
    kkn = kk / jnp.maximum(jnp.sqrt(ss), 1e-12)
    k2 = k * (1.0 + (a - 1.0) * ka_ref[...])
    bonus_ref[...] = _dot_r3(r * k2 * rk_ref[...], same_head) * v

    ri = lax.broadcasted_iota(jnp.int32, (tm, tm), 0)
    ci = lax.broadcasted_iota(jnp.int32, (tm, tm), 1)
    same_chunk = (ri // CHUNK) == (ci // CHUNK)
    tri = jnp.where(same_chunk & (ci <= ri), 1.0, 0.0).astype(BF16)
    allc = jnp.where(same_chunk, 1.0, 0.0).astype(BF16)
    cum = _dot_l3(tri, lw)
    tot = _dot_l3(allc, lw)

    p_in = jnp.exp(cum)
    p_inv = jnp.exp(-cum)
    p_tail = jnp.exp(tot - cum)
    b = kkn * a
    rt_ref[...] = r * p_in
    at_ref[...] = -kkn * jnp.exp(cum - lw)
    kh_ref[...] = k2 * p_inv
    bh_ref[...] = b * p_inv
    kb_ref[...] = k2 * p_tail
    bb_ref[...] = b * p_tail
    pc_ref[...] = jnp.exp(tot)


def _rwkv_prep(proj_r, wpre, a, k_k, k_a, r_k, tm=256, tw=512):
    s, w = wpre.shape
    ncol = w // tw
    blk = lambda off: pl.BlockSpec((tm, tw), lambda i, j, off=off: (i, j + off))
    vec = pl.BlockSpec((1, tw), lambda i, j: (0, j))
    out = jax.ShapeDtypeStruct((s, w), F32)
    return pl.pallas_call(
        _rwkv_prep_kernel,
        grid=(s // tm, ncol),
        in_specs=[blk(0), blk(ncol), blk(2 * ncol), blk(0), blk(0), vec, vec, vec],
        out_specs=[blk(0)] * 8,
        out_shape=[out] * 8,
        compiler_params=_params(("parallel", "parallel")),
        name="rwkv_prep",
    )(proj_r, proj_r, proj_r, wpre, a, k_k.reshape(1, w), k_a.reshape(1, w), r_k.reshape(1, w))


def _rwkv_chunk_kernel(rt_ref, at_ref, kh_ref, bh_ref, kb_ref, bb_ref, pc_ref, v_ref,
                       x_ref, y_ref, op_ref, qb_ref, bbt_ref, g_ref, pcm_ref):
    c = CHUNK
    nheads = LANES // RWKV_HEAD_DIM
    chunks = range(rt_ref.shape[0] // c)
    pairs = range(rt_ref.shape[1] // LANES)
    rows = [slice(q * c, (q + 1) * c) for q in chunks]
    lanes = [slice(p * LANES, (p + 1) * LANES) for p in pairs]
    tiles = [(q, p) for q in chunks for p in pairs]
    probs = [(n, h) for n in range(len(tiles)) for h in range(nheads)]
    lane = lax.broadcasted_iota(jnp.int32, (c, LANES), 1)
    ri = lax.broadcasted_iota(jnp.int32, (c, c), 0)
    ci = lax.broadcasted_iota(jnp.int32, (c, c), 1)
    strict = ci < ri
    incl = ci <= ri
    eye = jnp.where(ri == ci, 1.0, 0.0)
    head_mask = [(lane // RWKV_HEAD_DIM) == h for h in range(nheads)]
    tile = lambda ref, n: ref[rows[tiles[n][0]], lanes[tiles[n][1]]]

    kh = [tile(kh_ref, n).astype(BF16) for n in range(len(tiles))]
    bh = [tile(bh_ref, n).astype(BF16) for n in range(len(tiles))]
    a_h = [jnp.where(head_mask[h], tile(at_ref, n), 0.0).astype(BF16) for n, h in probs]
    r_h = [jnp.where(head_mask[h], tile(rt_ref, n), 0.0).astype(BF16) for n, h in probs]
    v_h = [jnp.where(head_mask[h], tile(v_ref, n), 0.0).astype(BF16) for n, h in probs]
    n_ab = [jnp.where(strict, _dot_nt(a, bh[n]), 0.0) for a, (n, _) in zip(a_h, probs)]
    n_ak = [jnp.where(strict, _dot_nt(a, kh[n]), 0.0).astype(BF16) for a, (n, _) in zip(a_h, probs)]
    q_k = [jnp.where(incl, _dot_nt(r, kh[n]), 0.0).astype(BF16) for r, (n, _) in zip(r_h, probs)]
    q_b = [jnp.where(incl, _dot_nt(r, bh[n]), 0.0) for r, (n, _) in zip(r_h, probs)]
    for qb, (n, h) in zip(q_b, probs):
        q, p = tiles[n]
        col = (p * nheads + h) * c
        qb_ref[rows[q], col:col + c] = qb.astype(qb_ref.dtype)

    t = [eye + n for n in n_ab]
    pw = n_ab
    for _ in range((c - 1).bit_length() - 1):
        pb = [x.astype(BF16) for x in pw]
        pw = [_dot(x, x) for x in pb]
        t = [ti + _dot(ti.astype(BF16), pi.astype(BF16)) for ti, pi in zip(t, pw)]
    tb = [ti.astype(BF16) for ti in t]
    xs = [_dot(ti, a) for ti, a in zip(tb, a_h)]
    nv = [_dot(n, v).astype(BF16) for n, v in zip(n_ak, v_h)]
    ys = [_dot(ti, z) for ti, z in zip(tb, nv)]
    os = [_dot(qk, v) for qk, v in zip(q_k, v_h)]
    same_head = _head_pair_ones()
    for n, (q, p) in enumerate(tiles):
        mine = [m for m, (nn, _) in enumerate(probs) if nn == n]
        x_ref[rows[q], lanes[p]] = sum(xs[m] for m in mine).astype(x_ref.dtype)
        y_ref[rows[q], lanes[p]] = sum(ys[m] for m in mine)
        op_ref[rows[q], lanes[p]] = sum(os[m] for m in mine)
        kbt = tile(kb_ref, n).T.astype(BF16)
        g_ref[rows[q], lanes[p]] = jnp.where(same_head, _dot(kbt, tile(v_ref, n).astype(BF16)), 0.0)
        bbt_ref[lanes[p], rows[q]] = tile(bb_ref, n).T.astype(bbt_ref.dtype)
        pcm_ref[rows[q], lanes[p]] = tile(pc_ref, n).T


def _rwkv_chunk(rt, at, kh, bh, kb, bb, pc, proj_r, tc=2 * CHUNK, tw=2 * LANES):
    s, w = rt.shape
    ncol = w // tw
    c = CHUNK
    blk = lambda off: pl.BlockSpec((tc, tw), lambda i, j, off=off: (i, j + off))
    f32o = jax.ShapeDtypeStruct((s, w), F32)
    return pl.pallas_call(
        _rwkv_chunk_kernel,
        grid=(s // tc, ncol),
        in_specs=[blk(0)] * 7 + [blk(2 * ncol)],
        out_specs=[blk(0), blk(0), blk(0),
                   pl.BlockSpec((tc, 2 * tw), lambda i, j: (i, j)),
                   pl.BlockSpec((tw, tc), lambda i, j: (j, i)),
                   blk(0), blk(0)],
        out_shape=[jax.ShapeDtypeStruct((s, w), BF16), f32o, f32o,
                   jax.ShapeDtypeStruct((s, 2 * w), BF16),
                   jax.ShapeDtypeStruct((w, s), BF16),
                   f32o, f32o],
        compiler_params=_params(("parallel", "parallel")),
        name="rwkv_chunk",
    )(rt, at, kh, bh, kb, bb, pc, proj_r)


def _rwkv_scan_kernel(x_ref, y_ref, rt_ref, op_ref, qb_ref, bbt_ref, g_ref, pcm_ref,
                      bonus_ref, gate_ref, lng_ref, lnb_ref, ya_ref, h_ref):
    c = CHUNK
    pairs = range(x_ref.shape[1] // LANES)
    lanes = [slice(p * LANES, (p + 1) * LANES) for p in pairs]

    @pl.when(pl.program_id(1) == 0)
    def _():
        h_ref[...] = jnp.zeros_like(h_ref)

    lane = lax.broadcasted_iota(jnp.int32, (c, LANES), 1)
    first = lane < RWKV_HEAD_DIM
    same_head = _head_pair_ones()
    ones_head = jnp.where(same_head, 1.0, 0.0).astype(BF16)
    inv_n = 1.0 / RWKV_HEAD_DIM
    for q in range(x_ref.shape[0] // c):
        rows = slice(q * c, (q + 1) * c)
        hs = [h_ref[p] for p in pairs]
        hb = [h.astype(BF16) for h in hs]
        us = [_dot(x_ref[rows, lanes[p]], hb[p]) + y_ref[rows, lanes[p]] for p in pairs]
        u2 = [jnp.concatenate([jnp.where(first, u, 0.0), jnp.where(first, 0.0, u)], axis=0).astype(BF16)
              for u in us]
        os = [_dot(rt_ref[rows, lanes[p]].astype(BF16), hb[p]) + op_ref[rows, lanes[p]]
              + _dot(qb_ref[rows, p * 2 * c:(p + 1) * 2 * c], u2[p]) for p in pairs]
        upd = [_dot(bbt_ref[lanes[p], rows], us[p].astype(BF16)) for p in pairs]
        for p in pairs:
            h_ref[p] = (pcm_ref[rows, lanes[p]] * hs[p] + g_ref[rows, lanes[p]]
                        + jnp.where(same_head, upd[p], 0.0))
        mus = [_dot_r3(o, ones_head) * inv_n for o in os]
        ds = [o - mu for o, mu in zip(os, mus)]
        vs = [_dot_r3(d * d, ones_head) * inv_n for d in ds]
        for p in pairs:
            y = ds[p] * lax.rsqrt(vs[p] + LN_X_EPS) * lng_ref[:, lanes[p]] + lnb_ref[:, lanes[p]]
            ya_ref[rows, lanes[p]] = ((y + bonus_ref[rows, lanes[p]])
                                      * gate_ref[rows, lanes[p]]).astype(ya_ref.dtype)


def _rwkv_scan(x, y, rt, opre, qb2, bbt, g, pcm, bonus, gate, lnx_g, lnx_b, ts=512, tw=2 * LANES):
    s, w = y.shape
    ts = min(ts, s)
    blk = pl.BlockSpec((ts, tw), lambda j, i: (i, j))
    vec = pl.BlockSpec((1, tw), lambda j, i: (0, j))
    return pl.pallas_call(
        _rwkv_scan_kernel,
        grid=(w // tw, s // ts),
        in_specs=[blk, blk, blk, blk,
                  pl.BlockSpec((ts, 2 * tw), lambda j, i: (i, j)),
                  pl.BlockSpec((tw, ts), lambda j, i: (j, i)),
                  blk, blk, blk, blk, vec, vec],
        out_specs=blk,
        out_shape=jax.ShapeDtypeStruct((s, w), BF16),
        scratch_shapes=[pltpu.VMEM((tw // LANES, LANES, LANES), F32)],
        compiler_params=_params(("parallel", "arbitrary")),
        name="rwkv_scan",
    )(x, y, rt, opre, qb2, bbt, g, pcm, bonus, gate, lnx_g.reshape(1, w), lnx_b.reshape(1, w))


V_ONES_ROWS = 16


LOG2_E = 1.4426950408889634
MOBA_LOG2_SCALE = (MOBA_HEAD_DIM ** -0.5) * LOG2_E


def _moba_prep_kernel(q_ref, k_ref, v_ref, cos_ref, sin_ref, qo_ref, qs_ref, ko_ref, vt_ref, km_ref):
    cos = cos_ref[...]
    sin = sin_ref[...]
    hd = MOBA_HEAD_DIM
    half = hd // 2
    tb = q_ref.shape[0]
    for h in range(q_ref.shape[1] // hd):
        cols = slice(h * hd, (h + 1) * hd)
        q = q_ref[:, cols]
        k = k_ref[:, cols]
        qr = q * cos + pltpu.roll(q, half, axis=1) * sin
        kr = k * cos + pltpu.roll(k, half, axis=1) * sin
        qo_ref[:, cols] = qr.astype(qo_ref.dtype)
        qs_ref[:, cols] = (qr * MOBA_LOG2_SCALE).astype(qs_ref.dtype)
        ko_ref[:, cols] = kr.astype(ko_ref.dtype)
        km_ref[0, :, cols] = jnp.mean(kr, axis=0, keepdims=True)
        vt_ref[h, 0, :hd, :] = v_ref[:, cols].T.astype(vt_ref.dtype)
        vt_ref[h, 0, hd:, :] = jnp.ones((V_ONES_ROWS, tb), vt_ref.dtype)


def _moba_prep(proj_m, cos, sin):
    s = proj_m.shape[0]
    w = proj_m.shape[1] // 3
    tb = MOBA_BLOCK
    nb = s // tb
    nh = w // MOBA_HEAD_DIM
    vrows = MOBA_HEAD_DIM + V_ONES_ROWS
    blk = lambda off: pl.BlockSpec((tb, w), lambda i, off=off: (i, off))
    tab = pl.BlockSpec((tb, MOBA_HEAD_DIM), lambda i: (i, 0))
    bo = jax.ShapeDtypeStruct((s, w), BF16)
    return pl.pallas_call(
        _moba_prep_kernel,
        grid=(nb,),
        in_specs=[blk(0), blk(1), blk(2), tab, tab],
        out_specs=[blk(0), blk(0), blk(0),
                   pl.BlockSpec((nh, 1, vrows, tb), lambda i: (0, i, 0, 0)),
                   pl.BlockSpec((1, 1, w), lambda i: (i, 0, 0))],
        out_shape=[bo, bo, bo, jax.ShapeDtypeStruct((nh, nb, vrows, tb), BF16),
                   jax.ShapeDtypeStruct((nb, 1, w), F32)],
        compiler_params=_params(("parallel",)),
        name="moba_prep",
    )(proj_m, proj_m, proj_m, cos, sin)


MOBA_CHAINS = 4


def _moba_attn_kernel(q_ref, qs_ref, k_ref, vt_ref, km_ref, o_ref, bias_ref):
    i = pl.program_id(1)
    tb = MOBA_BLOCK
    nb = km_ref.shape[0]
    hd = MOBA_HEAD_DIM
    q = qs_ref[...]

    own = pl.multiple_of(i * tb, tb)
    s = _dot_nt(k_ref[pl.ds(own, tb), :], q)
    raw0 = [_dot_nt(k_ref[g * tb:(g + 1) * tb, :], q) for g in range(MOBA_CHAINS)]

    gate = _dot_nt(km_ref[...].astype(BF16), q_ref[...])
    rid = lax.broadcasted_iota(jnp.int32, gate.shape, 0).astype(F32)
    gate = jnp.where(rid < i.astype(F32), gate, -jnp.inf)
    bias = jnp.full(gate.shape, -jnp.inf, F32)
    for _ in range(MOBA_TOPK):
        m = jnp.max(gate, axis=0, keepdims=True)
        first = jnp.min(jnp.where(gate == m, rid, float(nb)), axis=0, keepdims=True)
        pick = (rid == first) & (m > -jnp.inf)
        bias = jnp.where(pick, 0.0, bias)
        gate = jnp.where(pick, -jnp.inf, gate)
    bias_ref[...] = bias

    ss0 = [r.astype(BF16) + bias[g:g + 1, :].astype(BF16) for g, r in enumerate(raw0)]

    ki = lax.broadcasted_iota(jnp.int32, s.shape, 0)
    qi = lax.broadcasted_iota(jnp.int32, s.shape, 1)
    s = jnp.where(ki <= qi, s, -jnp.inf).astype(BF16)
    m0 = jnp.max(s, axis=0, keepdims=True)
    acc0 = _dot(vt_ref[i], jnp.exp2(s - m0))
    m0 = m0.astype(F32)

    def scores(t):
        js = [jnp.minimum(t * MOBA_CHAINS + g, nb - 1) for g in range(MOBA_CHAINS)]
        return [_dot_nt(k_ref[pl.ds(pl.multiple_of(j * tb, tb), tb), :], q).astype(BF16)
                + bias_ref[pl.ds(j, 1), :].astype(BF16) for j in js]

    def body(t, carry):
        chains, ss = carry
        ss_next = scores(t + 1)
        js = [t * MOBA_CHAINS + g for g in range(MOBA_CHAINS)]
        ms = [jnp.maximum(m, jnp.max(sj, axis=0, keepdims=True).astype(F32))
              for (m, _), sj in zip(chains, ss)]
        ps = [jnp.exp2(sj - mn.astype(BF16)) for sj, mn in zip(ss, ms)]
        chains = tuple((mn, jnp.exp2(m - mn) * acc + _dot(vt_ref[j], pj))
                       for (m, acc), mn, pj, j in zip(chains, ms, ps, js))
        return chains, ss_next

    init = ((m0, acc0),) + ((m0, jnp.zeros_like(acc0)),) * (MOBA_CHAINS - 1)
    chains, _ = lax.fori_loop(0, (i + MOBA_CHAINS - 1) // MOBA_CHAINS, body, (init, ss0))
    m = chains[0][0]
    for mg, _ in chains[1:]:
        m = jnp.maximum(m, mg)
    acc = sum(jnp.exp2(mg - m) * ag for mg, ag in chains)
    out = acc[:hd, :] / acc[hd:hd + 1, :]
    o_ref[...] = out.T.astype(o_ref.dtype)


def _moba_attn(q, qs, k, vt, kmean):
    s, w = q.shape
    tb = MOBA_BLOCK
    nb = s // tb
    hd = MOBA_HEAD_DIM
    vrows = vt.shape[2]
    return pl.pallas_call(
        _moba_attn_kernel,
        grid=(w // hd, nb),
        in_specs=[pl.BlockSpec((tb, hd), lambda h, i: (i, h)),
                  pl.BlockSpec((tb, hd), lambda h, i: (i, h)),
                  pl.BlockSpec((s, hd), lambda h, i: (0, h)),
                  pl.BlockSpec((None, nb, vrows, tb), lambda h, i: (h, 0, 0, 0)),
                  pl.BlockSpec((nb, hd), lambda h, i: (0, h))],
        out_specs=pl.BlockSpec((tb, hd), lambda h, i: (i, h)),
        out_shape=jax.ShapeDtypeStruct((s, w), BF16),
        scratch_shapes=[pltpu.VMEM((nb, tb), F32)],
        compiler_params=_params(("parallel", "arbitrary")),
        name="moba_attn",
    )(q, qs, k, vt, kmean)


def _top_rows(x, n):
    rows = x.shape[0]
    rid = lax.broadcasted_iota(jnp.int32, x.shape, 0).astype(F32)
    vals, idxs = [], []
    for _ in range(n):
        m = jnp.max(x, axis=0, keepdims=True)
        first = jnp.min(jnp.where(x == m, rid, float(rows)), axis=0, keepdims=True)
        vals.append(m)
        idxs.append(first)
        x = jnp.where(rid == first, -jnp.inf, x)
    return jnp.concatenate(vals, axis=0), jnp.concatenate(idxs, axis=0)


def _peer_route_kernel(q_ref, keys_ref, e1_ref, e2_ref, g_ref):
    n = PEER_TOPK
    tm = q_ref.shape[0]
    e1s, e2s, gs = [], [], []
    for h in range(PEER_HEADS):
        tops = []
        for p in range(2):
            hp = 2 * h + p
            cols = slice(hp * PEER_HALF, (hp + 1) * PEER_HALF)
            rows = slice(hp * PEER_N_KEYS, (hp + 1) * PEER_N_KEYS)
            st = _dot_nt(keys_ref[rows, :], q_ref[:, cols].astype(BF16))
            tops.append(_top_rows(st, n))
        (s1, i1), (s2, i2) = tops
        hn = n // 2
        cand = jnp.concatenate([s1[0:1, :] + s2]
                               + [s1[a:a + 1, :] + s2[:hn, :] for a in range(1, hn)]
                               + [s1[hn:, :] + s2[0:1, :]], axis=0)
        f_s, f_pos = _top_rows(cand, n)
        mid = jnp.floor((f_pos - n) * (1.0 / hn))
        tail0 = float(n + (hn - 1) * hn)
        pa = jnp.where(f_pos < n, 0.0, jnp.where(f_pos < tail0, 1.0 + mid, f_pos - tail0 + hn))
        pb = jnp.where(f_pos < n, f_pos, jnp.where(f_pos < tail0, f_pos - n - mid * hn, 0.0))
        e1 = jnp.zeros((n, tm), F32)
        e2 = jnp.zeros((n, tm), F32)
        for a in range(n):
            e1 = jnp.where(pa == float(a), i1[a:a + 1, :], e1)
            e2 = jnp.where(pb == float(a), i2[a:a + 1, :], e2)
        ex = jnp.exp(f_s - f_s[0:1, :])
        gs.append(ex / jnp.sum(ex, axis=0, keepdims=True))
        e1s.append(e1)
        e2s.append(e2)
    e1_ref[...] = jnp.concatenate(e1s, axis=0).T
    e2_ref[...] = jnp.concatenate(e2s, axis=0).T
    g_ref[...] = jnp.concatenate(gs, axis=0).T


def _peer_route(q, keys, tm=256):
    s, w = q.shape
    nsel = PEER_HEADS * PEER_TOPK
    tm = min(tm, s)
    out = jax.ShapeDtypeStruct((s, nsel), F32)
    ob = pl.BlockSpec((tm, nsel), lambda i: (i, 0))
    return pl.pallas_call(
        _peer_route_kernel,
        grid=(s // tm,),
        in_specs=[pl.BlockSpec((tm, w), lambda i: (i, 0)),
                  pl.BlockSpec(keys.shape, lambda i: (0, 0))],
        out_specs=[ob, ob, ob],
        out_shape=[out, out, out],
        compiler_params=_params(("parallel",)),
        name="peer_route",
    )(q, keys)


EXPAND_UNROLL = 16
EXPAND_GROUP = 4


def _peer_expand_kernel(e1_ref, e2_ref, g_ref, w_ref, stage_ref):
    nk = PEER_N_KEYS
    nsel = e1_ref.shape[1]
    rid = lax.broadcasted_iota(jnp.int32, (nk, nsel), 0).astype(F32)
    un = EXPAND_UNROLL

    def body(tt, carry):
        base = pl.multiple_of(tt * un, un)
        e1 = e1_ref[pl.ds(base, un), :]
        e2 = e2_ref[pl.ds(base, un), :]
        g = g_ref[pl.ds(base, un), :]
        for u0 in range(0, un, EXPAND_GROUP):
            us = range(u0, u0 + EXPAND_GROUP)
            lefts = [jnp.where(rid == e1[u:u + 1, :], g[u:u + 1, :], 0.0).astype(BF16) for u in us]
            rights = [jnp.where(rid == e2[u:u + 1, :], 1.0, 0.0).astype(BF16) for u in us]
            for u, l, r in zip(us, lefts, rights):
                stage_ref[u * nk:(u + 1) * nk, :] = _dot_nt(l, r)
        for a in range(nk):
            rows = stage_ref[pl.ds(a, un, stride=nk), :]
            w_ref[pl.ds(base, un), a * nk:(a + 1) * nk] = rows.astype(w_ref.dtype)
        return carry

    lax.fori_loop(0, e1_ref.shape[0] // un, body, 0)


def _peer_expand(e1, e2, g, tm=128):
    s, nsel = e1.shape
    nk = PEER_N_KEYS
    tm = min(tm, s)
    ib = pl.BlockSpec((tm, nsel), lambda i: (i, 0))
    return pl.pallas_call(
        _peer_expand_kernel,
        grid=(s // tm,),
        in_specs=[ib, ib, ib],
        out_specs=pl.BlockSpec((tm, nk * nk), lambda i: (i, 0)),
        out_shape=jax.ShapeDtypeStruct((s, nk * nk), BF16),
        scratch_shapes=[pltpu.VMEM((EXPAND_UNROLL * nk, nk), F32)],
        compiler_params=_params(("parallel",)),
        name="peer_expand",
    )(e1, e2, g)


FP8 = jnp.float8_e4m3fn
FP8_MAX = 448.0
SCALE_ROWS = 8


def _rmsnorm_fp8_kernel(x_ref, g_ref, o_ref, o8_ref, s_ref):
    x = x_ref[...]
    ms = jnp.mean(x * x, axis=-1, keepdims=True)
    y = x * lax.rsqrt(ms + NORM_EPS) * g_ref[...]
    o_ref[...] = y.astype(o_ref.dtype)
    scale = jnp.maximum(jnp.max(jnp.abs(y), axis=-1, keepdims=True), 1e-30) * (1.0 / FP8_MAX)
    o8_ref[...] = (y / scale).astype(o8_ref.dtype)
    s_ref[...] = jnp.broadcast_to(scale, s_ref.shape)


def _rmsnorm_fp8(x, g, tm=256):
    s, d = x.shape
    blk = pl.BlockSpec((tm, d), lambda i: (i, 0))
    return pl.pallas_call(
        _rmsnorm_fp8_kernel,
        grid=(s // tm,),
        in_specs=[blk, pl.BlockSpec((1, d), lambda i: (0, 0))],
        out_specs=[blk, blk, pl.BlockSpec((tm, LANES), lambda i: (i, 0))],
        out_shape=[jax.ShapeDtypeStruct((s, d), BF16), jax.ShapeDtypeStruct((s, d), FP8),
                   jax.ShapeDtypeStruct((s, LANES), F32)],
        compiler_params=_params(("parallel",)),
        name="rmsnorm_fp8",
    )(x, g.reshape(1, d))


def _quant_rows_kernel(u_ref, u8_ref, s_ref):
    u = u_ref[...]
    scale = jnp.maximum(jnp.max(jnp.abs(u), axis=-1, keepdims=True), 1e-30) * (1.0 / FP8_MAX)
    u8_ref[...] = (u / scale).astype(u8_ref.dtype)
    s_ref[...] = jnp.broadcast_to(scale, (u.shape[0], LANES)).T[:SCALE_ROWS, :]


def _quant_rows(u, te=512):
    ne, d = u.shape
    return pl.pallas_call(
        _quant_rows_kernel,
        grid=(ne // te,),
        in_specs=[pl.BlockSpec((te, d), lambda e: (e, 0))],
        out_specs=[pl.BlockSpec((te, d), lambda e: (e, 0)),
                   pl.BlockSpec((SCALE_ROWS, te), lambda e: (0, e))],
        out_shape=[jax.ShapeDtypeStruct((ne, d), FP8), jax.ShapeDtypeStruct((SCALE_ROWS, ne), F32)],
        compiler_params=_params(("parallel",)),
        name="peer_quant_u",
    )(u)


def _peer_dense_kernel(h_ref, hs_ref, u_ref, us_ref, v_ref, vs_ref, w_ref, o_ref):
    @pl.when(pl.program_id(1) == 0)
    def _():
        o_ref[...] = jnp.zeros_like(o_ref)

    act = _dot_nt(h_ref[...], u_ref[...]) * hs_ref[:, 0:1] * us_ref[0:1, :]
    gelu = 0.5 * act * (1.0 + lax.erf(act * (2.0 ** -0.5)))
    mix = w_ref[...].astype(F32) * gelu * vs_ref[0:1, :]
    scale = jnp.maximum(jnp.max(jnp.abs(mix), axis=-1, keepdims=True), 1e-30) * (1.0 / FP8_MAX)
    o_ref[...] += _dot((mix / scale).astype(FP8), v_ref[...]) * scale


def _peer_dense(h8, hs, u8, us, v8, vs, wdense, tm=512, te=1024):
    s, d = h8.shape
    ne = u8.shape[0]
    tm = min(tm, s)
    return pl.pallas_call(
        _peer_dense_kernel,
        grid=(s // tm, ne // te),
        in_specs=[pl.BlockSpec((tm, d), lambda i, e: (i, 0)),
                  pl.BlockSpec((tm, LANES), lambda i, e: (i, 0)),
                  pl.BlockSpec((te, d), lambda i, e: (e, 0)),
                  pl.BlockSpec((SCALE_ROWS, te), lambda i, e: (0, e)),
                  pl.BlockSpec((te, d), lambda i, e: (e, 0)),
                  pl.BlockSpec((SCALE_ROWS, te), lambda i, e: (0, e)),
                  pl.BlockSpec((tm, te), lambda i, e: (i, e))],
        out_specs=pl.BlockSpec((tm, d), lambda i, e: (i, 0)),
        out_shape=jax.ShapeDtypeStruct((s, d), F32),
        compiler_params=_params(("parallel", "arbitrary")),
        name="peer_dense",
    )(h8, hs, u8, us, v8, vs, wdense)


def _add_rmsnorm_kernel(x_ref, y_ref, g_ref, o_ref):
    x = x_ref[...] + y_ref[...]
    ms = jnp.mean(x * x, axis=-1, keepdims=True)
    o_ref[...] = x * lax.rsqrt(ms + NORM_EPS) * g_ref[...]


def _add_rmsnorm(x, y, g, tm=256):
    s, d = x.shape
    blk = pl.BlockSpec((tm, d), lambda i: (i, 0))
    return pl.pallas_call(
        _add_rmsnorm_kernel,
        grid=(s // tm,),
        in_specs=[blk, blk, pl.BlockSpec((1, d), lambda i: (0, 0))],
        out_specs=blk,
        out_shape=jax.ShapeDtypeStruct((s, d), F32),
        compiler_params=_params(("parallel",)),
        name="residual_final_norm",
    )(x, y, g.reshape(1, d))


def _rwkv_branch(proj_r, w0, w_up, a0, a_up, g_up, k_k, k_a, r_k, lnx_g, lnx_b):
    g_up_pad = jnp.pad(g_up, ((0, GATE_LORA_PAD - GATE_LORA), (0, 0))).astype(BF16)
    wpre, a, g = _rwkv_lora(proj_r, w0, w_up.astype(BF16), a0, a_up.astype(BF16), g_up_pad)
    rt, at, kh, bh, kb, bb, pc, bonus = _rwkv_prep(proj_r, wpre, a, k_k, k_a, r_k.reshape(-1))
    x, y, opre, qb2, bbt, gst, pcm = _rwkv_chunk(rt, at, kh, bh, kb, bb, pc, proj_r)
    return _rwkv_scan(x, y, rt, opre, qb2, bbt, gst, pcm, bonus, g, lnx_g, lnx_b)


def _rope_tables(s):
    half = MOBA_HEAD_DIM // 2
    inv_freq = ROPE_THETA ** (-jnp.arange(half, dtype=F32) / half)
    ang = jnp.arange(s, dtype=jnp.int32).astype(F32)[:, None] * inv_freq[None, :]
    cos = jnp.cos(ang)
    sin = jnp.sin(ang)
    return jnp.concatenate([cos, cos], axis=-1), jnp.concatenate([-sin, sin], axis=-1)


def _moba_branch(proj_m):
    s = proj_m.shape[0]
    cos, sin = _rope_tables(s)
    q, qs, k, vt, kmean = _moba_prep(proj_m, cos, sin)
    return _moba_attn(q, qs, k, vt, kmean.reshape(kmean.shape[0], kmean.shape[2]))


def _peer_layer(x1, norm2_g, w_q, sub_keys, expert_u, expert_v, final_g):
    s, d = x1.shape
    h2, h8, hs = _rmsnorm_fp8(x1, norm2_g)
    q = _matmul(h2, w_q.astype(BF16), F32, tm=min(1024, s), tn=512, name="peer_query")
    keys = sub_keys.reshape(PEER_HEADS * 2 * PEER_N_KEYS, PEER_HALF).astype(BF16)
    e1, e2, g = _peer_route(q, keys)
    wdense = _peer_expand(e1, e2, g)
    u8, us = _quant_rows(expert_u)
    v8, vs = _quant_rows(expert_v)
    peer = _peer_dense(h8, hs, u8, us, v8, vs, wdense)
    return _add_rmsnorm(x1, peer, final_g)


def kernel(x, norm1_g, w_in, rwkv_mu, rwkv_w0, rwkv_w_up, rwkv_a0, rwkv_a_up, rwkv_g_up, rwkv_k_k, rwkv_k_a, rwkv_r_k, rwkv_lnx_g, rwkv_lnx_b, w_branch_rwkv, w_branch_moba, w_out, norm2_g, peer_w_q, peer_sub_keys, peer_u, peer_v, final_g):
    b, s, d = x.shape
    depth = w_in.shape[0]
    assert b == 1 and depth == 1
    x2d = x.reshape(s, d)
    l = 0
    rw = rwkv_w0.shape[-1]
    shift_w = 3 * rw + DECAY_LORA + AAA_LORA + GATE_LORA
    tn = 512
    shift_pad = -(-shift_w // tn) * tn
    mw = w_branch_moba.shape[1]
    tm = min(1024, s)

    h = _rmsnorm(x2d, norm1_g[l], BF16)
    wt = jnp.transpose(w_in[l])
    mu = jnp.pad(rwkv_mu[l], (0, shift_pad - shift_w)).reshape(1, shift_pad)
    proj_r = _matmul_shift(h, wt, mu, tm=tm, tn=tn)
    wt_mg = _repack_rows(wt, shift_w, 3 * mw + 2 * d)
    proj_m = _matmul_nt(h, wt_mg, F32, tm=tm, tn=tn, name="in_proj_moba", row0=0, n=3 * mw)
    gates = _matmul_nt(h, wt_mg, F32, tm=tm, tn=tn, name="in_proj_gates", row0=3 * mw, n=2 * d)

    y_a = _rwkv_branch(proj_r, rwkv_w0[l], rwkv_w_up[l], rwkv_a0[l], rwkv_a_up[l], rwkv_g_up[l],
                       rwkv_k_k[l], rwkv_k_a[l], rwkv_r_k[l], rwkv_lnx_g[l], rwkv_lnx_b[l])
    y_b = _moba_branch(proj_m)
    mixed = _gated_pair(y_a, w_branch_rwkv[l].astype(BF16), y_b, w_branch_moba[l].astype(BF16),
                        gates, tm=tm, tn=512)
    x1 = _matmul_residual(mixed, w_out[l].astype(BF16), x2d, tm=tm, tn=512)
    out = _peer_layer(x1, norm2_g[l], peer_w_q[l], peer_sub_keys[l], peer_u[l], peer_v[l], final_g)
    return out.reshape(b, s, d)
```

```python
import functools

import jax
import jax.numpy as jnp
from jax import lax
from jax.experimental import pallas as pl
from jax.experimental.pallas import tpu as pltpu

F32 = jnp.float32
BF16 = jnp.bfloat16

NORM_EPS = 1e-6
LANES = 128
RWKV_HEAD_DIM = 64
DECAY_LORA = 128
AAA_LORA = 128
GATE_LORA = 480
GATE_LORA_PAD = 512
LN_X_EPS = 64e-5
CHUNK = 128
MOBA_HEAD_DIM = 128
MOBA_BLOCK = 256
MOBA_TOPK = 3
ROPE_THETA = 10000.0
PEER_HEADS = 8
PEER_N_KEYS = 128
PEER_HALF = 128
PEER_TOPK = 16

VMEM_LIMIT = 56 * 1024 * 1024

_NT = (((1,), (1,)), ((), ()))


def _params(sem, vmem=VMEM_LIMIT):
    return pltpu.CompilerParams(dimension_semantics=sem, vmem_limit_bytes=vmem)


def _dot(a, b):
    return jnp.dot(a, b, preferred_element_type=F32)


def _dot_nt(a, b):
    return lax.dot_general(a, b, _NT, preferred_element_type=F32)


def _split3(x):
    hi = x.astype(BF16)
    r1 = x - hi.astype(F32)
    mid = r1.astype(BF16)
    lo = (r1 - mid.astype(F32)).astype(BF16)
    return hi, mid, lo


def _dot_l3(l_bf16, x):
    hi, mid, lo = _split3(x)
    return _dot(l_bf16, hi) + _dot(l_bf16, mid) + _dot(l_bf16, lo)


def _dot_r3(x, r_bf16):
    hi, mid, lo = _split3(x)
    return _dot(hi, r_bf16) + _dot(mid, r_bf16) + _dot(lo, r_bf16)


def _rmsnorm_kernel(x_ref, g_ref, o_ref):
    x = x_ref[...]
    ms = jnp.mean(x * x, axis=-1, keepdims=True)
    o_ref[...] = (x * lax.rsqrt(ms + NORM_EPS) * g_ref[...]).astype(o_ref.dtype)


def _rmsnorm(x, g, out_dtype, tm=256):
    s, d = x.shape
    return pl.pallas_call(
        _rmsnorm_kernel,
        grid=(s // tm,),
        in_specs=[pl.BlockSpec((tm, d), lambda i: (i, 0)),
                  pl.BlockSpec((1, d), lambda i: (0, 0))],
        out_specs=pl.BlockSpec((tm, d), lambda i: (i, 0)),
        out_shape=jax.ShapeDtypeStruct((s, d), out_dtype),
        compiler_params=_params(("parallel",)),
        name="rmsnorm",
    )(x, g.reshape(1, d))


def _mm_kernel(a_ref, b_ref, o_ref):
    o_ref[...] = _dot(a_ref[...], b_ref[...]).astype(o_ref.dtype)


def _matmul(a, b, out_dtype, tm, tn, name, col0=0, n=None):
    m, k = a.shape
    n = b.shape[1] if n is None else n
    off = col0 // tn
    return pl.pallas_call(
        _mm_kernel,
        grid=(m // tm, n // tn),
        in_specs=[pl.BlockSpec((tm, k), lambda i, j: (i, 0)),
                  pl.BlockSpec((k, tn), lambda i, j: (0, j + off))],
        out_specs=pl.BlockSpec((tm, tn), lambda i, j: (i, j)),
        out_shape=jax.ShapeDtypeStruct((m, n), out_dtype),
        compiler_params=_params(("parallel", "parallel")),
        name=name,
    )(a, b)


BF16_SUBLANES = 16


def _mm_nt_kernel(a_ref, bt_ref, o_ref):
    o_ref[...] = _dot_nt(a_ref[...], bt_ref[...]).astype(o_ref.dtype)


def _matmul_nt(a, bt, out_dtype, tm, tn, name, row0=0, n=None):
    m, k = a.shape
    n = bt.shape[0] if n is None else n
    off = row0 // tn
    return pl.pallas_call(
        _mm_nt_kernel,
        grid=(m // tm, n // tn),
        in_specs=[pl.BlockSpec((tm, k), lambda i, j: (i, 0)),
                  pl.BlockSpec((tn, k), lambda i, j: (j + off, 0))],
        out_specs=pl.BlockSpec((tm, tn), lambda i, j: (i, j)),
        out_shape=jax.ShapeDtypeStruct((m, n), out_dtype),
        compiler_params=_params(("parallel", "parallel")),
        name=name,
    )(a, bt)


def _mm_shift_kernel(a_ref, ap_ref, bt_ref, mu_ref, o_ref):
    i = pl.program_id(0)
    b = bt_ref[...].astype(BF16)
    z = _dot_nt(a_ref[...], b)
    zp = _dot_nt(ap_ref[...], b)
    prev = jnp.where(i > 0, zp[BF16_SUBLANES - 1:BF16_SUBLANES, :], 0.0)
    row = lax.broadcasted_iota(jnp.int32, z.shape, 0)
    z_prev = jnp.where(row == 0, prev, pltpu.roll(z, 1, axis=0))
    o_ref[...] = z + (z_prev - z) * mu_ref[...]


def _matmul_shift(a, bt, mu, tm, tn):
    m, k = a.shape
    n = mu.shape[1]
    per = tm // BF16_SUBLANES
    return pl.pallas_call(
        _mm_shift_kernel,
        grid=(m // tm, n // tn),
        in_specs=[pl.BlockSpec((tm, k), lambda i, j: (i, 0)),
                  pl.BlockSpec((BF16_SUBLANES, k), lambda i, j: (jnp.maximum(i * per - 1, 0), 0)),
                  pl.BlockSpec((tn, k), lambda i, j: (j, 0)),
                  pl.BlockSpec((1, tn), lambda i, j: (0, j))],
        out_specs=pl.BlockSpec((tm, tn), lambda i, j: (i, j)),
        out_shape=jax.ShapeDtypeStruct((m, n), F32),
        compiler_params=_params(("parallel", "parallel")),
        name="in_proj_rwkv",
    )(a, a, bt, mu)


def _repack_kernel(a_ref, b_ref, o_ref, *, off):
    o_ref[...] = jnp.concatenate([a_ref[off:, :], b_ref[:off, :]], axis=0).astype(o_ref.dtype)


def _repack_rows(wt, row0, n, tr=512, tk=1024):
    k = wt.shape[1]
    first, off = divmod(row0, tr)
    assert off % BF16_SUBLANES == 0
    return pl.pallas_call(
        functools.partial(_repack_kernel, off=off),
        grid=(n // tr, k // tk),
        in_specs=[pl.BlockSpec((tr, tk), lambda i, j: (i + first, j)),
                  pl.BlockSpec((tr, tk), lambda i, j: (i + first + 1, j))],
        out_specs=pl.BlockSpec((tr, tk), lambda i, j: (i, j)),
        out_shape=jax.ShapeDtypeStruct((n, k), BF16),
        compiler_params=_params(("parallel", "parallel")),
        name="repack_w_in",
    )(wt, wt)


def _gated_pair_kernel(ya_ref, wa_ref, yb_ref, wb_ref, ga_ref, gb_ref, o_ref):
    pa = _dot(ya_ref[...], wa_ref[...])
    pb = _dot(yb_ref[...], wb_ref[...])
    mixed = jax.nn.sigmoid(ga_ref[...]) * pa + jax.nn.sigmoid(gb_ref[...]) * pb
    o_ref[...] = mixed.astype(o_ref.dtype)


def _gated_pair(ya, wa, yb, wb, gates, tm, tn):
    m, k = ya.shape
    n = wa.shape[1]
    nb = n // tn
    return pl.pallas_call(
        _gated_pair_kernel,
        grid=(m // tm, nb),
        in_specs=[pl.BlockSpec((tm, k), lambda i, j: (i, 0)),
                  pl.BlockSpec((k, tn), lambda i, j: (0, j)),
                  pl.BlockSpec((tm, k), lambda i, j: (i, 0)),
                  pl.BlockSpec((k, tn), lambda i, j: (0, j)),
                  pl.BlockSpec((tm, tn), lambda i, j: (i, j)),
                  pl.BlockSpec((tm, tn), lambda i, j: (i, j + nb))],
        out_specs=pl.BlockSpec((tm, tn), lambda i, j: (i, j)),
        out_shape=jax.ShapeDtypeStruct((m, n), BF16),
        compiler_params=_params(("parallel", "parallel")),
        name="branch_merge",
    )(ya, wa, yb, wb, gates, gates)


def _mm_res_kernel(a_ref, b_ref, r_ref, o_ref):
    o_ref[...] = r_ref[...] + _dot(a_ref[...], b_ref[...])


def _matmul_residual(a, b, res, tm, tn):
    m, k = a.shape
    n = b.shape[1]
    return pl.pallas_call(
        _mm_res_kernel,
        grid=(m // tm, n // tn),
        in_specs=[pl.BlockSpec((tm, k), lambda i, j: (i, 0)),
                  pl.BlockSpec((k, tn), lambda i, j: (0, j)),
                  pl.BlockSpec((tm, tn), lambda i, j: (i, j))],
        out_specs=pl.BlockSpec((tm, tn), lambda i, j: (i, j)),
        out_shape=jax.ShapeDtypeStruct((m, n), F32),
        compiler_params=_params(("parallel", "parallel")),
        name="out_proj",
    )(a, b, res)


def _rwkv_lora_kernel(wl_ref, al_ref, gl0_ref, gl1_ref, w0_ref, wup_ref, a0_ref, aup_ref,
                      gup_ref, wpre_ref, a_ref, g_ref):
    half = GATE_LORA_PAD // 2
    wpre_ref[...] = w0_ref[...] + _dot(jnp.tanh(wl_ref[...]).astype(BF16), wup_ref[...])
    a_ref[...] = jax.nn.sigmoid(a0_ref[...] + _dot(al_ref[...].astype(BF16), aup_ref[...]))
    g_ref[...] = (_dot(jax.nn.sigmoid(gl0_ref[...]).astype(BF16), gup_ref[:half, :])
                  + _dot(jax.nn.sigmoid(gl1_ref[...]).astype(BF16), gup_ref[half:, :]))


def _rwkv_lora(proj_r, w0, w_up, a0, a_up, g_up_pad, tm=256):
    s = proj_r.shape[0]
    w = w0.shape[-1]
    half = GATE_LORA_PAD // 2
    c_wl = 3 * w // DECAY_LORA
    c_al = (3 * w + DECAY_LORA) // AAA_LORA
    c_gl = (3 * w + DECAY_LORA + AAA_LORA) // half
    row = lambda i: (i, 0)
    const = lambda i: (0, 0)
    out = jax.ShapeDtypeStruct((s, w), F32)
    return pl.pallas_call(
        _rwkv_lora_kernel,
        grid=(s // tm,),
        in_specs=[pl.BlockSpec((tm, DECAY_LORA), lambda i: (i, c_wl)),
                  pl.BlockSpec((tm, AAA_LORA), lambda i: (i, c_al)),
                  pl.BlockSpec((tm, half), lambda i: (i, c_gl)),
                  pl.BlockSpec((tm, half), lambda i: (i, c_gl + 1)),
                  pl.BlockSpec((1, w), const),
                  pl.BlockSpec((DECAY_LORA, w), const),
                  pl.BlockSpec((1, w), const),
                  pl.BlockSpec((AAA_LORA, w), const),
                  pl.BlockSpec((GATE_LORA_PAD, w), const)],
        out_specs=[pl.BlockSpec((tm, w), row)] * 3,
        out_shape=[out, out, out],
        compiler_params=_params(("parallel",)),
        name="rwkv_lora",
    )(proj_r, proj_r, proj_r, proj_r, w0.reshape(1, w), w_up, a0.reshape(1, w), a_up, g_up_pad)


def _head_pair_ones(width=LANES):
    r = lax.broadcasted_iota(jnp.int32, (width, width), 0) // RWKV_HEAD_DIM
    c = lax.broadcasted_iota(jnp.int32, (width, width), 1) // RWKV_HEAD_DIM
    return r == c


def _rwkv_prep_kernel(r_ref, k_ref, v_ref, wpre_ref, a_ref, kk_ref, ka_ref, rk_ref,
                      rt_ref, at_ref, kh_ref, bh_ref, kb_ref, bb_ref, pc_ref, bonus_ref):
    tm = r_ref.shape[0]
    r = r_ref[...]
    k = k_ref[...]
    v = v_ref[...]
    a = a_ref[...]
    same_head = jnp.where(_head_pair_ones(r.shape[1]), 1.0, 0.0).astype(BF16)

    x = -wpre_ref[...]
    softplus = jnp.maximum(x, 0.0) + jnp.log1p(jnp.exp(-jnp.abs(x)))
    lw = -jnp.exp(-softplus - 0.5)

    kk = k * kk_ref[...]
    ss = _dot_r3(kk * kk, same_head)
    kkn = kk / jnp.maximum(jnp.sqrt(ss), 1e-12)
    k2 = k * (1.0 + (a - 1.0) * ka_ref[...])
    bonus_ref[...] = _dot_r3(r * k2 * rk_ref[...], same_head) * v

    ri = lax.broadcasted_iota(jnp.int32, (tm, tm), 0)
    ci = lax.broadcasted_iota(jnp.int32, (tm, tm), 1)
    same_chunk = (ri // CHUNK) == (ci // CHUNK)
    tri = jnp.where(same_chunk & (ci <= ri), 1.0, 0.0).astype(BF16)
    allc = jnp.where(same_chunk, 1.0, 0.0).astype(BF16)
    cum = _dot_l3(tri, lw)
    tot = _dot_l3(allc, lw)

    p_in = jnp.exp(cum)
    p_inv = jnp.exp(-cum)
    p_tail = jnp.exp(tot - cum)
    b = kkn * a
    rt_ref[...] = r * p_in
    at_ref[...] = -kkn * jnp.exp(cum - lw)
    kh_ref[...] = k2 * p_inv
    bh_ref[...] = b * p_inv
    kb_ref[...] = k2 * p_tail
    bb_ref[...] = b * p_tail
    pc_ref[...] = jnp.exp(tot)


def _rwkv_prep(proj_r, wpre, a, k_k, k_a, r_k, tm=256, tw=512):
    s, w = wpre.shape
    ncol = w // tw
    blk = lambda off: pl.BlockSpec((tm, tw), lambda i, j, off=off: (i, j + off))
    vec = pl.BlockSpec((1, tw), lambda i, j: (0, j))
    out = jax.ShapeDtypeStruct((s, w), F32)
    return pl.pallas_call(
        _rwkv_prep_kernel,
        grid=(s // tm, ncol),
        in_specs=[blk(0), blk(ncol), blk(2 * ncol), blk(0), blk(0), vec, vec, vec],
        out_specs=[blk(0)] * 8,
        out_shape=[out] * 8,
        compiler_params=_params(("parallel", "parallel")),
        name="rwkv_prep",
    )(proj_r, proj_r, proj_r, wpre, a, k_k.reshape(1, w), k_a.reshape(1, w), r_k.reshape(1, w))


def _rwkv_chunk_kernel(rt_ref, at_ref, kh_ref, bh_ref, kb_ref, bb_ref, pc_ref, v_ref,
                       x_ref, y_ref, op_ref, qb_ref, bbt_ref, g_ref, pcm_ref):
    c = CHUNK
    nheads = LANES // RWKV_HEAD_DIM
    chunks = range(rt_ref.shape[0] // c)
    pairs = range(rt_ref.shape[1] // LANES)
    rows = [slice(q * c, (q + 1) * c) for q in chunks]
    lanes = [slice(p * LANES, (p + 1) * LANES) for p in pairs]
    tiles = [(q, p) for q in chunks for p in pairs]
    probs = [(n, h) for n in range(len(tiles)) for h in range(nheads)]
    lane = lax.broadcasted_iota(jnp.int32, (c, LANES), 1)
    ri = lax.broadcasted_iota(jnp.int32, (c, c), 0)
    ci = lax.broadcasted_iota(jnp.int32, (c, c), 1)
    strict = ci < ri
    incl = ci <= ri
    eye = jnp.where(ri == ci, 1.0, 0.0)
    head_mask = [(lane // RWKV_HEAD_DIM) == h for h in range(nheads)]
    tile = lambda ref, n: ref[rows[tiles[n][0]], lanes[tiles[n][1]]]

    kh = [tile(kh_ref, n).astype(BF16) for n in range(len(tiles))]
    bh = [tile(bh_ref, n).astype(BF16) for n in range(len(tiles))]
    a_h = [jnp.where(head_mask[h], tile(at_ref, n), 0.0).astype(BF16) for n, h in probs]
    r_h = [jnp.where(head_mask[h], tile(rt_ref, n), 0.0).astype(BF16) for n, h in probs]
    v_h = [jnp.where(head_mask[h], tile(v_ref, n), 0.0).astype(BF16) for n, h in probs]
    n_ab = [jnp.where(strict, _dot_nt(a, bh[n]), 0.0) for a, (n, _) in zip(a_h, probs)]
    n_ak = [jnp.where(strict, _dot_nt(a, kh[n]), 0.0).astype(BF16) for a, (n, _) in zip(a_h, probs)]
    q_k = [jnp.where(incl, _dot_nt(r, kh[n]), 0.0).astype(BF16) for r, (n, _) in zip(r_h, probs)]
    q_b = [jnp.where(incl, _dot_nt(r, bh[n]), 0.0) for r, (n, _) in zip(r_h, probs)]
    for qb, (n, h) in zip(q_b, probs):
        q, p = tiles[n]
        col = (p * nheads + h) * c
        qb_ref[rows[q], col:col + c] = qb.astype(qb_ref.dtype)

    t = [eye + n for n in n_ab]
    pw = n_ab
    for _ in range((c - 1).bit_length() - 1):
        pb = [x.astype(BF16) for x in pw]
        pw = [_dot(x, x) for x in pb]
        t = [ti + _dot(ti.astype(BF16), pi.astype(BF16)) for ti, pi in zip(t, pw)]
    tb = [ti.astype(BF16) for ti in t]
    xs = [_dot(ti, a) for ti, a in zip(tb, a_h)]
    nv = [_dot(n, v).astype(BF16) for n, v in zip(n_ak, v_h)]
    ys = [_dot(ti, z) for ti, z in zip(tb, nv)]
    os = [_dot(qk, v) for qk, v in zip(q_k, v_h)]
    same_head = _head_pair_ones()
    for n, (q, p) in enumerate(tiles):
        mine = [m for m, (nn, _) in enumerate(probs) if nn == n]
        x_ref[rows[q], lanes[p]] = sum(xs[m] for m in mine).astype(x_ref.dtype)
        y_ref[rows[q], lanes[p]] = sum(ys[m] for m in mine)
        op_ref[rows[q], lanes[p]] = sum(os[m] for m in mine)
        kbt = tile(kb_ref, n).T.astype(BF16)
        g_ref[rows[q], lanes[p]] = jnp.where(same_head, _dot(kbt, tile(v_ref, n).astype(BF16)), 0.0)
        bbt_ref[lanes[p], rows[q]] = tile(bb_ref, n).T.astype(bbt_ref.dtype)
        pcm_ref[rows[q], lanes[p]] = tile(pc_ref, n).T


def _rwkv_chunk(rt, at, kh, bh, kb, bb, pc, proj_r, tc=2 * CHUNK, tw=2 * LANES):
    s, w = rt.shape
    ncol = w // tw
    c = CHUNK
    blk = lambda off: pl.BlockSpec((tc, tw), lambda i, j, off=off: (i, j + off))
    f32o = jax.ShapeDtypeStruct((s, w), F32)
    return pl.pallas_call(
        _rwkv_chunk_kernel,
        grid=(s // tc, ncol),
        in_specs=[blk(0)] * 7 + [blk(2 * ncol)],
        out_specs=[blk(0), blk(0), blk(0),
                   pl.BlockSpec((tc, 2 * tw), lambda i, j: (i, j)),
                   pl.BlockSpec((tw, tc), lambda i, j: (j, i)),
                   blk(0), blk(0)],
        out_shape=[jax.ShapeDtypeStruct((s, w), BF16), f32o, f32o,
                   jax.ShapeDtypeStruct((s, 2 * w), BF16),
                   jax.ShapeDtypeStruct((w, s), BF16),
                   f32o, f32o],
        compiler_params=_params(("parallel", "parallel")),
        name="rwkv_chunk",
    )(rt, at, kh, bh, kb, bb, pc, proj_r)


def _rwkv_scan_kernel(x_ref, y_ref, rt_ref, op_ref, qb_ref, bbt_ref, g_ref, pcm_ref,
                      bonus_ref, gate_ref, lng_ref, lnb_ref, ya_ref, h_ref):
    c = CHUNK
    pairs = range(x_ref.shape[1] // LANES)
    lanes = [slice(p * LANES, (p + 1) * LANES) for p in pairs]

    @pl.when(pl.program_id(1) == 0)
    def _():
        h_ref[...] = jnp.zeros_like(h_ref)

    lane = lax.broadcasted_iota(jnp.int32, (c, LANES), 1)
    first = lane < RWKV_HEAD_DIM
    same_head = _head_pair_ones()
    ones_head = jnp.where(same_head, 1.0, 0.0).astype(BF16)
    inv_n = 1.0 / RWKV_HEAD_DIM
    for q in range(x_ref.shape[0] // c):
        rows = slice(q * c, (q + 1) * c)
        hs = [h_ref[p] for p in pairs]
        hb = [h.astype(BF16) for h in hs]
        us = [_dot(x_ref[rows, lanes[p]], hb[p]) + y_ref[rows, lanes[p]] for p in pairs]
        u2 = [jnp.concatenate([jnp.where(first, u, 0.0), jnp.where(first, 0.0, u)], axis=0).astype(BF16)
              for u in us]
        os = [_dot(rt_ref[rows, lanes[p]].astype(BF16), hb[p]) + op_ref[rows, lanes[p]]
              + _dot(qb_ref[rows, p * 2 * c:(p + 1) * 2 * c], u2[p]) for p in pairs]
        upd = [_dot(bbt_ref[lanes[p], rows], us[p].astype(BF16)) for p in pairs]
        for p in pairs:
            h_ref[p] = (pcm_ref[rows, lanes[p]] * hs[p] + g_ref[rows, lanes[p]]
                        + jnp.where(same_head, upd[p], 0.0))
        mus = [_dot_r3(o, ones_head) * inv_n for o in os]
        ds = [o - mu for o, mu in zip(os, mus)]
        vs = [_dot_r3(d * d, ones_head) * inv_n for d in ds]
        for p in pairs:
            y = ds[p] * lax.rsqrt(vs[p] + LN_X_EPS) * lng_ref[:, lanes[p]] + lnb_ref[:, lanes[p]]
            ya_ref[rows, lanes[p]] = ((y + bonus_ref[rows, lanes[p]])
                                      * gate_ref[rows, lanes[p]]).astype(ya_ref.dtype)


def _rwkv_scan(x, y, rt, opre, qb2, bbt, g, pcm, bonus, gate, lnx_g, lnx_b, ts=512, tw=4 * LANES):
    s, w = y.shape
    ts = min(ts, s)
    blk = pl.BlockSpec((ts, tw), lambda j, i: (i, j))
    vec = pl.BlockSpec((1, tw), lambda j, i: (0, j))
    return pl.pallas_call(
        _rwkv_scan_kernel,
        grid=(w // tw, s // ts),
        in_specs=[blk, blk, blk, blk,
                  pl.BlockSpec((ts, 2 * tw), lambda j, i: (i, j)),
                  pl.BlockSpec((tw, ts), lambda j, i: (j, i)),
                  blk, blk, blk, blk, vec, vec],
        out_specs=blk,
        out_shape=jax.ShapeDtypeStruct((s, w), BF16),
        scratch_shapes=[pltpu.VMEM((tw // LANES, LANES, LANES), F32)],
        compiler_params=_params(("parallel", "arbitrary")),
        name="rwkv_scan",
    )(x, y, rt, opre, qb2, bbt, g, pcm, bonus, gate, lnx_g.reshape(1, w), lnx_b.reshape(1, w))


V_ONES_ROWS = 16


LOG2_E = 1.4426950408889634
MOBA_LOG2_SCALE = (MOBA_HEAD_DIM ** -0.5) * LOG2_E


def _moba_prep_kernel(q_ref, k_ref, v_ref, cos_ref, sin_ref, qo_ref, qs_ref, ko_ref, vt_ref, km_ref):
    cos = cos_ref[...]
    sin = sin_ref[...]
    hd = MOBA_HEAD_DIM
    half = hd // 2
    tb = q_ref.shape[0]
    for h in range(q_ref.shape[1] // hd):
        cols = slice(h * hd, (h + 1) * hd)
        q = q_ref[:, cols]
        k = k_ref[:, cols]
        qr = q * cos + pltpu.roll(q, half, axis=1) * sin
        kr = k * cos + pltpu.roll(k, half, axis=1) * sin
        qo_ref[:, cols] = qr.astype(qo_ref.dtype)
        qs_ref[:, cols] = (qr * MOBA_LOG2_SCALE).astype(qs_ref.dtype)
        ko_ref[:, cols] = kr.astype(ko_ref.dtype)
        km_ref[0, :, cols] = jnp.mean(kr, axis=0, keepdims=True)
        vt_ref[h, 0, :hd, :] = v_ref[:, cols].T.astype(vt_ref.dtype)
        vt_ref[h, 0, hd:, :] = jnp.ones((V_ONES_ROWS, tb), vt_ref.dtype)


def _moba_prep(proj_m, cos, sin):
    s = proj_m.shape[0]
    w = proj_m.shape[1] // 3
    tb = MOBA_BLOCK
    nb = s // tb
    nh = w // MOBA_HEAD_DIM
    vrows = MOBA_HEAD_DIM + V_ONES_ROWS
    blk = lambda off: pl.BlockSpec((tb, w), lambda i, off=off: (i, off))
    tab = pl.BlockSpec((tb, MOBA_HEAD_DIM), lambda i: (i, 0))
    bo = jax.ShapeDtypeStruct((s, w), BF16)
    return pl.pallas_call(
        _moba_prep_kernel,
        grid=(nb,),
        in_specs=[blk(0), blk(1), blk(2), tab, tab],
        out_specs=[blk(0), blk(0), blk(0),
                   pl.BlockSpec((nh, 1, vrows, tb), lambda i: (0, i, 0, 0)),
                   pl.BlockSpec((1, 1, w), lambda i: (i, 0, 0))],
        out_shape=[bo, bo, bo, jax.ShapeDtypeStruct((nh, nb, vrows, tb), BF16),
                   jax.ShapeDtypeStruct((nb, 1, w), F32)],
        compiler_params=_params(("parallel",)),
        name="moba_prep",
    )(proj_m, proj_m, proj_m, cos, sin)


MOBA_CHAINS = 4
MOBA_GROUPS = 2


def _moba_attn_kernel(q_ref, qs_ref, k_ref, vt_ref, km_ref, o_ref, bias_ref):
    i = pl.program_id(1)
    tb = MOBA_BLOCK
    nb = km_ref.shape[0]
    hd = MOBA_HEAD_DIM
    q = qs_ref[...]

    own = pl.multiple_of(i * tb, tb)
    s = _dot_nt(k_ref[pl.ds(own, tb), :], q)
    raw0 = [_dot_nt(k_ref[g * tb:(g + 1) * tb, :], q) for g in range(MOBA_CHAINS)]

    gate = _dot_nt(km_ref[...].astype(BF16), q_ref[...])
    rid = lax.broadcasted_iota(jnp.int32, gate.shape, 0).astype(F32)
    gate = jnp.where(rid < i.astype(F32), gate, -jnp.inf)
    bias = jnp.full(gate.shape, -jnp.inf, F32)
    for _ in range(MOBA_TOPK):
        m = jnp.max(gate, axis=0, keepdims=True)
        first = jnp.min(jnp.where(gate == m, rid, float(nb)), axis=0, keepdims=True)
        pick = (rid == first) & (m > -jnp.inf)
        bias = jnp.where(pick, 0.0, bias)
        gate = jnp.where(pick, -jnp.inf, gate)
    bias_ref[...] = bias

    ss0 = [r.astype(BF16) + bias[g:g + 1, :].astype(BF16) for g, r in enumerate(raw0)]

    ki = lax.broadcasted_iota(jnp.int32, s.shape, 0)
    qi = lax.broadcasted_iota(jnp.int32, s.shape, 1)
    s = jnp.where(ki <= qi, s, -jnp.inf).astype(BF16)
    m0 = jnp.max(s, axis=0, keepdims=True)
    acc0 = _dot(vt_ref[i], jnp.exp2(s - m0))
    m0 = m0.astype(F32)

    def blocks(grp):
        return [jnp.minimum(grp * MOBA_CHAINS + g, nb - 1) for g in range(MOBA_CHAINS)]

    def scores(grp):
        return [_dot_nt(k_ref[pl.ds(pl.multiple_of(j * tb, tb), tb), :], q).astype(BF16)
                + bias_ref[pl.ds(j, 1), :].astype(BF16) for j in blocks(grp)]

    def update(chains, ss, grp):
        ms = [jnp.maximum(m, jnp.max(sj, axis=0, keepdims=True).astype(F32))
              for (m, _), sj in zip(chains, ss)]
        ps = [jnp.exp2(sj - mn.astype(BF16)) for sj, mn in zip(ss, ms)]
        return tuple((mn, jnp.exp2(m - mn) * acc + _dot(vt_ref[j], pj))
                     for (m, acc), mn, pj, j in zip(chains, ms, ps, blocks(grp)))

    def body(t, carry):
        chains, ss = carry
        first = t * MOBA_GROUPS
        later = [scores(first + d) for d in range(1, MOBA_GROUPS)]
        ss_next = scores(first + MOBA_GROUPS)
        for d, sd in enumerate([ss] + later):
            chains = update(chains, sd, first + d)
        return chains, ss_next

    per_trip = MOBA_CHAINS * MOBA_GROUPS
    init = ((m0, acc0),) + ((m0, jnp.zeros_like(acc0)),) * (MOBA_CHAINS - 1)
    chains, _ = lax.fori_loop(0, (i + per_trip - 1) // per_trip, body, (init, ss0))
    m = chains[0][0]
    for mg, _ in chains[1:]:
        m = jnp.maximum(m, mg)
    acc = sum(jnp.exp2(mg - m) * ag for mg, ag in chains)
    out = acc[:hd, :] / acc[hd:hd + 1, :]
    o_ref[...] = out.T.astype(o_ref.dtype)


def _moba_attn(q, qs, k, vt, kmean):
    s, w = q.shape
    tb = MOBA_BLOCK
    nb = s // tb
    hd = MOBA_HEAD_DIM
    vrows = vt.shape[2]
    return pl.pallas_call(
        _moba_attn_kernel,
        grid=(w // hd, nb),
        in_specs=[pl.BlockSpec((tb, hd), lambda h, i: (i, h)),
                  pl.BlockSpec((tb, hd), lambda h, i: (i, h)),
                  pl.BlockSpec((s, hd), lambda h, i: (0, h)),
                  pl.BlockSpec((None, nb, vrows, tb), lambda h, i: (h, 0, 0, 0)),
                  pl.BlockSpec((nb, hd), lambda h, i: (0, h))],
        out_specs=pl.BlockSpec((tb, hd), lambda h, i: (i, h)),
        out_shape=jax.ShapeDtypeStruct((s, w), BF16),
        scratch_shapes=[pltpu.VMEM((nb, tb), F32)],
        compiler_params=_params(("parallel", "arbitrary")),
        name="moba_attn",
    )(q, qs, k, vt, kmean)


def _top_rows(x, n):
    rows = x.shape[0]
    rid = lax.broadcasted_iota(jnp.int32, x.shape, 0).astype(F32)
    vals, idxs = [], []
    for _ in range(n):
        m = jnp.max(x, axis=0, keepdims=True)
        first = jnp.min(jnp.where(x == m, rid, float(rows)), axis=0, keepdims=True)
        vals.append(m)
        idxs.append(first)
        x = jnp.where(rid == first, -jnp.inf, x)
    return jnp.concatenate(vals, axis=0), jnp.concatenate(idxs, axis=0)


def _peer_route_kernel(q_ref, keys_ref, e1_ref, e2_ref, g_ref):
    n = PEER_TOPK
    tm = q_ref.shape[0]
    e1s, e2s, gs = [], [], []
    for h in range(PEER_HEADS):
        tops = []
        for p in range(2):
            hp = 2 * h + p
            cols = slice(hp * PEER_HALF, (hp + 1) * PEER_HALF)
            rows = slice(hp * PEER_N_KEYS, (hp + 1) * PEER_N_KEYS)
            st = _dot_nt(keys_ref[rows, :], q_ref[:, cols].astype(BF16))
            tops.append(_top_rows(st, n))
        (s1, i1), (s2, i2) = tops
        hn = n // 2
        cand = jnp.concatenate([s1[0:1, :] + s2]
                               + [s1[a:a + 1, :] + s2[:hn, :] for a in range(1, hn)]
                               + [s1[hn:, :] + s2[0:1, :]], axis=0)
        f_s, f_pos = _top_rows(cand, n)
        mid = jnp.floor((f_pos - n) * (1.0 / hn))
        tail0 = float(n + (hn - 1) * hn)
        pa = jnp.where(f_pos < n, 0.0, jnp.where(f_pos < tail0, 1.0 + mid, f_pos - tail0 + hn))
        pb = jnp.where(f_pos < n, f_pos, jnp.where(f_pos < tail0, f_pos - n - mid * hn, 0.0))
        e1 = jnp.zeros((n, tm), F32)
        e2 = jnp.zeros((n, tm), F32)
        for a in range(n):
            e1 = jnp.where(pa == float(a), i1[a:a + 1, :], e1)
            e2 = jnp.where(pb == float(a), i2[a:a + 1, :], e2)
        ex = jnp.exp(f_s - f_s[0:1, :])
        gs.append(ex / jnp.sum(ex, axis=0, keepdims=True))
        e1s.append(e1)
        e2s.append(e2)
    e1_ref[...] = jnp.concatenate(e1s, axis=0).T
    e2_ref[...] = jnp.concatenate(e2s, axis=0).T
    g_ref[...] = jnp.concatenate(gs, axis=0).T


def _peer_route(q, keys, tm=256):
    s, w = q.shape
    nsel = PEER_HEADS * PEER_TOPK
    tm = min(tm, s)
    out = jax.ShapeDtypeStruct((s, nsel), F32)
    ob = pl.BlockSpec((tm, nsel), lambda i: (i, 0))
    return pl.pallas_call(
        _peer_route_kernel,
        grid=(s // tm,),
        in_specs=[pl.BlockSpec((tm, w), lambda i: (i, 0)),
                  pl.BlockSpec(keys.shape, lambda i: (0, 0))],
        out_specs=[ob, ob, ob],
        out_shape=[out, out, out],
        compiler_params=_params(("parallel",)),
        name="peer_route",
    )(q, keys)


EXPAND_UNROLL = 16
EXPAND_GROUP = 4


def _peer_expand_kernel(e1_ref, e2_ref, g_ref, w_ref, stage_ref):
    nk = PEER_N_KEYS
    nsel = e1_ref.shape[1]
    rid = lax.broadcasted_iota(jnp.int32, (nk, nsel), 0).astype(F32)
    un = EXPAND_UNROLL

    def body(tt, carry):
        base = pl.multiple_of(tt * un, un)
        e1 = e1_ref[pl.ds(base, un), :]
        e2 = e2_ref[pl.ds(base, un), :]
        g = g_ref[pl.ds(base, un), :]
        for u0 in range(0, un, EXPAND_GROUP):
            us = range(u0, u0 + EXPAND_GROUP)
            lefts = [jnp.where(rid == e1[u:u + 1, :], g[u:u + 1, :], 0.0).astype(BF16) for u in us]
            rights = [jnp.where(rid == e2[u:u + 1, :], 1.0, 0.0).astype(BF16) for u in us]
            for u, l, r in zip(us, lefts, rights):
                stage_ref[u * nk:(u + 1) * nk, :] = _dot_nt(l, r)
        for a in range(nk):
            rows = stage_ref[pl.ds(a, un, stride=nk), :]
            w_ref[pl.ds(base, un), a * nk:(a + 1) * nk] = rows.astype(w_ref.dtype)
        return carry

    lax.fori_loop(0, e1_ref.shape[0] // un, body, 0)


def _peer_expand(e1, e2, g, tm=128):
    s, nsel = e1.shape
    nk = PEER_N_KEYS
    tm = min(tm, s)
    ib = pl.BlockSpec((tm, nsel), lambda i: (i, 0))
    return pl.pallas_call(
        _peer_expand_kernel,
        grid=(s // tm,),
        in_specs=[ib, ib, ib],
        out_specs=pl.BlockSpec((tm, nk * nk), lambda i: (i, 0)),
        out_shape=jax.ShapeDtypeStruct((s, nk * nk), BF16),
        scratch_shapes=[pltpu.VMEM((EXPAND_UNROLL * nk, nk), F32)],
        compiler_params=_params(("parallel",)),
        name="peer_expand",
    )(e1, e2, g)


FP8 = jnp.float8_e4m3fn
FP8_MAX = 448.0
SCALE_ROWS = 8


def _rmsnorm_fp8_kernel(x_ref, g_ref, o_ref, o8_ref, s_ref):
    x = x_ref[...]
    ms = jnp.mean(x * x, axis=-1, keepdims=True)
    y = x * lax.rsqrt(ms + NORM_EPS) * g_ref[...]
    o_ref[...] = y.astype(o_ref.dtype)
    scale = jnp.maximum(jnp.max(jnp.abs(y), axis=-1, keepdims=True), 1e-30) * (1.0 / FP8_MAX)
    o8_ref[...] = (y / scale).astype(o8_ref.dtype)
    s_ref[...] = jnp.broadcast_to(scale, s_ref.shape)


def _rmsnorm_fp8(x, g, tm=256):
    s, d = x.shape
    blk = pl.BlockSpec((tm, d), lambda i: (i, 0))
    return pl.pallas_call(
        _rmsnorm_fp8_kernel,
        grid=(s // tm,),
        in_specs=[blk, pl.BlockSpec((1, d), lambda i: (0, 0))],
        out_specs=[blk, blk, pl.BlockSpec((tm, LANES), lambda i: (i, 0))],
        out_shape=[jax.ShapeDtypeStruct((s, d), BF16), jax.ShapeDtypeStruct((s, d), FP8),
                   jax.ShapeDtypeStruct((s, LANES), F32)],
        compiler_params=_params(("parallel",)),
        name="rmsnorm_fp8",
    )(x, g.reshape(1, d))


def _quant_rows_kernel(u_ref, u8_ref, s_ref):
    u = u_ref[...]
    scale = jnp.maximum(jnp.max(jnp.abs(u), axis=-1, keepdims=True), 1e-30) * (1.0 / FP8_MAX)
    u8_ref[...] = (u / scale).astype(u8_ref.dtype)
    s_ref[...] = jnp.broadcast_to(scale, (u.shape[0], LANES)).T[:SCALE_ROWS, :]


def _quant_rows(u, te=512):
    ne, d = u.shape
    return pl.pallas_call(
        _quant_rows_kernel,
        grid=(ne // te,),
        in_specs=[pl.BlockSpec((te, d), lambda e: (e, 0))],
        out_specs=[pl.BlockSpec((te, d), lambda e: (e, 0)),
                   pl.BlockSpec((SCALE_ROWS, te), lambda e: (0, e))],
        out_shape=[jax.ShapeDtypeStruct((ne, d), FP8), jax.ShapeDtypeStruct((SCALE_ROWS, ne), F32)],
        compiler_params=_params(("parallel",)),
        name="peer_quant_u",
    )(u)


def _peer_dense_kernel(h_ref, hs_ref, u_ref, us_ref, v_ref, vs_ref, w_ref, o_ref):
    @pl.when(pl.program_id(1) == 0)
    def _():
        o_ref[...] = jnp.zeros_like(o_ref)

    act = _dot_nt(h_ref[...], u_ref[...]) * hs_ref[:, 0:1] * us_ref[0:1, :]
    gelu = 0.5 * act * (1.0 + lax.erf(act * (2.0 ** -0.5)))
    mix = w_ref[...].astype(F32) * gelu * vs_ref[0:1, :]
    scale = jnp.maximum(jnp.max(jnp.abs(mix), axis=-1, keepdims=True), 1e-30) * (1.0 / FP8_MAX)
    o_ref[...] += _dot((mix / scale).astype(FP8), v_ref[...]) * scale


def _peer_dense(h8, hs, u8, us, v8, vs, wdense, tm=512, te=1024):
    s, d = h8.shape
    ne = u8.shape[0]
    tm = min(tm, s)
    return pl.pallas_call(
        _peer_dense_kernel,
        grid=(s // tm, ne // te),
        in_specs=[pl.BlockSpec((tm, d), lambda i, e: (i, 0)),
                  pl.BlockSpec((tm, LANES), lambda i, e: (i, 0)),
                  pl.BlockSpec((te, d), lambda i, e: (e, 0)),
                  pl.BlockSpec((SCALE_ROWS, te), lambda i, e: (0, e)),
                  pl.BlockSpec((te, d), lambda i, e: (e, 0)),
                  pl.BlockSpec((SCALE_ROWS, te), lambda i, e: (0, e)),
                  pl.BlockSpec((tm, te), lambda i, e: (i, e))],
        out_specs=pl.BlockSpec((tm, d), lambda i, e: (i, 0)),
        out_shape=jax.ShapeDtypeStruct((s, d), F32),
        compiler_params=_params(("parallel", "arbitrary")),
        name="peer_dense",
    )(h8, hs, u8, us, v8, vs, wdense)


def _add_rmsnorm_kernel(x_ref, y_ref, g_ref, o_ref):
    x = x_ref[...] + y_ref[...]
    ms = jnp.mean(x * x, axis=-1, keepdims=True)
    o_ref[...] = x * lax.rsqrt(ms + NORM_EPS) * g_ref[...]


def _add_rmsnorm(x, y, g, tm=256):
    s, d = x.shape
    blk = pl.BlockSpec((tm, d), lambda i: (i, 0))
    return pl.pallas_call(
        _add_rmsnorm_kernel,
        grid=(s // tm,),
        in_specs=[blk, blk, pl.BlockSpec((1, d), lambda i: (0, 0))],
        out_specs=blk,
        out_shape=jax.ShapeDtypeStruct((s, d), F32),
        compiler_params=_params(("parallel",)),
        name="residual_final_norm",
    )(x, y, g.reshape(1, d))


def _rwkv_branch(proj_r, w0, w_up, a0, a_up, g_up, k_k, k_a, r_k, lnx_g, lnx_b):
    g_up_pad = jnp.pad(g_up, ((0, GATE_LORA_PAD - GATE_LORA), (0, 0))).astype(BF16)
    wpre, a, g = _rwkv_lora(proj_r, w0, w_up.astype(BF16), a0, a_up.astype(BF16), g_up_pad)
    rt, at, kh, bh, kb, bb, pc, bonus = _rwkv_prep(proj_r, wpre, a, k_k, k_a, r_k.reshape(-1))
    x, y, opre, qb2, bbt, gst, pcm = _rwkv_chunk(rt, at, kh, bh, kb, bb, pc, proj_r)
    return _rwkv_scan(x, y, rt, opre, qb2, bbt, gst, pcm, bonus, g, lnx_g, lnx_b)


def _rope_tables(s):
    half = MOBA_HEAD_DIM // 2
    inv_freq = ROPE_THETA ** (-jnp.arange(half, dtype=F32) / half)
    ang = jnp.arange(s, dtype=jnp.int32).astype(F32)[:, None] * inv_freq[None, :]
    cos = jnp.cos(ang)
    sin = jnp.sin(ang)
    return jnp.concatenate([cos, cos], axis=-1), jnp.concatenate([-sin, sin], axis=-1)


def _moba_branch(proj_m):
    s = proj_m.shape[0]
    cos, sin = _rope_tables(s)
    q, qs, k, vt, kmean = _moba_prep(proj_m, cos, sin)
    return _moba_attn(q, qs, k, vt, kmean.reshape(kmean.shape[0], kmean.shape[2]))


def _peer_layer(x1, norm2_g, w_q, sub_keys, expert_u, expert_v, final_g):
    s, d = x1.shape
    h2, h8, hs = _rmsnorm_fp8(x1, norm2_g)
    q = _matmul(h2, w_q.astype(BF16), F32, tm=min(1024, s), tn=512, name="peer_query")
    keys = sub_keys.reshape(PEER_HEADS * 2 * PEER_N_KEYS, PEER_HALF).astype(BF16)
    e1, e2, g = _peer_route(q, keys)
    wdense = _peer_expand(e1, e2, g)
    u8, us = _quant_rows(expert_u)
    v8, vs = _quant_rows(expert_v)
    peer = _peer_dense(h8, hs, u8, us, v8, vs, wdense)
    return _add_rmsnorm(x1, peer, final_g)


def kernel(x, norm1_g, w_in, rwkv_mu, rwkv_w0, rwkv_w_up, rwkv_a0, rwkv_a_up, rwkv_g_up, rwkv_k_k, rwkv_k_a, rwkv_r_k, rwkv_lnx_g, rwkv_lnx_b, w_branch_rwkv, w_branch_moba, w_out, norm2_g, peer_w_q, peer_sub_keys, peer_u, peer_v, final_g):
    b, s, d = x.shape
    depth = w_in.shape[0]
    assert b == 1 and depth == 1
    x2d = x.reshape(s, d)
    l = 0
    rw = rwkv_w0.shape[-1]
    shift_w = 3 * rw + DECAY_LORA + AAA_LORA + GATE_LORA
    tn = 512
    shift_pad = -(-shift_w // tn) * tn
    mw = w_branch_moba.shape[1]
    tm = min(1024, s)

    h = _rmsnorm(x2d, norm1_g[l], BF16)
    wt = jnp.transpose(w_in[l])
    mu = jnp.pad(rwkv_mu[l], (0, shift_pad - shift_w)).reshape(1, shift_pad)
    proj_r = _matmul_shift(h, wt, mu, tm=tm, tn=tn)
    wt_mg = _repack_rows(wt, shift_w, 3 * mw + 2 * d)
    proj_m = _matmul_nt(h, wt_mg, F32, tm=tm, tn=tn, name="in_proj_moba", row0=0, n=3 * mw)
    gates = _matmul_nt(h, wt_mg, F32, tm=tm, tn=tn, name="in_proj_gates", row0=3 * mw, n=2 * d)

    y_a = _rwkv_branch(proj_r, rwkv_w0[l], rwkv_w_up[l], rwkv_a0[l], rwkv_a_up[l], rwkv_g_up[l],
                       rwkv_k_k[l], rwkv_k_a[l], rwkv_r_k[l], rwkv_lnx_g[l], rwkv_lnx_b[l])
    y_b = _moba_branch(proj_m)
    mixed = _gated_pair(y_a, w_branch_rwkv[l].astype(BF16), y_b, w_branch_moba[l].astype(BF16),
                        gates, tm=tm, tn=512)
    x1 = _matmul_residual(mixed, w_out[l].astype(BF16), x2d, tm=tm, tn=512)
    out = _peer_layer(x1, norm2_g[l], peer_w_q[l], peer_sub_keys[l], peer_u[l], peer_v[l], final_g)
    return out.reshape(b, s, d)
```

```python
import functools

import jax
import jax.numpy as jnp
from jax import lax
from jax.experimental import pallas as pl
from jax.experimental.pallas import tpu as pltpu

F32 = jnp.float32
BF16 = jnp.bfloat16

NORM_EPS = 1e-6
LANES = 128
RWKV_HEAD_DIM = 64
DECAY_LORA = 128
AAA_LORA = 128
GATE_LORA = 480
GATE_LORA_PAD = 512
LN_X_EPS = 64e-5
CHUNK = 128
MOBA_HEAD_DIM = 128
MOBA_BLOCK = 256
MOBA_TOPK = 3
ROPE_THETA = 10000.0
PEER_HEADS = 8
PEER_N_KEYS = 128
PEER_HALF = 128
PEER_TOPK = 16

VMEM_LIMIT = 56 * 1024 * 1024

_NT = (((1,), (1,)), ((), ()))


def _params(sem, vmem=VMEM_LIMIT):
    return pltpu.CompilerParams(dimension_semantics=sem, vmem_limit_bytes=vmem)


def _dot(a, b):
    return jnp.dot(a, b, preferred_element_type=F32)


def _dot_nt(a, b):
    return lax.dot_general(a, b, _NT, preferred_element_type=F32)


def _split3(x):
    hi = x.astype(BF16)
    r1 = x - hi.astype(F32)
    mid = r1.astype(BF16)
    lo = (r1 - mid.astype(F32)).astype(BF16)
    return hi, mid, lo


def _dot_l3(l_bf16, x):
    hi, mid, lo = _split3(x)
    return _dot(l_bf16, hi) + _dot(l_bf16, mid) + _dot(l_bf16, lo)


def _dot_r3(x, r_bf16):
    hi, mid, lo = _split3(x)
    return _dot(hi, r_bf16) + _dot(mid, r_bf16) + _dot(lo, r_bf16)


def _rmsnorm_kernel(x_ref, g_ref, o_ref):
    x = x_ref[...]
    ms = jnp.mean(x * x, axis=-1, keepdims=True)
    o_ref[...] = (x * lax.rsqrt(ms + NORM_EPS) * g_ref[...]).astype(o_ref.dtype)


def _rmsnorm(x, g, out_dtype, tm=256):
    s, d = x.shape
    return pl.pallas_call(
        _rmsnorm_kernel,
        grid=(s // tm,),
        in_specs=[pl.BlockSpec((tm, d), lambda i: (i, 0)),
                  pl.BlockSpec((1, d), lambda i: (0, 0))],
        out_specs=pl.BlockSpec((tm, d), lambda i: (i, 0)),
        out_shape=jax.ShapeDtypeStruct((s, d), out_dtype),
        compiler_params=_params(("parallel",)),
        name="rmsnorm",
    )(x, g.reshape(1, d))


def _mm_kernel(a_ref, b_ref, o_ref):
    o_ref[...] = _dot(a_ref[...], b_ref[...]).astype(o_ref.dtype)


def _matmul(a, b, out_dtype, tm, tn, name, col0=0, n=None):
    m, k = a.shape
    n = b.shape[1] if n is None else n
    off = col0 // tn
    return pl.pallas_call(
        _mm_kernel,
        grid=(m // tm, n // tn),
        in_specs=[pl.BlockSpec((tm, k), lambda i, j: (i, 0)),
                  pl.BlockSpec((k, tn), lambda i, j: (0, j + off))],
        out_specs=pl.BlockSpec((tm, tn), lambda i, j: (i, j)),
        out_shape=jax.ShapeDtypeStruct((m, n), out_dtype),
        compiler_params=_params(("parallel", "parallel")),
        name=name,
    )(a, b)


BF16_SUBLANES = 16


def _mm_nt_kernel(a_ref, bt_ref, o_ref):
    o_ref[...] = _dot_nt(a_ref[...], bt_ref[...]).astype(o_ref.dtype)


def _matmul_nt(a, bt, out_dtype, tm, tn, name, row0=0, n=None):
    m, k = a.shape
    n = bt.shape[0] if n is None else n
    off = row0 // tn
    return pl.pallas_call(
        _mm_nt_kernel,
        grid=(m // tm, n // tn),
        in_specs=[pl.BlockSpec((tm, k), lambda i, j: (i, 0)),
                  pl.BlockSpec((tn, k), lambda i, j: (j + off, 0))],
        out_specs=pl.BlockSpec((tm, tn), lambda i, j: (i, j)),
        out_shape=jax.ShapeDtypeStruct((m, n), out_dtype),
        compiler_params=_params(("parallel", "parallel")),
        name=name,
    )(a, bt)


def _mm_shift_kernel(a_ref, ap_ref, bt_ref, mu_ref, o_ref):
    i = pl.program_id(0)
    b = bt_ref[...].astype(BF16)
    z = _dot_nt(a_ref[...], b)
    zp = _dot_nt(ap_ref[...], b)
    prev = jnp.where(i > 0, zp[BF16_SUBLANES - 1:BF16_SUBLANES, :], 0.0)
    row = lax.broadcasted_iota(jnp.int32, z.shape, 0)
    z_prev = jnp.where(row == 0, prev, pltpu.roll(z, 1, axis=0))
    o_ref[...] = z + (z_prev - z) * mu_ref[...]


def _matmul_shift(a, bt, mu, tm, tn):
    m, k = a.shape
    n = mu.shape[1]
    per = tm // BF16_SUBLANES
    return pl.pallas_call(
        _mm_shift_kernel,
        grid=(m // tm, n // tn),
        in_specs=[pl.BlockSpec((tm, k), lambda i, j: (i, 0)),
                  pl.BlockSpec((BF16_SUBLANES, k), lambda i, j: (jnp.maximum(i * per - 1, 0), 0)),
                  pl.BlockSpec((tn, k), lambda i, j: (j, 0)),
                  pl.BlockSpec((1, tn), lambda i, j: (0, j))],
        out_specs=pl.BlockSpec((tm, tn), lambda i, j: (i, j)),
        out_shape=jax.ShapeDtypeStruct((m, n), F32),
        compiler_params=_params(("parallel", "parallel")),
        name="in_proj_rwkv",
    )(a, a, bt, mu)


def _repack_kernel(a_ref, b_ref, o_ref, *, off):
    o_ref[...] = jnp.concatenate([a_ref[off:, :], b_ref[:off, :]], axis=0).astype(o_ref.dtype)


def _repack_rows(wt, row0, n, tr=512, tk=1024):
    k = wt.shape[1]
    first, off = divmod(row0, tr)
    assert off % BF16_SUBLANES == 0
    return pl.pallas_call(
        functools.partial(_repack_kernel, off=off),
        grid=(n // tr, k // tk),
        in_specs=[pl.BlockSpec((tr, tk), lambda i, j: (i + first, j)),
                  pl.BlockSpec((tr, tk), lambda i, j: (i + first + 1, j))],
        out_specs=pl.BlockSpec((tr, tk), lambda i, j: (i, j)),
        out_shape=jax.ShapeDtypeStruct((n, k), BF16),
        compiler_params=_params(("parallel", "parallel")),
        name="repack_w_in",
    )(wt, wt)


def _gated_pair_kernel(ya_ref, wa_ref, yb_ref, wb_ref, ga_ref, gb_ref, o_ref):
    pa = _dot(ya_ref[...], wa_ref[...])
    pb = _dot(yb_ref[...], wb_ref[...])
    mixed = jax.nn.sigmoid(ga_ref[...]) * pa + jax.nn.sigmoid(gb_ref[...]) * pb
    o_ref[...] = mixed.astype(o_ref.dtype)


def _gated_pair(ya, wa, yb, wb, gates, tm, tn):
    m, k = ya.shape
    n = wa.shape[1]
    nb = n // tn
    return pl.pallas_call(
        _gated_pair_kernel,
        grid=(m // tm, nb),
        in_specs=[pl.BlockSpec((tm, k), lambda i, j: (i, 0)),
                  pl.BlockSpec((k, tn), lambda i, j: (0, j)),
                  pl.BlockSpec((tm, k), lambda i, j: (i, 0)),
                  pl.BlockSpec((k, tn), lambda i, j: (0, j)),
                  pl.BlockSpec((tm, tn), lambda i, j: (i, j)),
                  pl.BlockSpec((tm, tn), lambda i, j: (i, j + nb))],
        out_specs=pl.BlockSpec((tm, tn), lambda i, j: (i, j)),
        out_shape=jax.ShapeDtypeStruct((m, n), BF16),
        compiler_params=_params(("parallel", "parallel")),
        name="branch_merge",
    )(ya, wa, yb, wb, gates, gates)


def _mm_res_kernel(a_ref, b_ref, r_ref, o_ref):
    o_ref[...] = r_ref[...] + _dot(a_ref[...], b_ref[...])


def _matmul_residual(a, b, res, tm, tn):
    m, k = a.shape
    n = b.shape[1]
    return pl.pallas_call(
        _mm_res_kernel,
        grid=(m // tm, n // tn),
        in_specs=[pl.BlockSpec((tm, k), lambda i, j: (i, 0)),
                  pl.BlockSpec((k, tn), lambda i, j: (0, j)),
                  pl.BlockSpec((tm, tn), lambda i, j: (i, j))],
        out_specs=pl.BlockSpec((tm, tn), lambda i, j: (i, j)),
        out_shape=jax.ShapeDtypeStruct((m, n), F32),
        compiler_params=_params(("parallel", "parallel")),
        name="out_proj",
    )(a, b, res)


def _rwkv_lora_kernel(wl_ref, al_ref, gl0_ref, gl1_ref, w0_ref, wup_ref, a0_ref, aup_ref,
                      gup_ref, wpre_ref, a_ref, g_ref):
    half = GATE_LORA_PAD // 2
    wpre_ref[...] = w0_ref[...] + _dot(jnp.tanh(wl_ref[...]).astype(BF16), wup_ref[...])
    a_ref[...] = jax.nn.sigmoid(a0_ref[...] + _dot(al_ref[...].astype(BF16), aup_ref[...]))
    g_ref[...] = (_dot(jax.nn.sigmoid(gl0_ref[...]).astype(BF16), gup_ref[:half, :])
                  + _dot(jax.nn.sigmoid(gl1_ref[...]).astype(BF16), gup_ref[half:, :]))


def _rwkv_lora(proj_r, w0, w_up, a0, a_up, g_up_pad, tm=256):
    s = proj_r.shape[0]
    w = w0.shape[-1]
    half = GATE_LORA_PAD // 2
    c_wl = 3 * w // DECAY_LORA
    c_al = (3 * w + DECAY_LORA) // AAA_LORA
    c_gl = (3 * w + DECAY_LORA + AAA_LORA) // half
    row = lambda i: (i, 0)
    const = lambda i: (0, 0)
    out = jax.ShapeDtypeStruct((s, w), F32)
    return pl.pallas_call(
        _rwkv_lora_kernel,
        grid=(s // tm,),
        in_specs=[pl.BlockSpec((tm, DECAY_LORA), lambda i: (i, c_wl)),
                  pl.BlockSpec((tm, AAA_LORA), lambda i: (i, c_al)),
                  pl.BlockSpec((tm, half), lambda i: (i, c_gl)),
                  pl.BlockSpec((tm, half), lambda i: (i, c_gl + 1)),
                  pl.BlockSpec((1, w), const),
                  pl.BlockSpec((DECAY_LORA, w), const),
                  pl.BlockSpec((1, w), const),
                  pl.BlockSpec((AAA_LORA, w), const),
                  pl.BlockSpec((GATE_LORA_PAD, w), const)],
        out_specs=[pl.BlockSpec((tm, w), row)] * 3,
        out_shape=[out, out, out],
        compiler_params=_params(("parallel",)),
        name="rwkv_lora",
    )(proj_r, proj_r, proj_r, proj_r, w0.reshape(1, w), w_up, a0.reshape(1, w), a_up, g_up_pad)


def _head_pair_ones(width=LANES):
    r = lax.broadcasted_iota(jnp.int32, (width, width), 0) // RWKV_HEAD_DIM
    c = lax.broadcasted_iota(jnp.int32, (width, width), 1) // RWKV_HEAD_DIM
    return r == c


def _rwkv_prep_kernel(r_ref, k_ref, v_ref, wpre_ref, a_ref, kk_ref, ka_ref, rk_ref,
                      rt_ref, at_ref, kh_ref, bh_ref, kb_ref, bb_ref, pc_ref, bonus_ref):
    tm = r_ref.shape[0]
    r = r_ref[...]
    k = k_ref[...]
    v = v_ref[...]
    a = a_ref[...]
    same_head = jnp.where(_head_pair_ones(r.shape[1]), 1.0, 0.0).astype(BF16)

    x = -wpre_ref[...]
    softplus = jnp.maximum(x, 0.0) + jnp.log1p(jnp.exp(-jnp.abs(x)))
    lw = -jnp.exp(-softplus - 0.5)

    kk = k * kk_ref[...]
    ss = _dot_r3(kk * kk, same_head)
    kkn = kk / jnp.maximum(jnp.sqrt(ss), 1e-12)
    k2 = k * (1.0 + (a - 1.0) * ka_ref[...])
    bonus_ref[...] = _dot_r3(r * k2 * rk_ref[...], same_head) * v

    ri = lax.broadcasted_iota(jnp.int32, (tm, tm), 0)
    ci = lax.broadcasted_iota(jnp.int32, (tm, tm), 1)
    same_chunk = (ri // CHUNK) == (ci // CHUNK)
    tri = jnp.where(same_chunk & (ci <= ri), 1.0, 0.0).astype(BF16)
    allc = jnp.where(same_chunk, 1.0, 0.0).astype(BF16)
    cum = _dot_l3(tri, lw)
    tot = _dot_l3(allc, lw)

    p_in = jnp.exp(cum)
    p_inv = jnp.exp(-cum)
    p_tail = jnp.exp(tot - cum)
    b = kkn * a
    rt_ref[...] = r * p_in
    at_ref[...] = -kkn * jnp.exp(cum - lw)
    kh_ref[...] = k2 * p_inv
    bh_ref[...] = b * p_inv
    kb_ref[...] = k2 * p_tail
    bb_ref[...] = b * p_tail
    pc_ref[...] = jnp.exp(tot)


def _rwkv_prep(proj_r, wpre, a, k_k, k_a, r_k, tm=256, tw=512):
    s, w = wpre.shape
    ncol = w // tw
    blk = lambda off: pl.BlockSpec((tm, tw), lambda i, j, off=off: (i, j + off))
    vec = pl.BlockSpec((1, tw), lambda i, j: (0, j))
    out = jax.ShapeDtypeStruct((s, w), F32)
    return pl.pallas_call(
        _rwkv_prep_kernel,
        grid=(s // tm, ncol),
        in_specs=[blk(0), blk(ncol), blk(2 * ncol), blk(0), blk(0), vec, vec, vec],
        out_specs=[blk(0)] * 8,
        out_shape=[out] * 8,
        compiler_params=_params(("parallel", "parallel")),
        name="rwkv_prep",
    )(proj_r, proj_r, proj_r, wpre, a, k_k.reshape(1, w), k_a.reshape(1, w), r_k.reshape(1, w))


def _rwkv_chunk_kernel(rt_ref, at_ref, kh_ref, bh_ref, kb_ref, bb_ref, pc_ref, v_ref,
                       x_ref, y_ref, op_ref, qb_ref, bbt_ref, g_ref, pcm_ref):
    c = CHUNK
    nheads = LANES // RWKV_HEAD_DIM
    chunks = range(rt_ref.shape[0] // c)
    pairs = range(rt_ref.shape[1] // LANES)
    rows = [slice(q * c, (q + 1) * c) for q in chunks]
    lanes = [slice(p * LANES, (p + 1) * LANES) for p in pairs]
    tiles = [(q, p) for q in chunks for p in pairs]
    probs = [(n, h) for n in range(len(tiles)) for h in range(nheads)]
    lane = lax.broadcasted_iota(jnp.int32, (c, LANES), 1)
    ri = lax.broadcasted_iota(jnp.int32, (c, c), 0)
    ci = lax.broadcasted_iota(jnp.int32, (c, c), 1)
    strict = ci < ri
    incl = ci <= ri
    eye = jnp.where(ri == ci, 1.0, 0.0)
    head_mask = [(lane // RWKV_HEAD_DIM) == h for h in range(nheads)]
    tile = lambda ref, n: ref[rows[tiles[n][0]], lanes[tiles[n][1]]]

    kh = [tile(kh_ref, n).astype(BF16) for n in range(len(tiles))]
    bh = [tile(bh_ref, n).astype(BF16) for n in range(len(tiles))]
    a_h = [jnp.where(head_mask[h], tile(at_ref, n), 0.0).astype(BF16) for n, h in probs]
    r_h = [jnp.where(head_mask[h], tile(rt_ref, n), 0.0).astype(BF16) for n, h in probs]
    v_h = [jnp.where(head_mask[h], tile(v_ref, n), 0.0).astype(BF16) for n, h in probs]
    n_ab = [jnp.where(strict, _dot_nt(a, bh[n]), 0.0) for a, (n, _) in zip(a_h, probs)]
    n_ak = [jnp.where(strict, _dot_nt(a, kh[n]), 0.0).astype(BF16) for a, (n, _) in zip(a_h, probs)]
    q_k = [jnp.where(incl, _dot_nt(r, kh[n]), 0.0).astype(BF16) for r, (n, _) in zip(r_h, probs)]
    q_b = [jnp.where(incl, _dot_nt(r, bh[n]), 0.0) for r, (n, _) in zip(r_h, probs)]
    for qb, (n, h) in zip(q_b, probs):
        q, p = tiles[n]
        col = (p * nheads + h) * c
        qb_ref[rows[q], col:col + c] = qb.astype(qb_ref.dtype)

    t = [eye + n for n in n_ab]
    pw = n_ab
    for _ in range((c - 1).bit_length() - 1):
        pb = [x.astype(BF16) for x in pw]
        pw = [_dot(x, x) for x in pb]
        t = [ti + _dot(ti.astype(BF16), pi.astype(BF16)) for ti, pi in zip(t, pw)]
    tb = [ti.astype(BF16) for ti in t]
    xs = [_dot(ti, a) for ti, a in zip(tb, a_h)]
    nv = [_dot(n, v).astype(BF16) for n, v in zip(n_ak, v_h)]
    ys = [_dot(ti, z) for ti, z in zip(tb, nv)]
    os = [_dot(qk, v) for qk, v in zip(q_k, v_h)]
    same_head = _head_pair_ones()
    for n, (q, p) in enumerate(tiles):
        mine = [m for m, (nn, _) in enumerate(probs) if nn == n]
        x_ref[rows[q], lanes[p]] = sum(xs[m] for m in mine).astype(x_ref.dtype)
        y_ref[rows[q], lanes[p]] = sum(ys[m] for m in mine)
        op_ref[rows[q], lanes[p]] = sum(os[m] for m in mine)
        kbt = tile(kb_ref, n).T.astype(BF16)
        g_ref[rows[q], lanes[p]] = jnp.where(same_head, _dot(kbt, tile(v_ref, n).astype(BF16)), 0.0)
        bbt_ref[lanes[p], rows[q]] = tile(bb_ref, n).T.astype(bbt_ref.dtype)
        pcm_ref[rows[q], lanes[p]] = tile(pc_ref, n).T


def _rwkv_chunk(rt, at, kh, bh, kb, bb, pc, proj_r, tc=2 * CHUNK, tw=2 * LANES):
    s, w = rt.shape
    ncol = w // tw
    c = CHUNK
    blk = lambda off: pl.BlockSpec((tc, tw), lambda i, j, off=off: (i, j + off))
    f32o = jax.ShapeDtypeStruct((s, w), F32)
    return pl.pallas_call(
        _rwkv_chunk_kernel,
        grid=(s // tc, ncol),
        in_specs=[blk(0)] * 7 + [blk(2 * ncol)],
        out_specs=[blk(0), blk(0), blk(0),
                   pl.BlockSpec((tc, 2 * tw), lambda i, j: (i, j)),
                   pl.BlockSpec((tw, tc), lambda i, j: (j, i)),
                   blk(0), blk(0)],
        out_shape=[jax.ShapeDtypeStruct((s, w), BF16), f32o, f32o,
                   jax.ShapeDtypeStruct((s, 2 * w), BF16),
                   jax.ShapeDtypeStruct((w, s), BF16),
                   f32o, f32o],
        compiler_params=_params(("parallel", "parallel")),
        name="rwkv_chunk",
    )(rt, at, kh, bh, kb, bb, pc, proj_r)


def _rwkv_scan_kernel(x_ref, y_ref, rt_ref, op_ref, qb_ref, bbt_ref, g_ref, pcm_ref,
                      bonus_ref, gate_ref, lng_ref, lnb_ref, ya_ref, h_ref):
    c = CHUNK
    pairs = range(x_ref.shape[1] // LANES)
    lanes = [slice(p * LANES, (p + 1) * LANES) for p in pairs]

    @pl.when(pl.program_id(1) == 0)
    def _():
        h_ref[...] = jnp.zeros_like(h_ref)

    lane = lax.broadcasted_iota(jnp.int32, (c, LANES), 1)
    first = lane < RWKV_HEAD_DIM
    same_head = _head_pair_ones()
    ones_head = jnp.where(same_head, 1.0, 0.0).astype(BF16)
    inv_n = 1.0 / RWKV_HEAD_DIM
    for q in range(x_ref.shape[0] // c):
        rows = slice(q * c, (q + 1) * c)
        hs = [h_ref[p] for p in pairs]
        hb = [h.astype(BF16) for h in hs]
        us = [_dot(x_ref[rows, lanes[p]], hb[p]) + y_ref[rows, lanes[p]] for p in pairs]
        u2 = [jnp.concatenate([jnp.where(first, u, 0.0), jnp.where(first, 0.0, u)], axis=0).astype(BF16)
              for u in us]
        os = [_dot(rt_ref[rows, lanes[p]].astype(BF16), hb[p]) + op_ref[rows, lanes[p]]
              + _dot(qb_ref[rows, p * 2 * c:(p + 1) * 2 * c], u2[p]) for p in pairs]
        upd = [_dot(bbt_ref[lanes[p], rows], us[p].astype(BF16)) for p in pairs]
        for p in pairs:
            h_ref[p] = (pcm_ref[rows, lanes[p]] * hs[p] + g_ref[rows, lanes[p]]
                        + jnp.where(same_head, upd[p], 0.0))
        mus = [_dot_r3(o, ones_head) * inv_n for o in os]
        ds = [o - mu for o, mu in zip(os, mus)]
        vs = [_dot_r3(d * d, ones_head) * inv_n for d in ds]
        for p in pairs:
            y = ds[p] * lax.rsqrt(vs[p] + LN_X_EPS) * lng_ref[:, lanes[p]] + lnb_ref[:, lanes[p]]
            ya_ref[rows, lanes[p]] = ((y + bonus_ref[rows, lanes[p]])
                                      * gate_ref[rows, lanes[p]]).astype(ya_ref.dtype)


def _rwkv_scan(x, y, rt, opre, qb2, bbt, g, pcm, bonus, gate, lnx_g, lnx_b, ts=512, tw=4 * LANES):
    s, w = y.shape
    ts = min(ts, s)
    blk = pl.BlockSpec((ts, tw), lambda j, i: (i, j))
    vec = pl.BlockSpec((1, tw), lambda j, i: (0, j))
    return pl.pallas_call(
        _rwkv_scan_kernel,
        grid=(w // tw, s // ts),
        in_specs=[blk, blk, blk, blk,
                  pl.BlockSpec((ts, 2 * tw), lambda j, i: (i, j)),
                  pl.BlockSpec((tw, ts), lambda j, i: (j, i)),
                  blk, blk, blk, blk, vec, vec],
        out_specs=blk,
        out_shape=jax.ShapeDtypeStruct((s, w), BF16),
        scratch_shapes=[pltpu.VMEM((tw // LANES, LANES, LANES), F32)],
        compiler_params=_params(("parallel", "arbitrary")),
        name="rwkv_scan",
    )(x, y, rt, opre, qb2, bbt, g, pcm, bonus, gate, lnx_g.reshape(1, w), lnx_b.reshape(1, w))


V_ONES_ROWS = 16


LOG2_E = 1.4426950408889634
MOBA_LOG2_SCALE = (MOBA_HEAD_DIM ** -0.5) * LOG2_E


def _moba_prep_kernel(q_ref, k_ref, v_ref, cos_ref, sin_ref, qo_ref, qs_ref, ko_ref, vt_ref, km_ref):
    cos = cos_ref[...]
    sin = sin_ref[...]
    hd = MOBA_HEAD_DIM
    half = hd // 2
    tb = q_ref.shape[0]
    for h in range(q_ref.shape[1] // hd):
        cols = slice(h * hd, (h + 1) * hd)
        q = q_ref[:, cols]
        k = k_ref[:, cols]
        qr = q * cos + pltpu.roll(q, half, axis=1) * sin
        kr = k * cos + pltpu.roll(k, half, axis=1) * sin
        qo_ref[:, cols] = qr.astype(qo_ref.dtype)
        qs_ref[:, cols] = (qr * MOBA_LOG2_SCALE).astype(qs_ref.dtype)
        ko_ref[:, cols] = kr.astype(ko_ref.dtype)
        km_ref[0, :, cols] = jnp.mean(kr, axis=0, keepdims=True)
        vt_ref[h, 0, :hd, :] = v_ref[:, cols].T.astype(vt_ref.dtype)
        vt_ref[h, 0, hd:, :] = jnp.ones((V_ONES_ROWS, tb), vt_ref.dtype)


def _moba_prep(proj_m, cos, sin):
    s = proj_m.shape[0]
    w = proj_m.shape[1] // 3
    tb = MOBA_BLOCK
    nb = s // tb
    nh = w // MOBA_HEAD_DIM
    vrows = MOBA_HEAD_DIM + V_ONES_ROWS
    blk = lambda off: pl.BlockSpec((tb, w), lambda i, off=off: (i, off))
    tab = pl.BlockSpec((tb, MOBA_HEAD_DIM), lambda i: (i, 0))
    bo = jax.ShapeDtypeStruct((s, w), BF16)
    return pl.pallas_call(
        _moba_prep_kernel,
        grid=(nb,),
        in_specs=[blk(0), blk(1), blk(2), tab, tab],
        out_specs=[blk(0), blk(0), blk(0),
                   pl.BlockSpec((nh, 1, vrows, tb), lambda i: (0, i, 0, 0)),
                   pl.BlockSpec((1, 1, w), lambda i: (i, 0, 0))],
        out_shape=[bo, bo, bo, jax.ShapeDtypeStruct((nh, nb, vrows, tb), BF16),
                   jax.ShapeDtypeStruct((nb, 1, w), F32)],
        compiler_params=_params(("parallel",)),
        name="moba_prep",
    )(proj_m, proj_m, proj_m, cos, sin)


MOBA_CHAINS = 4
MOBA_GROUPS = 2


def _moba_attn_kernel(q_ref, qs_ref, k_ref, vt_ref, km_ref, o_ref, bias_ref):
    i = pl.program_id(1)
    tb = MOBA_BLOCK
    nb = km_ref.shape[0]
    hd = MOBA_HEAD_DIM
    q = qs_ref[...]

    own = pl.multiple_of(i * tb, tb)
    s = _dot_nt(k_ref[pl.ds(own, tb), :], q)
    raw0 = [_dot_nt(k_ref[g * tb:(g + 1) * tb, :], q) for g in range(MOBA_CHAINS)]

    gate = _dot_nt(km_ref[...].astype(BF16), q_ref[...])
    rid = lax.broadcasted_iota(jnp.int32, gate.shape, 0).astype(F32)
    gate = jnp.where(rid < i.astype(F32), gate, -jnp.inf)
    bias = jnp.full(gate.shape, -jnp.inf, F32)
    for _ in range(MOBA_TOPK):
        m = jnp.max(gate, axis=0, keepdims=True)
        first = jnp.min(jnp.where(gate == m, rid, float(nb)), axis=0, keepdims=True)
        pick = (rid == first) & (m > -jnp.inf)
        bias = jnp.where(pick, 0.0, bias)
        gate = jnp.where(pick, -jnp.inf, gate)
    bias_ref[...] = bias

    ss0 = [r.astype(BF16) + bias[g:g + 1, :].astype(BF16) for g, r in enumerate(raw0)]

    ki = lax.broadcasted_iota(jnp.int32, s.shape, 0)
    qi = lax.broadcasted_iota(jnp.int32, s.shape, 1)
    s = jnp.where(ki <= qi, s, -jnp.inf).astype(BF16)
    m0 = jnp.max(s, axis=0, keepdims=True)
    acc0 = _dot(vt_ref[i], jnp.exp2(s - m0))
    m0 = m0.astype(F32)

    def blocks(grp):
        return [jnp.minimum(grp * MOBA_CHAINS + g, nb - 1) for g in range(MOBA_CHAINS)]

    def scores(grp):
        return [_dot_nt(k_ref[pl.ds(pl.multiple_of(j * tb, tb), tb), :], q).astype(BF16)
                + bias_ref[pl.ds(j, 1), :].astype(BF16) for j in blocks(grp)]

    def update(chains, ss, grp):
        ms = [jnp.maximum(m, jnp.max(sj, axis=0, keepdims=True).astype(F32))
              for (m, _), sj in zip(chains, ss)]
        ps = [jnp.exp2(sj - mn.astype(BF16)) for sj, mn in zip(ss, ms)]
        return tuple((mn, jnp.exp2(m - mn) * acc + _dot(vt_ref[j], pj))
                     for (m, acc), mn, pj, j in zip(chains, ms, ps, blocks(grp)))

    def body(t, carry):
        chains, ss = carry
        first = t * MOBA_GROUPS
        later = [scores(first + d) for d in range(1, MOBA_GROUPS)]
        ss_next = scores(first + MOBA_GROUPS)
        for d, sd in enumerate([ss] + later):
            chains = update(chains, sd, first + d)
        return chains, ss_next

    per_trip = MOBA_CHAINS * MOBA_GROUPS
    init = ((m0, acc0),) + ((m0, jnp.zeros_like(acc0)),) * (MOBA_CHAINS - 1)
    chains, _ = lax.fori_loop(0, (i + per_trip - 1) // per_trip, body, (init, ss0))
    m = chains[0][0]
    for mg, _ in chains[1:]:
        m = jnp.maximum(m, mg)
    acc = sum(jnp.exp2(mg - m) * ag for mg, ag in chains)
    out = acc[:hd, :] / acc[hd:hd + 1, :]
    o_ref[...] = out.T.astype(o_ref.dtype)


def _moba_attn(q, qs, k, vt, kmean):
    s, w = q.shape
    tb = MOBA_BLOCK
    nb = s // tb
    hd = MOBA_HEAD_DIM
    vrows = vt.shape[2]
    return pl.pallas_call(
        _moba_attn_kernel,
        grid=(w // hd, nb),
        in_specs=[pl.BlockSpec((tb, hd), lambda h, i: (i, h)),
                  pl.BlockSpec((tb, hd), lambda h, i: (i, h)),
                  pl.BlockSpec((s, hd), lambda h, i: (0, h)),
                  pl.BlockSpec((None, nb, vrows, tb), lambda h, i: (h, 0, 0, 0)),
                  pl.BlockSpec((nb, hd), lambda h, i: (0, h))],
        out_specs=pl.BlockSpec((tb, hd), lambda h, i: (i, h)),
        out_shape=jax.ShapeDtypeStruct((s, w), BF16),
        scratch_shapes=[pltpu.VMEM((nb, tb), F32)],
        compiler_params=_params(("parallel", "arbitrary")),
        name="moba_attn",
    )(q, qs, k, vt, kmean)


def _top_rows(x, n):
    rows = x.shape[0]
    rid = lax.broadcasted_iota(jnp.int32, x.shape, 0).astype(F32)
    vals, idxs = [], []
    for _ in range(n):
        m = jnp.max(x, axis=0, keepdims=True)
        first = jnp.min(jnp.where(x == m, rid, float(rows)), axis=0, keepdims=True)
        vals.append(m)
        idxs.append(first)
        x = jnp.where(rid == first, -jnp.inf, x)
    return jnp.concatenate(vals, axis=0), jnp.concatenate(idxs, axis=0)


def _peer_route_kernel(q_ref, keys_ref, e1_ref, e2_ref, g_ref):
    n = PEER_TOPK
    tm = q_ref.shape[0]
    e1s, e2s, gs = [], [], []
    for h in range(PEER_HEADS):
        tops = []
        for p in range(2):
            hp = 2 * h + p
            cols = slice(hp * PEER_HALF, (hp + 1) * PEER_HALF)
            rows = slice(hp * PEER_N_KEYS, (hp + 1) * PEER_N_KEYS)
            st = _dot_nt(keys_ref[rows, :], q_ref[:, cols].astype(BF16))
            tops.append(_top_rows(st, n))
        (s1, i1), (s2, i2) = tops
        hn = n // 2
        cand = jnp.concatenate([s1[0:1, :] + s2]
                               + [s1[a:a + 1, :] + s2[:hn, :] for a in range(1, hn)]
                               + [s1[hn:, :] + s2[0:1, :]], axis=0)
        f_s, f_pos = _top_rows(cand, n)
        mid = jnp.floor((f_pos - n) * (1.0 / hn))
        tail0 = float(n + (hn - 1) * hn)
        pa = jnp.where(f_pos < n, 0.0, jnp.where(f_pos < tail0, 1.0 + mid, f_pos - tail0 + hn))
        pb = jnp.where(f_pos < n, f_pos, jnp.where(f_pos < tail0, f_pos - n - mid * hn, 0.0))
        e1 = jnp.zeros((n, tm), F32)
        e2 = jnp.zeros((n, tm), F32)
        for a in range(n):
            e1 = jnp.where(pa == float(a), i1[a:a + 1, :], e1)
            e2 = jnp.where(pb == float(a), i2[a:a + 1, :], e2)
        ex = jnp.exp(f_s - f_s[0:1, :])
        gs.append(ex / jnp.sum(ex, axis=0, keepdims=True))
        e1s.append(e1)
        e2s.append(e2)
    e1_ref[...] = jnp.concatenate(e1s, axis=0).T
    e2_ref[...] = jnp.concatenate(e2s, axis=0).T
    g_ref[...] = jnp.concatenate(gs, axis=0).T


def _peer_route(q, keys, tm=256):
    s, w = q.shape
    nsel = PEER_HEADS * PEER_TOPK
    tm = min(tm, s)
    out = jax.ShapeDtypeStruct((s, nsel), F32)
    ob = pl.BlockSpec((tm, nsel), lambda i: (i, 0))
    return pl.pallas_call(
        _peer_route_kernel,
        grid=(s // tm,),
        in_specs=[pl.BlockSpec((tm, w), lambda i: (i, 0)),
                  pl.BlockSpec(keys.shape, lambda i: (0, 0))],
        out_specs=[ob, ob, ob],
        out_shape=[out, out, out],
        compiler_params=_params(("parallel",)),
        name="peer_route",
    )(q, keys)


EXPAND_UNROLL = 16
EXPAND_GROUP = 4


def _peer_expand_kernel(e1_ref, e2_ref, g_ref, w_ref, stage_ref):
    nk = PEER_N_KEYS
    nsel = e1_ref.shape[1]
    rid = lax.broadcasted_iota(jnp.int32, (nk, nsel), 0).astype(F32)
    un = EXPAND_UNROLL

    def body(tt, carry):
        base = pl.multiple_of(tt * un, un)
        e1 = e1_ref[pl.ds(base, un), :]
        e2 = e2_ref[pl.ds(base, un), :]
        g = g_ref[pl.ds(base, un), :]
        for u0 in range(0, un, EXPAND_GROUP):
            us = range(u0, u0 + EXPAND_GROUP)
            lefts = [jnp.where(rid == e1[u:u + 1, :], g[u:u + 1, :], 0.0).astype(BF16) for u in us]
            rights = [jnp.where(rid == e2[u:u + 1, :], 1.0, 0.0).astype(BF16) for u in us]
            for u, l, r in zip(us, lefts, rights):
                stage_ref[u * nk:(u + 1) * nk, :] = _dot_nt(l, r)
        by_key = pltpu.einshape("uab->aub", stage_ref[...].reshape(un, nk, nk))
        for a in range(nk):
            w_ref[pl.ds(base, un), a * nk:(a + 1) * nk] = by_key[a].astype(w_ref.dtype)
        return carry

    lax.fori_loop(0, e1_ref.shape[0] // un, body, 0)


def _peer_expand(e1, e2, g, tm=128):
    s, nsel = e1.shape
    nk = PEER_N_KEYS
    tm = min(tm, s)
    ib = pl.BlockSpec((tm, nsel), lambda i: (i, 0))
    return pl.pallas_call(
        _peer_expand_kernel,
        grid=(s // tm,),
        in_specs=[ib, ib, ib],
        out_specs=pl.BlockSpec((tm, nk * nk), lambda i: (i, 0)),
        out_shape=jax.ShapeDtypeStruct((s, nk * nk), BF16),
        scratch_shapes=[pltpu.VMEM((EXPAND_UNROLL * nk, nk), F32)],
        compiler_params=_params(("parallel",)),
        name="peer_expand",
    )(e1, e2, g)


FP8 = jnp.float8_e4m3fn
FP8_MAX = 448.0
SCALE_ROWS = 8


def _rmsnorm_fp8_kernel(x_ref, g_ref, o_ref, o8_ref, s_ref):
    x = x_ref[...]
    ms = jnp.mean(x * x, axis=-1, keepdims=True)
    y = x * lax.rsqrt(ms + NORM_EPS) * g_ref[...]
    o_ref[...] = y.astype(o_ref.dtype)
    scale = jnp.maximum(jnp.max(jnp.abs(y), axis=-1, keepdims=True), 1e-30) * (1.0 / FP8_MAX)
    o8_ref[...] = (y / scale).astype(o8_ref.dtype)
    s_ref[...] = jnp.broadcast_to(scale, s_ref.shape)


def _rmsnorm_fp8(x, g, tm=256):
    s, d = x.shape
    blk = pl.BlockSpec((tm, d), lambda i: (i, 0))
    return pl.pallas_call(
        _rmsnorm_fp8_kernel,
        grid=(s // tm,),
        in_specs=[blk, pl.BlockSpec((1, d), lambda i: (0, 0))],
        out_specs=[blk, blk, pl.BlockSpec((tm, LANES), lambda i: (i, 0))],
        out_shape=[jax.ShapeDtypeStruct((s, d), BF16), jax.ShapeDtypeStruct((s, d), FP8),
                   jax.ShapeDtypeStruct((s, LANES), F32)],
        compiler_params=_params(("parallel",)),
        name="rmsnorm_fp8",
    )(x, g.reshape(1, d))


def _quant_rows_kernel(u_ref, u8_ref, s_ref):
    u = u_ref[...]
    scale = jnp.maximum(jnp.max(jnp.abs(u), axis=-1, keepdims=True), 1e-30) * (1.0 / FP8_MAX)
    u8_ref[...] = (u / scale).astype(u8_ref.dtype)
    s_ref[...] = jnp.broadcast_to(scale, (u.shape[0], LANES)).T[:SCALE_ROWS, :]


def _quant_rows(u, te=512):
    ne, d = u.shape
    return pl.pallas_call(
        _quant_rows_kernel,
        grid=(ne // te,),
        in_specs=[pl.BlockSpec((te, d), lambda e: (e, 0))],
        out_specs=[pl.BlockSpec((te, d), lambda e: (e, 0)),
                   pl.BlockSpec((SCALE_ROWS, te), lambda e: (0, e))],
        out_shape=[jax.ShapeDtypeStruct((ne, d), FP8), jax.ShapeDtypeStruct((SCALE_ROWS, ne), F32)],
        compiler_params=_params(("parallel",)),
        name="peer_quant_u",
    )(u)


def _peer_dense_kernel(h_ref, hs_ref, u_ref, us_ref, v_ref, vs_ref, w_ref, o_ref):
    @pl.when(pl.program_id(1) == 0)
    def _():
        o_ref[...] = jnp.zeros_like(o_ref)

    act = _dot_nt(h_ref[...], u_ref[...]) * hs_ref[:, 0:1] * us_ref[0:1, :]
    gelu = 0.5 * act * (1.0 + lax.erf(act * (2.0 ** -0.5)))
    mix = w_ref[...].astype(F32) * gelu * vs_ref[0:1, :]
    scale = jnp.maximum(jnp.max(jnp.abs(mix), axis=-1, keepdims=True), 1e-30) * (1.0 / FP8_MAX)
    o_ref[...] += _dot((mix / scale).astype(FP8), v_ref[...]) * scale


def _peer_dense(h8, hs, u8, us, v8, vs, wdense, tm=512, te=1024):
    s, d = h8.shape
    ne = u8.shape[0]
    tm = min(tm, s)
    return pl.pallas_call(
        _peer_dense_kernel,
        grid=(s // tm, ne // te),
        in_specs=[pl.BlockSpec((tm, d), lambda i, e: (i, 0)),
                  pl.BlockSpec((tm, LANES), lambda i, e: (i, 0)),
                  pl.BlockSpec((te, d), lambda i, e: (e, 0)),
                  pl.BlockSpec((SCALE_ROWS, te), lambda i, e: (0, e)),
                  pl.BlockSpec((te, d), lambda i, e: (e, 0)),
                  pl.BlockSpec((SCALE_ROWS, te), lambda i, e: (0, e)),
                  pl.BlockSpec((tm, te), lambda i, e: (i, e))],
        out_specs=pl.BlockSpec((tm, d), lambda i, e: (i, 0)),
        out_shape=jax.ShapeDtypeStruct((s, d), F32),
        compiler_params=_params(("parallel", "arbitrary")),
        name="peer_dense",
    )(h8, hs, u8, us, v8, vs, wdense)


def _add_rmsnorm_kernel(x_ref, y_ref, g_ref, o_ref):
    x = x_ref[...] + y_ref[...]
    ms = jnp.mean(x * x, axis=-1, keepdims=True)
    o_ref[...] = x * lax.rsqrt(ms + NORM_EPS) * g_ref[...]


def _add_rmsnorm(x, y, g, tm=256):
    s, d = x.shape
    blk = pl.BlockSpec((tm, d), lambda i: (i, 0))
    return pl.pallas_call(
        _add_rmsnorm_kernel,
        grid=(s // tm,),
        in_specs=[blk, blk, pl.BlockSpec((1, d), lambda i: (0, 0))],
        out_specs=blk,
        out_shape=jax.ShapeDtypeStruct((s, d), F32),
        compiler_params=_params(("parallel",)),
        name="residual_final_norm",
    )(x, y, g.reshape(1, d))


def _rwkv_branch(proj_r, w0, w_up, a0, a_up, g_up, k_k, k_a, r_k, lnx_g, lnx_b):
    g_up_pad = jnp.pad(g_up, ((0, GATE_LORA_PAD - GATE_LORA), (0, 0))).astype(BF16)
    wpre, a, g = _rwkv_lora(proj_r, w0, w_up.astype(BF16), a0, a_up.astype(BF16), g_up_pad)
    rt, at, kh, bh, kb, bb, pc, bonus = _rwkv_prep(proj_r, wpre, a, k_k, k_a, r_k.reshape(-1))
    x, y, opre, qb2, bbt, gst, pcm = _rwkv_chunk(rt, at, kh, bh, kb, bb, pc, proj_r)
    return _rwkv_scan(x, y, rt, opre, qb2, bbt, gst, pcm, bonus, g, lnx_g, lnx_b)


def _rope_tables(s):
    half = MOBA_HEAD_DIM // 2
    inv_freq = ROPE_THETA ** (-jnp.arange(half, dtype=F32) / half)
    ang = jnp.arange(s, dtype=jnp.int32).astype(F32)[:, None] * inv_freq[None, :]
    cos = jnp.cos(ang)
    sin = jnp.sin(ang)
    return jnp.concatenate([cos, cos], axis=-1), jnp.concatenate([-sin, sin], axis=-1)


def _moba_branch(proj_m):
    s = proj_m.shape[0]
    cos, sin = _rope_tables(s)
    q, qs, k, vt, kmean = _moba_prep(proj_m, cos, sin)
    return _moba_attn(q, qs, k, vt, kmean.reshape(kmean.shape[0], kmean.shape[2]))


def _peer_layer(x1, norm2_g, w_q, sub_keys, expert_u, expert_v, final_g):
    s, d = x1.shape
    h2, h8, hs = _rmsnorm_fp8(x1, norm2_g)
    q = _matmul(h2, w_q.astype(BF16), F32, tm=min(1024, s), tn=512, name="peer_query")
    keys = sub_keys.reshape(PEER_HEADS * 2 * PEER_N_KEYS, PEER_HALF).astype(BF16)
    e1, e2, g = _peer_route(q, keys)
    wdense = _peer_expand(e1, e2, g)
    u8, us = _quant_rows(expert_u)
    v8, vs = _quant_rows(expert_v)
    peer = _peer_dense(h8, hs, u8, us, v8, vs, wdense)
    return _add_rmsnorm(x1, peer, final_g)


def kernel(x, norm1_g, w_in, rwkv_mu, rwkv_w0, rwkv_w_up, rwkv_a0, rwkv_a_up, rwkv_g_up, rwkv_k_k, rwkv_k_a, rwkv_r_k, rwkv_lnx_g, rwkv_lnx_b, w_branch_rwkv, w_branch_moba, w_out, norm2_g, peer_w_q, peer_sub_keys, peer_u, peer_v, final_g):
    b, s, d = x.shape
    depth = w_in.shape[0]
    assert b == 1 and depth == 1
    x2d = x.reshape(s, d)
    l = 0
    rw = rwkv_w0.shape[-1]
    shift_w = 3 * rw + DECAY_LORA + AAA_LORA + GATE_LORA
    tn = 512
    shift_pad = -(-shift_w // tn) * tn
    mw = w_branch_moba.shape[1]
    tm = min(1024, s)

    h = _rmsnorm(x2d, norm1_g[l], BF16)
    wt = jnp.transpose(w_in[l])
    mu = jnp.pad(rwkv_mu[l], (0, shift_pad - shift_w)).reshape(1, shift_pad)
    proj_r = _matmul_shift(h, wt, mu, tm=tm, tn=tn)
    wt_mg = _repack_rows(wt, shift_w, 3 * mw + 2 * d)
    proj_m = _matmul_nt(h, wt_mg, F32, tm=tm, tn=tn, name="in_proj_moba", row0=0, n=3 * mw)
    gates = _matmul_nt(h, wt_mg, F32, tm=tm, tn=tn, name="in_proj_gates", row0=3 * mw, n=2 * d)

    y_a = _rwkv_branch(proj_r, rwkv_w0[l], rwkv_w_up[l], rwkv_a0[l], rwkv_a_up[l], rwkv_g_up[l],
                       rwkv_k_k[l], rwkv_k_a[l], rwkv_r_k[l], rwkv_lnx_g[l], rwkv_lnx_b[l])
    y_b = _moba_branch(proj_m)
    mixed = _gated_pair(y_a, w_branch_rwkv[l].astype(BF16), y_b, w_branch_moba[l].astype(BF16),
                        gates, tm=tm, tn=512)
    x1 = _matmul_residual(mixed, w_out[l].astype(BF16), x2d, tm=tm, tn=512)
    out = _peer_layer(x1, norm2_g[l], peer_w_q[l], peer_sub_keys[l], peer_u[l], peer_v[l], final_g)
    return out.reshape(b, s, d)
```

```python
import functools

import jax
import jax.numpy as jnp
from jax import lax
from jax.experimental import pallas as pl
from jax.experimental.pallas import tpu as pltpu

F32 = jnp.float32
BF16 = jnp.bfloat16

NORM_EPS = 1e-6
LANES = 128
RWKV_HEAD_DIM = 64
DECAY_LORA = 128
AAA_LORA = 128
GATE_LORA = 480
GATE_LORA_PAD = 512
LN_X_EPS = 64e-5
CHUNK = 128
MOBA_HEAD_DIM = 128
MOBA_BLOCK = 256
MOBA_TOPK = 3
ROPE_THETA = 10000.0
PEER_HEADS = 8
PEER_N_KEYS = 128
PEER_HALF = 128
PEER_TOPK = 16

VMEM_LIMIT = 56 * 1024 * 1024

_NT = (((1,), (1,)), ((), ()))


def _params(sem, vmem=VMEM_LIMIT):
    return pltpu.CompilerParams(dimension_semantics=sem, vmem_limit_bytes=vmem)


def _dot(a, b):
    return jnp.dot(a, b, preferred_element_type=F32)


def _dot_nt(a, b):
    return lax.dot_general(a, b, _NT, preferred_element_type=F32)


def _split3(x):
    hi = x.astype(BF16)
    r1 = x - hi.astype(F32)
    mid = r1.astype(BF16)
    lo = (r1 - mid.astype(F32)).astype(BF16)
    return hi, mid, lo


def _dot_r3(x, r_bf16):
    hi, mid, lo = _split3(x)
    return _dot(hi, r_bf16) + _dot(mid, r_bf16) + _dot(lo, r_bf16)


def _rmsnorm_kernel(x_ref, g_ref, o_ref):
    x = x_ref[...]
    ms = jnp.mean(x * x, axis=-1, keepdims=True)
    o_ref[...] = (x * lax.rsqrt(ms + NORM_EPS) * g_ref[...]).astype(o_ref.dtype)


def _rmsnorm(x, g, out_dtype, tm=256):
    s, d = x.shape
    return pl.pallas_call(
        _rmsnorm_kernel,
        grid=(s // tm,),
        in_specs=[pl.BlockSpec((tm, d), lambda i: (i, 0)),
                  pl.BlockSpec((1, d), lambda i: (0, 0))],
        out_specs=pl.BlockSpec((tm, d), lambda i: (i, 0)),
        out_shape=jax.ShapeDtypeStruct((s, d), out_dtype),
        compiler_params=_params(("parallel",)),
        name="rmsnorm",
    )(x, g.reshape(1, d))


def _mm_kernel(a_ref, b_ref, o_ref):
    o_ref[...] = _dot(a_ref[...], b_ref[...]).astype(o_ref.dtype)


def _matmul(a, b, out_dtype, tm, tn, name, col0=0, n=None):
    m, k = a.shape
    n = b.shape[1] if n is None else n
    off = col0 // tn
    return pl.pallas_call(
        _mm_kernel,
        grid=(m // tm, n // tn),
        in_specs=[pl.BlockSpec((tm, k), lambda i, j: (i, 0)),
                  pl.BlockSpec((k, tn), lambda i, j: (0, j + off))],
        out_specs=pl.BlockSpec((tm, tn), lambda i, j: (i, j)),
        out_shape=jax.ShapeDtypeStruct((m, n), out_dtype),
        compiler_params=_params(("parallel", "parallel")),
        name=name,
    )(a, b)


BF16_SUBLANES = 16


def _mm_nt_kernel(a_ref, bt_ref, o_ref):
    o_ref[...] = _dot_nt(a_ref[...], bt_ref[...]).astype(o_ref.dtype)


def _matmul_nt(a, bt, out_dtype, tm, tn, name, row0=0, n=None):
    m, k = a.shape
    n = bt.shape[0] if n is None else n
    off = row0 // tn
    return pl.pallas_call(
        _mm_nt_kernel,
        grid=(m // tm, n // tn),
        in_specs=[pl.BlockSpec((tm, k), lambda i, j: (i, 0)),
                  pl.BlockSpec((tn, k), lambda i, j: (j + off, 0))],
        out_specs=pl.BlockSpec((tm, tn), lambda i, j: (i, j)),
        out_shape=jax.ShapeDtypeStruct((m, n), out_dtype),
        compiler_params=_params(("parallel", "parallel")),
        name=name,
    )(a, bt)


def _mm_shift_kernel(a_ref, ap_ref, bt_ref, mu_ref, o_ref, b16_ref):
    i = pl.program_id(1)

    @pl.when(i == 0)
    def _():
        b16_ref[...] = bt_ref[...].astype(b16_ref.dtype)

    b = b16_ref[...]
    z = _dot_nt(a_ref[...], b)
    zp = _dot_nt(ap_ref[...], b)
    prev = jnp.where(i > 0, zp[BF16_SUBLANES - 1:BF16_SUBLANES, :], 0.0)
    row = lax.broadcasted_iota(jnp.int32, z.shape, 0)
    z_prev = jnp.where(row == 0, prev, pltpu.roll(z, 1, axis=0))
    o_ref[...] = z + (z_prev - z) * mu_ref[...]


def _matmul_shift(a, bt, mu, tm, tn):
    m, k = a.shape
    n = mu.shape[1]
    per = tm // BF16_SUBLANES
    return pl.pallas_call(
        _mm_shift_kernel,
        grid=(n // tn, m // tm),
        in_specs=[pl.BlockSpec((tm, k), lambda j, i: (i, 0)),
                  pl.BlockSpec((BF16_SUBLANES, k), lambda j, i: (jnp.maximum(i * per - 1, 0), 0)),
                  pl.BlockSpec((tn, k), lambda j, i: (j, 0)),
                  pl.BlockSpec((1, tn), lambda j, i: (0, j))],
        out_specs=pl.BlockSpec((tm, tn), lambda j, i: (i, j)),
        out_shape=jax.ShapeDtypeStruct((m, n), F32),
        scratch_shapes=[pltpu.VMEM((tn, k), BF16)],
        compiler_params=_params(("parallel", "arbitrary")),
        name="in_proj_rwkv",
    )(a, a, bt, mu)


def _repack_kernel(a_ref, b_ref, o_ref, *, off):
    o_ref[...] = jnp.concatenate([a_ref[off:, :], b_ref[:off, :]], axis=0).astype(o_ref.dtype)


def _repack_rows(wt, row0, n, tr=512, tk=1024):
    k = wt.shape[1]
    first, off = divmod(row0, tr)
    assert off % BF16_SUBLANES == 0
    return pl.pallas_call(
        functools.partial(_repack_kernel, off=off),
        grid=(n // tr, k // tk),
        in_specs=[pl.BlockSpec((tr, tk), lambda i, j: (i + first, j)),
                  pl.BlockSpec((tr, tk), lambda i, j: (i + first + 1, j))],
        out_specs=pl.BlockSpec((tr, tk), lambda i, j: (i, j)),
        out_shape=jax.ShapeDtypeStruct((n, k), BF16),
        compiler_params=_params(("parallel", "parallel")),
        name="repack_w_in",
    )(wt, wt)


def _gated_pair_kernel(ya_ref, wa_ref, yb_ref, wb_ref, ga_ref, gb_ref, o_ref):
    pa = _dot(ya_ref[...], wa_ref[...])
    pb = _dot(yb_ref[...], wb_ref[...])
    mixed = jax.nn.sigmoid(ga_ref[...]) * pa + jax.nn.sigmoid(gb_ref[...]) * pb
    o_ref[...] = mixed.astype(o_ref.dtype)


def _gated_pair(ya, wa, yb, wb, gates, tm, tn):
    m, k = ya.shape
    n = wa.shape[1]
    nb = n // tn
    return pl.pallas_call(
        _gated_pair_kernel,
        grid=(m // tm, nb),
        in_specs=[pl.BlockSpec((tm, k), lambda i, j: (i, 0)),
                  pl.BlockSpec((k, tn), lambda i, j: (0, j)),
                  pl.BlockSpec((tm, k), lambda i, j: (i, 0)),
                  pl.BlockSpec((k, tn), lambda i, j: (0, j)),
                  pl.BlockSpec((tm, tn), lambda i, j: (i, j)),
                  pl.BlockSpec((tm, tn), lambda i, j: (i, j + nb))],
        out_specs=pl.BlockSpec((tm, tn), lambda i, j: (i, j)),
        out_shape=jax.ShapeDtypeStruct((m, n), BF16),
        compiler_params=_params(("parallel", "parallel")),
        name="branch_merge",
    )(ya, wa, yb, wb, gates, gates)


def _mm_res_kernel(a_ref, b_ref, r_ref, o_ref):
    o_ref[...] = r_ref[...] + _dot(a_ref[...], b_ref[...])


def _matmul_residual(a, b, res, tm, tn):
    m, k = a.shape
    n = b.shape[1]
    return pl.pallas_call(
        _mm_res_kernel,
        grid=(m // tm, n // tn),
        in_specs=[pl.BlockSpec((tm, k), lambda i, j: (i, 0)),
                  pl.BlockSpec((k, tn), lambda i, j: (0, j)),
                  pl.BlockSpec((tm, tn), lambda i, j: (i, j))],
        out_specs=pl.BlockSpec((tm, tn), lambda i, j: (i, j)),
        out_shape=jax.ShapeDtypeStruct((m, n), F32),
        compiler_params=_params(("parallel", "parallel")),
        name="out_proj",
    )(a, b, res)


def _rwkv_lora_kernel(wl_ref, al_ref, gl0_ref, gl1_ref, w0_ref, wup_ref, a0_ref, aup_ref,
                      gup_ref, wpre_ref, a_ref, g_ref):
    half = GATE_LORA_PAD // 2
    wpre_ref[...] = w0_ref[...] + _dot(jnp.tanh(wl_ref[...]).astype(BF16), wup_ref[...])
    a_ref[...] = jax.nn.sigmoid(a0_ref[...] + _dot(al_ref[...].astype(BF16), aup_ref[...]))
    g_ref[...] = (_dot(jax.nn.sigmoid(gl0_ref[...]).astype(BF16), gup_ref[:half, :])
                  + _dot(jax.nn.sigmoid(gl1_ref[...]).astype(BF16), gup_ref[half:, :]))


def _rwkv_lora(proj_r, w0, w_up, a0, a_up, g_up_pad, tm=256):
    s = proj_r.shape[0]
    w = w0.shape[-1]
    half = GATE_LORA_PAD // 2
    c_wl = 3 * w // DECAY_LORA
    c_al = (3 * w + DECAY_LORA) // AAA_LORA
    c_gl = (3 * w + DECAY_LORA + AAA_LORA) // half
    row = lambda i: (i, 0)
    const = lambda i: (0, 0)
    out = jax.ShapeDtypeStruct((s, w), F32)
    return pl.pallas_call(
        _rwkv_lora_kernel,
        grid=(s // tm,),
        in_specs=[pl.BlockSpec((tm, DECAY_LORA), lambda i: (i, c_wl)),
                  pl.BlockSpec((tm, AAA_LORA), lambda i: (i, c_al)),
                  pl.BlockSpec((tm, half), lambda i: (i, c_gl)),
                  pl.BlockSpec((tm, half), lambda i: (i, c_gl + 1)),
                  pl.BlockSpec((1, w), const),
                  pl.BlockSpec((DECAY_LORA, w), const),
                  pl.BlockSpec((1, w), const),
                  pl.BlockSpec((AAA_LORA, w), const),
                  pl.BlockSpec((GATE_LORA_PAD, w), const)],
        out_specs=[pl.BlockSpec((tm, w), row)] * 3,
        out_shape=[out, out, out],
        compiler_params=_params(("parallel",)),
        name="rwkv_lora",
    )(proj_r, proj_r, proj_r, proj_r, w0.reshape(1, w), w_up, a0.reshape(1, w), a_up, g_up_pad)


def _head_pair_ones(width=LANES):
    r = lax.broadcasted_iota(jnp.int32, (width, width), 0) // RWKV_HEAD_DIM
    c = lax.broadcasted_iota(jnp.int32, (width, width), 1) // RWKV_HEAD_DIM
    return r == c


def _rwkv_prep_kernel(r_ref, k_ref, v_ref, wpre_ref, a_ref, kk_ref, ka_ref, rk_ref,
                      rt_ref, at_ref, kh_ref, bh_ref, kb_ref, bb_ref, pc_ref, bonus_ref):
    tm = r_ref.shape[0]
    r = r_ref[...]
    k = k_ref[...]
    v = v_ref[...]
    a = a_ref[...]
    same_head = jnp.where(_head_pair_ones(r.shape[1]), 1.0, 0.0).astype(BF16)

    x = -wpre_ref[...]
    softplus = jnp.maximum(x, 0.0) + jnp.log1p(jnp.exp(-jnp.abs(x)))
    lw = -jnp.exp(-softplus - 0.5)

    kk = k * kk_ref[...]
    ss = _dot_r3(kk * kk, same_head)
    kkn = kk / jnp.maximum(jnp.sqrt(ss), 1e-12)
    k2 = k * (1.0 + (a - 1.0) * ka_ref[...])
    bonus_ref[...] = _dot_r3(r * k2 * rk_ref[...], same_head) * v

    ri = lax.broadcasted_iota(jnp.int32, (tm, tm), 0)
    ci = lax.broadcasted_iota(jnp.int32, (tm, tm), 1)
    same_chunk = (ri // CHUNK) == (ci // CHUNK)
    tri = jnp.where(same_chunk & (ci <= ri), 1.0, 0.0).astype(BF16)
    allc = jnp.where(same_chunk, 1.0, 0.0).astype(BF16)
    lw_parts = _split3(lw)
    cum = sum(_dot(tri, part) for part in lw_parts)
    tot = sum(_dot(allc, part) for part in lw_parts)

    p_in = jnp.exp(cum)
    p_inv = jnp.exp(-cum)
    p_tail = jnp.exp(tot - cum)
    b = kkn * a
    rt_ref[...] = r * p_in
    at_ref[...] = -kkn * jnp.exp(cum - lw)
    kh_ref[...] = k2 * p_inv
    bh_ref[...] = b * p_inv
    kb_ref[...] = k2 * p_tail
    bb_ref[...] = b * p_tail
    pc_ref[...] = jnp.exp(tot)


def _rwkv_prep(proj_r, wpre, a, k_k, k_a, r_k, tm=256, tw=512):
    s, w = wpre.shape
    ncol = w // tw
    blk = lambda off: pl.BlockSpec((tm, tw), lambda i, j, off=off: (i, j + off))
    vec = pl.BlockSpec((1, tw), lambda i, j: (0, j))
    out = jax.ShapeDtypeStruct((s, w), F32)
    return pl.pallas_call(
        _rwkv_prep_kernel,
        grid=(s // tm, ncol),
        in_specs=[blk(0), blk(ncol), blk(2 * ncol), blk(0), blk(0), vec, vec, vec],
        out_specs=[blk(0)] * 8,
        out_shape=[out] * 8,
        compiler_params=_params(("parallel", "parallel")),
        name="rwkv_prep",
    )(proj_r, proj_r, proj_r, wpre, a, k_k.reshape(1, w), k_a.reshape(1, w), r_k.reshape(1, w))


def _rwkv_chunk_kernel(rt_ref, at_ref, kh_ref, bh_ref, kb_ref, bb_ref, pc_ref, v_ref,
                       x_ref, y_ref, op_ref, qb_ref, bbt_ref, g_ref, pcm_ref):
    c = CHUNK
    nheads = LANES // RWKV_HEAD_DIM
    chunks = range(rt_ref.shape[0] // c)
    pairs = range(rt_ref.shape[1] // LANES)
    rows = [slice(q * c, (q + 1) * c) for q in chunks]
    lanes = [slice(p * LANES, (p + 1) * LANES) for p in pairs]
    tiles = [(q, p) for q in chunks for p in pairs]
    probs = [(n, h) for n in range(len(tiles)) for h in range(nheads)]
    lane = lax.broadcasted_iota(jnp.int32, (c, LANES), 1)
    ri = lax.broadcasted_iota(jnp.int32, (c, c), 0)
    ci = lax.broadcasted_iota(jnp.int32, (c, c), 1)
    strict = ci < ri
    incl = ci <= ri
    eye = jnp.where(ri == ci, 1.0, 0.0)
    head_mask = [(lane // RWKV_HEAD_DIM) == h for h in range(nheads)]
    tile = lambda ref, n: ref[rows[tiles[n][0]], lanes[tiles[n][1]]]

    kh = [tile(kh_ref, n).astype(BF16) for n in range(len(tiles))]
    bh = [tile(bh_ref, n).astype(BF16) for n in range(len(tiles))]
    a_h = [jnp.where(head_mask[h], tile(at_ref, n), 0.0).astype(BF16) for n, h in probs]
    r_h = [jnp.where(head_mask[h], tile(rt_ref, n), 0.0).astype(BF16) for n, h in probs]
    v_h = [jnp.where(head_mask[h], tile(v_ref, n), 0.0).astype(BF16) for n, h in probs]
    n_ab = [jnp.where(strict, _dot_nt(a, bh[n]), 0.0) for a, (n, _) in zip(a_h, probs)]
    n_ak = [jnp.where(strict, _dot_nt(a, kh[n]), 0.0).astype(BF16) for a, (n, _) in zip(a_h, probs)]
    q_k = [jnp.where(incl, _dot_nt(r, kh[n]), 0.0).astype(BF16) for r, (n, _) in zip(r_h, probs)]
    q_b = [jnp.where(incl, _dot_nt(r, bh[n]), 0.0) for r, (n, _) in zip(r_h, probs)]
    for qb, (n, h) in zip(q_b, probs):
        q, p = tiles[n]
        col = (p * nheads + h) * c
        qb_ref[rows[q], col:col + c] = qb.astype(qb_ref.dtype)

    t = [eye + n for n in n_ab]
    pw = n_ab
    for _ in range((c - 1).bit_length() - 1):
        pb = [x.astype(BF16) for x in pw]
        pw = [_dot(x, x) for x in pb]
        t = [ti + _dot(ti.astype(BF16), pi.astype(BF16)) for ti, pi in zip(t, pw)]
    tb = [ti.astype(BF16) for ti in t]
    xs = [_dot(ti, a) for ti, a in zip(tb, a_h)]
    nv = [_dot(n, v).astype(BF16) for n, v in zip(n_ak, v_h)]
    ys = [_dot(ti, z) for ti, z in zip(tb, nv)]
    os = [_dot(qk, v) for qk, v in zip(q_k, v_h)]
    same_head = _head_pair_ones()
    for n, (q, p) in enumerate(tiles):
        mine = [m for m, (nn, _) in enumerate(probs) if nn == n]
        x_ref[rows[q], lanes[p]] = sum(xs[m] for m in mine).astype(x_ref.dtype)
        y_ref[rows[q], lanes[p]] = sum(ys[m] for m in mine)
        op_ref[rows[q], lanes[p]] = sum(os[m] for m in mine)
        kbt = tile(kb_ref, n).T.astype(BF16)
        g_ref[rows[q], lanes[p]] = jnp.where(same_head, _dot(kbt, tile(v_ref, n).astype(BF16)), 0.0)
        bbt_ref[lanes[p], rows[q]] = tile(bb_ref, n).T.astype(bbt_ref.dtype)
        pcm_ref[rows[q], lanes[p]] = tile(pc_ref, n).T


def _rwkv_chunk(rt, at, kh, bh, kb, bb, pc, proj_r, tc=2 * CHUNK, tw=2 * LANES):
    s, w = rt.shape
    ncol = w // tw
    c = CHUNK
    blk = lambda off: pl.BlockSpec((tc, tw), lambda i, j, off=off: (i, j + off))
    f32o = jax.ShapeDtypeStruct((s, w), F32)
    return pl.pallas_call(
        _rwkv_chunk_kernel,
        grid=(s // tc, ncol),
        in_specs=[blk(0)] * 7 + [blk(2 * ncol)],
        out_specs=[blk(0), blk(0), blk(0),
                   pl.BlockSpec((tc, 2 * tw), lambda i, j: (i, j)),
                   pl.BlockSpec((tw, tc), lambda i, j: (j, i)),
                   blk(0), blk(0)],
        out_shape=[jax.ShapeDtypeStruct((s, w), BF16), f32o, f32o,
                   jax.ShapeDtypeStruct((s, 2 * w), BF16),
                   jax.ShapeDtypeStruct((w, s), BF16),
                   f32o, f32o],
        compiler_params=_params(("parallel", "parallel")),
        name="rwkv_chunk",
    )(rt, at, kh, bh, kb, bb, pc, proj_r)


def _rwkv_scan_kernel(x_ref, y_ref, rt_ref, op_ref, qb_ref, bbt_ref, g_ref, pcm_ref,
                      bonus_ref, gate_ref, lng_ref, lnb_ref, ya_ref, h_ref):
    c = CHUNK
    pairs = range(x_ref.shape[1] // LANES)
    lanes = [slice(p * LANES, (p + 1) * LANES) for p in pairs]

    @pl.when(pl.program_id(1) == 0)
    def _():
        h_ref[...] = jnp.zeros_like(h_ref)

    lane = lax.broadcasted_iota(jnp.int32, (c, LANES), 1)
    first = lane < RWKV_HEAD_DIM
    same_head = _head_pair_ones()
    ones_head = jnp.where(same_head, 1.0, 0.0).astype(BF16)
    inv_n = 1.0 / RWKV_HEAD_DIM
    for q in range(x_ref.shape[0] // c):
        rows = slice(q * c, (q + 1) * c)
        hs = [h_ref[p] for p in pairs]
        hb = [h.astype(BF16) for h in hs]
        us = [_dot(x_ref[rows, lanes[p]], hb[p]) + y_ref[rows, lanes[p]] for p in pairs]
        u2 = [jnp.concatenate([jnp.where(first, u, 0.0), jnp.where(first, 0.0, u)], axis=0).astype(BF16)
              for u in us]
        os = [_dot(rt_ref[rows, lanes[p]].astype(BF16), hb[p]) + op_ref[rows, lanes[p]]
              + _dot(qb_ref[rows, p * 2 * c:(p + 1) * 2 * c], u2[p]) for p in pairs]
        upd = [_dot(bbt_ref[lanes[p], rows], us[p].astype(BF16)) for p in pairs]
        for p in pairs:
            h_ref[p] = (pcm_ref[rows, lanes[p]] * hs[p] + g_ref[rows, lanes[p]]
                        + jnp.where(same_head, upd[p], 0.0))
        mus = [_dot_r3(o, ones_head) * inv_n for o in os]
        ds = [o - mu for o, mu in zip(os, mus)]
        vs = [_dot_r3(d * d, ones_head) * inv_n for d in ds]
        for p in pairs:
            y = ds[p] * lax.rsqrt(vs[p] + LN_X_EPS) * lng_ref[:, lanes[p]] + lnb_ref[:, lanes[p]]
            ya_ref[rows, lanes[p]] = ((y + bonus_ref[rows, lanes[p]])
                                      * gate_ref[rows, lanes[p]]).astype(ya_ref.dtype)


def _rwkv_scan(x, y, rt, opre, qb2, bbt, g, pcm, bonus, gate, lnx_g, lnx_b, ts=512, tw=4 * LANES):
    s, w = y.shape
    ts = min(ts, s)
    blk = pl.BlockSpec((ts, tw), lambda j, i: (i, j))
    vec = pl.BlockSpec((1, tw), lambda j, i: (0, j))
    return pl.pallas_call(
        _rwkv_scan_kernel,
        grid=(w // tw, s // ts),
        in_specs=[blk, blk, blk, blk,
                  pl.BlockSpec((ts, 2 * tw), lambda j, i: (i, j)),
                  pl.BlockSpec((tw, ts), lambda j, i: (j, i)),
                  blk, blk, blk, blk, vec, vec],
        out_specs=blk,
        out_shape=jax.ShapeDtypeStruct((s, w), BF16),
        scratch_shapes=[pltpu.VMEM((tw // LANES, LANES, LANES), F32)],
        compiler_params=_params(("parallel", "arbitrary")),
        name="rwkv_scan",
    )(x, y, rt, opre, qb2, bbt, g, pcm, bonus, gate, lnx_g.reshape(1, w), lnx_b.reshape(1, w))


V_ONES_ROWS = 16


LOG2_E = 1.4426950408889634
MOBA_LOG2_SCALE = (MOBA_HEAD_DIM ** -0.5) * LOG2_E


def _moba_prep_kernel(q_ref, k_ref, v_ref, cos_ref, sin_ref, qo_ref, qs_ref, ko_ref, vt_ref, km_ref):
    cos = cos_ref[...]
    sin = sin_ref[...]
    hd = MOBA_HEAD_DIM
    half = hd // 2
    tb = q_ref.shape[0]
    for h in range(q_ref.shape[1] // hd):
        cols = slice(h * hd, (h + 1) * hd)
        q = q_ref[:, cols]
        k = k_ref[:, cols]
        qr = q * cos + pltpu.roll(q, half, axis=1) * sin
        kr = k * cos + pltpu.roll(k, half, axis=1) * sin
        qo_ref[:, cols] = qr.astype(qo_ref.dtype)
        qs_ref[:, cols] = (qr * MOBA_LOG2_SCALE).astype(qs_ref.dtype)
        ko_ref[:, cols] = kr.astype(ko_ref.dtype)
        km_ref[0, :, cols] = jnp.mean(kr, axis=0, keepdims=True)
        vt_ref[h, 0, :hd, :] = v_ref[:, cols].T.astype(vt_ref.dtype)
        vt_ref[h, 0, hd:, :] = jnp.ones((V_ONES_ROWS, tb), vt_ref.dtype)


def _moba_prep(proj_m, cos, sin):
    s = proj_m.shape[0]
    w = proj_m.shape[1] // 3
    tb = MOBA_BLOCK
    nb = s // tb
    nh = w // MOBA_HEAD_DIM
    vrows = MOBA_HEAD_DIM + V_ONES_ROWS
    blk = lambda off: pl.BlockSpec((tb, w), lambda i, off=off: (i, off))
    tab = pl.BlockSpec((tb, MOBA_HEAD_DIM), lambda i: (i, 0))
    bo = jax.ShapeDtypeStruct((s, w), BF16)
    return pl.pallas_call(
        _moba_prep_kernel,
        grid=(nb,),
        in_specs=[blk(0), blk(1), blk(2), tab, tab],
        out_specs=[blk(0), blk(0), blk(0),
                   pl.BlockSpec((nh, 1, vrows, tb), lambda i: (0, i, 0, 0)),
                   pl.BlockSpec((1, 1, w), lambda i: (i, 0, 0))],
        out_shape=[bo, bo, bo, jax.ShapeDtypeStruct((nh, nb, vrows, tb), BF16),
                   jax.ShapeDtypeStruct((nb, 1, w), F32)],
        compiler_params=_params(("parallel",)),
        name="moba_prep",
    )(proj_m, proj_m, proj_m, cos, sin)


MOBA_CHAINS = 4
MOBA_GROUPS = 2
MOBA_HEADS_PER_STEP = 2


def _moba_attn_kernel(q_ref, qs_ref, k_ref, vt_ref, km_ref, o_ref, bias_ref):
    i = pl.program_id(1)
    tb = MOBA_BLOCK
    nb = km_ref.shape[0]
    hd = MOBA_HEAD_DIM
    heads = range(q_ref.shape[1] // hd)
    cols = [slice(h * hd, (h + 1) * hd) for h in heads]
    q = [qs_ref[:, c] for c in cols]

    own = pl.multiple_of(i * tb, tb)
    s_own = [_dot_nt(k_ref[pl.ds(own, tb), cols[h]], q[h]) for h in heads]
    raw0 = [[_dot_nt(k_ref[g * tb:(g + 1) * tb, cols[h]], q[h]) for g in range(MOBA_CHAINS)] for h in heads]

    gates = [_dot_nt(km_ref[:, cols[h]].astype(BF16), q_ref[:, cols[h]]) for h in heads]
    rid = lax.broadcasted_iota(jnp.int32, (nb, tb), 0).astype(F32)
    biases = []
    for h in heads:
        gate = jnp.where(rid < i.astype(F32), gates[h], -jnp.inf)
        bias = jnp.full(gate.shape, -jnp.inf, F32)
        for _ in range(MOBA_TOPK):
            m = jnp.max(gate, axis=0, keepdims=True)
            first = jnp.min(jnp.where(gate == m, rid, float(nb)), axis=0, keepdims=True)
            pick = (rid == first) & (m > -jnp.inf)
            bias = jnp.where(pick, 0.0, bias)
            gate = jnp.where(pick, -jnp.inf, gate)
        bias_ref[h] = bias
        biases.append(bias)

    ss0 = [[r.astype(BF16) + biases[h][g:g + 1, :].astype(BF16) for g, r in enumerate(raw0[h])]
           for h in heads]

    ki = lax.broadcasted_iota(jnp.int32, (tb, tb), 0)
    qi = lax.broadcasted_iota(jnp.int32, (tb, tb), 1)
    init = []
    for h in heads:
        s = jnp.where(ki <= qi, s_own[h], -jnp.inf).astype(BF16)
        m0 = jnp.max(s, axis=0, keepdims=True)
        acc0 = _dot(vt_ref[h, i], jnp.exp2(s - m0))
        m0 = m0.astype(F32)
        init.append(((m0, acc0),) + ((m0, jnp.zeros_like(acc0)),) * (MOBA_CHAINS - 1))

    def blocks(grp):
        return [jnp.minimum(grp * MOBA_CHAINS + g, nb - 1) for g in range(MOBA_CHAINS)]

    def scores(h, grp):
        return [_dot_nt(k_ref[pl.ds(pl.multiple_of(j * tb, tb), tb), cols[h]], q[h]).astype(BF16)
                + bias_ref[h, pl.ds(j, 1), :].astype(BF16) for j in blocks(grp)]

    def update(h, chains, ss, grp):
        ms = [jnp.maximum(m, jnp.max(sj, axis=0, keepdims=True).astype(F32))
              for (m, _), sj in zip(chains, ss)]
        ps = [jnp.exp2(sj - mn.astype(BF16)) for sj, mn in zip(ss, ms)]
        return tuple((mn, jnp.exp2(m - mn) * acc + _dot(vt_ref[h, j], pj))
                     for (m, acc), mn, pj, j in zip(chains, ms, ps, blocks(grp)))

    def body(t, carry):
        first = t * MOBA_GROUPS
        out = []
        for h in heads:
            chains, ss = carry[h]
            later = [scores(h, first + d) for d in range(1, MOBA_GROUPS)]
            ss_next = scores(h, first + MOBA_GROUPS)
            for d, sd in enumerate([ss] + later):
                chains = update(h, chains, sd, first + d)
            out.append((chains, ss_next))
        return tuple(out)

    per_trip = MOBA_CHAINS * MOBA_GROUPS
    state = lax.fori_loop(0, (i + per_trip - 1) // per_trip, body,
                          tuple((init[h], ss0[h]) for h in heads))
    for h in heads:
        chains = state[h][0]
        m = chains[0][0]
        for mg, _ in chains[1:]:
            m = jnp.maximum(m, mg)
        acc = sum(jnp.exp2(mg - m) * ag for mg, ag in chains)
        out = acc[:hd, :] / acc[hd:hd + 1, :]
        o_ref[:, cols[h]] = out.T.astype(o_ref.dtype)


def _moba_attn(q, qs, k, vt, kmean):
    s, w = q.shape
    tb = MOBA_BLOCK
    nb = s // tb
    nh = MOBA_HEADS_PER_STEP
    hd = nh * MOBA_HEAD_DIM
    vrows = vt.shape[2]
    return pl.pallas_call(
        _moba_attn_kernel,
        grid=(w // hd, nb),
        in_specs=[pl.BlockSpec((tb, hd), lambda h, i: (i, h)),
                  pl.BlockSpec((tb, hd), lambda h, i: (i, h)),
                  pl.BlockSpec((s, hd), lambda h, i: (0, h)),
                  pl.BlockSpec((nh, nb, vrows, tb), lambda h, i: (h, 0, 0, 0)),
                  pl.BlockSpec((nb, hd), lambda h, i: (0, h))],
        out_specs=pl.BlockSpec((tb, hd), lambda h, i: (i, h)),
        out_shape=jax.ShapeDtypeStruct((s, w), BF16),
        scratch_shapes=[pltpu.VMEM((nh, nb, tb), F32)],
        compiler_params=_params(("parallel", "arbitrary")),
        name="moba_attn",
    )(q, qs, k, vt, kmean)


def _top_rows(x, n):
    rows = x.shape[0]
    rid = lax.broadcasted_iota(jnp.int32, x.shape, 0).astype(F32)
    vals, idxs = [], []
    for _ in range(n):
        m = jnp.max(x, axis=0, keepdims=True)
        first = jnp.min(jnp.where(x == m, rid, float(rows)), axis=0, keepdims=True)
        vals.append(m)
        idxs.append(first)
        x = jnp.where(rid == first, -jnp.inf, x)
    return jnp.concatenate(vals, axis=0), jnp.concatenate(idxs, axis=0)


def _peer_route_kernel(q_ref, keys_ref, e1_ref, e2_ref, g_ref):
    n = PEER_TOPK
    tm = q_ref.shape[0]
    e1s, e2s, gs = [], [], []
    for h in range(PEER_HEADS):
        tops = []
        for p in range(2):
            hp = 2 * h + p
            cols = slice(hp * PEER_HALF, (hp + 1) * PEER_HALF)
            rows = slice(hp * PEER_N_KEYS, (hp + 1) * PEER_N_KEYS)
            st = _dot_nt(keys_ref[rows, :], q_ref[:, cols].astype(BF16))
            tops.append(_top_rows(st, n))
        (s1, i1), (s2, i2) = tops
        hn = n // 2
        cand = jnp.concatenate([s1[0:1, :] + s2]
                               + [s1[a:a + 1, :] + s2[:hn, :] for a in range(1, hn)]
                               + [s1[hn:, :] + s2[0:1, :]], axis=0)
        f_s, f_pos = _top_rows(cand, n)
        mid = jnp.floor((f_pos - n) * (1.0 / hn))
        tail0 = float(n + (hn - 1) * hn)
        pa = jnp.where(f_pos < n, 0.0, jnp.where(f_pos < tail0, 1.0 + mid, f_pos - tail0 + hn))
        pb = jnp.where(f_pos < n, f_pos, jnp.where(f_pos < tail0, f_pos - n - mid * hn, 0.0))
        e1 = jnp.zeros((n, tm), F32)
        e2 = jnp.zeros((n, tm), F32)
        for a in range(n):
            e1 = jnp.where(pa == float(a), i1[a:a + 1, :], e1)
            e2 = jnp.where(pb == float(a), i2[a:a + 1, :], e2)
        ex = jnp.exp(f_s - f_s[0:1, :])
        gs.append(ex / jnp.sum(ex, axis=0, keepdims=True))
        e1s.append(e1)
        e2s.append(e2)
    e1_ref[...] = jnp.concatenate(e1s, axis=0).T
    e2_ref[...] = jnp.concatenate(e2s, axis=0).T
    g_ref[...] = jnp.concatenate(gs, axis=0).T


def _peer_route(q, keys, tm=256):
    s, w = q.shape
    nsel = PEER_HEADS * PEER_TOPK
    tm = min(tm, s)
    out = jax.ShapeDtypeStruct((s, nsel), F32)
    ob = pl.BlockSpec((tm, nsel), lambda i: (i, 0))
    return pl.pallas_call(
        _peer_route_kernel,
        grid=(s // tm,),
        in_specs=[pl.BlockSpec((tm, w), lambda i: (i, 0)),
                  pl.BlockSpec(keys.shape, lambda i: (0, 0))],
        out_specs=[ob, ob, ob],
        out_shape=[out, out, out],
        compiler_params=_params(("parallel",)),
        name="peer_route",
    )(q, keys)


EXPAND_UNROLL = 16
EXPAND_GROUP = 4


def _peer_expand_kernel(e1_ref, e2_ref, g_ref, w_ref, stage_ref):
    nk = PEER_N_KEYS
    nsel = e1_ref.shape[1]
    rid = lax.broadcasted_iota(jnp.int32, (nk, nsel), 0).astype(F32)
    un = EXPAND_UNROLL

    def body(tt, carry):
        base = pl.multiple_of(tt * un, un)
        e1 = e1_ref[pl.ds(base, un), :]
        e2 = e2_ref[pl.ds(base, un), :]
        g = g_ref[pl.ds(base, un), :]
        for u0 in range(0, un, EXPAND_GROUP):
            us = range(u0, u0 + EXPAND_GROUP)
            lefts = [jnp.where(rid == e1[u:u + 1, :], g[u:u + 1, :], 0.0).astype(BF16) for u in us]
            rights = [jnp.where(rid == e2[u:u + 1, :], 1.0, 0.0).astype(BF16) for u in us]
            for u, l, r in zip(us, lefts, rights):
                stage_ref[u * nk:(u + 1) * nk, :] = _dot_nt(l, r)
        by_key = jnp.swapaxes(stage_ref[...].reshape(un, nk, nk), 0, 1)
        for a in range(nk):
            w_ref[pl.ds(base, un), a * nk:(a + 1) * nk] = by_key[a].astype(w_ref.dtype)
        return carry

    lax.fori_loop(0, e1_ref.shape[0] // un, body, 0)


def _peer_expand(e1, e2, g, tm=128):
    s, nsel = e1.shape
    nk = PEER_N_KEYS
    tm = min(tm, s)
    ib = pl.BlockSpec((tm, nsel), lambda i: (i, 0))
    return pl.pallas_call(
        _peer_expand_kernel,
        grid=(s // tm,),
        in_specs=[ib, ib, ib],
        out_specs=pl.BlockSpec((tm, nk * nk), lambda i: (i, 0)),
        out_shape=jax.ShapeDtypeStruct((s, nk * nk), BF16),
        scratch_shapes=[pltpu.VMEM((EXPAND_UNROLL * nk, nk), F32)],
        compiler_params=_params(("parallel",)),
        name="peer_expand",
    )(e1, e2, g)


FP8 = jnp.float8_e4m3fn
FP8_MAX = 448.0
SCALE_ROWS = 8


def _rmsnorm_fp8_kernel(x_ref, g_ref, o_ref, o8_ref, s_ref):
    x = x_ref[...]
    ms = jnp.mean(x * x, axis=-1, keepdims=True)
    y = x * lax.rsqrt(ms + NORM_EPS) * g_ref[...]
    o_ref[...] = y.astype(o_ref.dtype)
    scale = jnp.maximum(jnp.max(jnp.abs(y), axis=-1, keepdims=True), 1e-30) * (1.0 / FP8_MAX)
    o8_ref[...] = (y / scale).astype(o8_ref.dtype)
    s_ref[...] = jnp.broadcast_to(scale, s_ref.shape)


def _rmsnorm_fp8(x, g, tm=256):
    s, d = x.shape
    blk = pl.BlockSpec((tm, d), lambda i: (i, 0))
    return pl.pallas_call(
        _rmsnorm_fp8_kernel,
        grid=(s // tm,),
        in_specs=[blk, pl.BlockSpec((1, d), lambda i: (0, 0))],
        out_specs=[blk, blk, pl.BlockSpec((tm, LANES), lambda i: (i, 0))],
        out_shape=[jax.ShapeDtypeStruct((s, d), BF16), jax.ShapeDtypeStruct((s, d), FP8),
                   jax.ShapeDtypeStruct((s, LANES), F32)],
        compiler_params=_params(("parallel",)),
        name="rmsnorm_fp8",
    )(x, g.reshape(1, d))


def _quant_rows_kernel(u_ref, u8_ref, s_ref):
    u = u_ref[...]
    scale = jnp.maximum(jnp.max(jnp.abs(u), axis=-1, keepdims=True), 1e-30) * (1.0 / FP8_MAX)
    u8_ref[...] = (u / scale).astype(u8_ref.dtype)
    s_ref[...] = jnp.broadcast_to(scale, (u.shape[0], LANES)).T[:SCALE_ROWS, :]


def _quant_rows(u, te=512):
    ne, d = u.shape
    return pl.pallas_call(
        _quant_rows_kernel,
        grid=(ne // te,),
        in_specs=[pl.BlockSpec((te, d), lambda e: (e, 0))],
        out_specs=[pl.BlockSpec((te, d), lambda e: (e, 0)),
                   pl.BlockSpec((SCALE_ROWS, te), lambda e: (0, e))],
        out_shape=[jax.ShapeDtypeStruct((ne, d), FP8), jax.ShapeDtypeStruct((SCALE_ROWS, ne), F32)],
        compiler_params=_params(("parallel",)),
        name="peer_quant_u",
    )(u)


def _peer_dense_kernel(h_ref, hs_ref, u_ref, us_ref, v_ref, vs_ref, w_ref, o_ref):
    @pl.when(pl.program_id(1) == 0)
    def _():
        o_ref[...] = jnp.zeros_like(o_ref)

    act = _dot_nt(h_ref[...], u_ref[...]) * hs_ref[:, 0:1] * us_ref[0:1, :]
    gelu = 0.5 * act * (1.0 + lax.erf(act * (2.0 ** -0.5)))
    mix = w_ref[...].astype(F32) * gelu * vs_ref[0:1, :]
    scale = jnp.maximum(jnp.max(jnp.abs(mix), axis=-1, keepdims=True), 1e-30) * (1.0 / FP8_MAX)
    o_ref[...] += _dot((mix / scale).astype(FP8), v_ref[...]) * scale


def _peer_dense(h8, hs, u8, us, v8, vs, wdense, tm=512, te=1024):
    s, d = h8.shape
    ne = u8.shape[0]
    tm = min(tm, s)
    return pl.pallas_call(
        _peer_dense_kernel,
        grid=(s // tm, ne // te),
        in_specs=[pl.BlockSpec((tm, d), lambda i, e: (i, 0)),
                  pl.BlockSpec((tm, LANES), lambda i, e: (i, 0)),
                  pl.BlockSpec((te, d), lambda i, e: (e, 0)),
                  pl.BlockSpec((SCALE_ROWS, te), lambda i, e: (0, e)),
                  pl.BlockSpec((te, d), lambda i, e: (e, 0)),
                  pl.BlockSpec((SCALE_ROWS, te), lambda i, e: (0, e)),
                  pl.BlockSpec((tm, te), lambda i, e: (i, e))],
        out_specs=pl.BlockSpec((tm, d), lambda i, e: (i, 0)),
        out_shape=jax.ShapeDtypeStruct((s, d), F32),
        compiler_params=_params(("parallel", "arbitrary")),
        name="peer_dense",
    )(h8, hs, u8, us, v8, vs, wdense)


def _add_rmsnorm_kernel(x_ref, y_ref, g_ref, o_ref):
    x = x_ref[...] + y_ref[...]
    ms = jnp.mean(x * x, axis=-1, keepdims=True)
    o_ref[...] = x * lax.rsqrt(ms + NORM_EPS) * g_ref[...]


def _add_rmsnorm(x, y, g, tm=256):
    s, d = x.shape
    blk = pl.BlockSpec((tm, d), lambda i: (i, 0))
    return pl.pallas_call(
        _add_rmsnorm_kernel,
        grid=(s // tm,),
        in_specs=[blk, blk, pl.BlockSpec((1, d), lambda i: (0, 0))],
        out_specs=blk,
        out_shape=jax.ShapeDtypeStruct((s, d), F32),
        compiler_params=_params(("parallel",)),
        name="residual_final_norm",
    )(x, y, g.reshape(1, d))


def _rwkv_branch(proj_r, w0, w_up, a0, a_up, g_up, k_k, k_a, r_k, lnx_g, lnx_b):
    g_up_pad = jnp.pad(g_up, ((0, GATE_LORA_PAD - GATE_LORA), (0, 0))).astype(BF16)
    wpre, a, g = _rwkv_lora(proj_r, w0, w_up.astype(BF16), a0, a_up.astype(BF16), g_up_pad)
    rt, at, kh, bh, kb, bb, pc, bonus = _rwkv_prep(proj_r, wpre, a, k_k, k_a, r_k.reshape(-1))
    x, y, opre, qb2, bbt, gst, pcm = _rwkv_chunk(rt, at, kh, bh, kb, bb, pc, proj_r)
    return _rwkv_scan(x, y, rt, opre, qb2, bbt, gst, pcm, bonus, g, lnx_g, lnx_b)


def _rope_tables(s):
    half = MOBA_HEAD_DIM // 2
    inv_freq = ROPE_THETA ** (-jnp.arange(half, dtype=F32) / half)
    ang = jnp.arange(s, dtype=jnp.int32).astype(F32)[:, None] * inv_freq[None, :]
    cos = jnp.cos(ang)
    sin = jnp.sin(ang)
    return jnp.concatenate([cos, cos], axis=-1), jnp.concatenate([-sin, sin], axis=-1)


def _moba_branch(proj_m):
    s = proj_m.shape[0]
    cos, sin = _rope_tables(s)
    q, qs, k, vt, kmean = _moba_prep(proj_m, cos, sin)
    return _moba_attn(q, qs, k, vt, kmean.reshape(kmean.shape[0], kmean.shape[2]))


def _peer_layer(x1, norm2_g, w_q, sub_keys, expert_u, expert_v, final_g):
    s, d = x1.shape
    h2, h8, hs = _rmsnorm_fp8(x1, norm2_g)
    q = _matmul(h2, w_q.astype(BF16), F32, tm=min(1024, s), tn=512, name="peer_query")
    keys = sub_keys.reshape(PEER_HEADS * 2 * PEER_N_KEYS, PEER_HALF).astype(BF16)
    e1, e2, g = _peer_route(q, keys)
    wdense = _peer_expand(e1, e2, g)
    u8, us = _quant_rows(expert_u)
    v8, vs = _quant_rows(expert_v)
    peer = _peer_dense(h8, hs, u8, us, v8, vs, wdense)
    return _add_rmsnorm(x1, peer, final_g)


def kernel(x, norm1_g, w_in, rwkv_mu, rwkv_w0, rwkv_w_up, rwkv_a0, rwkv_a_up, rwkv_g_up, rwkv_k_k, rwkv_k_a, rwkv_r_k, rwkv_lnx_g, rwkv_lnx_b, w_branch_rwkv, w_branch_moba, w_out, norm2_g, peer_w_q, peer_sub_keys, peer_u, peer_v, final_g):
    b, s, d = x.shape
    depth = w_in.shape[0]
    assert b == 1 and depth == 1
    x2d = x.reshape(s, d)
    l = 0
    rw = rwkv_w0.shape[-1]
    shift_w = 3 * rw + DECAY_LORA + AAA_LORA + GATE_LORA
    tn = 512
    shift_pad = -(-shift_w // tn) * tn
    mw = w_branch_moba.shape[1]
    tm = min(1024, s)

    h = _rmsnorm(x2d, norm1_g[l], BF16)
    wt = jnp.transpose(w_in[l])
    mu = jnp.pad(rwkv_mu[l], (0, shift_pad - shift_w)).reshape(1, shift_pad)
    proj_r = _matmul_shift(h, wt, mu, tm=tm, tn=tn)
    wt_mg = _repack_rows(wt, shift_w, 3 * mw + 2 * d)
    proj_m = _matmul_nt(h, wt_mg, F32, tm=tm, tn=tn, name="in_proj_moba", row0=0, n=3 * mw)
    gates = _matmul_nt(h, wt_mg, F32, tm=tm, tn=tn, name="in_proj_gates", row0=3 * mw, n=2 * d)

    y_a = _rwkv_branch(proj_r, rwkv_w0[l], rwkv_w_up[l], rwkv_a0[l], rwkv_a_up[l], rwkv_g_up[l],
                       rwkv_k_k[l], rwkv_k_a[l], rwkv_r_k[l], rwkv_lnx_g[l], rwkv_lnx_b[l])
    y_b = _moba_branch(proj_m)
    mixed = _gated_pair(y_a, w_branch_rwkv[l].astype(BF16), y_b, w_branch_moba[l].astype(BF16),
                        gates, tm=tm, tn=512)
    x1 = _matmul_residual(mixed, w_out[l].astype(BF16), x2d, tm=tm, tn=512)
    out = _peer_layer(x1, norm2_g[l], peer_w_q[l], peer_sub_keys[l], peer_u[l], peer_v[l], final_g)
    return out.reshape(b, s, d)
```

```python
import functools

import jax
import jax.numpy as jnp
from jax import lax
from jax.experimental import pallas as pl
from jax.experimental.pallas import tpu as pltpu

F32 = jnp.float32
BF16 = jnp.bfloat16

NORM_EPS = 1e-6
LANES = 128
F32_SUBLANES = 8
RWKV_HEAD_DIM = 64
DECAY_LORA = 128
AAA_LORA = 128
GATE_LORA = 480
GATE_LORA_PAD = 512
LN_X_EPS = 64e-5
CHUNK = 128
MOBA_HEAD_DIM = 128
MOBA_BLOCK = 256
MOBA_TOPK = 3
ROPE_THETA = 10000.0
PEER_HEADS = 8
PEER_N_KEYS = 128
PEER_HALF = 128
PEER_TOPK = 16

VMEM_LIMIT = 56 * 1024 * 1024

_NT = (((1,), (1,)), ((), ()))


def _params(sem, vmem=VMEM_LIMIT):
    return pltpu.CompilerParams(dimension_semantics=sem, vmem_limit_bytes=vmem)


def _dot(a, b):
    return jnp.dot(a, b, preferred_element_type=F32)


def _dot_nt(a, b):
    return lax.dot_general(a, b, _NT, preferred_element_type=F32)


def _split3(x):
    hi = x.astype(BF16)
    r1 = x - hi.astype(F32)
    mid = r1.astype(BF16)
    lo = (r1 - mid.astype(F32)).astype(BF16)
    return hi, mid, lo


def _dot_r3(x, r_bf16):
    hi, mid, lo = _split3(x)
    return _dot(hi, r_bf16) + _dot(mid, r_bf16) + _dot(lo, r_bf16)


def _rmsnorm_kernel(x_ref, g_ref, o_ref):
    x = x_ref[...]
    ms = jnp.mean(x * x, axis=-1, keepdims=True)
    o_ref[...] = (x * lax.rsqrt(ms + NORM_EPS) * g_ref[...]).astype(o_ref.dtype)


def _rmsnorm(x, g, out_dtype, tm=256):
    s, d = x.shape
    return pl.pallas_call(
        _rmsnorm_kernel,
        grid=(s // tm,),
        in_specs=[pl.BlockSpec((tm, d), lambda i: (i, 0)),
                  pl.BlockSpec((1, d), lambda i: (0, 0))],
        out_specs=pl.BlockSpec((tm, d), lambda i: (i, 0)),
        out_shape=jax.ShapeDtypeStruct((s, d), out_dtype),
        compiler_params=_params(("parallel",)),
        name="rmsnorm",
    )(x, g.reshape(1, d))


def _mm_kernel(a_ref, b_ref, o_ref):
    o_ref[...] = _dot(a_ref[...], b_ref[...]).astype(o_ref.dtype)


def _matmul(a, b, out_dtype, tm, tn, name, col0=0, n=None):
    m, k = a.shape
    n = b.shape[1] if n is None else n
    off = col0 // tn
    return pl.pallas_call(
        _mm_kernel,
        grid=(m // tm, n // tn),
        in_specs=[pl.BlockSpec((tm, k), lambda i, j: (i, 0)),
                  pl.BlockSpec((k, tn), lambda i, j: (0, j + off))],
        out_specs=pl.BlockSpec((tm, tn), lambda i, j: (i, j)),
        out_shape=jax.ShapeDtypeStruct((m, n), out_dtype),
        compiler_params=_params(("parallel", "parallel")),
        name=name,
    )(a, b)


BF16_SUBLANES = 16


def _mm_nt_kernel(a_ref, bt_ref, o_ref, b16_ref):
    @pl.when(pl.program_id(1) == 0)
    def _():
        b16_ref[...] = bt_ref[...].astype(b16_ref.dtype)

    o_ref[...] = _dot_nt(a_ref[...], b16_ref[...]).astype(o_ref.dtype)


def _matmul_nt(a, bt, out_dtype, tm, tn, name, row0, n):
    m, k = a.shape
    return pl.pallas_call(
        _mm_nt_kernel,
        grid=(n // tn, m // tm),
        in_specs=[pl.BlockSpec((tm, k), lambda j, i: (i, 0)),
                  pl.BlockSpec((pl.Element(tn), pl.Element(k)),
                               lambda j, i: (pl.multiple_of(row0 + j * tn, F32_SUBLANES), 0))],
        out_specs=pl.BlockSpec((tm, tn), lambda j, i: (i, j)),
        out_shape=jax.ShapeDtypeStruct((m, n), out_dtype),
        scratch_shapes=[pltpu.VMEM((tn, k), BF16)],
        compiler_params=_params(("parallel", "arbitrary")),
        name=name,
    )(a, bt)


def _mm_shift_kernel(a_ref, ap_ref, bt_ref, mu_ref, o_ref, b16_ref):
    i = pl.program_id(1)

    @pl.when(i == 0)
    def _():
        b16_ref[...] = bt_ref[...].astype(b16_ref.dtype)

    b = b16_ref[...]
    z = _dot_nt(a_ref[...], b)
    zp = _dot_nt(ap_ref[...], b)
    prev = jnp.where(i > 0, zp[BF16_SUBLANES - 1:BF16_SUBLANES, :], 0.0)
    row = lax.broadcasted_iota(jnp.int32, z.shape, 0)
    z_prev = jnp.where(row == 0, prev, pltpu.roll(z, 1, axis=0))
    o_ref[...] = z + (z_prev - z) * mu_ref[...]


def _matmul_shift(a, bt, mu, tm, tn):
    m, k = a.shape
    n = mu.shape[1]
    per = tm // BF16_SUBLANES
    return pl.pallas_call(
        _mm_shift_kernel,
        grid=(n // tn, m // tm),
        in_specs=[pl.BlockSpec((tm, k), lambda j, i: (i, 0)),
                  pl.BlockSpec((BF16_SUBLANES, k), lambda j, i: (jnp.maximum(i * per - 1, 0), 0)),
                  pl.BlockSpec((tn, k), lambda j, i: (j, 0)),
                  pl.BlockSpec((1, tn), lambda j, i: (0, j))],
        out_specs=pl.BlockSpec((tm, tn), lambda j, i: (i, j)),
        out_shape=jax.ShapeDtypeStruct((m, n), F32),
        scratch_shapes=[pltpu.VMEM((tn, k), BF16)],
        compiler_params=_params(("parallel", "arbitrary")),
        name="in_proj_rwkv",
    )(a, a, bt, mu)


def _gated_pair_kernel(ya_ref, wa_ref, yb_ref, wb_ref, ga_ref, gb_ref, o_ref):
    pa = _dot(ya_ref[...], wa_ref[...])
    pb = _dot(yb_ref[...], wb_ref[...])
    mixed = jax.nn.sigmoid(ga_ref[...]) * pa + jax.nn.sigmoid(gb_ref[...]) * pb
    o_ref[...] = mixed.astype(o_ref.dtype)


def _gated_pair(ya, wa, yb, wb, gates, tm, tn):
    m, k = ya.shape
    n = wa.shape[1]
    nb = n // tn
    return pl.pallas_call(
        _gated_pair_kernel,
        grid=(m // tm, nb),
        in_specs=[pl.BlockSpec((tm, k), lambda i, j: (i, 0)),
                  pl.BlockSpec((k, tn), lambda i, j: (0, j)),
                  pl.BlockSpec((tm, k), lambda i, j: (i, 0)),
                  pl.BlockSpec((k, tn), lambda i, j: (0, j)),
                  pl.BlockSpec((tm, tn), lambda i, j: (i, j)),
                  pl.BlockSpec((tm, tn), lambda i, j: (i, j + nb))],
        out_specs=pl.BlockSpec((tm, tn), lambda i, j: (i, j)),
        out_shape=jax.ShapeDtypeStruct((m, n), BF16),
        compiler_params=_params(("parallel", "parallel")),
        name="branch_merge",
    )(ya, wa, yb, wb, gates, gates)


def _mm_res_kernel(a_ref, b_ref, r_ref, o_ref):
    o_ref[...] = r_ref[...] + _dot(a_ref[...], b_ref[...])


def _matmul_residual(a, b, res, tm, tn):
    m, k = a.shape
    n = b.shape[1]
    return pl.pallas_call(
        _mm_res_kernel,
        grid=(m // tm, n // tn),
        in_specs=[pl.BlockSpec((tm, k), lambda i, j: (i, 0)),
                  pl.BlockSpec((k, tn), lambda i, j: (0, j)),
                  pl.BlockSpec((tm, tn), lambda i, j: (i, j))],
        out_specs=pl.BlockSpec((tm, tn), lambda i, j: (i, j)),
        out_shape=jax.ShapeDtypeStruct((m, n), F32),
        compiler_params=_params(("parallel", "parallel")),
        name="out_proj",
    )(a, b, res)


def _rwkv_lora_kernel(wl_ref, al_ref, gl0_ref, gl1_ref, w0_ref, wup_ref, a0_ref, aup_ref,
                      gup_ref, wpre_ref, a_ref, g_ref):
    half = GATE_LORA_PAD // 2
    wpre_ref[...] = w0_ref[...] + _dot(jnp.tanh(wl_ref[...]).astype(BF16), wup_ref[...])
    a_ref[...] = jax.nn.sigmoid(a0_ref[...] + _dot(al_ref[...].astype(BF16), aup_ref[...]))
    g_ref[...] = (_dot(jax.nn.sigmoid(gl0_ref[...]).astype(BF16), gup_ref[:half, :])
                  + _dot(jax.nn.sigmoid(gl1_ref[...]).astype(BF16), gup_ref[half:, :]))


def _rwkv_lora(proj_r, w0, w_up, a0, a_up, g_up_pad, tm=256):
    s = proj_r.shape[0]
    w = w0.shape[-1]
    half = GATE_LORA_PAD // 2
    c_wl = 3 * w // DECAY_LORA
    c_al = (3 * w + DECAY_LORA) // AAA_LORA
    c_gl = (3 * w + DECAY_LORA + AAA_LORA) // half
    row = lambda i: (i, 0)
    const = lambda i: (0, 0)
    out = jax.ShapeDtypeStruct((s, w), F32)
    return pl.pallas_call(
        _rwkv_lora_kernel,
        grid=(s // tm,),
        in_specs=[pl.BlockSpec((tm, DECAY_LORA), lambda i: (i, c_wl)),
                  pl.BlockSpec((tm, AAA_LORA), lambda i: (i, c_al)),
                  pl.BlockSpec((tm, half), lambda i: (i, c_gl)),
                  pl.BlockSpec((tm, half), lambda i: (i, c_gl + 1)),
                  pl.BlockSpec((1, w), const),
                  pl.BlockSpec((DECAY_LORA, w), const),
                  pl.BlockSpec((1, w), const),
                  pl.BlockSpec((AAA_LORA, w), const),
                  pl.BlockSpec((GATE_LORA_PAD, w), const)],
        out_specs=[pl.BlockSpec((tm, w), row)] * 3,
        out_shape=[out, out, out],
        compiler_params=_params(("parallel",)),
        name="rwkv_lora",
    )(proj_r, proj_r, proj_r, proj_r, w0.reshape(1, w), w_up, a0.reshape(1, w), a_up, g_up_pad)


def _head_pair_ones(width=LANES):
    r = lax.broadcasted_iota(jnp.int32, (width, width), 0) // RWKV_HEAD_DIM
    c = lax.broadcasted_iota(jnp.int32, (width, width), 1) // RWKV_HEAD_DIM
    return r == c


def _rwkv_prep_kernel(r_ref, k_ref, v_ref, wpre_ref, a_ref, kk_ref, ka_ref, rk_ref,
                      rt_ref, at_ref, kh_ref, bh_ref, kb_ref, bb_ref, pc_ref, bonus_ref):
    tm = r_ref.shape[0]
    r = r_ref[...]
    k = k_ref[...]
    v = v_ref[...]
    a = a_ref[...]
    same_head = jnp.where(_head_pair_ones(r.shape[1]), 1.0, 0.0).astype(BF16)

    x = -wpre_ref[...]
    softplus = jnp.maximum(x, 0.0) + jnp.log1p(jnp.exp(-jnp.abs(x)))
    lw = -jnp.exp(-softplus - 0.5)

    kk = k * kk_ref[...]
    ss = _dot_r3(kk * kk, same_head)
    kkn = kk / jnp.maximum(jnp.sqrt(ss), 1e-12)
    k2 = k * (1.0 + (a - 1.0) * ka_ref[...])
    bonus_ref[...] = _dot_r3(r * k2 * rk_ref[...], same_head) * v

    ri = lax.broadcasted_iota(jnp.int32, (tm, tm), 0)
    ci = lax.broadcasted_iota(jnp.int32, (tm, tm), 1)
    same_chunk = (ri // CHUNK) == (ci // CHUNK)
    tri = jnp.where(same_chunk & (ci <= ri), 1.0, 0.0).astype(BF16)
    allc = jnp.where(same_chunk, 1.0, 0.0).astype(BF16)
    lw_parts = _split3(lw)
    cum = sum(_dot(tri, part) for part in lw_parts)
    tot = sum(_dot(allc, part) for part in lw_parts)

    p_in = jnp.exp(cum)
    p_inv = jnp.exp(-cum)
    p_tail = jnp.exp(tot - cum)
    b = kkn * a
    rt_ref[...] = r * p_in
    at_ref[...] = -kkn * jnp.exp(cum - lw)
    kh_ref[...] = k2 * p_inv
    bh_ref[...] = b * p_inv
    kb_ref[...] = k2 * p_tail
    bb_ref[...] = b * p_tail
    pc_ref[...] = jnp.exp(tot)


def _rwkv_prep(proj_r, wpre, a, k_k, k_a, r_k, tm=256, tw=512):
    s, w = wpre.shape
    ncol = w // tw
    blk = lambda off: pl.BlockSpec((tm, tw), lambda i, j, off=off: (i, j + off))
    vec = pl.BlockSpec((1, tw), lambda i, j: (0, j))
    out = jax.ShapeDtypeStruct((s, w), F32)
    return pl.pallas_call(
        _rwkv_prep_kernel,
        grid=(s // tm, ncol),
        in_specs=[blk(0), blk(ncol), blk(2 * ncol), blk(0), blk(0), vec, vec, vec],
        out_specs=[blk(0)] * 8,
        out_shape=[out] * 8,
        compiler_params=_params(("parallel", "parallel")),
        name="rwkv_prep",
    )(proj_r, proj_r, proj_r, wpre, a, k_k.reshape(1, w), k_a.reshape(1, w), r_k.reshape(1, w))


def _rwkv_chunk_kernel(rt_ref, at_ref, kh_ref, bh_ref, kb_ref, bb_ref, pc_ref, v_ref,
                       x_ref, y_ref, op_ref, qb_ref, bbt_ref, g_ref, pcm_ref):
    c = CHUNK
    nheads = LANES // RWKV_HEAD_DIM
    chunks = range(rt_ref.shape[0] // c)
    pairs = range(rt_ref.shape[1] // LANES)
    rows = [slice(q * c, (q + 1) * c) for q in chunks]
    lanes = [slice(p * LANES, (p + 1) * LANES) for p in pairs]
    tiles = [(q, p) for q in chunks for p in pairs]
    probs = [(n, h) for n in range(len(tiles)) for h in range(nheads)]
    lane = lax.broadcasted_iota(jnp.int32, (c, LANES), 1)
    ri = lax.broadcasted_iota(jnp.int32, (c, c), 0)
    ci = lax.broadcasted_iota(jnp.int32, (c, c), 1)
    strict = ci < ri
    incl = ci <= ri
    eye = jnp.where(ri == ci, 1.0, 0.0)
    head_mask = [(lane // RWKV_HEAD_DIM) == h for h in range(nheads)]
    tile = lambda ref, n: ref[rows[tiles[n][0]], lanes[tiles[n][1]]]

    kh = [tile(kh_ref, n).astype(BF16) for n in range(len(tiles))]
    bh = [tile(bh_ref, n).astype(BF16) for n in range(len(tiles))]
    a_h = [jnp.where(head_mask[h], tile(at_ref, n), 0.0).astype(BF16) for n, h in probs]
    r_h = [jnp.where(head_mask[h], tile(rt_ref, n), 0.0).astype(BF16) for n, h in probs]
    v_h = [jnp.where(head_mask[h], tile(v_ref, n), 0.0).astype(BF16) for n, h in probs]
    n_ab = [jnp.where(strict, _dot_nt(a, bh[n]), 0.0) for a, (n, _) in zip(a_h, probs)]
    n_ak = [jnp.where(strict, _dot_nt(a, kh[n]), 0.0).astype(BF16) for a, (n, _) in zip(a_h, probs)]
    q_k = [jnp.where(incl, _dot_nt(r, kh[n]), 0.0).astype(BF16) for r, (n, _) in zip(r_h, probs)]
    q_b = [jnp.where(incl, _dot_nt(r, bh[n]), 0.0) for r, (n, _) in zip(r_h, probs)]
    for qb, (n, h) in zip(q_b, probs):
        q, p = tiles[n]
        col = (p * nheads + h) * c
        qb_ref[rows[q], col:col + c] = qb.astype(qb_ref.dtype)

    t = [eye + n for n in n_ab]
    pw = n_ab
    for _ in range((c - 1).bit_length() - 1):
        pb = [x.astype(BF16) for x in pw]
        pw = [_dot(x, x) for x in pb]
        t = [ti + _dot(ti.astype(BF16), pi.astype(BF16)) for ti, pi in zip(t, pw)]
    tb = [ti.astype(BF16) for ti in t]
    xs = [_dot(ti, a) for ti, a in zip(tb, a_h)]
    nv = [_dot(n, v).astype(BF16) for n, v in zip(n_ak, v_h)]
    ys = [_dot(ti, z) for ti, z in zip(tb, nv)]
    os = [_dot(qk, v) for qk, v in zip(q_k, v_h)]
    same_head = _head_pair_ones()
    for n, (q, p) in enumerate(tiles):
        mine = [m for m, (nn, _) in enumerate(probs) if nn == n]
        x_ref[rows[q], lanes[p]] = sum(xs[m] for m in mine).astype(x_ref.dtype)
        y_ref[rows[q], lanes[p]] = sum(ys[m] for m in mine)
        op_ref[rows[q], lanes[p]] = sum(os[m] for m in mine)
        kbt = tile(kb_ref, n).T.astype(BF16)
        g_ref[rows[q], lanes[p]] = jnp.where(same_head, _dot(kbt, tile(v_ref, n).astype(BF16)), 0.0)
        bbt_ref[lanes[p], rows[q]] = tile(bb_ref, n).T.astype(bbt_ref.dtype)
        pcm_ref[rows[q], lanes[p]] = tile(pc_ref, n).T


def _rwkv_chunk(rt, at, kh, bh, kb, bb, pc, proj_r, tc=2 * CHUNK, tw=2 * LANES):
    s, w = rt.shape
    ncol = w // tw
    c = CHUNK
    blk = lambda off: pl.BlockSpec((tc, tw), lambda i, j, off=off: (i, j + off))
    f32o = jax.ShapeDtypeStruct((s, w), F32)
    return pl.pallas_call(
        _rwkv_chunk_kernel,
        grid=(s // tc, ncol),
        in_specs=[blk(0)] * 7 + [blk(2 * ncol)],
        out_specs=[blk(0), blk(0), blk(0),
                   pl.BlockSpec((tc, 2 * tw), lambda i, j: (i, j)),
                   pl.BlockSpec((tw, tc), lambda i, j: (j, i)),
                   blk(0), blk(0)],
        out_shape=[jax.ShapeDtypeStruct((s, w), BF16), f32o, f32o,
                   jax.ShapeDtypeStruct((s, 2 * w), BF16),
                   jax.ShapeDtypeStruct((w, s), BF16),
                   f32o, f32o],
        compiler_params=_params(("parallel", "parallel")),
        name="rwkv_chunk",
    )(rt, at, kh, bh, kb, bb, pc, proj_r)


def _rwkv_scan_kernel(x_ref, y_ref, rt_ref, op_ref, qb_ref, bbt_ref, g_ref, pcm_ref,
                      bonus_ref, gate_ref, lng_ref, lnb_ref, ya_ref, h_ref):
    c = CHUNK
    pairs = range(x_ref.shape[1] // LANES)
    lanes = [slice(p * LANES, (p + 1) * LANES) for p in pairs]

    @pl.when(pl.program_id(1) == 0)
    def _():
        h_ref[...] = jnp.zeros_like(h_ref)

    lane = lax.broadcasted_iota(jnp.int32, (c, LANES), 1)
    first = lane < RWKV_HEAD_DIM
    same_head = _head_pair_ones()
    ones_head = jnp.where(same_head, 1.0, 0.0).astype(BF16)
    inv_n = 1.0 / RWKV_HEAD_DIM
    for q in range(x_ref.shape[0] // c):
        rows = slice(q * c, (q + 1) * c)
        hs = [h_ref[p] for p in pairs]
        hb = [h.astype(BF16) for h in hs]
        us = [_dot(x_ref[rows, lanes[p]], hb[p]) + y_ref[rows, lanes[p]] for p in pairs]
        u2 = [jnp.concatenate([jnp.where(first, u, 0.0), jnp.where(first, 0.0, u)], axis=0).astype(BF16)
              for u in us]
        os = [_dot(rt_ref[rows, lanes[p]].astype(BF16), hb[p]) + op_ref[rows, lanes[p]]
              + _dot(qb_ref[rows, p * 2 * c:(p + 1) * 2 * c], u2[p]) for p in pairs]
        upd = [_dot(bbt_ref[lanes[p], rows], us[p].astype(BF16)) for p in pairs]
        for p in pairs:
            h_ref[p] = (pcm_ref[rows, lanes[p]] * hs[p] + g_ref[rows, lanes[p]]
                        + jnp.where(same_head, upd[p], 0.0))
        mus = [_dot_r3(o, ones_head) * inv_n for o in os]
        ds = [o - mu for o, mu in zip(os, mus)]
        vs = [_dot_r3(d * d, ones_head) * inv_n for d in ds]
        for p in pairs:
            y = ds[p] * lax.rsqrt(vs[p] + LN_X_EPS) * lng_ref[:, lanes[p]] + lnb_ref[:, lanes[p]]
            ya_ref[rows, lanes[p]] = ((y + bonus_ref[rows, lanes[p]])
                                      * gate_ref[rows, lanes[p]]).astype(ya_ref.dtype)


def _rwkv_scan(x, y, rt, opre, qb2, bbt, g, pcm, bonus, gate, lnx_g, lnx_b, ts=512, tw=4 * LANES):
    s, w = y.shape
    ts = min(ts, s)
    blk = pl.BlockSpec((ts, tw), lambda j, i: (i, j))
    vec = pl.BlockSpec((1, tw), lambda j, i: (0, j))
    return pl.pallas_call(
        _rwkv_scan_kernel,
        grid=(w // tw, s // ts),
        in_specs=[blk, blk, blk, blk,
                  pl.BlockSpec((ts, 2 * tw), lambda j, i: (i, j)),
                  pl.BlockSpec((tw, ts), lambda j, i: (j, i)),
                  blk, blk, blk, blk, vec, vec],
        out_specs=blk,
        out_shape=jax.ShapeDtypeStruct((s, w), BF16),
        scratch_shapes=[pltpu.VMEM((tw // LANES, LANES, LANES), F32)],
        compiler_params=_params(("parallel", "arbitrary")),
        name="rwkv_scan",
    )(x, y, rt, opre, qb2, bbt, g, pcm, bonus, gate, lnx_g.reshape(1, w), lnx_b.reshape(1, w))


V_ONES_ROWS = 16


LOG2_E = 1.4426950408889634
MOBA_LOG2_SCALE = (MOBA_HEAD_DIM ** -0.5) * LOG2_E


def _moba_prep_kernel(q_ref, k_ref, v_ref, cos_ref, sin_ref, qo_ref, qs_ref, ko_ref, vt_ref, km_ref):
    cos = cos_ref[...]
    sin = sin_ref[...]
    hd = MOBA_HEAD_DIM
    half = hd // 2
    tb = q_ref.shape[0]
    for h in range(q_ref.shape[1] // hd):
        cols = slice(h * hd, (h + 1) * hd)
        q = q_ref[:, cols]
        k = k_ref[:, cols]
        qr = q * cos + pltpu.roll(q, half, axis=1) * sin
        kr = k * cos + pltpu.roll(k, half, axis=1) * sin
        qo_ref[:, cols] = qr.astype(qo_ref.dtype)
        qs_ref[:, cols] = (qr * MOBA_LOG2_SCALE).astype(qs_ref.dtype)
        ko_ref[:, cols] = kr.astype(ko_ref.dtype)
        km_ref[0, :, cols] = jnp.mean(kr, axis=0, keepdims=True)
        vt_ref[h, 0, :hd, :] = v_ref[:, cols].T.astype(vt_ref.dtype)
        vt_ref[h, 0, hd:, :] = jnp.ones((V_ONES_ROWS, tb), vt_ref.dtype)


def _moba_prep(proj_m, cos, sin):
    s = proj_m.shape[0]
    w = proj_m.shape[1] // 3
    tb = MOBA_BLOCK
    nb = s // tb
    nh = w // MOBA_HEAD_DIM
    vrows = MOBA_HEAD_DIM + V_ONES_ROWS
    blk = lambda off: pl.BlockSpec((tb, w), lambda i, off=off: (i, off))
    tab = pl.BlockSpec((tb, MOBA_HEAD_DIM), lambda i: (i, 0))
    bo = jax.ShapeDtypeStruct((s, w), BF16)
    return pl.pallas_call(
        _moba_prep_kernel,
        grid=(nb,),
        in_specs=[blk(0), blk(1), blk(2), tab, tab],
        out_specs=[blk(0), blk(0), blk(0),
                   pl.BlockSpec((nh, 1, vrows, tb), lambda i: (0, i, 0, 0)),
                   pl.BlockSpec((1, 1, w), lambda i: (i, 0, 0))],
        out_shape=[bo, bo, bo, jax.ShapeDtypeStruct((nh, nb, vrows, tb), BF16),
                   jax.ShapeDtypeStruct((nb, 1, w), F32)],
        compiler_params=_params(("parallel",)),
        name="moba_prep",
    )(proj_m, proj_m, proj_m, cos, sin)


MOBA_CHAINS = 4
MOBA_GROUPS = 2
MOBA_HEADS_PER_STEP = 2


def _moba_attn_kernel(q_ref, qs_ref, k_ref, vt_ref, km_ref, o_ref, bias_ref):
    i = pl.program_id(1)
    tb = MOBA_BLOCK
    nb = km_ref.shape[0]
    hd = MOBA_HEAD_DIM
    heads = range(q_ref.shape[1] // hd)
    cols = [slice(h * hd, (h + 1) * hd) for h in heads]
    q = [qs_ref[:, c] for c in cols]

    own = pl.multiple_of(i * tb, tb)
    s_own = [_dot_nt(k_ref[pl.ds(own, tb), cols[h]], q[h]) for h in heads]
    raw0 = [[_dot_nt(k_ref[g * tb:(g + 1) * tb, cols[h]], q[h]) for g in range(MOBA_CHAINS)] for h in heads]

    gates = [_dot_nt(km_ref[:, cols[h]].astype(BF16), q_ref[:, cols[h]]) for h in heads]
    rid = lax.broadcasted_iota(jnp.int32, (nb, tb), 0).astype(F32)
    biases = []
    for h in heads:
        gate = jnp.where(rid < i.astype(F32), gates[h], -jnp.inf)
        bias = jnp.full(gate.shape, -jnp.inf, F32)
        for _ in range(MOBA_TOPK):
            m = jnp.max(gate, axis=0, keepdims=True)
            first = jnp.min(jnp.where(gate == m, rid, float(nb)), axis=0, keepdims=True)
            pick = (rid == first) & (m > -jnp.inf)
            bias = jnp.where(pick, 0.0, bias)
            gate = jnp.where(pick, -jnp.inf, gate)
        bias_ref[h] = bias
        biases.append(bias)

    ss0 = [[r.astype(BF16) + biases[h][g:g + 1, :].astype(BF16) for g, r in enumerate(raw0[h])]
           for h in heads]

    ki = lax.broadcasted_iota(jnp.int32, (tb, tb), 0)
    qi = lax.broadcasted_iota(jnp.int32, (tb, tb), 1)
    init = []
    for h in heads:
        s = jnp.where(ki <= qi, s_own[h], -jnp.inf).astype(BF16)
        m0 = jnp.max(s, axis=0, keepdims=True)
        acc0 = _dot(vt_ref[h, i], jnp.exp2(s - m0))
        m0 = m0.astype(F32)
        init.append(((m0, acc0),) + ((m0, jnp.zeros_like(acc0)),) * (MOBA_CHAINS - 1))

    def blocks(grp):
        return [jnp.minimum(grp * MOBA_CHAINS + g, nb - 1) for g in range(MOBA_CHAINS)]

    def scores(h, grp):
        return [_dot_nt(k_ref[pl.ds(pl.multiple_of(j * tb, tb), tb), cols[h]], q[h]).astype(BF16)
                + bias_ref[h, pl.ds(j, 1), :].astype(BF16) for j in blocks(grp)]

    def update(h, chains, ss, grp):
        ms = [jnp.maximum(m, jnp.max(sj, axis=0, keepdims=True).astype(F32))
              for (m, _), sj in zip(chains, ss)]
        ps = [jnp.exp2(sj - mn.astype(BF16)) for sj, mn in zip(ss, ms)]
        return tuple((mn, jnp.exp2(m - mn) * acc + _dot(vt_ref[h, j], pj))
                     for (m, acc), mn, pj, j in zip(chains, ms, ps, blocks(grp)))

    def body(t, carry):
        first = t * MOBA_GROUPS
        out = []
        for h in heads:
            chains, ss = carry[h]
            later = [scores(h, first + d) for d in range(1, MOBA_GROUPS)]
            ss_next = scores(h, first + MOBA_GROUPS)
            for d, sd in enumerate([ss] + later):
                chains = update(h, chains, sd, first + d)
            out.append((chains, ss_next))
        return tuple(out)

    per_trip = MOBA_CHAINS * MOBA_GROUPS
    state = lax.fori_loop(0, (i + per_trip - 1) // per_trip, body,
                          tuple((init[h], ss0[h]) for h in heads))
    for h in heads:
        chains = state[h][0]
        m = chains[0][0]
        for mg, _ in chains[1:]:
            m = jnp.maximum(m, mg)
        acc = sum(jnp.exp2(mg - m) * ag for mg, ag in chains)
        out = acc[:hd, :] / acc[hd:hd + 1, :]
        o_ref[:, cols[h]] = out.T.astype(o_ref.dtype)


def _moba_attn(q, qs, k, vt, kmean):
    s, w = q.shape
    tb = MOBA_BLOCK
    nb = s // tb
    nh = MOBA_HEADS_PER_STEP
    hd = nh * MOBA_HEAD_DIM
    vrows = vt.shape[2]
    return pl.pallas_call(
        _moba_attn_kernel,
        grid=(w // hd, nb),
        in_specs=[pl.BlockSpec((tb, hd), lambda h, i: (i, h)),
                  pl.BlockSpec((tb, hd), lambda h, i: (i, h)),
                  pl.BlockSpec((s, hd), lambda h, i: (0, h)),
                  pl.BlockSpec((nh, nb, vrows, tb), lambda h, i: (h, 0, 0, 0)),
                  pl.BlockSpec((nb, hd), lambda h, i: (0, h))],
        out_specs=pl.BlockSpec((tb, hd), lambda h, i: (i, h)),
        out_shape=jax.ShapeDtypeStruct((s, w), BF16),
        scratch_shapes=[pltpu.VMEM((nh, nb, tb), F32)],
        compiler_params=_params(("parallel", "arbitrary")),
        name="moba_attn",
    )(q, qs, k, vt, kmean)


def _top_rows(x, n):
    rows = x.shape[0]
    rid = lax.broadcasted_iota(jnp.int32, x.shape, 0).astype(F32)
    vals, idxs = [], []
    for _ in range(n):
        m = jnp.max(x, axis=0, keepdims=True)
        first = jnp.min(jnp.where(x == m, rid, float(rows)), axis=0, keepdims=True)
        vals.append(m)
        idxs.append(first)
        x = jnp.where(rid == first, -jnp.inf, x)
    return jnp.concatenate(vals, axis=0), jnp.concatenate(idxs, axis=0)


def _peer_route_kernel(q_ref, keys_ref, e1_ref, e2_ref, g_ref):
    n = PEER_TOPK
    tm = q_ref.shape[0]
    e1s, e2s, gs = [], [], []
    for h in range(PEER_HEADS):
        tops = []
        for p in range(2):
            hp = 2 * h + p
            cols = slice(hp * PEER_HALF, (hp + 1) * PEER_HALF)
            rows = slice(hp * PEER_N_KEYS, (hp + 1) * PEER_N_KEYS)
            st = _dot_nt(keys_ref[rows, :], q_ref[:, cols].astype(BF16))
            tops.append(_top_rows(st, n))
        (s1, i1), (s2, i2) = tops
        hn = n // 2
        cand = jnp.concatenate([s1[0:1, :] + s2]
                               + [s1[a:a + 1, :] + s2[:hn, :] for a in range(1, hn)]
                               + [s1[hn:, :] + s2[0:1, :]], axis=0)
        f_s, f_pos = _top_rows(cand, n)
        mid = jnp.floor((f_pos - n) * (1.0 / hn))
        tail0 = float(n + (hn - 1) * hn)
        pa = jnp.where(f_pos < n, 0.0, jnp.where(f_pos < tail0, 1.0 + mid, f_pos - tail0 + hn))
        pb = jnp.where(f_pos < n, f_pos, jnp.where(f_pos < tail0, f_pos - n - mid * hn, 0.0))
        e1 = jnp.zeros((n, tm), F32)
        e2 = jnp.zeros((n, tm), F32)
        for a in range(n):
            e1 = jnp.where(pa == float(a), i1[a:a + 1, :], e1)
            e2 = jnp.where(pb == float(a), i2[a:a + 1, :], e2)
        ex = jnp.exp(f_s - f_s[0:1, :])
        gs.append(ex / jnp.sum(ex, axis=0, keepdims=True))
        e1s.append(e1)
        e2s.append(e2)
    e1_ref[...] = jnp.concatenate(e1s, axis=0).T
    e2_ref[...] = jnp.concatenate(e2s, axis=0).T
    g_ref[...] = jnp.concatenate(gs, axis=0).T


def _peer_route(q, keys, tm=256):
    s, w = q.shape
    nsel = PEER_HEADS * PEER_TOPK
    tm = min(tm, s)
    out = jax.ShapeDtypeStruct((s, nsel), F32)
    ob = pl.BlockSpec((tm, nsel), lambda i: (i, 0))
    return pl.pallas_call(
        _peer_route_kernel,
        grid=(s // tm,),
        in_specs=[pl.BlockSpec((tm, w), lambda i: (i, 0)),
                  pl.BlockSpec(keys.shape, lambda i: (0, 0))],
        out_specs=[ob, ob, ob],
        out_shape=[out, out, out],
        compiler_params=_params(("parallel",)),
        name="peer_route",
    )(q, keys)


EXPAND_UNROLL = 16
EXPAND_GROUP = 4


def _peer_expand_kernel(e1_ref, e2_ref, g_ref, w_ref, stage_ref):
    nk = PEER_N_KEYS
    nsel = e1_ref.shape[1]
    rid = lax.broadcasted_iota(jnp.int32, (nk, nsel), 0).astype(F32)
    un = EXPAND_UNROLL

    def body(tt, carry):
        base = pl.multiple_of(tt * un, un)
        e1 = e1_ref[pl.ds(base, un), :]
        e2 = e2_ref[pl.ds(base, un), :]
        g = g_ref[pl.ds(base, un), :]
        for u0 in range(0, un, EXPAND_GROUP):
            us = range(u0, u0 + EXPAND_GROUP)
            lefts = [jnp.where(rid == e1[u:u + 1, :], g[u:u + 1, :], 0.0).astype(BF16) for u in us]
            rights = [jnp.where(rid == e2[u:u + 1, :], 1.0, 0.0).astype(BF16) for u in us]
            for u, l, r in zip(us, lefts, rights):
                stage_ref[u * nk:(u + 1) * nk, :] = _dot_nt(l, r)
        by_key = jnp.swapaxes(stage_ref[...].reshape(un, nk, nk), 0, 1)
        for a in range(nk):
            w_ref[pl.ds(base, un), a * nk:(a + 1) * nk] = by_key[a].astype(w_ref.dtype)
        return carry

    lax.fori_loop(0, e1_ref.shape[0] // un, body, 0)


def _peer_expand(e1, e2, g, tm=128):
    s, nsel = e1.shape
    nk = PEER_N_KEYS
    tm = min(tm, s)
    ib = pl.BlockSpec((tm, nsel), lambda i: (i, 0))
    return pl.pallas_call(
        _peer_expand_kernel,
        grid=(s // tm,),
        in_specs=[ib, ib, ib],
        out_specs=pl.BlockSpec((tm, nk * nk), lambda i: (i, 0)),
        out_shape=jax.ShapeDtypeStruct((s, nk * nk), BF16),
        scratch_shapes=[pltpu.VMEM((EXPAND_UNROLL * nk, nk), F32)],
        compiler_params=_params(("parallel",)),
        name="peer_expand",
    )(e1, e2, g)


FP8 = jnp.float8_e4m3fn
FP8_MAX = 448.0
SCALE_ROWS = 8


def _rmsnorm_fp8_kernel(x_ref, g_ref, o_ref, o8_ref, s_ref):
    x = x_ref[...]
    ms = jnp.mean(x * x, axis=-1, keepdims=True)
    y = x * lax.rsqrt(ms + NORM_EPS) * g_ref[...]
    o_ref[...] = y.astype(o_ref.dtype)
    scale = jnp.maximum(jnp.max(jnp.abs(y), axis=-1, keepdims=True), 1e-30) * (1.0 / FP8_MAX)
    o8_ref[...] = (y / scale).astype(o8_ref.dtype)
    s_ref[...] = jnp.broadcast_to(scale, s_ref.shape)


def _rmsnorm_fp8(x, g, tm=256):
    s, d = x.shape
    blk = pl.BlockSpec((tm, d), lambda i: (i, 0))
    return pl.pallas_call(
        _rmsnorm_fp8_kernel,
        grid=(s // tm,),
        in_specs=[blk, pl.BlockSpec((1, d), lambda i: (0, 0))],
        out_specs=[blk, blk, pl.BlockSpec((tm, LANES), lambda i: (i, 0))],
        out_shape=[jax.ShapeDtypeStruct((s, d), BF16), jax.ShapeDtypeStruct((s, d), FP8),
                   jax.ShapeDtypeStruct((s, LANES), F32)],
        compiler_params=_params(("parallel",)),
        name="rmsnorm_fp8",
    )(x, g.reshape(1, d))


def _quant_rows_kernel(u_ref, u8_ref, s_ref):
    u = u_ref[...]
    scale = jnp.maximum(jnp.max(jnp.abs(u), axis=-1, keepdims=True), 1e-30) * (1.0 / FP8_MAX)
    u8_ref[...] = (u / scale).astype(u8_ref.dtype)
    s_ref[...] = jnp.broadcast_to(scale, (u.shape[0], LANES)).T[:SCALE_ROWS, :]


def _quant_rows(u, te=512):
    ne, d = u.shape
    return pl.pallas_call(
        _quant_rows_kernel,
        grid=(ne // te,),
        in_specs=[pl.BlockSpec((te, d), lambda e: (e, 0))],
        out_specs=[pl.BlockSpec((te, d), lambda e: (e, 0)),
                   pl.BlockSpec((SCALE_ROWS, te), lambda e: (0, e))],
        out_shape=[jax.ShapeDtypeStruct((ne, d), FP8), jax.ShapeDtypeStruct((SCALE_ROWS, ne), F32)],
        compiler_params=_params(("parallel",)),
        name="peer_quant_u",
    )(u)


def _peer_dense_kernel(h_ref, hs_ref, u_ref, us_ref, v_ref, vs_ref, w_ref, o_ref):
    @pl.when(pl.program_id(1) == 0)
    def _():
        o_ref[...] = jnp.zeros_like(o_ref)

    act = _dot_nt(h_ref[...], u_ref[...]) * hs_ref[:, 0:1] * us_ref[0:1, :]
    gelu = 0.5 * act * (1.0 + lax.erf(act * (2.0 ** -0.5)))
    mix = w_ref[...].astype(F32) * gelu * vs_ref[0:1, :]
    scale = jnp.maximum(jnp.max(jnp.abs(mix), axis=-1, keepdims=True), 1e-30) * (1.0 / FP8_MAX)
    o_ref[...] += _dot((mix / scale).astype(FP8), v_ref[...]) * scale


def _peer_dense(h8, hs, u8, us, v8, vs, wdense, tm=512, te=1024):
    s, d = h8.shape
    ne = u8.shape[0]
    tm = min(tm, s)
    return pl.pallas_call(
        _peer_dense_kernel,
        grid=(s // tm, ne // te),
        in_specs=[pl.BlockSpec((tm, d), lambda i, e: (i, 0)),
                  pl.BlockSpec((tm, LANES), lambda i, e: (i, 0)),
                  pl.BlockSpec((te, d), lambda i, e: (e, 0)),
                  pl.BlockSpec((SCALE_ROWS, te), lambda i, e: (0, e)),
                  pl.BlockSpec((te, d), lambda i, e: (e, 0)),
                  pl.BlockSpec((SCALE_ROWS, te), lambda i, e: (0, e)),
                  pl.BlockSpec((tm, te), lambda i, e: (i, e))],
        out_specs=pl.BlockSpec((tm, d), lambda i, e: (i, 0)),
        out_shape=jax.ShapeDtypeStruct((s, d), F32),
        compiler_params=_params(("parallel", "arbitrary")),
        name="peer_dense",
    )(h8, hs, u8, us, v8, vs, wdense)


def _add_rmsnorm_kernel(x_ref, y_ref, g_ref, o_ref):
    x = x_ref[...] + y_ref[...]
    ms = jnp.mean(x * x, axis=-1, keepdims=True)
    o_ref[...] = x * lax.rsqrt(ms + NORM_EPS) * g_ref[...]


def _add_rmsnorm(x, y, g, tm=256):
    s, d = x.shape
    blk = pl.BlockSpec((tm, d), lambda i: (i, 0))
    return pl.pallas_call(
        _add_rmsnorm_kernel,
        grid=(s // tm,),
        in_specs=[blk, blk, pl.BlockSpec((1, d), lambda i: (0, 0))],
        out_specs=blk,
        out_shape=jax.ShapeDtypeStruct((s, d), F32),
        compiler_params=_params(("parallel",)),
        name="residual_final_norm",
    )(x, y, g.reshape(1, d))


def _rwkv_branch(proj_r, w0, w_up, a0, a_up, g_up, k_k, k_a, r_k, lnx_g, lnx_b):
    g_up_pad = jnp.pad(g_up, ((0, GATE_LORA_PAD - GATE_LORA), (0, 0))).astype(BF16)
    wpre, a, g = _rwkv_lora(proj_r, w0, w_up.astype(BF16), a0, a_up.astype(BF16), g_up_pad)
    rt, at, kh, bh, kb, bb, pc, bonus = _rwkv_prep(proj_r, wpre, a, k_k, k_a, r_k.reshape(-1))
    x, y, opre, qb2, bbt, gst, pcm = _rwkv_chunk(rt, at, kh, bh, kb, bb, pc, proj_r)
    return _rwkv_scan(x, y, rt, opre, qb2, bbt, gst, pcm, bonus, g, lnx_g, lnx_b)


def _rope_tables(s):
    half = MOBA_HEAD_DIM // 2
    inv_freq = ROPE_THETA ** (-jnp.arange(half, dtype=F32) / half)
    ang = jnp.arange(s, dtype=jnp.int32).astype(F32)[:, None] * inv_freq[None, :]
    cos = jnp.cos(ang)
    sin = jnp.sin(ang)
    return jnp.concatenate([cos, cos], axis=-1), jnp.concatenate([-sin, sin], axis=-1)


def _moba_branch(proj_m):
    s = proj_m.shape[0]
    cos, sin = _rope_tables(s)
    q, qs, k, vt, kmean = _moba_prep(proj_m, cos, sin)
    return _moba_attn(q, qs, k, vt, kmean.reshape(kmean.shape[0], kmean.shape[2]))


def _peer_layer(x1, norm2_g, w_q, sub_keys, expert_u, expert_v, final_g):
    s, d = x1.shape
    h2, h8, hs = _rmsnorm_fp8(x1, norm2_g)
    q = _matmul(h2, w_q.astype(BF16), F32, tm=min(1024, s), tn=512, name="peer_query")
    keys = sub_keys.reshape(PEER_HEADS * 2 * PEER_N_KEYS, PEER_HALF).astype(BF16)
    e1, e2, g = _peer_route(q, keys)
    wdense = _peer_expand(e1, e2, g)
    u8, us = _quant_rows(expert_u)
    v8, vs = _quant_rows(expert_v)
    peer = _peer_dense(h8, hs, u8, us, v8, vs, wdense)
    return _add_rmsnorm(x1, peer, final_g)


def kernel(x, norm1_g, w_in, rwkv_mu, rwkv_w0, rwkv_w_up, rwkv_a0, rwkv_a_up, rwkv_g_up, rwkv_k_k, rwkv_k_a, rwkv_r_k, rwkv_lnx_g, rwkv_lnx_b, w_branch_rwkv, w_branch_moba, w_out, norm2_g, peer_w_q, peer_sub_keys, peer_u, peer_v, final_g):
    b, s, d = x.shape
    depth = w_in.shape[0]
    assert b == 1 and depth == 1
    x2d = x.reshape(s, d)
    l = 0
    rw = rwkv_w0.shape[-1]
    shift_w = 3 * rw + DECAY_LORA + AAA_LORA + GATE_LORA
    tn = 512
    shift_pad = -(-shift_w // tn) * tn
    mw = w_branch_moba.shape[1]
    tm = min(1024, s)

    h = _rmsnorm(x2d, norm1_g[l], BF16)
    wt = jnp.transpose(w_in[l])
    mu = jnp.pad(rwkv_mu[l], (0, shift_pad - shift_w)).reshape(1, shift_pad)
    proj_r = _matmul_shift(h, wt, mu, tm=tm, tn=tn)
    proj_m = _matmul_nt(h, wt, F32, tm=tm, tn=tn, name="in_proj_moba", row0=shift_w, n=3 * mw)
    gates = _matmul_nt(h, wt, F32, tm=tm, tn=tn, name="in_proj_gates", row0=shift_w + 3 * mw, n=2 * d)

    y_a = _rwkv_branch(proj_r, rwkv_w0[l], rwkv_w_up[l], rwkv_a0[l], rwkv_a_up[l], rwkv_g_up[l],
                       rwkv_k_k[l], rwkv_k_a[l], rwkv_r_k[l], rwkv_lnx_g[l], rwkv_lnx_b[l])
    y_b = _moba_branch(proj_m)
    mixed = _gated_pair(y_a, w_branch_rwkv[l].astype(BF16), y_b, w_branch_moba[l].astype(BF16),
                        gates, tm=tm, tn=512)
    x1 = _matmul_residual(mixed, w_out[l].astype(BF16), x2d, tm=tm, tn=512)
    out = _peer_layer(x1, norm2_g[l], peer_w_q[l], peer_sub_keys[l], peer_u[l], peer_v[l], final_g)
    return out.reshape(b, s, d)
```

```python
import functools

import jax
import jax.numpy as jnp
from jax import lax
from jax.experimental import pallas as pl
from jax.experimental.pallas import tpu as pltpu

F32 = jnp.float32
BF16 = jnp.bfloat16

NORM_EPS = 1e-6
LANES = 128
F32_SUBLANES = 8
RWKV_HEAD_DIM = 64
DECAY_LORA = 128
AAA_LORA = 128
GATE_LORA = 480
GATE_LORA_PAD = 512
LN_X_EPS = 64e-5
CHUNK = 128
MOBA_HEAD_DIM = 128
MOBA_BLOCK = 256
MOBA_TOPK = 3
ROPE_THETA = 10000.0
PEER_HEADS = 8
PEER_N_KEYS = 128
PEER_HALF = 128
PEER_TOPK = 16

VMEM_LIMIT = 56 * 1024 * 1024

_NT = (((1,), (1,)), ((), ()))


def _params(sem, vmem=VMEM_LIMIT):
    return pltpu.CompilerParams(dimension_semantics=sem, vmem_limit_bytes=vmem)


def _dot(a, b):
    return jnp.dot(a, b, preferred_element_type=F32)


def _dot_nt(a, b):
    return lax.dot_general(a, b, _NT, preferred_element_type=F32)


def _split3(x):
    hi = x.astype(BF16)
    r1 = x - hi.astype(F32)
    mid = r1.astype(BF16)
    lo = (r1 - mid.astype(F32)).astype(BF16)
    return hi, mid, lo


def _dot_r3(x, r_bf16):
    hi, mid, lo = _split3(x)
    return _dot(hi, r_bf16) + _dot(mid, r_bf16) + _dot(lo, r_bf16)


def _rmsnorm_kernel(x_ref, g_ref, o_ref):
    x = x_ref[...]
    ms = jnp.mean(x * x, axis=-1, keepdims=True)
    o_ref[...] = (x * lax.rsqrt(ms + NORM_EPS) * g_ref[...]).astype(o_ref.dtype)


def _rmsnorm(x, g, out_dtype, tm=256):
    s, d = x.shape
    return pl.pallas_call(
        _rmsnorm_kernel,
        grid=(s // tm,),
        in_specs=[pl.BlockSpec((tm, d), lambda i: (i, 0)),
                  pl.BlockSpec((1, d), lambda i: (0, 0))],
        out_specs=pl.BlockSpec((tm, d), lambda i: (i, 0)),
        out_shape=jax.ShapeDtypeStruct((s, d), out_dtype),
        compiler_params=_params(("parallel",)),
        name="rmsnorm",
    )(x, g.reshape(1, d))


def _mm_kernel(a_ref, b_ref, o_ref):
    o_ref[...] = _dot(a_ref[...], b_ref[...]).astype(o_ref.dtype)


def _matmul(a, b, out_dtype, tm, tn, name, col0=0, n=None):
    m, k = a.shape
    n = b.shape[1] if n is None else n
    off = col0 // tn
    return pl.pallas_call(
        _mm_kernel,
        grid=(m // tm, n // tn),
        in_specs=[pl.BlockSpec((tm, k), lambda i, j: (i, 0)),
                  pl.BlockSpec((k, tn), lambda i, j: (0, j + off))],
        out_specs=pl.BlockSpec((tm, tn), lambda i, j: (i, j)),
        out_shape=jax.ShapeDtypeStruct((m, n), out_dtype),
        compiler_params=_params(("parallel", "parallel")),
        name=name,
    )(a, b)


BF16_SUBLANES = 16


def _mm_nt_kernel(a_ref, bt_ref, o_ref, b16_ref):
    @pl.when(pl.program_id(1) == 0)
    def _():
        b16_ref[...] = bt_ref[...].astype(b16_ref.dtype)

    o_ref[...] = _dot_nt(a_ref[...], b16_ref[...]).astype(o_ref.dtype)


def _matmul_nt(a, bt, out_dtype, tm, tn, name, row0, n):
    m, k = a.shape
    return pl.pallas_call(
        _mm_nt_kernel,
        grid=(n // tn, m // tm),
        in_specs=[pl.BlockSpec((tm, k), lambda j, i: (i, 0)),
                  pl.BlockSpec((pl.Element(tn), pl.Element(k)),
                               lambda j, i: (pl.multiple_of(row0 + j * tn, F32_SUBLANES), 0))],
        out_specs=pl.BlockSpec((tm, tn), lambda j, i: (i, j)),
        out_shape=jax.ShapeDtypeStruct((m, n), out_dtype),
        scratch_shapes=[pltpu.VMEM((tn, k), BF16)],
        compiler_params=_params(("parallel", "arbitrary")),
        name=name,
    )(a, bt)


def _mm_shift_kernel(a_ref, ap_ref, bt_ref, mu_ref, o_ref, b16_ref):
    i = pl.program_id(1)

    @pl.when(i == 0)
    def _():
        b16_ref[...] = bt_ref[...].astype(b16_ref.dtype)

    b = b16_ref[...]
    z = _dot_nt(a_ref[...], b)
    zp = _dot_nt(ap_ref[...], b)
    prev = jnp.where(i > 0, zp[BF16_SUBLANES - 1:BF16_SUBLANES, :], 0.0)
    row = lax.broadcasted_iota(jnp.int32, z.shape, 0)
    z_prev = jnp.where(row == 0, prev, pltpu.roll(z, 1, axis=0))
    o_ref[...] = z + (z_prev - z) * mu_ref[...]


def _matmul_shift(a, bt, mu, tm, tn):
    m, k = a.shape
    n = mu.shape[1]
    per = tm // BF16_SUBLANES
    return pl.pallas_call(
        _mm_shift_kernel,
        grid=(n // tn, m // tm),
        in_specs=[pl.BlockSpec((tm, k), lambda j, i: (i, 0)),
                  pl.BlockSpec((BF16_SUBLANES, k), lambda j, i: (jnp.maximum(i * per - 1, 0), 0)),
                  pl.BlockSpec((tn, k), lambda j, i: (j, 0)),
                  pl.BlockSpec((1, tn), lambda j, i: (0, j))],
        out_specs=pl.BlockSpec((tm, tn), lambda j, i: (i, j)),
        out_shape=jax.ShapeDtypeStruct((m, n), F32),
        scratch_shapes=[pltpu.VMEM((tn, k), BF16)],
        compiler_params=_params(("parallel", "arbitrary")),
        name="in_proj_rwkv",
    )(a, a, bt, mu)


def _gated_pair_kernel(ya_ref, wa_ref, yb_ref, wb_ref, ga_ref, gb_ref, o_ref):
    pa = _dot(ya_ref[...], wa_ref[...])
    pb = _dot(yb_ref[...], wb_ref[...])
    mixed = jax.nn.sigmoid(ga_ref[...]) * pa + jax.nn.sigmoid(gb_ref[...]) * pb
    o_ref[...] = mixed.astype(o_ref.dtype)


def _gated_pair(ya, wa, yb, wb, gates, tm, tn):
    m, k = ya.shape
    n = wa.shape[1]
    nb = n // tn
    return pl.pallas_call(
        _gated_pair_kernel,
        grid=(m // tm, nb),
        in_specs=[pl.BlockSpec((tm, k), lambda i, j: (i, 0)),
                  pl.BlockSpec((k, tn), lambda i, j: (0, j)),
                  pl.BlockSpec((tm, k), lambda i, j: (i, 0)),
                  pl.BlockSpec((k, tn), lambda i, j: (0, j)),
                  pl.BlockSpec((tm, tn), lambda i, j: (i, j)),
                  pl.BlockSpec((tm, tn), lambda i, j: (i, j + nb))],
        out_specs=pl.BlockSpec((tm, tn), lambda i, j: (i, j)),
        out_shape=jax.ShapeDtypeStruct((m, n), BF16),
        compiler_params=_params(("parallel", "parallel")),
        name="branch_merge",
    )(ya, wa, yb, wb, gates, gates)


def _mm_res_kernel(a_ref, b_ref, r_ref, o_ref):
    o_ref[...] = r_ref[...] + _dot(a_ref[...], b_ref[...])


def _matmul_residual(a, b, res, tm, tn):
    m, k = a.shape
    n = b.shape[1]
    return pl.pallas_call(
        _mm_res_kernel,
        grid=(m // tm, n // tn),
        in_specs=[pl.BlockSpec((tm, k), lambda i, j: (i, 0)),
                  pl.BlockSpec((k, tn), lambda i, j: (0, j)),
                  pl.BlockSpec((tm, tn), lambda i, j: (i, j))],
        out_specs=pl.BlockSpec((tm, tn), lambda i, j: (i, j)),
        out_shape=jax.ShapeDtypeStruct((m, n), F32),
        compiler_params=_params(("parallel", "parallel")),
        name="out_proj",
    )(a, b, res)


def _rwkv_lora_kernel(wl_ref, al_ref, gl0_ref, gl1_ref, w0_ref, wup_ref, a0_ref, aup_ref,
                      gup_ref, wpre_ref, a_ref, g_ref):
    half = GATE_LORA_PAD // 2
    wpre_ref[...] = w0_ref[...] + _dot(jnp.tanh(wl_ref[...]).astype(BF16), wup_ref[...])
    a_ref[...] = jax.nn.sigmoid(a0_ref[...] + _dot(al_ref[...].astype(BF16), aup_ref[...]))
    g_ref[...] = (_dot(jax.nn.sigmoid(gl0_ref[...]).astype(BF16), gup_ref[:half, :])
                  + _dot(jax.nn.sigmoid(gl1_ref[...]).astype(BF16), gup_ref[half:, :]))


def _rwkv_lora(proj_r, w0, w_up, a0, a_up, g_up_pad, tm=256):
    s = proj_r.shape[0]
    w = w0.shape[-1]
    half = GATE_LORA_PAD // 2
    c_wl = 3 * w // DECAY_LORA
    c_al = (3 * w + DECAY_LORA) // AAA_LORA
    c_gl = (3 * w + DECAY_LORA + AAA_LORA) // half
    row = lambda i: (i, 0)
    const = lambda i: (0, 0)
    out = jax.ShapeDtypeStruct((s, w), F32)
    return pl.pallas_call(
        _rwkv_lora_kernel,
        grid=(s // tm,),
        in_specs=[pl.BlockSpec((tm, DECAY_LORA), lambda i: (i, c_wl)),
                  pl.BlockSpec((tm, AAA_LORA), lambda i: (i, c_al)),
                  pl.BlockSpec((tm, half), lambda i: (i, c_gl)),
                  pl.BlockSpec((tm, half), lambda i: (i, c_gl + 1)),
                  pl.BlockSpec((1, w), const),
                  pl.BlockSpec((DECAY_LORA, w), const),
                  pl.BlockSpec((1, w), const),
                  pl.BlockSpec((AAA_LORA, w), const),
                  pl.BlockSpec((GATE_LORA_PAD, w), const)],
        out_specs=[pl.BlockSpec((tm, w), row)] * 3,
        out_shape=[out, out, out],
        compiler_params=_params(("parallel",)),
        name="rwkv_lora",
    )(proj_r, proj_r, proj_r, proj_r, w0.reshape(1, w), w_up, a0.reshape(1, w), a_up, g_up_pad)


def _head_pair_ones(width=LANES):
    r = lax.broadcasted_iota(jnp.int32, (width, width), 0) // RWKV_HEAD_DIM
    c = lax.broadcasted_iota(jnp.int32, (width, width), 1) // RWKV_HEAD_DIM
    return r == c


def _rwkv_prep(r_ref, k_ref, v_ref, wpre_ref, a_ref, kk_ref, ka_ref, rk_ref,
               rt_ref, at_ref, kh_ref, bh_ref, kb_ref, bb_ref, pc_ref, bonus_ref):
    tm = r_ref.shape[0]
    r = r_ref[...]
    k = k_ref[...]
    v = v_ref[...]
    a = a_ref[...]
    same_head = jnp.where(_head_pair_ones(r.shape[1]), 1.0, 0.0).astype(BF16)

    x = -wpre_ref[...]
    softplus = jnp.maximum(x, 0.0) + jnp.log1p(jnp.exp(-jnp.abs(x)))
    lw = -jnp.exp(-softplus - 0.5)

    kk = k * kk_ref[...]
    ss = _dot_r3(kk * kk, same_head)
    kkn = kk / jnp.maximum(jnp.sqrt(ss), 1e-12)
    k2 = k * (1.0 + (a - 1.0) * ka_ref[...])
    bonus_ref[...] = _dot_r3(r * k2 * rk_ref[...], same_head) * v

    ri = lax.broadcasted_iota(jnp.int32, (tm, tm), 0)
    ci = lax.broadcasted_iota(jnp.int32, (tm, tm), 1)
    same_chunk = (ri // CHUNK) == (ci // CHUNK)
    tri = jnp.where(same_chunk & (ci <= ri), 1.0, 0.0).astype(BF16)
    allc = jnp.where(same_chunk, 1.0, 0.0).astype(BF16)
    lw_parts = _split3(lw)
    cum = sum(_dot(tri, part) for part in lw_parts)
    tot = sum(_dot(allc, part) for part in lw_parts)

    p_in = jnp.exp(cum)
    p_inv = jnp.exp(-cum)
    p_tail = jnp.exp(tot - cum)
    b = kkn * a
    rt_ref[...] = r * p_in
    at_ref[...] = -kkn * jnp.exp(cum - lw)
    kh_ref[...] = k2 * p_inv
    bh_ref[...] = b * p_inv
    kb_ref[...] = k2 * p_tail
    bb_ref[...] = b * p_tail
    pc_ref[...] = jnp.exp(tot)


def _rwkv_chunk_kernel(r_ref, k_ref, v_ref, wpre_ref, a_ref, kk_ref, ka_ref, rk_ref,
                       rt_ref, bonus_ref, x_ref, y_ref, op_ref, qb_ref, bbt_ref, g_ref, pcm_ref,
                       at_ref, kh_ref, bh_ref, kb_ref, bb_ref, pc_ref):
    _rwkv_prep(r_ref, k_ref, v_ref, wpre_ref, a_ref, kk_ref, ka_ref, rk_ref,
               rt_ref, at_ref, kh_ref, bh_ref, kb_ref, bb_ref, pc_ref, bonus_ref)
    c = CHUNK
    nheads = LANES // RWKV_HEAD_DIM
    chunks = range(rt_ref.shape[0] // c)
    pairs = range(rt_ref.shape[1] // LANES)
    rows = [slice(q * c, (q + 1) * c) for q in chunks]
    lanes = [slice(p * LANES, (p + 1) * LANES) for p in pairs]
    tiles = [(q, p) for q in chunks for p in pairs]
    probs = [(n, h) for n in range(len(tiles)) for h in range(nheads)]
    lane = lax.broadcasted_iota(jnp.int32, (c, LANES), 1)
    ri = lax.broadcasted_iota(jnp.int32, (c, c), 0)
    ci = lax.broadcasted_iota(jnp.int32, (c, c), 1)
    strict = ci < ri
    incl = ci <= ri
    eye = jnp.where(ri == ci, 1.0, 0.0)
    head_mask = [(lane // RWKV_HEAD_DIM) == h for h in range(nheads)]
    tile = lambda ref, n: ref[rows[tiles[n][0]], lanes[tiles[n][1]]]

    kh = [tile(kh_ref, n).astype(BF16) for n in range(len(tiles))]
    bh = [tile(bh_ref, n).astype(BF16) for n in range(len(tiles))]
    a_h = [jnp.where(head_mask[h], tile(at_ref, n), 0.0).astype(BF16) for n, h in probs]
    r_h = [jnp.where(head_mask[h], tile(rt_ref, n), 0.0).astype(BF16) for n, h in probs]
    v_h = [jnp.where(head_mask[h], tile(v_ref, n), 0.0).astype(BF16) for n, h in probs]
    n_ab = [jnp.where(strict, _dot_nt(a, bh[n]), 0.0) for a, (n, _) in zip(a_h, probs)]
    n_ak = [jnp.where(strict, _dot_nt(a, kh[n]), 0.0).astype(BF16) for a, (n, _) in zip(a_h, probs)]
    q_k = [jnp.where(incl, _dot_nt(r, kh[n]), 0.0).astype(BF16) for r, (n, _) in zip(r_h, probs)]
    q_b = [jnp.where(incl, _dot_nt(r, bh[n]), 0.0) for r, (n, _) in zip(r_h, probs)]
    for qb, (n, h) in zip(q_b, probs):
        q, p = tiles[n]
        col = (p * nheads + h) * c
        qb_ref[rows[q], col:col + c] = qb.astype(qb_ref.dtype)

    t = [eye + n for n in n_ab]
    pw = n_ab
    for _ in range((c - 1).bit_length() - 1):
        pb = [x.astype(BF16) for x in pw]
        pw = [_dot(x, x) for x in pb]
        t = [ti + _dot(ti.astype(BF16), pi.astype(BF16)) for ti, pi in zip(t, pw)]
    tb = [ti.astype(BF16) for ti in t]
    xs = [_dot(ti, a) for ti, a in zip(tb, a_h)]
    nv = [_dot(n, v).astype(BF16) for n, v in zip(n_ak, v_h)]
    ys = [_dot(ti, z) for ti, z in zip(tb, nv)]
    os = [_dot(qk, v) for qk, v in zip(q_k, v_h)]
    same_head = _head_pair_ones()
    for n, (q, p) in enumerate(tiles):
        mine = [m for m, (nn, _) in enumerate(probs) if nn == n]
        x_ref[rows[q], lanes[p]] = sum(xs[m] for m in mine).astype(x_ref.dtype)
        y_ref[rows[q], lanes[p]] = sum(ys[m] for m in mine)
        op_ref[rows[q], lanes[p]] = sum(os[m] for m in mine)
        kbt = tile(kb_ref, n).T.astype(BF16)
        g_ref[rows[q], lanes[p]] = jnp.where(same_head, _dot(kbt, tile(v_ref, n).astype(BF16)), 0.0)
        bbt_ref[lanes[p], rows[q]] = tile(bb_ref, n).T.astype(bbt_ref.dtype)
        pcm_ref[rows[q], lanes[p]] = tile(pc_ref, n).T


def _rwkv_chunk(proj_r, wpre, a, k_k, k_a, r_k, tc=2 * CHUNK, tw=2 * LANES):
    s, w = wpre.shape
    ncol = w // tw
    blk = lambda off: pl.BlockSpec((tc, tw), lambda i, j, off=off: (i, j + off))
    vec = pl.BlockSpec((1, tw), lambda i, j: (0, j))
    f32o = jax.ShapeDtypeStruct((s, w), F32)
    return pl.pallas_call(
        _rwkv_chunk_kernel,
        grid=(s // tc, ncol),
        in_specs=[blk(0), blk(ncol), blk(2 * ncol), blk(0), blk(0), vec, vec, vec],
        out_specs=[blk(0), blk(0), blk(0), blk(0), blk(0),
                   pl.BlockSpec((tc, 2 * tw), lambda i, j: (i, j)),
                   pl.BlockSpec((tw, tc), lambda i, j: (j, i)),
                   blk(0), blk(0)],
        out_shape=[f32o, f32o, jax.ShapeDtypeStruct((s, w), BF16), f32o, f32o,
                   jax.ShapeDtypeStruct((s, 2 * w), BF16),
                   jax.ShapeDtypeStruct((w, s), BF16),
                   f32o, f32o],
        scratch_shapes=[pltpu.VMEM((tc, tw), F32)] * 6,
        compiler_params=_params(("parallel", "parallel")),
        name="rwkv_chunk",
    )(proj_r, proj_r, proj_r, wpre, a, k_k.reshape(1, w), k_a.reshape(1, w), r_k.reshape(1, w))


def _rwkv_scan_kernel(x_ref, y_ref, rt_ref, op_ref, qb_ref, bbt_ref, g_ref, pcm_ref,
                      bonus_ref, gate_ref, lng_ref, lnb_ref, ya_ref, h_ref):
    c = CHUNK
    pairs = range(x_ref.shape[1] // LANES)
    lanes = [slice(p * LANES, (p + 1) * LANES) for p in pairs]

    @pl.when(pl.program_id(1) == 0)
    def _():
        h_ref[...] = jnp.zeros_like(h_ref)

    lane = lax.broadcasted_iota(jnp.int32, (c, LANES), 1)
    first = lane < RWKV_HEAD_DIM
    same_head = _head_pair_ones()
    ones_head = jnp.where(same_head, 1.0, 0.0).astype(BF16)
    inv_n = 1.0 / RWKV_HEAD_DIM
    for q in range(x_ref.shape[0] // c):
        rows = slice(q * c, (q + 1) * c)
        hs = [h_ref[p] for p in pairs]
        hb = [h.astype(BF16) for h in hs]
        us = [_dot(x_ref[rows, lanes[p]], hb[p]) + y_ref[rows, lanes[p]] for p in pairs]
        u2 = [jnp.concatenate([jnp.where(first, u, 0.0), jnp.where(first, 0.0, u)], axis=0).astype(BF16)
              for u in us]
        os = [_dot(rt_ref[rows, lanes[p]].astype(BF16), hb[p]) + op_ref[rows, lanes[p]]
              + _dot(qb_ref[rows, p * 2 * c:(p + 1) * 2 * c], u2[p]) for p in pairs]
        upd = [_dot(bbt_ref[lanes[p], rows], us[p].astype(BF16)) for p in pairs]
        for p in pairs:
            h_ref[p] = (pcm_ref[rows, lanes[p]] * hs[p] + g_ref[rows, lanes[p]]
                        + jnp.where(same_head, upd[p], 0.0))
        mus = [_dot_r3(o, ones_head) * inv_n for o in os]
        ds = [o - mu for o, mu in zip(os, mus)]
        vs = [_dot_r3(d * d, ones_head) * inv_n for d in ds]
        for p in pairs:
            y = ds[p] * lax.rsqrt(vs[p] + LN_X_EPS) * lng_ref[:, lanes[p]] + lnb_ref[:, lanes[p]]
            ya_ref[rows, lanes[p]] = ((y + bonus_ref[rows, lanes[p]])
                                      * gate_ref[rows, lanes[p]]).astype(ya_ref.dtype)


def _rwkv_scan(x, y, rt, opre, qb2, bbt, g, pcm, bonus, gate, lnx_g, lnx_b, ts=512, tw=4 * LANES):
    s, w = y.shape
    ts = min(ts, s)
    blk = pl.BlockSpec((ts, tw), lambda j, i: (i, j))
    vec = pl.BlockSpec((1, tw), lambda j, i: (0, j))
    return pl.pallas_call(
        _rwkv_scan_kernel,
        grid=(w // tw, s // ts),
        in_specs=[blk, blk, blk, blk,
                  pl.BlockSpec((ts, 2 * tw), lambda j, i: (i, j)),
                  pl.BlockSpec((tw, ts), lambda j, i: (j, i)),
                  blk, blk, blk, blk, vec, vec],
        out_specs=blk,
        out_shape=jax.ShapeDtypeStruct((s, w), BF16),
        scratch_shapes=[pltpu.VMEM((tw // LANES, LANES, LANES), F32)],
        compiler_params=_params(("parallel", "arbitrary")),
        name="rwkv_scan",
    )(x, y, rt, opre, qb2, bbt, g, pcm, bonus, gate, lnx_g.reshape(1, w), lnx_b.reshape(1, w))


V_ONES_ROWS = 16


LOG2_E = 1.4426950408889634
MOBA_LOG2_SCALE = (MOBA_HEAD_DIM ** -0.5) * LOG2_E


def _moba_prep_kernel(q_ref, k_ref, v_ref, cos_ref, sin_ref, qo_ref, qs_ref, ko_ref, vt_ref, km_ref):
    cos = cos_ref[...]
    sin = sin_ref[...]
    hd = MOBA_HEAD_DIM
    half = hd // 2
    tb = q_ref.shape[0]
    for h in range(q_ref.shape[1] // hd):
        cols = slice(h * hd, (h + 1) * hd)
        q = q_ref[:, cols]
        k = k_ref[:, cols]
        qr = q * cos + pltpu.roll(q, half, axis=1) * sin
        kr = k * cos + pltpu.roll(k, half, axis=1) * sin
        qo_ref[:, cols] = qr.astype(qo_ref.dtype)
        qs_ref[:, cols] = (qr * MOBA_LOG2_SCALE).astype(qs_ref.dtype)
        ko_ref[:, cols] = kr.astype(ko_ref.dtype)
        km_ref[0, :, cols] = jnp.mean(kr, axis=0, keepdims=True)
        vt_ref[h, 0, :hd, :] = v_ref[:, cols].T.astype(vt_ref.dtype)
        vt_ref[h, 0, hd:, :] = jnp.ones((V_ONES_ROWS, tb), vt_ref.dtype)


def _moba_prep(proj_m, cos, sin):
    s = proj_m.shape[0]
    w = proj_m.shape[1] // 3
    tb = MOBA_BLOCK
    nb = s // tb
    nh = w // MOBA_HEAD_DIM
    vrows = MOBA_HEAD_DIM + V_ONES_ROWS
    blk = lambda off: pl.BlockSpec((tb, w), lambda i, off=off: (i, off))
    tab = pl.BlockSpec((tb, MOBA_HEAD_DIM), lambda i: (i, 0))
    bo = jax.ShapeDtypeStruct((s, w), BF16)
    return pl.pallas_call(
        _moba_prep_kernel,
        grid=(nb,),
        in_specs=[blk(0), blk(1), blk(2), tab, tab],
        out_specs=[blk(0), blk(0), blk(0),
                   pl.BlockSpec((nh, 1, vrows, tb), lambda i: (0, i, 0, 0)),
                   pl.BlockSpec((1, 1, w), lambda i: (i, 0, 0))],
        out_shape=[bo, bo, bo, jax.ShapeDtypeStruct((nh, nb, vrows, tb), BF16),
                   jax.ShapeDtypeStruct((nb, 1, w), F32)],
        compiler_params=_params(("parallel",)),
        name="moba_prep",
    )(proj_m, proj_m, proj_m, cos, sin)


MOBA_CHAINS = 4
MOBA_GROUPS = 2
MOBA_HEADS_PER_STEP = 2


def _moba_attn_kernel(q_ref, qs_ref, k_ref, vt_ref, km_ref, o_ref, bias_ref):
    i = pl.program_id(1)
    tb = MOBA_BLOCK
    nb = km_ref.shape[0]
    hd = MOBA_HEAD_DIM
    heads = range(q_ref.shape[1] // hd)
    cols = [slice(h * hd, (h + 1) * hd) for h in heads]
    q = [qs_ref[:, c] for c in cols]

    own = pl.multiple_of(i * tb, tb)
    s_own = [_dot_nt(k_ref[pl.ds(own, tb), cols[h]], q[h]) for h in heads]
    raw0 = [[_dot_nt(k_ref[g * tb:(g + 1) * tb, cols[h]], q[h]) for g in range(MOBA_CHAINS)] for h in heads]

    gates = [_dot_nt(km_ref[:, cols[h]].astype(BF16), q_ref[:, cols[h]]) for h in heads]
    rid = lax.broadcasted_iota(jnp.int32, (nb, tb), 0).astype(F32)
    biases = []
    for h in heads:
        gate = jnp.where(rid < i.astype(F32), gates[h], -jnp.inf)
        bias = jnp.full(gate.shape, -jnp.inf, F32)
        for _ in range(MOBA_TOPK):
            m = jnp.max(gate, axis=0, keepdims=True)
            first = jnp.min(jnp.where(gate == m, rid, float(nb)), axis=0, keepdims=True)
            pick = (rid == first) & (m > -jnp.inf)
            bias = jnp.where(pick, 0.0, bias)
            gate = jnp.where(pick, -jnp.inf, gate)
        bias_ref[h] = bias
        biases.append(bias)

    ss0 = [[r.astype(BF16) + biases[h][g:g + 1, :].astype(BF16) for g, r in enumerate(raw0[h])]
           for h in heads]

    ki = lax.broadcasted_iota(jnp.int32, (tb, tb), 0)
    qi = lax.broadcasted_iota(jnp.int32, (tb, tb), 1)
    init = []
    for h in heads:
        s = jnp.where(ki <= qi, s_own[h], -jnp.inf).astype(BF16)
        m0 = jnp.max(s, axis=0, keepdims=True)
        acc0 = _dot(vt_ref[h, i], jnp.exp2(s - m0))
        m0 = m0.astype(F32)
        init.append(((m0, acc0),) + ((m0, jnp.zeros_like(acc0)),) * (MOBA_CHAINS - 1))

    def blocks(grp):
        return [jnp.minimum(grp * MOBA_CHAINS + g, nb - 1) for g in range(MOBA_CHAINS)]

    def scores(h, grp):
        return [_dot_nt(k_ref[pl.ds(pl.multiple_of(j * tb, tb), tb), cols[h]], q[h]).astype(BF16)
                + bias_ref[h, pl.ds(j, 1), :].astype(BF16) for j in blocks(grp)]

    def update(h, chains, ss, grp):
        ms = [jnp.maximum(m, jnp.max(sj, axis=0, keepdims=True).astype(F32))
              for (m, _), sj in zip(chains, ss)]
        ps = [jnp.exp2(sj - mn.astype(BF16)) for sj, mn in zip(ss, ms)]
        return tuple((mn, jnp.exp2(m - mn) * acc + _dot(vt_ref[h, j], pj))
                     for (m, acc), mn, pj, j in zip(chains, ms, ps, blocks(grp)))

    def body(t, carry):
        first = t * MOBA_GROUPS
        out = []
        for h in heads:
            chains, ss = carry[h]
            later = [scores(h, first + d) for d in range(1, MOBA_GROUPS)]
            ss_next = scores(h, first + MOBA_GROUPS)
            for d, sd in enumerate([ss] + later):
                chains = update(h, chains, sd, first + d)
            out.append((chains, ss_next))
        return tuple(out)

    per_trip = MOBA_CHAINS * MOBA_GROUPS
    state = lax.fori_loop(0, (i + per_trip - 1) // per_trip, body,
                          tuple((init[h], ss0[h]) for h in heads))
    for h in heads:
        chains = state[h][0]
        m = chains[0][0]
        for mg, _ in chains[1:]:
            m = jnp.maximum(m, mg)
        acc = sum(jnp.exp2(mg - m) * ag for mg, ag in chains)
        out = acc[:hd, :] / acc[hd:hd + 1, :]
        o_ref[:, cols[h]] = out.T.astype(o_ref.dtype)


def _moba_attn(q, qs, k, vt, kmean):
    s, w = q.shape
    tb = MOBA_BLOCK
    nb = s // tb
    nh = MOBA_HEADS_PER_STEP
    hd = nh * MOBA_HEAD_DIM
    vrows = vt.shape[2]
    return pl.pallas_call(
        _moba_attn_kernel,
        grid=(w // hd, nb),
        in_specs=[pl.BlockSpec((tb, hd), lambda h, i: (i, h)),
                  pl.BlockSpec((tb, hd), lambda h, i: (i, h)),
                  pl.BlockSpec((s, hd), lambda h, i: (0, h)),
                  pl.BlockSpec((nh, nb, vrows, tb), lambda h, i: (h, 0, 0, 0)),
                  pl.BlockSpec((nb, hd), lambda h, i: (0, h))],
        out_specs=pl.BlockSpec((tb, hd), lambda h, i: (i, h)),
        out_shape=jax.ShapeDtypeStruct((s, w), BF16),
        scratch_shapes=[pltpu.VMEM((nh, nb, tb), F32)],
        compiler_params=_params(("parallel", "arbitrary")),
        name="moba_attn",
    )(q, qs, k, vt, kmean)


def _top_rows(x, n):
    rows = x.shape[0]
    rid = lax.broadcasted_iota(jnp.int32, x.shape, 0).astype(F32)
    vals, idxs = [], []
    for _ in range(n):
        m = jnp.max(x, axis=0, keepdims=True)
        first = jnp.min(jnp.where(x == m, rid, float(rows)), axis=0, keepdims=True)
        vals.append(m)
        idxs.append(first)
        x = jnp.where(rid == first, -jnp.inf, x)
    return jnp.concatenate(vals, axis=0), jnp.concatenate(idxs, axis=0)


def _peer_route_kernel(q_ref, keys_ref, e1_ref, e2_ref, g_ref):
    n = PEER_TOPK
    tm = q_ref.shape[0]
    e1s, e2s, gs = [], [], []
    for h in range(PEER_HEADS):
        tops = []
        for p in range(2):
            hp = 2 * h + p
            cols = slice(hp * PEER_HALF, (hp + 1) * PEER_HALF)
            rows = slice(hp * PEER_N_KEYS, (hp + 1) * PEER_N_KEYS)
            st = _dot_nt(keys_ref[rows, :], q_ref[:, cols].astype(BF16))
            tops.append(_top_rows(st, n))
        (s1, i1), (s2, i2) = tops
        hn = n // 2
        cand = jnp.concatenate([s1[0:1, :] + s2]
                               + [s1[a:a + 1, :] + s2[:hn, :] for a in range(1, hn)]
                               + [s1[hn:, :] + s2[0:1, :]], axis=0)
        f_s, f_pos = _top_rows(cand, n)
        mid = jnp.floor((f_pos - n) * (1.0 / hn))
        tail0 = float(n + (hn - 1) * hn)
        pa = jnp.where(f_pos < n, 0.0, jnp.where(f_pos < tail0, 1.0 + mid, f_pos - tail0 + hn))
        pb = jnp.where(f_pos < n, f_pos, jnp.where(f_pos < tail0, f_pos - n - mid * hn, 0.0))
        e1 = jnp.zeros((n, tm), F32)
        e2 = jnp.zeros((n, tm), F32)
        for a in range(n):
            e1 = jnp.where(pa == float(a), i1[a:a + 1, :], e1)
            e2 = jnp.where(pb == float(a), i2[a:a + 1, :], e2)
        ex = jnp.exp(f_s - f_s[0:1, :])
        gs.append(ex / jnp.sum(ex, axis=0, keepdims=True))
        e1s.append(e1)
        e2s.append(e2)
    e1_ref[...] = jnp.concatenate(e1s, axis=0).T
    e2_ref[...] = jnp.concatenate(e2s, axis=0).T
    g_ref[...] = jnp.concatenate(gs, axis=0).T


def _peer_route(q, keys, tm=256):
    s, w = q.shape
    nsel = PEER_HEADS * PEER_TOPK
    tm = min(tm, s)
    out = jax.ShapeDtypeStruct((s, nsel), F32)
    ob = pl.BlockSpec((tm, nsel), lambda i: (i, 0))
    return pl.pallas_call(
        _peer_route_kernel,
        grid=(s // tm,),
        in_specs=[pl.BlockSpec((tm, w), lambda i: (i, 0)),
                  pl.BlockSpec(keys.shape, lambda i: (0, 0))],
        out_specs=[ob, ob, ob],
        out_shape=[out, out, out],
        compiler_params=_params(("parallel",)),
        name="peer_route",
    )(q, keys)


EXPAND_UNROLL = 16
EXPAND_GROUP = 4


def _peer_expand_kernel(e1_ref, e2_ref, g_ref, w_ref, stage_ref):
    nk = PEER_N_KEYS
    nsel = e1_ref.shape[1]
    rid = lax.broadcasted_iota(jnp.int32, (nk, nsel), 0).astype(F32)
    un = EXPAND_UNROLL

    def body(tt, carry):
        base = pl.multiple_of(tt * un, un)
        e1 = e1_ref[pl.ds(base, un), :]
        e2 = e2_ref[pl.ds(base, un), :]
        g = g_ref[pl.ds(base, un), :]
        for u0 in range(0, un, EXPAND_GROUP):
            us = range(u0, u0 + EXPAND_GROUP)
            lefts = [jnp.where(rid == e1[u:u + 1, :], g[u:u + 1, :], 0.0).astype(BF16) for u in us]
            rights = [jnp.where(rid == e2[u:u + 1, :], 1.0, 0.0).astype(BF16) for u in us]
            for u, l, r in zip(us, lefts, rights):
                stage_ref[u * nk:(u + 1) * nk, :] = _dot_nt(l, r)
        by_key = jnp.swapaxes(stage_ref[...].reshape(un, nk, nk), 0, 1)
        for a in range(nk):
            w_ref[pl.ds(base, un), a * nk:(a + 1) * nk] = by_key[a].astype(w_ref.dtype)
        return carry

    lax.fori_loop(0, e1_ref.shape[0] // un, body, 0)


def _peer_expand(e1, e2, g, tm=128):
    s, nsel = e1.shape
    nk = PEER_N_KEYS
    tm = min(tm, s)
    ib = pl.BlockSpec((tm, nsel), lambda i: (i, 0))
    return pl.pallas_call(
        _peer_expand_kernel,
        grid=(s // tm,),
        in_specs=[ib, ib, ib],
        out_specs=pl.BlockSpec((tm, nk * nk), lambda i: (i, 0)),
        out_shape=jax.ShapeDtypeStruct((s, nk * nk), BF16),
        scratch_shapes=[pltpu.VMEM((EXPAND_UNROLL * nk, nk), F32)],
        compiler_params=_params(("parallel",)),
        name="peer_expand",
    )(e1, e2, g)


FP8 = jnp.float8_e4m3fn
FP8_MAX = 448.0
SCALE_ROWS = 8


def _rmsnorm_fp8_kernel(x_ref, g_ref, o_ref, o8_ref, s_ref):
    x = x_ref[...]
    ms = jnp.mean(x * x, axis=-1, keepdims=True)
    y = x * lax.rsqrt(ms + NORM_EPS) * g_ref[...]
    o_ref[...] = y.astype(o_ref.dtype)
    scale = jnp.maximum(jnp.max(jnp.abs(y), axis=-1, keepdims=True), 1e-30) * (1.0 / FP8_MAX)
    o8_ref[...] = (y / scale).astype(o8_ref.dtype)
    s_ref[...] = jnp.broadcast_to(scale, s_ref.shape)


def _rmsnorm_fp8(x, g, tm=256):
    s, d = x.shape
    blk = pl.BlockSpec((tm, d), lambda i: (i, 0))
    return pl.pallas_call(
        _rmsnorm_fp8_kernel,
        grid=(s // tm,),
        in_specs=[blk, pl.BlockSpec((1, d), lambda i: (0, 0))],
        out_specs=[blk, blk, pl.BlockSpec((tm, LANES), lambda i: (i, 0))],
        out_shape=[jax.ShapeDtypeStruct((s, d), BF16), jax.ShapeDtypeStruct((s, d), FP8),
                   jax.ShapeDtypeStruct((s, LANES), F32)],
        compiler_params=_params(("parallel",)),
        name="rmsnorm_fp8",
    )(x, g.reshape(1, d))


def _quant_rows_kernel(u_ref, u8_ref, s_ref):
    u = u_ref[...]
    scale = jnp.maximum(jnp.max(jnp.abs(u), axis=-1, keepdims=True), 1e-30) * (1.0 / FP8_MAX)
    u8_ref[...] = (u / scale).astype(u8_ref.dtype)
    s_ref[...] = jnp.broadcast_to(scale, (u.shape[0], LANES)).T[:SCALE_ROWS, :]


def _quant_rows(u, te=512):
    ne, d = u.shape
    return pl.pallas_call(
        _quant_rows_kernel,
        grid=(ne // te,),
        in_specs=[pl.BlockSpec((te, d), lambda e: (e, 0))],
        out_specs=[pl.BlockSpec((te, d), lambda e: (e, 0)),
                   pl.BlockSpec((SCALE_ROWS, te), lambda e: (0, e))],
        out_shape=[jax.ShapeDtypeStruct((ne, d), FP8), jax.ShapeDtypeStruct((SCALE_ROWS, ne), F32)],
        compiler_params=_params(("parallel",)),
        name="peer_quant_u",
    )(u)


def _peer_dense_kernel(h_ref, hs_ref, u_ref, us_ref, v_ref, vs_ref, w_ref, o_ref):
    @pl.when(pl.program_id(1) == 0)
    def _():
        o_ref[...] = jnp.zeros_like(o_ref)

    act = _dot_nt(h_ref[...], u_ref[...]) * hs_ref[:, 0:1] * us_ref[0:1, :]
    gelu = 0.5 * act * (1.0 + lax.erf(act * (2.0 ** -0.5)))
    mix = w_ref[...].astype(F32) * gelu * vs_ref[0:1, :]
    scale = jnp.maximum(jnp.max(jnp.abs(mix), axis=-1, keepdims=True), 1e-30) * (1.0 / FP8_MAX)
    o_ref[...] += _dot((mix / scale).astype(FP8), v_ref[...]) * scale


def _peer_dense(h8, hs, u8, us, v8, vs, wdense, tm=512, te=1024):
    s, d = h8.shape
    ne = u8.shape[0]
    tm = min(tm, s)
    return pl.pallas_call(
        _peer_dense_kernel,
        grid=(s // tm, ne // te),
        in_specs=[pl.BlockSpec((tm, d), lambda i, e: (i, 0)),
                  pl.BlockSpec((tm, LANES), lambda i, e: (i, 0)),
                  pl.BlockSpec((te, d), lambda i, e: (e, 0)),
                  pl.BlockSpec((SCALE_ROWS, te), lambda i, e: (0, e)),
                  pl.BlockSpec((te, d), lambda i, e: (e, 0)),
                  pl.BlockSpec((SCALE_ROWS, te), lambda i, e: (0, e)),
                  pl.BlockSpec((tm, te), lambda i, e: (i, e))],
        out_specs=pl.BlockSpec((tm, d), lambda i, e: (i, 0)),
        out_shape=jax.ShapeDtypeStruct((s, d), F32),
        compiler_params=_params(("parallel", "arbitrary")),
        name="peer_dense",
    )(h8, hs, u8, us, v8, vs, wdense)


def _add_rmsnorm_kernel(x_ref, y_ref, g_ref, o_ref):
    x = x_ref[...] + y_ref[...]
    ms = jnp.mean(x * x, axis=-1, keepdims=True)
    o_ref[...] = x * lax.rsqrt(ms + NORM_EPS) * g_ref[...]


def _add_rmsnorm(x, y, g, tm=256):
    s, d = x.shape
    blk = pl.BlockSpec((tm, d), lambda i: (i, 0))
    return pl.pallas_call(
        _add_rmsnorm_kernel,
        grid=(s // tm,),
        in_specs=[blk, blk, pl.BlockSpec((1, d), lambda i: (0, 0))],
        out_specs=blk,
        out_shape=jax.ShapeDtypeStruct((s, d), F32),
        compiler_params=_params(("parallel",)),
        name="residual_final_norm",
    )(x, y, g.reshape(1, d))


def _rwkv_branch(proj_r, w0, w_up, a0, a_up, g_up, k_k, k_a, r_k, lnx_g, lnx_b):
    g_up_pad = jnp.pad(g_up, ((0, GATE_LORA_PAD - GATE_LORA), (0, 0))).astype(BF16)
    wpre, a, g = _rwkv_lora(proj_r, w0, w_up.astype(BF16), a0, a_up.astype(BF16), g_up_pad)
    rt, bonus, x, y, opre, qb2, bbt, gst, pcm = _rwkv_chunk(proj_r, wpre, a, k_k, k_a, r_k.reshape(-1))
    return _rwkv_scan(x, y, rt, opre, qb2, bbt, gst, pcm, bonus, g, lnx_g, lnx_b)


def _rope_tables(s):
    half = MOBA_HEAD_DIM // 2
    inv_freq = ROPE_THETA ** (-jnp.arange(half, dtype=F32) / half)
    ang = jnp.arange(s, dtype=jnp.int32).astype(F32)[:, None] * inv_freq[None, :]
    cos = jnp.cos(ang)
    sin = jnp.sin(ang)
    return jnp.concatenate([cos, cos], axis=-1), jnp.concatenate([-sin, sin], axis=-1)


def _moba_branch(proj_m):
    s = proj_m.shape[0]
    cos, sin = _rope_tables(s)
    q, qs, k, vt, kmean = _moba_prep(proj_m, cos, sin)
    return _moba_attn(q, qs, k, vt, kmean.reshape(kmean.shape[0], kmean.shape[2]))


def _peer_layer(x1, norm2_g, w_q, sub_keys, expert_u, expert_v, final_g):
    s, d = x1.shape
    h2, h8, hs = _rmsnorm_fp8(x1, norm2_g)
    q = _matmul(h2, w_q.astype(BF16), F32, tm=min(1024, s), tn=512, name="peer_query")
    keys = sub_keys.reshape(PEER_HEADS * 2 * PEER_N_KEYS, PEER_HALF).astype(BF16)
    e1, e2, g = _peer_route(q, keys)
    wdense = _peer_expand(e1, e2, g)
    u8, us = _quant_rows(expert_u)
    v8, vs = _quant_rows(expert_v)
    peer = _peer_dense(h8, hs, u8, us, v8, vs, wdense)
    return _add_rmsnorm(x1, peer, final_g)


def kernel(x, norm1_g, w_in, rwkv_mu, rwkv_w0, rwkv_w_up, rwkv_a0, rwkv_a_up, rwkv_g_up, rwkv_k_k, rwkv_k_a, rwkv_r_k, rwkv_lnx_g, rwkv_lnx_b, w_branch_rwkv, w_branch_moba, w_out, norm2_g, peer_w_q, peer_sub_keys, peer_u, peer_v, final_g):
    b, s, d = x.shape
    depth = w_in.shape[0]
    assert b == 1 and depth == 1
    x2d = x.reshape(s, d)
    l = 0
    rw = rwkv_w0.shape[-1]
    shift_w = 3 * rw + DECAY_LORA + AAA_LORA + GATE_LORA
    tn = 512
    shift_pad = -(-shift_w // tn) * tn
    mw = w_branch_moba.shape[1]
    tm = min(1024, s)

    h = _rmsnorm(x2d, norm1_g[l], BF16)
    wt = jnp.transpose(w_in[l])
    mu = jnp.pad(rwkv_mu[l], (0, shift_pad - shift_w)).reshape(1, shift_pad)
    proj_r = _matmul_shift(h, wt, mu, tm=tm, tn=tn)
    proj_m = _matmul_nt(h, wt, F32, tm=tm, tn=tn, name="in_proj_moba", row0=shift_w, n=3 * mw)
    gates = _matmul_nt(h, wt, F32, tm=tm, tn=tn, name="in_proj_gates", row0=shift_w + 3 * mw, n=2 * d)

    y_a = _rwkv_branch(proj_r, rwkv_w0[l], rwkv_w_up[l], rwkv_a0[l], rwkv_a_up[l], rwkv_g_up[l],
                       rwkv_k_k[l], rwkv_k_a[l], rwkv_r_k[l], rwkv_lnx_g[l], rwkv_lnx_b[l])
    y_b = _moba_branch(proj_m)
    mixed = _gated_pair(y_a, w_branch_rwkv[l].astype(BF16), y_b, w_branch_moba[l].astype(BF16),
                        gates, tm=tm, tn=512)
    x1 = _matmul_residual(mixed, w_out[l].astype(BF16), x2d, tm=tm, tn=512)
    out = _peer_layer(x1, norm2_g[l], peer_w_q[l], peer_sub_keys[l], peer_u[l], peer_v[l], final_g)
    return out.reshape(b, s, d)
```

```python
import functools

import jax
import jax.numpy as jnp
from jax import lax
from jax.experimental import pallas as pl
from jax.experimental.pallas import tpu as pltpu

F32 = jnp.float32
BF16 = jnp.bfloat16

NORM_EPS = 1e-6
LANES = 128
F32_SUBLANES = 8
RWKV_HEAD_DIM = 64
DECAY_LORA = 128
AAA_LORA = 128
GATE_LORA = 480
GATE_LORA_PAD = 512
LN_X_EPS = 64e-5
CHUNK = 128
MOBA_HEAD_DIM = 128
MOBA_BLOCK = 256
MOBA_TOPK = 3
ROPE_THETA = 10000.0
PEER_HEADS = 8
PEER_N_KEYS = 128
PEER_HALF = 128
PEER_TOPK = 16

VMEM_LIMIT = 56 * 1024 * 1024

_NT = (((1,), (1,)), ((), ()))


def _params(sem, vmem=VMEM_LIMIT):
    return pltpu.CompilerParams(dimension_semantics=sem, vmem_limit_bytes=vmem)


def _dot(a, b):
    return jnp.dot(a, b, preferred_element_type=F32)


def _dot_nt(a, b):
    return lax.dot_general(a, b, _NT, preferred_element_type=F32)


def _split3(x):
    hi = x.astype(BF16)
    r1 = x - hi.astype(F32)
    mid = r1.astype(BF16)
    lo = (r1 - mid.astype(F32)).astype(BF16)
    return hi, mid, lo


def _dot_r3(x, r_bf16):
    hi, mid, lo = _split3(x)
    return _dot(hi, r_bf16) + _dot(mid, r_bf16) + _dot(lo, r_bf16)


def _rmsnorm_kernel(x_ref, g_ref, o_ref):
    x = x_ref[...]
    ms = jnp.mean(x * x, axis=-1, keepdims=True)
    o_ref[...] = (x * lax.rsqrt(ms + NORM_EPS) * g_ref[...]).astype(o_ref.dtype)


def _rmsnorm(x, g, out_dtype, tm=256):
    s, d = x.shape
    return pl.pallas_call(
        _rmsnorm_kernel,
        grid=(s // tm,),
        in_specs=[pl.BlockSpec((tm, d), lambda i: (i, 0)),
                  pl.BlockSpec((1, d), lambda i: (0, 0))],
        out_specs=pl.BlockSpec((tm, d), lambda i: (i, 0)),
        out_shape=jax.ShapeDtypeStruct((s, d), out_dtype),
        compiler_params=_params(("parallel",)),
        name="rmsnorm",
    )(x, g.reshape(1, d))


def _mm_kernel(a_ref, b_ref, o_ref):
    o_ref[...] = _dot(a_ref[...], b_ref[...]).astype(o_ref.dtype)


def _matmul(a, b, out_dtype, tm, tn, name, col0=0, n=None):
    m, k = a.shape
    n = b.shape[1] if n is None else n
    off = col0 // tn
    return pl.pallas_call(
        _mm_kernel,
        grid=(m // tm, n // tn),
        in_specs=[pl.BlockSpec((tm, k), lambda i, j: (i, 0)),
                  pl.BlockSpec((k, tn), lambda i, j: (0, j + off))],
        out_specs=pl.BlockSpec((tm, tn), lambda i, j: (i, j)),
        out_shape=jax.ShapeDtypeStruct((m, n), out_dtype),
        compiler_params=_params(("parallel", "parallel")),
        name=name,
    )(a, b)


BF16_SUBLANES = 16


def _mm_nt_kernel(a_ref, bt_ref, o_ref, b16_ref):
    @pl.when(pl.program_id(1) == 0)
    def _():
        b16_ref[...] = bt_ref[...].astype(b16_ref.dtype)

    o_ref[...] = _dot_nt(a_ref[...], b16_ref[...]).astype(o_ref.dtype)


def _matmul_nt(a, bt, out_dtype, tm, tn, name, row0, n):
    m, k = a.shape
    return pl.pallas_call(
        _mm_nt_kernel,
        grid=(n // tn, m // tm),
        in_specs=[pl.BlockSpec((tm, k), lambda j, i: (i, 0)),
                  pl.BlockSpec((pl.Element(tn), pl.Element(k)),
                               lambda j, i: (pl.multiple_of(row0 + j * tn, F32_SUBLANES), 0))],
        out_specs=pl.BlockSpec((tm, tn), lambda j, i: (i, j)),
        out_shape=jax.ShapeDtypeStruct((m, n), out_dtype),
        scratch_shapes=[pltpu.VMEM((tn, k), BF16)],
        compiler_params=_params(("parallel", "arbitrary")),
        name=name,
    )(a, bt)


def _mm_shift_kernel(a_ref, ap_ref, bt_ref, mu_ref, o_ref, b16_ref):
    i = pl.program_id(1)

    @pl.when(i == 0)
    def _():
        b16_ref[...] = bt_ref[...].astype(b16_ref.dtype)

    b = b16_ref[...]
    z = _dot_nt(a_ref[...], b)
    zp = _dot_nt(ap_ref[...], b)
    prev = jnp.where(i > 0, zp[BF16_SUBLANES - 1:BF16_SUBLANES, :], 0.0)
    row = lax.broadcasted_iota(jnp.int32, z.shape, 0)
    z_prev = jnp.where(row == 0, prev, pltpu.roll(z, 1, axis=0))
    o_ref[...] = z + (z_prev - z) * mu_ref[...]


def _matmul_shift(a, bt, mu, tm, tn):
    m, k = a.shape
    n = mu.shape[1]
    per = tm // BF16_SUBLANES
    return pl.pallas_call(
        _mm_shift_kernel,
        grid=(n // tn, m // tm),
        in_specs=[pl.BlockSpec((tm, k), lambda j, i: (i, 0)),
                  pl.BlockSpec((BF16_SUBLANES, k), lambda j, i: (jnp.maximum(i * per - 1, 0), 0)),
                  pl.BlockSpec((tn, k), lambda j, i: (j, 0)),
                  pl.BlockSpec((1, tn), lambda j, i: (0, j))],
        out_specs=pl.BlockSpec((tm, tn), lambda j, i: (i, j)),
        out_shape=jax.ShapeDtypeStruct((m, n), F32),
        scratch_shapes=[pltpu.VMEM((tn, k), BF16)],
        compiler_params=_params(("parallel", "arbitrary")),
        name="in_proj_rwkv",
    )(a, a, bt, mu)


def _gated_pair_kernel(ya_ref, wa_ref, yb_ref, wb_ref, ga_ref, gb_ref, o_ref):
    pa = _dot(ya_ref[...], wa_ref[...])
    pb = _dot(yb_ref[...], wb_ref[...])
    mixed = jax.nn.sigmoid(ga_ref[...]) * pa + jax.nn.sigmoid(gb_ref[...]) * pb
    o_ref[...] = mixed.astype(o_ref.dtype)


def _gated_pair(ya, wa, yb, wb, gates, tm, tn):
    m, k = ya.shape
    n = wa.shape[1]
    nb = n // tn
    return pl.pallas_call(
        _gated_pair_kernel,
        grid=(m // tm, nb),
        in_specs=[pl.BlockSpec((tm, k), lambda i, j: (i, 0)),
                  pl.BlockSpec((k, tn), lambda i, j: (0, j)),
                  pl.BlockSpec((tm, k), lambda i, j: (i, 0)),
                  pl.BlockSpec((k, tn), lambda i, j: (0, j)),
                  pl.BlockSpec((tm, tn), lambda i, j: (i, j)),
                  pl.BlockSpec((tm, tn), lambda i, j: (i, j + nb))],
        out_specs=pl.BlockSpec((tm, tn), lambda i, j: (i, j)),
        out_shape=jax.ShapeDtypeStruct((m, n), BF16),
        compiler_params=_params(("parallel", "parallel")),
        name="branch_merge",
    )(ya, wa, yb, wb, gates, gates)


def _mm_res_kernel(a_ref, b_ref, r_ref, o_ref):
    o_ref[...] = r_ref[...] + _dot(a_ref[...], b_ref[...])


def _matmul_residual(a, b, res, tm, tn):
    m, k = a.shape
    n = b.shape[1]
    return pl.pallas_call(
        _mm_res_kernel,
        grid=(m // tm, n // tn),
        in_specs=[pl.BlockSpec((tm, k), lambda i, j: (i, 0)),
                  pl.BlockSpec((k, tn), lambda i, j: (0, j)),
                  pl.BlockSpec((tm, tn), lambda i, j: (i, j))],
        out_specs=pl.BlockSpec((tm, tn), lambda i, j: (i, j)),
        out_shape=jax.ShapeDtypeStruct((m, n), F32),
        compiler_params=_params(("parallel", "parallel")),
        name="out_proj",
    )(a, b, res)


def _rwkv_lora_kernel(wl_ref, al_ref, gl0_ref, gl1_ref, w0_ref, wup_ref, a0_ref, aup_ref,
                      gup_ref, wpre_ref, a_ref, g_ref):
    half = GATE_LORA_PAD // 2
    wpre_ref[...] = w0_ref[...] + _dot(jnp.tanh(wl_ref[...]).astype(BF16), wup_ref[...])
    a_ref[...] = jax.nn.sigmoid(a0_ref[...] + _dot(al_ref[...].astype(BF16), aup_ref[...]))
    g_ref[...] = (_dot(jax.nn.sigmoid(gl0_ref[...]).astype(BF16), gup_ref[:half, :])
                  + _dot(jax.nn.sigmoid(gl1_ref[...]).astype(BF16), gup_ref[half:, :]))


def _rwkv_lora(proj_r, w0, w_up, a0, a_up, g_up_pad, tm=256):
    s = proj_r.shape[0]
    w = w0.shape[-1]
    half = GATE_LORA_PAD // 2
    c_wl = 3 * w // DECAY_LORA
    c_al = (3 * w + DECAY_LORA) // AAA_LORA
    c_gl = (3 * w + DECAY_LORA + AAA_LORA) // half
    row = lambda i: (i, 0)
    const = lambda i: (0, 0)
    out = jax.ShapeDtypeStruct((s, w), F32)
    return pl.pallas_call(
        _rwkv_lora_kernel,
        grid=(s // tm,),
        in_specs=[pl.BlockSpec((tm, DECAY_LORA), lambda i: (i, c_wl)),
                  pl.BlockSpec((tm, AAA_LORA), lambda i: (i, c_al)),
                  pl.BlockSpec((tm, half), lambda i: (i, c_gl)),
                  pl.BlockSpec((tm, half), lambda i: (i, c_gl + 1)),
                  pl.BlockSpec((1, w), const),
                  pl.BlockSpec((DECAY_LORA, w), const),
                  pl.BlockSpec((1, w), const),
                  pl.BlockSpec((AAA_LORA, w), const),
                  pl.BlockSpec((GATE_LORA_PAD, w), const)],
        out_specs=[pl.BlockSpec((tm, w), row)] * 3,
        out_shape=[out, out, out],
        compiler_params=_params(("parallel",)),
        name="rwkv_lora",
    )(proj_r, proj_r, proj_r, proj_r, w0.reshape(1, w), w_up, a0.reshape(1, w), a_up, g_up_pad)


def _head_pair_ones(width=LANES):
    r = lax.broadcasted_iota(jnp.int32, (width, width), 0) // RWKV_HEAD_DIM
    c = lax.broadcasted_iota(jnp.int32, (width, width), 1) // RWKV_HEAD_DIM
    return r == c


def _rwkv_prep(r_ref, k_ref, v_ref, wpre_ref, a_ref, kk_ref, ka_ref, rk_ref,
               rt_ref, at_ref, kh_ref, bh_ref, kb_ref, bb_ref, pc_ref, bonus_ref):
    tm = r_ref.shape[0]
    r = r_ref[...]
    k = k_ref[...]
    v = v_ref[...]
    a = a_ref[...]
    same_head = jnp.where(_head_pair_ones(r.shape[1]), 1.0, 0.0).astype(BF16)

    x = -wpre_ref[...]
    softplus = jnp.maximum(x, 0.0) + jnp.log1p(jnp.exp(-jnp.abs(x)))
    lw = -jnp.exp(-softplus - 0.5)

    kk = k * kk_ref[...]
    ss = _dot_r3(kk * kk, same_head)
    kkn = kk / jnp.maximum(jnp.sqrt(ss), 1e-12)
    k2 = k * (1.0 + (a - 1.0) * ka_ref[...])
    bonus_ref[...] = _dot_r3(r * k2 * rk_ref[...], same_head) * v

    ri = lax.broadcasted_iota(jnp.int32, (tm, tm), 0)
    ci = lax.broadcasted_iota(jnp.int32, (tm, tm), 1)
    same_chunk = (ri // CHUNK) == (ci // CHUNK)
    tri = jnp.where(same_chunk & (ci <= ri), 1.0, 0.0).astype(BF16)
    allc = jnp.where(same_chunk, 1.0, 0.0).astype(BF16)
    lw_parts = _split3(lw)
    cum = sum(_dot(tri, part) for part in lw_parts)
    tot = sum(_dot(allc, part) for part in lw_parts)

    p_in = jnp.exp(cum)
    p_inv = jnp.exp(-cum)
    p_tail = jnp.exp(tot - cum)
    b = kkn * a
    rt_ref[...] = r * p_in
    at_ref[...] = -kkn * jnp.exp(cum - lw)
    kh_ref[...] = k2 * p_inv
    bh_ref[...] = b * p_inv
    kb_ref[...] = k2 * p_tail
    bb_ref[...] = b * p_tail
    pc_ref[...] = jnp.exp(tot)


def _rwkv_chunk_kernel(r_ref, k_ref, v_ref, wpre_ref, a_ref, kk_ref, ka_ref, rk_ref,
                       rt_ref, bonus_ref, x_ref, y_ref, op_ref, qb_ref, bbt_ref, g_ref, pcm_ref,
                       at_ref, kh_ref, bh_ref, kb_ref, bb_ref, pc_ref):
    _rwkv_prep(r_ref, k_ref, v_ref, wpre_ref, a_ref, kk_ref, ka_ref, rk_ref,
               rt_ref, at_ref, kh_ref, bh_ref, kb_ref, bb_ref, pc_ref, bonus_ref)
    c = CHUNK
    nheads = LANES // RWKV_HEAD_DIM
    chunks = range(rt_ref.shape[0] // c)
    pairs = range(rt_ref.shape[1] // LANES)
    rows = [slice(q * c, (q + 1) * c) for q in chunks]
    lanes = [slice(p * LANES, (p + 1) * LANES) for p in pairs]
    tiles = [(q, p) for q in chunks for p in pairs]
    probs = [(n, h) for n in range(len(tiles)) for h in range(nheads)]
    lane = lax.broadcasted_iota(jnp.int32, (c, LANES), 1)
    ri = lax.broadcasted_iota(jnp.int32, (c, c), 0)
    ci = lax.broadcasted_iota(jnp.int32, (c, c), 1)
    strict = ci < ri
    incl = ci <= ri
    eye = jnp.where(ri == ci, 1.0, 0.0)
    head_mask = [(lane // RWKV_HEAD_DIM) == h for h in range(nheads)]
    tile = lambda ref, n: ref[rows[tiles[n][0]], lanes[tiles[n][1]]]

    kh = [tile(kh_ref, n).astype(BF16) for n in range(len(tiles))]
    bh = [tile(bh_ref, n).astype(BF16) for n in range(len(tiles))]
    a_h = [jnp.where(head_mask[h], tile(at_ref, n), 0.0).astype(BF16) for n, h in probs]
    r_h = [jnp.where(head_mask[h], tile(rt_ref, n), 0.0).astype(BF16) for n, h in probs]
    v_h = [jnp.where(head_mask[h], tile(v_ref, n), 0.0).astype(BF16) for n, h in probs]
    n_ab = [jnp.where(strict, _dot_nt(a, bh[n]), 0.0) for a, (n, _) in zip(a_h, probs)]
    n_ak = [jnp.where(strict, _dot_nt(a, kh[n]), 0.0).astype(BF16) for a, (n, _) in zip(a_h, probs)]
    q_k = [jnp.where(incl, _dot_nt(r, kh[n]), 0.0).astype(BF16) for r, (n, _) in zip(r_h, probs)]
    q_b = [jnp.where(incl, _dot_nt(r, bh[n]), 0.0) for r, (n, _) in zip(r_h, probs)]
    for qb, (n, h) in zip(q_b, probs):
        q, p = tiles[n]
        col = (p * nheads + h) * c
        qb_ref[rows[q], col:col + c] = qb.astype(qb_ref.dtype)

    t = [eye + n for n in n_ab]
    pw = n_ab
    for _ in range((c - 1).bit_length() - 1):
        pb = [x.astype(BF16) for x in pw]
        pw = [_dot(x, x) for x in pb]
        t = [ti + _dot(ti.astype(BF16), pi.astype(BF16)) for ti, pi in zip(t, pw)]
    tb = [ti.astype(BF16) for ti in t]
    xs = [_dot(ti, a) for ti, a in zip(tb, a_h)]
    nv = [_dot(n, v).astype(BF16) for n, v in zip(n_ak, v_h)]
    ys = [_dot(ti, z) for ti, z in zip(tb, nv)]
    os = [_dot(qk, v) for qk, v in zip(q_k, v_h)]
    same_head = _head_pair_ones()
    for n, (q, p) in enumerate(tiles):
        mine = [m for m, (nn, _) in enumerate(probs) if nn == n]
        x_ref[rows[q], lanes[p]] = sum(xs[m] for m in mine).astype(x_ref.dtype)
        y_ref[rows[q], lanes[p]] = sum(ys[m] for m in mine)
        op_ref[rows[q], lanes[p]] = sum(os[m] for m in mine)
        kbt = tile(kb_ref, n).T.astype(BF16)
        g_ref[rows[q], lanes[p]] = jnp.where(same_head, _dot(kbt, tile(v_ref, n).astype(BF16)), 0.0)
        bbt_ref[lanes[p], rows[q]] = tile(bb_ref, n).T.astype(bbt_ref.dtype)
        pcm_ref[rows[q], lanes[p]] = tile(pc_ref, n).T


def _rwkv_chunk(proj_r, wpre, a, k_k, k_a, r_k, tc=2 * CHUNK, tw=2 * LANES):
    s, w = wpre.shape
    ncol = w // tw
    blk = lambda off: pl.BlockSpec((tc, tw), lambda i, j, off=off: (i, j + off))
    vec = pl.BlockSpec((1, tw), lambda i, j: (0, j))
    f32o = jax.ShapeDtypeStruct((s, w), F32)
    return pl.pallas_call(
        _rwkv_chunk_kernel,
        grid=(s // tc, ncol),
        in_specs=[blk(0), blk(ncol), blk(2 * ncol), blk(0), blk(0), vec, vec, vec],
        out_specs=[blk(0), blk(0), blk(0), blk(0), blk(0),
                   pl.BlockSpec((tc, 2 * tw), lambda i, j: (i, j)),
                   pl.BlockSpec((tw, tc), lambda i, j: (j, i)),
                   blk(0), blk(0)],
        out_shape=[f32o, f32o, jax.ShapeDtypeStruct((s, w), BF16), f32o, f32o,
                   jax.ShapeDtypeStruct((s, 2 * w), BF16),
                   jax.ShapeDtypeStruct((w, s), BF16),
                   f32o, f32o],
        scratch_shapes=[pltpu.VMEM((tc, tw), F32)] * 6,
        compiler_params=_params(("parallel", "parallel")),
        name="rwkv_chunk",
    )(proj_r, proj_r, proj_r, wpre, a, k_k.reshape(1, w), k_a.reshape(1, w), r_k.reshape(1, w))


def _rwkv_scan_kernel(x_ref, y_ref, rt_ref, op_ref, qb_ref, bbt_ref, g_ref, pcm_ref,
                      bonus_ref, gate_ref, lng_ref, lnb_ref, ya_ref, h_ref):
    c = CHUNK
    pairs = range(x_ref.shape[1] // LANES)
    lanes = [slice(p * LANES, (p + 1) * LANES) for p in pairs]

    @pl.when(pl.program_id(1) == 0)
    def _():
        h_ref[...] = jnp.zeros_like(h_ref)

    lane = lax.broadcasted_iota(jnp.int32, (c, LANES), 1)
    first = lane < RWKV_HEAD_DIM
    same_head = _head_pair_ones()
    ones_head = jnp.where(same_head, 1.0, 0.0).astype(BF16)
    inv_n = 1.0 / RWKV_HEAD_DIM
    for q in range(x_ref.shape[0] // c):
        rows = slice(q * c, (q + 1) * c)
        hs = [h_ref[p] for p in pairs]
        hb = [h.astype(BF16) for h in hs]
        us = [_dot(x_ref[rows, lanes[p]], hb[p]) + y_ref[rows, lanes[p]] for p in pairs]
        u2 = [jnp.concatenate([jnp.where(first, u, 0.0), jnp.where(first, 0.0, u)], axis=0).astype(BF16)
              for u in us]
        os = [_dot(rt_ref[rows, lanes[p]].astype(BF16), hb[p]) + op_ref[rows, lanes[p]]
              + _dot(qb_ref[rows, p * 2 * c:(p + 1) * 2 * c], u2[p]) for p in pairs]
        upd = [_dot(bbt_ref[lanes[p], rows], us[p].astype(BF16)) for p in pairs]
        for p in pairs:
            h_ref[p] = (pcm_ref[rows, lanes[p]] * hs[p] + g_ref[rows, lanes[p]]
                        + jnp.where(same_head, upd[p], 0.0))
        mus = [_dot_r3(o, ones_head) * inv_n for o in os]
        ds = [o - mu for o, mu in zip(os, mus)]
        vs = [_dot_r3(d * d, ones_head) * inv_n for d in ds]
        for p in pairs:
            y = ds[p] * lax.rsqrt(vs[p] + LN_X_EPS) * lng_ref[:, lanes[p]] + lnb_ref[:, lanes[p]]
            ya_ref[rows, lanes[p]] = ((y + bonus_ref[rows, lanes[p]])
                                      * gate_ref[rows, lanes[p]]).astype(ya_ref.dtype)


def _rwkv_scan(x, y, rt, opre, qb2, bbt, g, pcm, bonus, gate, lnx_g, lnx_b, ts=512, tw=4 * LANES):
    s, w = y.shape
    ts = min(ts, s)
    blk = pl.BlockSpec((ts, tw), lambda j, i: (i, j))
    vec = pl.BlockSpec((1, tw), lambda j, i: (0, j))
    return pl.pallas_call(
        _rwkv_scan_kernel,
        grid=(w // tw, s // ts),
        in_specs=[blk, blk, blk, blk,
                  pl.BlockSpec((ts, 2 * tw), lambda j, i: (i, j)),
                  pl.BlockSpec((tw, ts), lambda j, i: (j, i)),
                  blk, blk, blk, blk, vec, vec],
        out_specs=blk,
        out_shape=jax.ShapeDtypeStruct((s, w), BF16),
        scratch_shapes=[pltpu.VMEM((tw // LANES, LANES, LANES), F32)],
        compiler_params=_params(("parallel", "arbitrary")),
        name="rwkv_scan",
    )(x, y, rt, opre, qb2, bbt, g, pcm, bonus, gate, lnx_g.reshape(1, w), lnx_b.reshape(1, w))


V_ONES_ROWS = 16


LOG2_E = 1.4426950408889634
MOBA_LOG2_SCALE = (MOBA_HEAD_DIM ** -0.5) * LOG2_E


def _moba_prep_kernel(q_ref, k_ref, v_ref, cos_ref, sin_ref, qo_ref, qs_ref, ko_ref, vt_ref, km_ref):
    cos = cos_ref[...]
    sin = sin_ref[...]
    hd = MOBA_HEAD_DIM
    half = hd // 2
    tb = q_ref.shape[0]
    for h in range(q_ref.shape[1] // hd):
        cols = slice(h * hd, (h + 1) * hd)
        q = q_ref[:, cols]
        k = k_ref[:, cols]
        qr = q * cos + pltpu.roll(q, half, axis=1) * sin
        kr = k * cos + pltpu.roll(k, half, axis=1) * sin
        qo_ref[:, cols] = qr.astype(qo_ref.dtype)
        qs_ref[:, cols] = (qr * MOBA_LOG2_SCALE).astype(qs_ref.dtype)
        ko_ref[:, cols] = kr.astype(ko_ref.dtype)
        km_ref[0, :, cols] = jnp.mean(kr, axis=0, keepdims=True)
        vt_ref[h, 0, :hd, :] = v_ref[:, cols].T.astype(vt_ref.dtype)
        vt_ref[h, 0, hd:, :] = jnp.ones((V_ONES_ROWS, tb), vt_ref.dtype)


def _moba_prep(proj_m, cos, sin):
    s = proj_m.shape[0]
    w = proj_m.shape[1] // 3
    tb = MOBA_BLOCK
    nb = s // tb
    nh = w // MOBA_HEAD_DIM
    vrows = MOBA_HEAD_DIM + V_ONES_ROWS
    blk = lambda off: pl.BlockSpec((tb, w), lambda i, off=off: (i, off))
    tab = pl.BlockSpec((tb, MOBA_HEAD_DIM), lambda i: (i, 0))
    bo = jax.ShapeDtypeStruct((s, w), BF16)
    return pl.pallas_call(
        _moba_prep_kernel,
        grid=(nb,),
        in_specs=[blk(0), blk(1), blk(2), tab, tab],
        out_specs=[blk(0), blk(0), blk(0),
                   pl.BlockSpec((nh, 1, vrows, tb), lambda i: (0, i, 0, 0)),
                   pl.BlockSpec((1, 1, w), lambda i: (i, 0, 0))],
        out_shape=[bo, bo, bo, jax.ShapeDtypeStruct((nh, nb, vrows, tb), BF16),
                   jax.ShapeDtypeStruct((nb, 1, w), F32)],
        compiler_params=_params(("parallel",)),
        name="moba_prep",
    )(proj_m, proj_m, proj_m, cos, sin)


MOBA_TRIP_BLOCKS = 8
MOBA_HEADS_PER_STEP = 2


def _moba_attn_kernel(q_ref, qs_ref, k_ref, vt_ref, km_ref, o_ref, bias_ref, s_ref):
    i = pl.program_id(1)
    tb = MOBA_BLOCK
    nb = km_ref.shape[0]
    hd = MOBA_HEAD_DIM
    heads = range(q_ref.shape[1] // hd)
    cols = [slice(h * hd, (h + 1) * hd) for h in heads]
    q = [qs_ref[:, c] for c in cols]

    own = pl.multiple_of(i * tb, tb)
    s_own = [_dot_nt(k_ref[pl.ds(own, tb), cols[h]], q[h]) for h in heads]

    gates = [_dot_nt(km_ref[:, cols[h]].astype(BF16), q_ref[:, cols[h]]) for h in heads]
    rid = lax.broadcasted_iota(jnp.int32, (nb, tb), 0).astype(F32)
    for h in heads:
        gate = jnp.where(rid < i.astype(F32), gates[h], -jnp.inf)
        bias = jnp.full(gate.shape, -jnp.inf, F32)
        for _ in range(MOBA_TOPK):
            m = jnp.max(gate, axis=0, keepdims=True)
            first = jnp.min(jnp.where(gate == m, rid, float(nb)), axis=0, keepdims=True)
            pick = (rid == first) & (m > -jnp.inf)
            bias = jnp.where(pick, 0.0, bias)
            gate = jnp.where(pick, -jnp.inf, gate)
        bias_ref[h] = bias

    ki = lax.broadcasted_iota(jnp.int32, (tb, tb), 0)
    qi = lax.broadcasted_iota(jnp.int32, (tb, tb), 1)
    s_own = [jnp.where(ki <= qi, s, -jnp.inf).astype(BF16) for s in s_own]
    m_own = tuple(jnp.max(s, axis=0, keepdims=True) for s in s_own)

    def blocks(t):
        return [jnp.minimum(t * MOBA_TRIP_BLOCKS + g, nb - 1) for g in range(MOBA_TRIP_BLOCKS)]

    trips = (i + MOBA_TRIP_BLOCKS - 1) // MOBA_TRIP_BLOCKS

    def score_pass(t, ms):
        out = []
        for h in heads:
            ss = [_dot_nt(k_ref[pl.ds(pl.multiple_of(j * tb, tb), tb), cols[h]], q[h]).astype(BF16)
                  + bias_ref[h, pl.ds(j, 1), :].astype(BF16) for j in blocks(t)]
            m = ms[h]
            for j, sj in zip(blocks(t), ss):
                s_ref[h, j] = sj
                m = jnp.maximum(m, jnp.max(sj, axis=0, keepdims=True))
            out.append(m)
        return tuple(out)

    ms = lax.fori_loop(0, trips, score_pass, m_own)

    def value_pass(t, accs):
        out = []
        for h in heads:
            ps = [jnp.exp2(s_ref[h, j] - ms[h]) for j in blocks(t)]
            acc = accs[h]
            for j, pj in zip(blocks(t), ps):
                acc = acc + _dot(vt_ref[h, j], pj)
            out.append(acc)
        return tuple(out)

    acc0 = tuple(_dot(vt_ref[h, i], jnp.exp2(s_own[h] - ms[h])) for h in heads)
    accs = lax.fori_loop(0, trips, value_pass, acc0)
    for h in heads:
        out = accs[h][:hd, :] / accs[h][hd:hd + 1, :]
        o_ref[:, cols[h]] = out.T.astype(o_ref.dtype)


def _moba_attn(q, qs, k, vt, kmean):
    s, w = q.shape
    tb = MOBA_BLOCK
    nb = s // tb
    nh = MOBA_HEADS_PER_STEP
    hd = nh * MOBA_HEAD_DIM
    vrows = vt.shape[2]
    return pl.pallas_call(
        _moba_attn_kernel,
        grid=(w // hd, nb),
        in_specs=[pl.BlockSpec((tb, hd), lambda h, i: (i, h)),
                  pl.BlockSpec((tb, hd), lambda h, i: (i, h)),
                  pl.BlockSpec((s, hd), lambda h, i: (0, h)),
                  pl.BlockSpec((nh, nb, vrows, tb), lambda h, i: (h, 0, 0, 0)),
                  pl.BlockSpec((nb, hd), lambda h, i: (0, h))],
        out_specs=pl.BlockSpec((tb, hd), lambda h, i: (i, h)),
        out_shape=jax.ShapeDtypeStruct((s, w), BF16),
        scratch_shapes=[pltpu.VMEM((nh, nb, tb), F32), pltpu.VMEM((nh, nb, tb, tb), BF16)],
        compiler_params=_params(("parallel", "arbitrary")),
        name="moba_attn",
    )(q, qs, k, vt, kmean)


def _top_rows(x, n):
    rows = x.shape[0]
    rid = lax.broadcasted_iota(jnp.int32, x.shape, 0).astype(F32)
    vals, idxs = [], []
    for _ in range(n):
        m = jnp.max(x, axis=0, keepdims=True)
        first = jnp.min(jnp.where(x == m, rid, float(rows)), axis=0, keepdims=True)
        vals.append(m)
        idxs.append(first)
        x = jnp.where(rid == first, -jnp.inf, x)
    return jnp.concatenate(vals, axis=0), jnp.concatenate(idxs, axis=0)


def _peer_route_kernel(q_ref, keys_ref, e1_ref, e2_ref, g_ref):
    n = PEER_TOPK
    tm = q_ref.shape[0]
    e1s, e2s, gs = [], [], []
    for h in range(PEER_HEADS):
        tops = []
        for p in range(2):
            hp = 2 * h + p
            cols = slice(hp * PEER_HALF, (hp + 1) * PEER_HALF)
            rows = slice(hp * PEER_N_KEYS, (hp + 1) * PEER_N_KEYS)
            st = _dot_nt(keys_ref[rows, :], q_ref[:, cols].astype(BF16))
            tops.append(_top_rows(st, n))
        (s1, i1), (s2, i2) = tops
        hn = n // 2
        cand = jnp.concatenate([s1[0:1, :] + s2]
                               + [s1[a:a + 1, :] + s2[:hn, :] for a in range(1, hn)]
                               + [s1[hn:, :] + s2[0:1, :]], axis=0)
        f_s, f_pos = _top_rows(cand, n)
        mid = jnp.floor((f_pos - n) * (1.0 / hn))
        tail0 = float(n + (hn - 1) * hn)
        pa = jnp.where(f_pos < n, 0.0, jnp.where(f_pos < tail0, 1.0 + mid, f_pos - tail0 + hn))
        pb = jnp.where(f_pos < n, f_pos, jnp.where(f_pos < tail0, f_pos - n - mid * hn, 0.0))
        e1 = jnp.zeros((n, tm), F32)
        e2 = jnp.zeros((n, tm), F32)
        for a in range(n):
            e1 = jnp.where(pa == float(a), i1[a:a + 1, :], e1)
            e2 = jnp.where(pb == float(a), i2[a:a + 1, :], e2)
        ex = jnp.exp(f_s - f_s[0:1, :])
        gs.append(ex / jnp.sum(ex, axis=0, keepdims=True))
        e1s.append(e1)
        e2s.append(e2)
    e1_ref[...] = jnp.concatenate(e1s, axis=0).T
    e2_ref[...] = jnp.concatenate(e2s, axis=0).T
    g_ref[...] = jnp.concatenate(gs, axis=0).T


def _peer_route(q, keys, tm=256):
    s, w = q.shape
    nsel = PEER_HEADS * PEER_TOPK
    tm = min(tm, s)
    out = jax.ShapeDtypeStruct((s, nsel), F32)
    ob = pl.BlockSpec((tm, nsel), lambda i: (i, 0))
    return pl.pallas_call(
        _peer_route_kernel,
        grid=(s // tm,),
        in_specs=[pl.BlockSpec((tm, w), lambda i: (i, 0)),
                  pl.BlockSpec(keys.shape, lambda i: (0, 0))],
        out_specs=[ob, ob, ob],
        out_shape=[out, out, out],
        compiler_params=_params(("parallel",)),
        name="peer_route",
    )(q, keys)


EXPAND_UNROLL = 16
EXPAND_GROUP = 4


def _peer_expand_kernel(e1_ref, e2_ref, g_ref, w_ref, stage_ref):
    nk = PEER_N_KEYS
    nsel = e1_ref.shape[1]
    rid = lax.broadcasted_iota(jnp.int32, (nk, nsel), 0).astype(F32)
    un = EXPAND_UNROLL

    def body(tt, carry):
        base = pl.multiple_of(tt * un, un)
        e1 = e1_ref[pl.ds(base, un), :]
        e2 = e2_ref[pl.ds(base, un), :]
        g = g_ref[pl.ds(base, un), :]
        for u0 in range(0, un, EXPAND_GROUP):
            us = range(u0, u0 + EXPAND_GROUP)
            lefts = [jnp.where(rid == e1[u:u + 1, :], g[u:u + 1, :], 0.0).astype(BF16) for u in us]
            rights = [jnp.where(rid == e2[u:u + 1, :], 1.0, 0.0).astype(BF16) for u in us]
            for u, l, r in zip(us, lefts, rights):
                stage_ref[u * nk:(u + 1) * nk, :] = _dot_nt(l, r)
        by_key = jnp.swapaxes(stage_ref[...].reshape(un, nk, nk), 0, 1)
        for a in range(nk):
            w_ref[pl.ds(base, un), a * nk:(a + 1) * nk] = by_key[a].astype(w_ref.dtype)
        return carry

    lax.fori_loop(0, e1_ref.shape[0] // un, body, 0)


def _peer_expand(e1, e2, g, tm=128):
    s, nsel = e1.shape
    nk = PEER_N_KEYS
    tm = min(tm, s)
    ib = pl.BlockSpec((tm, nsel), lambda i: (i, 0))
    return pl.pallas_call(
        _peer_expand_kernel,
        grid=(s // tm,),
        in_specs=[ib, ib, ib],
        out_specs=pl.BlockSpec((tm, nk * nk), lambda i: (i, 0)),
        out_shape=jax.ShapeDtypeStruct((s, nk * nk), BF16),
        scratch_shapes=[pltpu.VMEM((EXPAND_UNROLL * nk, nk), F32)],
        compiler_params=_params(("parallel",)),
        name="peer_expand",
    )(e1, e2, g)


FP8 = jnp.float8_e4m3fn
FP8_MAX = 448.0
SCALE_ROWS = 8


def _rmsnorm_fp8_kernel(x_ref, g_ref, o_ref, o8_ref, s_ref):
    x = x_ref[...]
    ms = jnp.mean(x * x, axis=-1, keepdims=True)
    y = x * lax.rsqrt(ms + NORM_EPS) * g_ref[...]
    o_ref[...] = y.astype(o_ref.dtype)
    scale = jnp.maximum(jnp.max(jnp.abs(y), axis=-1, keepdims=True), 1e-30) * (1.0 / FP8_MAX)
    o8_ref[...] = (y / scale).astype(o8_ref.dtype)
    s_ref[...] = jnp.broadcast_to(scale, s_ref.shape)


def _rmsnorm_fp8(x, g, tm=256):
    s, d = x.shape
    blk = pl.BlockSpec((tm, d), lambda i: (i, 0))
    return pl.pallas_call(
        _rmsnorm_fp8_kernel,
        grid=(s // tm,),
        in_specs=[blk, pl.BlockSpec((1, d), lambda i: (0, 0))],
        out_specs=[blk, blk, pl.BlockSpec((tm, LANES), lambda i: (i, 0))],
        out_shape=[jax.ShapeDtypeStruct((s, d), BF16), jax.ShapeDtypeStruct((s, d), FP8),
                   jax.ShapeDtypeStruct((s, LANES), F32)],
        compiler_params=_params(("parallel",)),
        name="rmsnorm_fp8",
    )(x, g.reshape(1, d))


def _quant_rows_kernel(u_ref, u8_ref, s_ref):
    u = u_ref[...]
    scale = jnp.maximum(jnp.max(jnp.abs(u), axis=-1, keepdims=True), 1e-30) * (1.0 / FP8_MAX)
    u8_ref[...] = (u / scale).astype(u8_ref.dtype)
    s_ref[...] = jnp.broadcast_to(scale, (u.shape[0], LANES)).T[:SCALE_ROWS, :]


def _quant_rows(u, te=512):
    ne, d = u.shape
    return pl.pallas_call(
        _quant_rows_kernel,
        grid=(ne // te,),
        in_specs=[pl.BlockSpec((te, d), lambda e: (e, 0))],
        out_specs=[pl.BlockSpec((te, d), lambda e: (e, 0)),
                   pl.BlockSpec((SCALE_ROWS, te), lambda e: (0, e))],
        out_shape=[jax.ShapeDtypeStruct((ne, d), FP8), jax.ShapeDtypeStruct((SCALE_ROWS, ne), F32)],
        compiler_params=_params(("parallel",)),
        name="peer_quant_u",
    )(u)


def _peer_dense_kernel(h_ref, hs_ref, u_ref, us_ref, v_ref, vs_ref, w_ref, o_ref):
    @pl.when(pl.program_id(1) == 0)
    def _():
        o_ref[...] = jnp.zeros_like(o_ref)

    act = _dot_nt(h_ref[...], u_ref[...]) * hs_ref[:, 0:1] * us_ref[0:1, :]
    gelu = 0.5 * act * (1.0 + lax.erf(act * (2.0 ** -0.5)))
    mix = w_ref[...].astype(F32) * gelu * vs_ref[0:1, :]
    scale = jnp.maximum(jnp.max(jnp.abs(mix), axis=-1, keepdims=True), 1e-30) * (1.0 / FP8_MAX)
    o_ref[...] += _dot((mix / scale).astype(FP8), v_ref[...]) * scale


def _peer_dense(h8, hs, u8, us, v8, vs, wdense, tm=512, te=1024):
    s, d = h8.shape
    ne = u8.shape[0]
    tm = min(tm, s)
    return pl.pallas_call(
        _peer_dense_kernel,
        grid=(s // tm, ne // te),
        in_specs=[pl.BlockSpec((tm, d), lambda i, e: (i, 0)),
                  pl.BlockSpec((tm, LANES), lambda i, e: (i, 0)),
                  pl.BlockSpec((te, d), lambda i, e: (e, 0)),
                  pl.BlockSpec((SCALE_ROWS, te), lambda i, e: (0, e)),
                  pl.BlockSpec((te, d), lambda i, e: (e, 0)),
                  pl.BlockSpec((SCALE_ROWS, te), lambda i, e: (0, e)),
                  pl.BlockSpec((tm, te), lambda i, e: (i, e))],
        out_specs=pl.BlockSpec((tm, d), lambda i, e: (i, 0)),
        out_shape=jax.ShapeDtypeStruct((s, d), F32),
        compiler_params=_params(("parallel", "arbitrary")),
        name="peer_dense",
    )(h8, hs, u8, us, v8, vs, wdense)


def _add_rmsnorm_kernel(x_ref, y_ref, g_ref, o_ref):
    x = x_ref[...] + y_ref[...]
    ms = jnp.mean(x * x, axis=-1, keepdims=True)
    o_ref[...] = x * lax.rsqrt(ms + NORM_EPS) * g_ref[...]


def _add_rmsnorm(x, y, g, tm=256):
    s, d = x.shape
    blk = pl.BlockSpec((tm, d), lambda i: (i, 0))
    return pl.pallas_call(
        _add_rmsnorm_kernel,
        grid=(s // tm,),
        in_specs=[blk, blk, pl.BlockSpec((1, d), lambda i: (0, 0))],
        out_specs=blk,
        out_shape=jax.ShapeDtypeStruct((s, d), F32),
        compiler_params=_params(("parallel",)),
        name="residual_final_norm",
    )(x, y, g.reshape(1, d))


def _rwkv_branch(proj_r, w0, w_up, a0, a_up, g_up, k_k, k_a, r_k, lnx_g, lnx_b):
    g_up_pad = jnp.pad(g_up, ((0, GATE_LORA_PAD - GATE_LORA), (0, 0))).astype(BF16)
    wpre, a, g = _rwkv_lora(proj_r, w0, w_up.astype(BF16), a0, a_up.astype(BF16), g_up_pad)
    rt, bonus, x, y, opre, qb2, bbt, gst, pcm = _rwkv_chunk(proj_r, wpre, a, k_k, k_a, r_k.reshape(-1))
    return _rwkv_scan(x, y, rt, opre, qb2, bbt, gst, pcm, bonus, g, lnx_g, lnx_b)


def _rope_tables(s):
    half = MOBA_HEAD_DIM // 2
    inv_freq = ROPE_THETA ** (-jnp.arange(half, dtype=F32) / half)
    ang = jnp.arange(s, dtype=jnp.int32).astype(F32)[:, None] * inv_freq[None, :]
    cos = jnp.cos(ang)
    sin = jnp.sin(ang)
    return jnp.concatenate([cos, cos], axis=-1), jnp.concatenate([-sin, sin], axis=-1)


def _moba_branch(proj_m):
    s = proj_m.shape[0]
    cos, sin = _rope_tables(s)
    q, qs, k, vt, kmean = _moba_prep(proj_m, cos, sin)
    return _moba_attn(q, qs, k, vt, kmean.reshape(kmean.shape[0], kmean.shape[2]))


def _peer_layer(x1, norm2_g, w_q, sub_keys, expert_u, expert_v, final_g):
    s, d = x1.shape
    h2, h8, hs = _rmsnorm_fp8(x1, norm2_g)
    q = _matmul(h2, w_q.astype(BF16), F32, tm=min(1024, s), tn=512, name="peer_query")
    keys = sub_keys.reshape(PEER_HEADS * 2 * PEER_N_KEYS, PEER_HALF).astype(BF16)
    e1, e2, g = _peer_route(q, keys)
    wdense = _peer_expand(e1, e2, g)
    u8, us = _quant_rows(expert_u)
    v8, vs = _quant_rows(expert_v)
    peer = _peer_dense(h8, hs, u8, us, v8, vs, wdense)
    return _add_rmsnorm(x1, peer, final_g)


def kernel(x, norm1_g, w_in, rwkv_mu, rwkv_w0, rwkv_w_up, rwkv_a0, rwkv_a_up, rwkv_g_up, rwkv_k_k, rwkv_k_a, rwkv_r_k, rwkv_lnx_g, rwkv_lnx_b, w_branch_rwkv, w_branch_moba, w_out, norm2_g, peer_w_q, peer_sub_keys, peer_u, peer_v, final_g):
    b, s, d = x.shape
    depth = w_in.shape[0]
    assert b == 1 and depth == 1
    x2d = x.reshape(s, d)
    l = 0
    rw = rwkv_w0.shape[-1]
    shift_w = 3 * rw + DECAY_LORA + AAA_LORA + GATE_LORA
    tn = 512
    shift_pad = -(-shift_w // tn) * tn
    mw = w_branch_moba.shape[1]
    tm = min(1024, s)

    h = _rmsnorm(x2d, norm1_g[l], BF16)
    wt = jnp.transpose(w_in[l])
    mu = jnp.pad(rwkv_mu[l], (0, shift_pad - shift_w)).reshape(1, shift_pad)
    proj_r = _matmul_shift(h, wt, mu, tm=tm, tn=tn)
    proj_m = _matmul_nt(h, wt, F32, tm=tm, tn=tn, name="in_proj_moba", row0=shift_w, n=3 * mw)
    gates = _matmul_nt(h, wt, F32, tm=tm, tn=tn, name="in_proj_gates", row0=shift_w + 3 * mw, n=2 * d)

    y_a = _rwkv_branch(proj_r, rwkv_w0[l], rwkv_w_up[l], rwkv_a0[l], rwkv_a_up[l], rwkv_g_up[l],
                       rwkv_k_k[l], rwkv_k_a[l], rwkv_r_k[l], rwkv_lnx_g[l], rwkv_lnx_b[l])
    y_b = _moba_branch(proj_m)
    mixed = _gated_pair(y_a, w_branch_rwkv[l].astype(BF16), y_b, w_branch_moba[l].astype(BF16),
                        gates, tm=tm, tn=512)
    x1 = _matmul_residual(mixed, w_out[l].astype(BF16), x2d, tm=tm, tn=512)
    out = _peer_layer(x1, norm2_g[l], peer_w_q[l], peer_sub_keys[l], peer_u[l], peer_v[l], final_g)
    return out.reshape(b, s, d)
```

```python
import functools

import jax
import jax.numpy as jnp
from jax import lax
from jax.experimental import pallas as pl
from jax.experimental.pallas import tpu as pltpu

F32 = jnp.float32
BF16 = jnp.bfloat16

NORM_EPS = 1e-6
LANES = 128
F32_SUBLANES = 8
RWKV_HEAD_DIM = 64
DECAY_LORA = 128
AAA_LORA = 128
GATE_LORA = 480
GATE_LORA_PAD = 512
LN_X_EPS = 64e-5
CHUNK = 128
MOBA_HEAD_DIM = 128
MOBA_BLOCK = 256
MOBA_TOPK = 3
ROPE_THETA = 10000.0
PEER_HEADS = 8
PEER_N_KEYS = 128
PEER_HALF = 128
PEER_TOPK = 16

VMEM_LIMIT = 56 * 1024 * 1024

_NT = (((1,), (1,)), ((), ()))


def _params(sem, vmem=VMEM_LIMIT):
    return pltpu.CompilerParams(dimension_semantics=sem, vmem_limit_bytes=vmem)


def _dot(a, b):
    return jnp.dot(a, b, preferred_element_type=F32)


def _dot_nt(a, b):
    return lax.dot_general(a, b, _NT, preferred_element_type=F32)


def _split3(x):
    hi = x.astype(BF16)
    r1 = x - hi.astype(F32)
    mid = r1.astype(BF16)
    lo = (r1 - mid.astype(F32)).astype(BF16)
    return hi, mid, lo


def _dot_r3(x, r_bf16):
    hi, mid, lo = _split3(x)
    return _dot(hi, r_bf16) + _dot(mid, r_bf16) + _dot(lo, r_bf16)


def _rmsnorm_kernel(x_ref, g_ref, o_ref):
    x = x_ref[...]
    ms = jnp.mean(x * x, axis=-1, keepdims=True)
    o_ref[...] = (x * lax.rsqrt(ms + NORM_EPS) * g_ref[...]).astype(o_ref.dtype)


def _rmsnorm(x, g, out_dtype, tm=256):
    s, d = x.shape
    return pl.pallas_call(
        _rmsnorm_kernel,
        grid=(s // tm,),
        in_specs=[pl.BlockSpec((tm, d), lambda i: (i, 0)),
                  pl.BlockSpec((1, d), lambda i: (0, 0))],
        out_specs=pl.BlockSpec((tm, d), lambda i: (i, 0)),
        out_shape=jax.ShapeDtypeStruct((s, d), out_dtype),
        compiler_params=_params(("parallel",)),
        name="rmsnorm",
    )(x, g.reshape(1, d))


def _mm_kernel(a_ref, b_ref, o_ref):
    o_ref[...] = _dot(a_ref[...], b_ref[...]).astype(o_ref.dtype)


def _matmul(a, b, out_dtype, tm, tn, name, col0=0, n=None):
    m, k = a.shape
    n = b.shape[1] if n is None else n
    off = col0 // tn
    return pl.pallas_call(
        _mm_kernel,
        grid=(m // tm, n // tn),
        in_specs=[pl.BlockSpec((tm, k), lambda i, j: (i, 0)),
                  pl.BlockSpec((k, tn), lambda i, j: (0, j + off))],
        out_specs=pl.BlockSpec((tm, tn), lambda i, j: (i, j)),
        out_shape=jax.ShapeDtypeStruct((m, n), out_dtype),
        compiler_params=_params(("parallel", "parallel")),
        name=name,
    )(a, b)


BF16_SUBLANES = 16


def _mm_nt_kernel(a_ref, bt_ref, o_ref, b16_ref):
    @pl.when(pl.program_id(1) == 0)
    def _():
        b16_ref[...] = bt_ref[...].astype(b16_ref.dtype)

    o_ref[...] = _dot_nt(a_ref[...], b16_ref[...]).astype(o_ref.dtype)


def _matmul_nt(a, bt, out_dtype, tm, tn, name, row0, n):
    m, k = a.shape
    return pl.pallas_call(
        _mm_nt_kernel,
        grid=(n // tn, m // tm),
        in_specs=[pl.BlockSpec((tm, k), lambda j, i: (i, 0)),
                  pl.BlockSpec((pl.Element(tn), pl.Element(k)),
                               lambda j, i: (pl.multiple_of(row0 + j * tn, F32_SUBLANES), 0))],
        out_specs=pl.BlockSpec((tm, tn), lambda j, i: (i, j)),
        out_shape=jax.ShapeDtypeStruct((m, n), out_dtype),
        scratch_shapes=[pltpu.VMEM((tn, k), BF16)],
        compiler_params=_params(("parallel", "arbitrary")),
        name=name,
    )(a, bt)


def _mm_shift_kernel(a_ref, ap_ref, bt_ref, mu_ref, o_ref, b16_ref):
    i = pl.program_id(1)

    @pl.when(i == 0)
    def _():
        b16_ref[...] = bt_ref[...].astype(b16_ref.dtype)

    b = b16_ref[...]
    z = _dot_nt(a_ref[...], b)
    zp = _dot_nt(ap_ref[...], b)
    prev = jnp.where(i > 0, zp[BF16_SUBLANES - 1:BF16_SUBLANES, :], 0.0)
    row = lax.broadcasted_iota(jnp.int32, z.shape, 0)
    z_prev = jnp.where(row == 0, prev, pltpu.roll(z, 1, axis=0))
    o_ref[...] = z + (z_prev - z) * mu_ref[...]


def _matmul_shift(a, bt, mu, tm, tn):
    m, k = a.shape
    n = mu.shape[1]
    per = tm // BF16_SUBLANES
    return pl.pallas_call(
        _mm_shift_kernel,
        grid=(n // tn, m // tm),
        in_specs=[pl.BlockSpec((tm, k), lambda j, i: (i, 0)),
                  pl.BlockSpec((BF16_SUBLANES, k), lambda j, i: (jnp.maximum(i * per - 1, 0), 0)),
                  pl.BlockSpec((tn, k), lambda j, i: (j, 0)),
                  pl.BlockSpec((1, tn), lambda j, i: (0, j))],
        out_specs=pl.BlockSpec((tm, tn), lambda j, i: (i, j)),
        out_shape=jax.ShapeDtypeStruct((m, n), F32),
        scratch_shapes=[pltpu.VMEM((tn, k), BF16)],
        compiler_params=_params(("parallel", "arbitrary")),
        name="in_proj_rwkv",
    )(a, a, bt, mu)


def _gated_pair_kernel(ya_ref, wa_ref, yb_ref, wb_ref, ga_ref, gb_ref, o_ref):
    pa = _dot(ya_ref[...], wa_ref[...])
    pb = _dot(yb_ref[...], wb_ref[...])
    mixed = jax.nn.sigmoid(ga_ref[...]) * pa + jax.nn.sigmoid(gb_ref[...]) * pb
    o_ref[...] = mixed.astype(o_ref.dtype)


def _gated_pair(ya, wa, yb, wb, gates, tm, tn):
    m, k = ya.shape
    n = wa.shape[1]
    nb = n // tn
    return pl.pallas_call(
        _gated_pair_kernel,
        grid=(m // tm, nb),
        in_specs=[pl.BlockSpec((tm, k), lambda i, j: (i, 0)),
                  pl.BlockSpec((k, tn), lambda i, j: (0, j)),
                  pl.BlockSpec((tm, k), lambda i, j: (i, 0)),
                  pl.BlockSpec((k, tn), lambda i, j: (0, j)),
                  pl.BlockSpec((tm, tn), lambda i, j: (i, j)),
                  pl.BlockSpec((tm, tn), lambda i, j: (i, j + nb))],
        out_specs=pl.BlockSpec((tm, tn), lambda i, j: (i, j)),
        out_shape=jax.ShapeDtypeStruct((m, n), BF16),
        compiler_params=_params(("parallel", "parallel")),
        name="branch_merge",
    )(ya, wa, yb, wb, gates, gates)


def _mm_res_kernel(a_ref, b_ref, r_ref, o_ref):
    o_ref[...] = r_ref[...] + _dot(a_ref[...], b_ref[...])


def _matmul_residual(a, b, res, tm, tn):
    m, k = a.shape
    n = b.shape[1]
    return pl.pallas_call(
        _mm_res_kernel,
        grid=(m // tm, n // tn),
        in_specs=[pl.BlockSpec((tm, k), lambda i, j: (i, 0)),
                  pl.BlockSpec((k, tn), lambda i, j: (0, j)),
                  pl.BlockSpec((tm, tn), lambda i, j: (i, j))],
        out_specs=pl.BlockSpec((tm, tn), lambda i, j: (i, j)),
        out_shape=jax.ShapeDtypeStruct((m, n), F32),
        compiler_params=_params(("parallel", "parallel")),
        name="out_proj",
    )(a, b, res)


def _rwkv_lora_kernel(wl_ref, al_ref, gl0_ref, gl1_ref, w0_ref, wup_ref, a0_ref, aup_ref,
                      gup_ref, wpre_ref, a_ref, g_ref):
    half = GATE_LORA_PAD // 2
    wpre_ref[...] = w0_ref[...] + _dot(jnp.tanh(wl_ref[...]).astype(BF16), wup_ref[...])
    a_ref[...] = jax.nn.sigmoid(a0_ref[...] + _dot(al_ref[...].astype(BF16), aup_ref[...]))
    g_ref[...] = (_dot(jax.nn.sigmoid(gl0_ref[...]).astype(BF16), gup_ref[:half, :])
                  + _dot(jax.nn.sigmoid(gl1_ref[...]).astype(BF16), gup_ref[half:, :]))


def _rwkv_lora(proj_r, w0, w_up, a0, a_up, g_up_pad, tm=256):
    s = proj_r.shape[0]
    w = w0.shape[-1]
    half = GATE_LORA_PAD // 2
    c_wl = 3 * w // DECAY_LORA
    c_al = (3 * w + DECAY_LORA) // AAA_LORA
    c_gl = (3 * w + DECAY_LORA + AAA_LORA) // half
    row = lambda i: (i, 0)
    const = lambda i: (0, 0)
    out = jax.ShapeDtypeStruct((s, w), F32)
    return pl.pallas_call(
        _rwkv_lora_kernel,
        grid=(s // tm,),
        in_specs=[pl.BlockSpec((tm, DECAY_LORA), lambda i: (i, c_wl)),
                  pl.BlockSpec((tm, AAA_LORA), lambda i: (i, c_al)),
                  pl.BlockSpec((tm, half), lambda i: (i, c_gl)),
                  pl.BlockSpec((tm, half), lambda i: (i, c_gl + 1)),
                  pl.BlockSpec((1, w), const),
                  pl.BlockSpec((DECAY_LORA, w), const),
                  pl.BlockSpec((1, w), const),
                  pl.BlockSpec((AAA_LORA, w), const),
                  pl.BlockSpec((GATE_LORA_PAD, w), const)],
        out_specs=[pl.BlockSpec((tm, w), row)] * 3,
        out_shape=[out, out, out],
        compiler_params=_params(("parallel",)),
        name="rwkv_lora",
    )(proj_r, proj_r, proj_r, proj_r, w0.reshape(1, w), w_up, a0.reshape(1, w), a_up, g_up_pad)


def _head_pair_ones(width=LANES):
    r = lax.broadcasted_iota(jnp.int32, (width, width), 0) // RWKV_HEAD_DIM
    c = lax.broadcasted_iota(jnp.int32, (width, width), 1) // RWKV_HEAD_DIM
    return r == c


def _rwkv_prep(r_ref, k_ref, v_ref, wpre_ref, a_ref, kk_ref, ka_ref, rk_ref,
               rt_ref, at_ref, kh_ref, bh_ref, kb_ref, bb_ref, pc_ref, bonus_ref):
    tm = r_ref.shape[0]
    r = r_ref[...]
    k = k_ref[...]
    v = v_ref[...]
    a = a_ref[...]
    same_head = jnp.where(_head_pair_ones(r.shape[1]), 1.0, 0.0).astype(BF16)

    x = -wpre_ref[...]
    softplus = jnp.maximum(x, 0.0) + jnp.log1p(jnp.exp(-jnp.abs(x)))
    lw = -jnp.exp(-softplus - 0.5)

    kk = k * kk_ref[...]
    ss = _dot_r3(kk * kk, same_head)
    kkn = kk / jnp.maximum(jnp.sqrt(ss), 1e-12)
    k2 = k * (1.0 + (a - 1.0) * ka_ref[...])
    bonus_ref[...] = _dot_r3(r * k2 * rk_ref[...], same_head) * v

    ri = lax.broadcasted_iota(jnp.int32, (tm, tm), 0)
    ci = lax.broadcasted_iota(jnp.int32, (tm, tm), 1)
    same_chunk = (ri // CHUNK) == (ci // CHUNK)
    tri = jnp.where(same_chunk & (ci <= ri), 1.0, 0.0).astype(BF16)
    allc = jnp.where(same_chunk, 1.0, 0.0).astype(BF16)
    lw_parts = _split3(lw)
    cum = sum(_dot(tri, part) for part in lw_parts)
    tot = sum(_dot(allc, part) for part in lw_parts)

    p_in = jnp.exp(cum)
    p_inv = jnp.exp(-cum)
    p_tail = jnp.exp(tot - cum)
    b = kkn * a
    rt_ref[...] = r * p_in
    at_ref[...] = -kkn * jnp.exp(cum - lw)
    kh_ref[...] = k2 * p_inv
    bh_ref[...] = b * p_inv
    kb_ref[...] = k2 * p_tail
    bb_ref[...] = b * p_tail
    pc_ref[...] = jnp.exp(tot)


def _rwkv_chunk_kernel(r_ref, k_ref, v_ref, wpre_ref, a_ref, kk_ref, ka_ref, rk_ref,
                       rt_ref, bonus_ref, x_ref, y_ref, op_ref, qb_ref, bbt_ref, g_ref, pcm_ref,
                       at_ref, kh_ref, bh_ref, kb_ref, bb_ref, pc_ref):
    _rwkv_prep(r_ref, k_ref, v_ref, wpre_ref, a_ref, kk_ref, ka_ref, rk_ref,
               rt_ref, at_ref, kh_ref, bh_ref, kb_ref, bb_ref, pc_ref, bonus_ref)
    c = CHUNK
    nheads = LANES // RWKV_HEAD_DIM
    chunks = range(rt_ref.shape[0] // c)
    pairs = range(rt_ref.shape[1] // LANES)
    rows = [slice(q * c, (q + 1) * c) for q in chunks]
    lanes = [slice(p * LANES, (p + 1) * LANES) for p in pairs]
    tiles = [(q, p) for q in chunks for p in pairs]
    probs = [(n, h) for n in range(len(tiles)) for h in range(nheads)]
    lane = lax.broadcasted_iota(jnp.int32, (c, LANES), 1)
    ri = lax.broadcasted_iota(jnp.int32, (c, c), 0)
    ci = lax.broadcasted_iota(jnp.int32, (c, c), 1)
    strict = ci < ri
    incl = ci <= ri
    eye = jnp.where(ri == ci, 1.0, 0.0)
    head_mask = [(lane // RWKV_HEAD_DIM) == h for h in range(nheads)]
    tile = lambda ref, n: ref[rows[tiles[n][0]], lanes[tiles[n][1]]]

    kh = [tile(kh_ref, n).astype(BF16) for n in range(len(tiles))]
    bh = [tile(bh_ref, n).astype(BF16) for n in range(len(tiles))]
    a_h = [jnp.where(head_mask[h], tile(at_ref, n), 0.0).astype(BF16) for n, h in probs]
    r_h = [jnp.where(head_mask[h], tile(rt_ref, n), 0.0).astype(BF16) for n, h in probs]
    v_h = [jnp.where(head_mask[h], tile(v_ref, n), 0.0).astype(BF16) for n, h in probs]
    n_ab = [jnp.where(strict, _dot_nt(a, bh[n]), 0.0) for a, (n, _) in zip(a_h, probs)]
    n_ak = [jnp.where(strict, _dot_nt(a, kh[n]), 0.0).astype(BF16) for a, (n, _) in zip(a_h, probs)]
    q_k = [jnp.where(incl, _dot_nt(r, kh[n]), 0.0).astype(BF16) for r, (n, _) in zip(r_h, probs)]
    q_b = [jnp.where(incl, _dot_nt(r, bh[n]), 0.0) for r, (n, _) in zip(r_h, probs)]
    for qb, (n, h) in zip(q_b, probs):
        q, p = tiles[n]
        col = (p * nheads + h) * c
        qb_ref[rows[q], col:col + c] = qb.astype(qb_ref.dtype)

    t = [eye + n for n in n_ab]
    pw = n_ab
    for _ in range((c - 1).bit_length() - 1):
        pb = [x.astype(BF16) for x in pw]
        pw = [_dot(x, x) for x in pb]
        t = [ti + _dot(ti.astype(BF16), pi.astype(BF16)) for ti, pi in zip(t, pw)]
    tb = [ti.astype(BF16) for ti in t]
    xs = [_dot(ti, a) for ti, a in zip(tb, a_h)]
    nv = [_dot(n, v).astype(BF16) for n, v in zip(n_ak, v_h)]
    ys = [_dot(ti, z) for ti, z in zip(tb, nv)]
    os = [_dot(qk, v) for qk, v in zip(q_k, v_h)]
    same_head = _head_pair_ones()
    for n, (q, p) in enumerate(tiles):
        mine = [m for m, (nn, _) in enumerate(probs) if nn == n]
        x_ref[rows[q], lanes[p]] = sum(xs[m] for m in mine).astype(x_ref.dtype)
        y_ref[rows[q], lanes[p]] = sum(ys[m] for m in mine)
        op_ref[rows[q], lanes[p]] = sum(os[m] for m in mine)
        kbt = tile(kb_ref, n).T.astype(BF16)
        g_ref[rows[q], lanes[p]] = jnp.where(same_head, _dot(kbt, tile(v_ref, n).astype(BF16)), 0.0)
        bbt_ref[lanes[p], rows[q]] = tile(bb_ref, n).T.astype(bbt_ref.dtype)
        pcm_ref[rows[q], lanes[p]] = tile(pc_ref, n).T


def _rwkv_chunk(proj_r, wpre, a, k_k, k_a, r_k, tc=2 * CHUNK, tw=2 * LANES):
    s, w = wpre.shape
    ncol = w // tw
    blk = lambda off: pl.BlockSpec((tc, tw), lambda i, j, off=off: (i, j + off))
    vec = pl.BlockSpec((1, tw), lambda i, j: (0, j))
    f32o = jax.ShapeDtypeStruct((s, w), F32)
    return pl.pallas_call(
        _rwkv_chunk_kernel,
        grid=(s // tc, ncol),
        in_specs=[blk(0), blk(ncol), blk(2 * ncol), blk(0), blk(0), vec, vec, vec],
        out_specs=[blk(0), blk(0), blk(0), blk(0), blk(0),
                   pl.BlockSpec((tc, 2 * tw), lambda i, j: (i, j)),
                   pl.BlockSpec((tw, tc), lambda i, j: (j, i)),
                   blk(0), blk(0)],
        out_shape=[f32o, f32o, jax.ShapeDtypeStruct((s, w), BF16), f32o, f32o,
                   jax.ShapeDtypeStruct((s, 2 * w), BF16),
                   jax.ShapeDtypeStruct((w, s), BF16),
                   f32o, f32o],
        scratch_shapes=[pltpu.VMEM((tc, tw), F32)] * 6,
        compiler_params=_params(("parallel", "parallel")),
        name="rwkv_chunk",
    )(proj_r, proj_r, proj_r, wpre, a, k_k.reshape(1, w), k_a.reshape(1, w), r_k.reshape(1, w))


def _rwkv_scan_kernel(x_ref, y_ref, rt_ref, op_ref, qb_ref, bbt_ref, g_ref, pcm_ref,
                      bonus_ref, gate_ref, lng_ref, lnb_ref, ya_ref, h_ref):
    c = CHUNK
    pairs = range(x_ref.shape[1] // LANES)
    lanes = [slice(p * LANES, (p + 1) * LANES) for p in pairs]

    @pl.when(pl.program_id(1) == 0)
    def _():
        h_ref[...] = jnp.zeros_like(h_ref)

    lane = lax.broadcasted_iota(jnp.int32, (c, LANES), 1)
    first = lane < RWKV_HEAD_DIM
    same_head = _head_pair_ones()
    ones_head = jnp.where(same_head, 1.0, 0.0).astype(BF16)
    inv_n = 1.0 / RWKV_HEAD_DIM
    for q in range(x_ref.shape[0] // c):
        rows = slice(q * c, (q + 1) * c)
        hs = [h_ref[p] for p in pairs]
        hb = [h.astype(BF16) for h in hs]
        us = [_dot(x_ref[rows, lanes[p]], hb[p]) + y_ref[rows, lanes[p]] for p in pairs]
        u2 = [jnp.concatenate([jnp.where(first, u, 0.0), jnp.where(first, 0.0, u)], axis=0).astype(BF16)
              for u in us]
        os = [_dot(rt_ref[rows, lanes[p]].astype(BF16), hb[p]) + op_ref[rows, lanes[p]]
              + _dot(qb_ref[rows, p * 2 * c:(p + 1) * 2 * c], u2[p]) for p in pairs]
        upd = [_dot(bbt_ref[lanes[p], rows], us[p].astype(BF16)) for p in pairs]
        for p in pairs:
            h_ref[p] = (pcm_ref[rows, lanes[p]] * hs[p] + g_ref[rows, lanes[p]]
                        + jnp.where(same_head, upd[p], 0.0))
        mus = [_dot_r3(o, ones_head) * inv_n for o in os]
        ds = [o - mu for o, mu in zip(os, mus)]
        vs = [_dot_r3(d * d, ones_head) * inv_n for d in ds]
        for p in pairs:
            y = ds[p] * lax.rsqrt(vs[p] + LN_X_EPS) * lng_ref[:, lanes[p]] + lnb_ref[:, lanes[p]]
            ya_ref[rows, lanes[p]] = ((y + bonus_ref[rows, lanes[p]])
                                      * gate_ref[rows, lanes[p]]).astype(ya_ref.dtype)


def _rwkv_scan(x, y, rt, opre, qb2, bbt, g, pcm, bonus, gate, lnx_g, lnx_b, ts=512, tw=4 * LANES):
    s, w = y.shape
    ts = min(ts, s)
    blk = pl.BlockSpec((ts, tw), lambda j, i: (i, j))
    vec = pl.BlockSpec((1, tw), lambda j, i: (0, j))
    return pl.pallas_call(
        _rwkv_scan_kernel,
        grid=(w // tw, s // ts),
        in_specs=[blk, blk, blk, blk,
                  pl.BlockSpec((ts, 2 * tw), lambda j, i: (i, j)),
                  pl.BlockSpec((tw, ts), lambda j, i: (j, i)),
                  blk, blk, blk, blk, vec, vec],
        out_specs=blk,
        out_shape=jax.ShapeDtypeStruct((s, w), BF16),
        scratch_shapes=[pltpu.VMEM((tw // LANES, LANES, LANES), F32)],
        compiler_params=_params(("parallel", "arbitrary")),
        name="rwkv_scan",
    )(x, y, rt, opre, qb2, bbt, g, pcm, bonus, gate, lnx_g.reshape(1, w), lnx_b.reshape(1, w))


V_ONES_ROWS = 16


LOG2_E = 1.4426950408889634
MOBA_LOG2_SCALE = (MOBA_HEAD_DIM ** -0.5) * LOG2_E


def _moba_prep_kernel(q_ref, k_ref, v_ref, cos_ref, sin_ref, qo_ref, qs_ref, ko_ref, vt_ref, km_ref):
    cos = cos_ref[...]
    sin = sin_ref[...]
    hd = MOBA_HEAD_DIM
    half = hd // 2
    tb = q_ref.shape[0]
    for h in range(q_ref.shape[1] // hd):
        cols = slice(h * hd, (h + 1) * hd)
        q = q_ref[:, cols]
        k = k_ref[:, cols]
        qr = q * cos + pltpu.roll(q, half, axis=1) * sin
        kr = k * cos + pltpu.roll(k, half, axis=1) * sin
        qo_ref[:, cols] = qr.astype(qo_ref.dtype)
        qs_ref[:, cols] = (qr * MOBA_LOG2_SCALE).astype(qs_ref.dtype)
        ko_ref[:, cols] = kr.astype(ko_ref.dtype)
        km_ref[0, :, cols] = jnp.mean(kr, axis=0, keepdims=True)
        vt_ref[h, 0, :hd, :] = v_ref[:, cols].T.astype(vt_ref.dtype)
        vt_ref[h, 0, hd:, :] = jnp.ones((V_ONES_ROWS, tb), vt_ref.dtype)


def _moba_prep(proj_m, cos, sin):
    s = proj_m.shape[0]
    w = proj_m.shape[1] // 3
    tb = MOBA_BLOCK
    nb = s // tb
    nh = w // MOBA_HEAD_DIM
    vrows = MOBA_HEAD_DIM + V_ONES_ROWS
    blk = lambda off: pl.BlockSpec((tb, w), lambda i, off=off: (i, off))
    tab = pl.BlockSpec((tb, MOBA_HEAD_DIM), lambda i: (i, 0))
    bo = jax.ShapeDtypeStruct((s, w), BF16)
    return pl.pallas_call(
        _moba_prep_kernel,
        grid=(nb,),
        in_specs=[blk(0), blk(1), blk(2), tab, tab],
        out_specs=[blk(0), blk(0), blk(0),
                   pl.BlockSpec((nh, 1, vrows, tb), lambda i: (0, i, 0, 0)),
                   pl.BlockSpec((1, 1, w), lambda i: (i, 0, 0))],
        out_shape=[bo, bo, bo, jax.ShapeDtypeStruct((nh, nb, vrows, tb), BF16),
                   jax.ShapeDtypeStruct((nb, 1, w), F32)],
        compiler_params=_params(("parallel",)),
        name="moba_prep",
    )(proj_m, proj_m, proj_m, cos, sin)


MOBA_TRIP_BLOCKS = 8
MOBA_HEADS_PER_STEP = 4


def _moba_attn_kernel(q_ref, qs_ref, k_ref, vt_ref, km_ref, o_ref, bias_ref, s_ref):
    i = pl.program_id(1)
    tb = MOBA_BLOCK
    nb = km_ref.shape[0]
    hd = MOBA_HEAD_DIM
    heads = range(q_ref.shape[1] // hd)
    cols = [slice(h * hd, (h + 1) * hd) for h in heads]
    q = [qs_ref[:, c] for c in cols]

    own = pl.multiple_of(i * tb, tb)
    s_own = [_dot_nt(k_ref[pl.ds(own, tb), cols[h]], q[h]) for h in heads]

    gates = [_dot_nt(km_ref[:, cols[h]].astype(BF16), q_ref[:, cols[h]]) for h in heads]
    rid = lax.broadcasted_iota(jnp.int32, (nb, tb), 0).astype(F32)
    for h in heads:
        gate = jnp.where(rid < i.astype(F32), gates[h], -jnp.inf)
        bias = jnp.full(gate.shape, -jnp.inf, F32)
        for _ in range(MOBA_TOPK):
            m = jnp.max(gate, axis=0, keepdims=True)
            first = jnp.min(jnp.where(gate == m, rid, float(nb)), axis=0, keepdims=True)
            pick = (rid == first) & (m > -jnp.inf)
            bias = jnp.where(pick, 0.0, bias)
            gate = jnp.where(pick, -jnp.inf, gate)
        bias_ref[h] = bias

    ki = lax.broadcasted_iota(jnp.int32, (tb, tb), 0)
    qi = lax.broadcasted_iota(jnp.int32, (tb, tb), 1)
    s_own = [jnp.where(ki <= qi, s, -jnp.inf).astype(BF16) for s in s_own]
    m_own = tuple(jnp.max(s, axis=0, keepdims=True) for s in s_own)

    def blocks(t):
        return [jnp.minimum(t * MOBA_TRIP_BLOCKS + g, nb - 1) for g in range(MOBA_TRIP_BLOCKS)]

    trips = (i + MOBA_TRIP_BLOCKS - 1) // MOBA_TRIP_BLOCKS

    def score_pass(t, ms):
        out = []
        for h in heads:
            ss = [_dot_nt(k_ref[pl.ds(pl.multiple_of(j * tb, tb), tb), cols[h]], q[h]).astype(BF16)
                  + bias_ref[h, pl.ds(j, 1), :].astype(BF16) for j in blocks(t)]
            m = ms[h]
            for j, sj in zip(blocks(t), ss):
                s_ref[h, j] = sj
                m = jnp.maximum(m, jnp.max(sj, axis=0, keepdims=True))
            out.append(m)
        return tuple(out)

    ms = lax.fori_loop(0, trips, score_pass, m_own)

    def value_pass(t, accs):
        out = []
        for h in heads:
            ps = [jnp.exp2(s_ref[h, j] - ms[h]) for j in blocks(t)]
            acc = accs[h]
            for j, pj in zip(blocks(t), ps):
                acc = acc + _dot(vt_ref[h, j], pj)
            out.append(acc)
        return tuple(out)

    acc0 = tuple(_dot(vt_ref[h, i], jnp.exp2(s_own[h] - ms[h])) for h in heads)
    accs = lax.fori_loop(0, trips, value_pass, acc0)
    for h in heads:
        out = accs[h][:hd, :] / accs[h][hd:hd + 1, :]
        o_ref[:, cols[h]] = out.T.astype(o_ref.dtype)


def _moba_attn(q, qs, k, vt, kmean):
    s, w = q.shape
    tb = MOBA_BLOCK
    nb = s // tb
    nh = MOBA_HEADS_PER_STEP
    hd = nh * MOBA_HEAD_DIM
    vrows = vt.shape[2]
    return pl.pallas_call(
        _moba_attn_kernel,
        grid=(w // hd, nb),
        in_specs=[pl.BlockSpec((tb, hd), lambda h, i: (i, h)),
                  pl.BlockSpec((tb, hd), lambda h, i: (i, h)),
                  pl.BlockSpec((s, hd), lambda h, i: (0, h), pipeline_mode=pl.Buffered(1)),
                  pl.BlockSpec((nh, nb, vrows, tb), lambda h, i: (h, 0, 0, 0),
                               pipeline_mode=pl.Buffered(1)),
                  pl.BlockSpec((nb, hd), lambda h, i: (0, h))],
        out_specs=pl.BlockSpec((tb, hd), lambda h, i: (i, h)),
        out_shape=jax.ShapeDtypeStruct((s, w), BF16),
        scratch_shapes=[pltpu.VMEM((nh, nb, tb), F32), pltpu.VMEM((nh, nb, tb, tb), BF16)],
        compiler_params=_params(("parallel", "arbitrary")),
        name="moba_attn",
    )(q, qs, k, vt, kmean)


def _top_rows(x, n):
    rows = x.shape[0]
    rid = lax.broadcasted_iota(jnp.int32, x.shape, 0).astype(F32)
    vals, idxs = [], []
    for _ in range(n):
        m = jnp.max(x, axis=0, keepdims=True)
        first = jnp.min(jnp.where(x == m, rid, float(rows)), axis=0, keepdims=True)
        vals.append(m)
        idxs.append(first)
        x = jnp.where(rid == first, -jnp.inf, x)
    return jnp.concatenate(vals, axis=0), jnp.concatenate(idxs, axis=0)


def _peer_route_kernel(q_ref, keys_ref, e1_ref, e2_ref, g_ref):
    n = PEER_TOPK
    tm = q_ref.shape[0]
    e1s, e2s, gs = [], [], []
    for h in range(PEER_HEADS):
        tops = []
        for p in range(2):
            hp = 2 * h + p
            cols = slice(hp * PEER_HALF, (hp + 1) * PEER_HALF)
            rows = slice(hp * PEER_N_KEYS, (hp + 1) * PEER_N_KEYS)
            st = _dot_nt(keys_ref[rows, :], q_ref[:, cols].astype(BF16))
            tops.append(_top_rows(st, n))
        (s1, i1), (s2, i2) = tops
        hn = n // 2
        cand = jnp.concatenate([s1[0:1, :] + s2]
                               + [s1[a:a + 1, :] + s2[:hn, :] for a in range(1, hn)]
                               + [s1[hn:, :] + s2[0:1, :]], axis=0)
        f_s, f_pos = _top_rows(cand, n)
        mid = jnp.floor((f_pos - n) * (1.0 / hn))
        tail0 = float(n + (hn - 1) * hn)
        pa = jnp.where(f_pos < n, 0.0, jnp.where(f_pos < tail0, 1.0 + mid, f_pos - tail0 + hn))
        pb = jnp.where(f_pos < n, f_pos, jnp.where(f_pos < tail0, f_pos - n - mid * hn, 0.0))
        e1 = jnp.zeros((n, tm), F32)
        e2 = jnp.zeros((n, tm), F32)
        for a in range(n):
            e1 = jnp.where(pa == float(a), i1[a:a + 1, :], e1)
            e2 = jnp.where(pb == float(a), i2[a:a + 1, :], e2)
        ex = jnp.exp(f_s - f_s[0:1, :])
        gs.append(ex / jnp.sum(ex, axis=0, keepdims=True))
        e1s.append(e1)
        e2s.append(e2)
    e1_ref[...] = jnp.concatenate(e1s, axis=0).T
    e2_ref[...] = jnp.concatenate(e2s, axis=0).T
    g_ref[...] = jnp.concatenate(gs, axis=0).T


def _peer_route(q, keys, tm=256):
    s, w = q.shape
    nsel = PEER_HEADS * PEER_TOPK
    tm = min(tm, s)
    out = jax.ShapeDtypeStruct((s, nsel), F32)
    ob = pl.BlockSpec((tm, nsel), lambda i: (i, 0))
    return pl.pallas_call(
        _peer_route_kernel,
        grid=(s // tm,),
        in_specs=[pl.BlockSpec((tm, w), lambda i: (i, 0)),
                  pl.BlockSpec(keys.shape, lambda i: (0, 0))],
        out_specs=[ob, ob, ob],
        out_shape=[out, out, out],
        compiler_params=_params(("parallel",)),
        name="peer_route",
    )(q, keys)


EXPAND_UNROLL = 16
EXPAND_GROUP = 4


def _peer_expand_kernel(e1_ref, e2_ref, g_ref, w_ref, stage_ref):
    nk = PEER_N_KEYS
    nsel = e1_ref.shape[1]
    rid = lax.broadcasted_iota(jnp.int32, (nk, nsel), 0).astype(F32)
    un = EXPAND_UNROLL

    def body(tt, carry):
        base = pl.multiple_of(tt * un, un)
        e1 = e1_ref[pl.ds(base, un), :]
        e2 = e2_ref[pl.ds(base, un), :]
        g = g_ref[pl.ds(base, un), :]
        for u0 in range(0, un, EXPAND_GROUP):
            us = range(u0, u0 + EXPAND_GROUP)
            lefts = [jnp.where(rid == e1[u:u + 1, :], g[u:u + 1, :], 0.0).astype(BF16) for u in us]
            rights = [jnp.where(rid == e2[u:u + 1, :], 1.0, 0.0).astype(BF16) for u in us]
            for u, l, r in zip(us, lefts, rights):
                stage_ref[u * nk:(u + 1) * nk, :] = _dot_nt(l, r)
        by_key = jnp.swapaxes(stage_ref[...].reshape(un, nk, nk), 0, 1)
        for a in range(nk):
            w_ref[pl.ds(base, un), a * nk:(a + 1) * nk] = by_key[a].astype(w_ref.dtype)
        return carry

    lax.fori_loop(0, e1_ref.shape[0] // un, body, 0)


def _peer_expand(e1, e2, g, tm=128):
    s, nsel = e1.shape
    nk = PEER_N_KEYS
    tm = min(tm, s)
    ib = pl.BlockSpec((tm, nsel), lambda i: (i, 0))
    return pl.pallas_call(
        _peer_expand_kernel,
        grid=(s // tm,),
        in_specs=[ib, ib, ib],
        out_specs=pl.BlockSpec((tm, nk * nk), lambda i: (i, 0)),
        out_shape=jax.ShapeDtypeStruct((s, nk * nk), BF16),
        scratch_shapes=[pltpu.VMEM((EXPAND_UNROLL * nk, nk), F32)],
        compiler_params=_params(("parallel",)),
        name="peer_expand",
    )(e1, e2, g)


FP8 = jnp.float8_e4m3fn
FP8_MAX = 448.0
SCALE_ROWS = 8


def _rmsnorm_fp8_kernel(x_ref, g_ref, o_ref, o8_ref, s_ref):
    x = x_ref[...]
    ms = jnp.mean(x * x, axis=-1, keepdims=True)
    y = x * lax.rsqrt(ms + NORM_EPS) * g_ref[...]
    o_ref[...] = y.astype(o_ref.dtype)
    scale = jnp.maximum(jnp.max(jnp.abs(y), axis=-1, keepdims=True), 1e-30) * (1.0 / FP8_MAX)
    o8_ref[...] = (y / scale).astype(o8_ref.dtype)
    s_ref[...] = jnp.broadcast_to(scale, s_ref.shape)


def _rmsnorm_fp8(x, g, tm=256):
    s, d = x.shape
    blk = pl.BlockSpec((tm, d), lambda i: (i, 0))
    return pl.pallas_call(
        _rmsnorm_fp8_kernel,
        grid=(s // tm,),
        in_specs=[blk, pl.BlockSpec((1, d), lambda i: (0, 0))],
        out_specs=[blk, blk, pl.BlockSpec((tm, LANES), lambda i: (i, 0))],
        out_shape=[jax.ShapeDtypeStruct((s, d), BF16), jax.ShapeDtypeStruct((s, d), FP8),
                   jax.ShapeDtypeStruct((s, LANES), F32)],
        compiler_params=_params(("parallel",)),
        name="rmsnorm_fp8",
    )(x, g.reshape(1, d))


def _quant_rows_kernel(u_ref, u8_ref, s_ref):
    u = u_ref[...]
    scale = jnp.maximum(jnp.max(jnp.abs(u), axis=-1, keepdims=True), 1e-30) * (1.0 / FP8_MAX)
    u8_ref[...] = (u / scale).astype(u8_ref.dtype)
    s_ref[...] = jnp.broadcast_to(scale, (u.shape[0], LANES)).T[:SCALE_ROWS, :]


def _quant_rows(u, te=512):
    ne, d = u.shape
    return pl.pallas_call(
        _quant_rows_kernel,
        grid=(ne // te,),
        in_specs=[pl.BlockSpec((te, d), lambda e: (e, 0))],
        out_specs=[pl.BlockSpec((te, d), lambda e: (e, 0)),
                   pl.BlockSpec((SCALE_ROWS, te), lambda e: (0, e))],
        out_shape=[jax.ShapeDtypeStruct((ne, d), FP8), jax.ShapeDtypeStruct((SCALE_ROWS, ne), F32)],
        compiler_params=_params(("parallel",)),
        name="peer_quant_u",
    )(u)


def _peer_dense_kernel(h_ref, hs_ref, u_ref, us_ref, v_ref, vs_ref, w_ref, o_ref):
    @pl.when(pl.program_id(1) == 0)
    def _():
        o_ref[...] = jnp.zeros_like(o_ref)

    act = _dot_nt(h_ref[...], u_ref[...]) * hs_ref[:, 0:1] * us_ref[0:1, :]
    gelu = 0.5 * act * (1.0 + lax.erf(act * (2.0 ** -0.5)))
    mix = w_ref[...].astype(F32) * gelu * vs_ref[0:1, :]
    scale = jnp.maximum(jnp.max(jnp.abs(mix), axis=-1, keepdims=True), 1e-30) * (1.0 / FP8_MAX)
    o_ref[...] += _dot((mix / scale).astype(FP8), v_ref[...]) * scale


def _peer_dense(h8, hs, u8, us, v8, vs, wdense, tm=512, te=1024):
    s, d = h8.shape
    ne = u8.shape[0]
    tm = min(tm, s)
    return pl.pallas_call(
        _peer_dense_kernel,
        grid=(s // tm, ne // te),
        in_specs=[pl.BlockSpec((tm, d), lambda i, e: (i, 0)),
                  pl.BlockSpec((tm, LANES), lambda i, e: (i, 0)),
                  pl.BlockSpec((te, d), lambda i, e: (e, 0)),
                  pl.BlockSpec((SCALE_ROWS, te), lambda i, e: (0, e)),
                  pl.BlockSpec((te, d), lambda i, e: (e, 0)),
                  pl.BlockSpec((SCALE_ROWS, te), lambda i, e: (0, e)),
                  pl.BlockSpec((tm, te), lambda i, e: (i, e))],
        out_specs=pl.BlockSpec((tm, d), lambda i, e: (i, 0)),
        out_shape=jax.ShapeDtypeStruct((s, d), F32),
        compiler_params=_params(("parallel", "arbitrary")),
        name="peer_dense",
    )(h8, hs, u8, us, v8, vs, wdense)


def _add_rmsnorm_kernel(x_ref, y_ref, g_ref, o_ref):
    x = x_ref[...] + y_ref[...]
    ms = jnp.mean(x * x, axis=-1, keepdims=True)
    o_ref[...] = x * lax.rsqrt(ms + NORM_EPS) * g_ref[...]


def _add_rmsnorm(x, y, g, tm=256):
    s, d = x.shape
    blk = pl.BlockSpec((tm, d), lambda i: (i, 0))
    return pl.pallas_call(
        _add_rmsnorm_kernel,
        grid=(s // tm,),
        in_specs=[blk, blk, pl.BlockSpec((1, d), lambda i: (0, 0))],
        out_specs=blk,
        out_shape=jax.ShapeDtypeStruct((s, d), F32),
        compiler_params=_params(("parallel",)),
        name="residual_final_norm",
    )(x, y, g.reshape(1, d))


def _rwkv_branch(proj_r, w0, w_up, a0, a_up, g_up, k_k, k_a, r_k, lnx_g, lnx_b):
    g_up_pad = jnp.pad(g_up, ((0, GATE_LORA_PAD - GATE_LORA), (0, 0))).astype(BF16)
    wpre, a, g = _rwkv_lora(proj_r, w0, w_up.astype(BF16), a0, a_up.astype(BF16), g_up_pad)
    rt, bonus, x, y, opre, qb2, bbt, gst, pcm = _rwkv_chunk(proj_r, wpre, a, k_k, k_a, r_k.reshape(-1))
    return _rwkv_scan(x, y, rt, opre, qb2, bbt, gst, pcm, bonus, g, lnx_g, lnx_b)


def _rope_tables(s):
    half = MOBA_HEAD_DIM // 2
    inv_freq = ROPE_THETA ** (-jnp.arange(half, dtype=F32) / half)
    ang = jnp.arange(s, dtype=jnp.int32).astype(F32)[:, None] * inv_freq[None, :]
    cos = jnp.cos(ang)
    sin = jnp.sin(ang)
    return jnp.concatenate([cos, cos], axis=-1), jnp.concatenate([-sin, sin], axis=-1)


def _moba_branch(proj_m):
    s = proj_m.shape[0]
    cos, sin = _rope_tables(s)
    q, qs, k, vt, kmean = _moba_prep(proj_m, cos, sin)
    return _moba_attn(q, qs, k, vt, kmean.reshape(kmean.shape[0], kmean.shape[2]))


def _peer_layer(x1, norm2_g, w_q, sub_keys, expert_u, expert_v, final_g):
    s, d = x1.shape
    h2, h8, hs = _rmsnorm_fp8(x1, norm2_g)
    q = _matmul(h2, w_q.astype(BF16), F32, tm=min(1024, s), tn=512, name="peer_query")
    keys = sub_keys.reshape(PEER_HEADS * 2 * PEER_N_KEYS, PEER_HALF).astype(BF16)
    e1, e2, g = _peer_route(q, keys)
    wdense = _peer_expand(e1, e2, g)
    u8, us = _quant_rows(expert_u)
    v8, vs = _quant_rows(expert_v)
    peer = _peer_dense(h8, hs, u8, us, v8, vs, wdense)
    return _add_rmsnorm(x1, peer, final_g)


def kernel(x, norm1_g, w_in, rwkv_mu, rwkv_w0, rwkv_w_up, rwkv_a0, rwkv_a_up, rwkv_g_up, rwkv_k_k, rwkv_k_a, rwkv_r_k, rwkv_lnx_g, rwkv_lnx_b, w_branch_rwkv, w_branch_moba, w_out, norm2_g, peer_w_q, peer_sub_keys, peer_u, peer_v, final_g):
    b, s, d = x.shape
    depth = w_in.shape[0]
    assert b == 1 and depth == 1
    x2d = x.reshape(s, d)
    l = 0
    rw = rwkv_w0.shape[-1]
    shift_w = 3 * rw + DECAY_LORA + AAA_LORA + GATE_LORA
    tn = 512
    shift_pad = -(-shift_w // tn) * tn
    mw = w_branch_moba.shape[1]
    tm = min(1024, s)

    h = _rmsnorm(x2d, norm1_g[l], BF16)
    wt = jnp.transpose(w_in[l])
    mu = jnp.pad(rwkv_mu[l], (0, shift_pad - shift_w)).reshape(1, shift_pad)
    proj_r = _matmul_shift(h, wt, mu, tm=tm, tn=tn)
    proj_m = _matmul_nt(h, wt, F32, tm=tm, tn=tn, name="in_proj_moba", row0=shift_w, n=3 * mw)
    gates = _matmul_nt(h, wt, F32, tm=tm, tn=tn, name="in_proj_gates", row0=shift_w + 3 * mw, n=2 * d)

    y_a = _rwkv_branch(proj_r, rwkv_w0[l], rwkv_w_up[l], rwkv_a0[l], rwkv_a_up[l], rwkv_g_up[l],
                       rwkv_k_k[l], rwkv_k_a[l], rwkv_r_k[l], rwkv_lnx_g[l], rwkv_lnx_b[l])
    y_b = _moba_branch(proj_m)
    mixed = _gated_pair(y_a, w_branch_rwkv[l].astype(BF16), y_b, w_branch_moba[l].astype(BF16),
                        gates, tm=tm, tn=512)
    x1 = _matmul_residual(mixed, w_out[l].astype(BF16), x2d, tm=tm, tn=512)
    out = _peer_layer(x1, norm2_g[l], peer_w_q[l], peer_sub_keys[l], peer_u[l], peer_v[l], final_g)
    return out.reshape(b, s, d)
```

```python
import functools

import jax
import jax.numpy as jnp
from jax import lax
from jax.experimental import pallas as pl
from jax.experimental.pallas import tpu as pltpu

F32 = jnp.float32
BF16 = jnp.bfloat16

NORM_EPS = 1e-6
LANES = 128
F32_SUBLANES = 8
RWKV_HEAD_DIM = 64
DECAY_LORA = 128
AAA_LORA = 128
GATE_LORA = 480
GATE_LORA_PAD = 512
LN_X_EPS = 64e-5
CHUNK = 128
MOBA_HEAD_DIM = 128
MOBA_BLOCK = 256
MOBA_TOPK = 3
ROPE_THETA = 10000.0
PEER_HEADS = 8
PEER_N_KEYS = 128
PEER_HALF = 128
PEER_TOPK = 16

VMEM_LIMIT = 56 * 1024 * 1024

_NT = (((1,), (1,)), ((), ()))


def _params(sem, vmem=VMEM_LIMIT):
    return pltpu.CompilerParams(dimension_semantics=sem, vmem_limit_bytes=vmem)


def _dot(a, b):
    return jnp.dot(a, b, preferred_element_type=F32)


def _dot_nt(a, b):
    return lax.dot_general(a, b, _NT, preferred_element_type=F32)


def _split3(x):
    hi = x.astype(BF16)
    r1 = x - hi.astype(F32)
    mid = r1.astype(BF16)
    lo = (r1 - mid.astype(F32)).astype(BF16)
    return hi, mid, lo


def _dot_r3(x, r_bf16):
    hi, mid, lo = _split3(x)
    return _dot(hi, r_bf16) + _dot(mid, r_bf16) + _dot(lo, r_bf16)


def _rmsnorm_kernel(x_ref, g_ref, o_ref):
    x = x_ref[...]
    ms = jnp.mean(x * x, axis=-1, keepdims=True)
    o_ref[...] = (x * lax.rsqrt(ms + NORM_EPS) * g_ref[...]).astype(o_ref.dtype)


def _rmsnorm(x, g, out_dtype, tm=256):
    s, d = x.shape
    return pl.pallas_call(
        _rmsnorm_kernel,
        grid=(s // tm,),
        in_specs=[pl.BlockSpec((tm, d), lambda i: (i, 0)),
                  pl.BlockSpec((1, d), lambda i: (0, 0))],
        out_specs=pl.BlockSpec((tm, d), lambda i: (i, 0)),
        out_shape=jax.ShapeDtypeStruct((s, d), out_dtype),
        compiler_params=_params(("parallel",)),
        name="rmsnorm",
    )(x, g.reshape(1, d))


def _mm_kernel(a_ref, b_ref, o_ref):
    o_ref[...] = _dot(a_ref[...], b_ref[...]).astype(o_ref.dtype)


def _matmul(a, b, out_dtype, tm, tn, name, col0=0, n=None):
    m, k = a.shape
    n = b.shape[1] if n is None else n
    off = col0 // tn
    return pl.pallas_call(
        _mm_kernel,
        grid=(m // tm, n // tn),
        in_specs=[pl.BlockSpec((tm, k), lambda i, j: (i, 0)),
                  pl.BlockSpec((k, tn), lambda i, j: (0, j + off))],
        out_specs=pl.BlockSpec((tm, tn), lambda i, j: (i, j)),
        out_shape=jax.ShapeDtypeStruct((m, n), out_dtype),
        compiler_params=_params(("parallel", "parallel")),
        name=name,
    )(a, b)


BF16_SUBLANES = 16


def _mm_nt_kernel(a_ref, bt_ref, o_ref, b16_ref):
    @pl.when(pl.program_id(1) == 0)
    def _():
        b16_ref[...] = bt_ref[...].astype(b16_ref.dtype)

    o_ref[...] = _dot_nt(a_ref[...], b16_ref[...]).astype(o_ref.dtype)


def _matmul_nt(a, bt, out_dtype, tm, tn, name, row0, n):
    m, k = a.shape
    return pl.pallas_call(
        _mm_nt_kernel,
        grid=(n // tn, m // tm),
        in_specs=[pl.BlockSpec((tm, k), lambda j, i: (i, 0)),
                  pl.BlockSpec((pl.Element(tn), pl.Element(k)),
                               lambda j, i: (pl.multiple_of(row0 + j * tn, F32_SUBLANES), 0))],
        out_specs=pl.BlockSpec((tm, tn), lambda j, i: (i, j)),
        out_shape=jax.ShapeDtypeStruct((m, n), out_dtype),
        scratch_shapes=[pltpu.VMEM((tn, k), BF16)],
        compiler_params=_params(("parallel", "arbitrary")),
        name=name,
    )(a, bt)


def _mm_shift_kernel(a_ref, ap_ref, bt_ref, mu_ref, o_ref, b16_ref):
    i = pl.program_id(1)

    @pl.when(i == 0)
    def _():
        b16_ref[...] = bt_ref[...].astype(b16_ref.dtype)

    b = b16_ref[...]
    z = _dot_nt(a_ref[...], b)
    zp = _dot_nt(ap_ref[...], b)
    prev = jnp.where(i > 0, zp[BF16_SUBLANES - 1:BF16_SUBLANES, :], 0.0)
    row = lax.broadcasted_iota(jnp.int32, z.shape, 0)
    z_prev = jnp.where(row == 0, prev, pltpu.roll(z, 1, axis=0))
    o_ref[...] = z + (z_prev - z) * mu_ref[...]


def _matmul_shift(a, bt, mu, tm, tn):
    m, k = a.shape
    n = mu.shape[1]
    per = tm // BF16_SUBLANES
    return pl.pallas_call(
        _mm_shift_kernel,
        grid=(n // tn, m // tm),
        in_specs=[pl.BlockSpec((tm, k), lambda j, i: (i, 0)),
                  pl.BlockSpec((BF16_SUBLANES, k), lambda j, i: (jnp.maximum(i * per - 1, 0), 0)),
                  pl.BlockSpec((tn, k), lambda j, i: (j, 0)),
                  pl.BlockSpec((1, tn), lambda j, i: (0, j))],
        out_specs=pl.BlockSpec((tm, tn), lambda j, i: (i, j)),
        out_shape=jax.ShapeDtypeStruct((m, n), F32),
        scratch_shapes=[pltpu.VMEM((tn, k), BF16)],
        compiler_params=_params(("parallel", "arbitrary")),
        name="in_proj_rwkv",
    )(a, a, bt, mu)


def _gated_pair_kernel(ya_ref, wa_ref, yb_ref, wb_ref, ga_ref, gb_ref, o_ref):
    pa = _dot(ya_ref[...], wa_ref[...])
    pb = _dot(yb_ref[...], wb_ref[...])
    mixed = jax.nn.sigmoid(ga_ref[...]) * pa + jax.nn.sigmoid(gb_ref[...]) * pb
    o_ref[...] = mixed.astype(o_ref.dtype)


def _gated_pair(ya, wa, yb, wb, gates, tm, tn):
    m, k = ya.shape
    n = wa.shape[1]
    nb = n // tn
    return pl.pallas_call(
        _gated_pair_kernel,
        grid=(m // tm, nb),
        in_specs=[pl.BlockSpec((tm, k), lambda i, j: (i, 0)),
                  pl.BlockSpec((k, tn), lambda i, j: (0, j)),
                  pl.BlockSpec((tm, k), lambda i, j: (i, 0)),
                  pl.BlockSpec((k, tn), lambda i, j: (0, j)),
                  pl.BlockSpec((tm, tn), lambda i, j: (i, j)),
                  pl.BlockSpec((tm, tn), lambda i, j: (i, j + nb))],
        out_specs=pl.BlockSpec((tm, tn), lambda i, j: (i, j)),
        out_shape=jax.ShapeDtypeStruct((m, n), BF16),
        compiler_params=_params(("parallel", "parallel")),
        name="branch_merge",
    )(ya, wa, yb, wb, gates, gates)


def _mm_res_kernel(a_ref, b_ref, r_ref, o_ref):
    o_ref[...] = r_ref[...] + _dot(a_ref[...], b_ref[...])


def _matmul_residual(a, b, res, tm, tn):
    m, k = a.shape
    n = b.shape[1]
    return pl.pallas_call(
        _mm_res_kernel,
        grid=(m // tm, n // tn),
        in_specs=[pl.BlockSpec((tm, k), lambda i, j: (i, 0)),
                  pl.BlockSpec((k, tn), lambda i, j: (0, j)),
                  pl.BlockSpec((tm, tn), lambda i, j: (i, j))],
        out_specs=pl.BlockSpec((tm, tn), lambda i, j: (i, j)),
        out_shape=jax.ShapeDtypeStruct((m, n), F32),
        compiler_params=_params(("parallel", "parallel")),
        name="out_proj",
    )(a, b, res)


def _rwkv_lora_kernel(wl_ref, al_ref, gl0_ref, gl1_ref, w0_ref, wup_ref, a0_ref, aup_ref,
                      gup_ref, wpre_ref, a_ref, g_ref):
    half = GATE_LORA_PAD // 2
    wpre_ref[...] = w0_ref[...] + _dot(jnp.tanh(wl_ref[...]).astype(BF16), wup_ref[...])
    a_ref[...] = jax.nn.sigmoid(a0_ref[...] + _dot(al_ref[...].astype(BF16), aup_ref[...]))
    g_ref[...] = (_dot(jax.nn.sigmoid(gl0_ref[...]).astype(BF16), gup_ref[:half, :])
                  + _dot(jax.nn.sigmoid(gl1_ref[...]).astype(BF16), gup_ref[half:, :]))


def _rwkv_lora(proj_r, w0, w_up, a0, a_up, g_up_pad, tm=256):
    s = proj_r.shape[0]
    w = w0.shape[-1]
    half = GATE_LORA_PAD // 2
    c_wl = 3 * w // DECAY_LORA
    c_al = (3 * w + DECAY_LORA) // AAA_LORA
    c_gl = (3 * w + DECAY_LORA + AAA_LORA) // half
    row = lambda i: (i, 0)
    const = lambda i: (0, 0)
    out = jax.ShapeDtypeStruct((s, w), F32)
    return pl.pallas_call(
        _rwkv_lora_kernel,
        grid=(s // tm,),
        in_specs=[pl.BlockSpec((tm, DECAY_LORA), lambda i: (i, c_wl)),
                  pl.BlockSpec((tm, AAA_LORA), lambda i: (i, c_al)),
                  pl.BlockSpec((tm, half), lambda i: (i, c_gl)),
                  pl.BlockSpec((tm, half), lambda i: (i, c_gl + 1)),
                  pl.BlockSpec((1, w), const),
                  pl.BlockSpec((DECAY_LORA, w), const),
                  pl.BlockSpec((1, w), const),
                  pl.BlockSpec((AAA_LORA, w), const),
                  pl.BlockSpec((GATE_LORA_PAD, w), const)],
        out_specs=[pl.BlockSpec((tm, w), row)] * 3,
        out_shape=[out, out, out],
        compiler_params=_params(("parallel",)),
        name="rwkv_lora",
    )(proj_r, proj_r, proj_r, proj_r, w0.reshape(1, w), w_up, a0.reshape(1, w), a_up, g_up_pad)


def _head_pair_ones(width=LANES):
    r = lax.broadcasted_iota(jnp.int32, (width, width), 0) // RWKV_HEAD_DIM
    c = lax.broadcasted_iota(jnp.int32, (width, width), 1) // RWKV_HEAD_DIM
    return r == c


def _rwkv_prep(r_ref, k_ref, v_ref, wpre_ref, a_ref, kk_ref, ka_ref, rk_ref,
               rt_ref, at_ref, kh_ref, bh_ref, kb_ref, bb_ref, pc_ref, bonus_ref):
    tm = r_ref.shape[0]
    r = r_ref[...]
    k = k_ref[...]
    v = v_ref[...]
    a = a_ref[...]
    same_head = jnp.where(_head_pair_ones(r.shape[1]), 1.0, 0.0).astype(BF16)

    x = -wpre_ref[...]
    softplus = jnp.maximum(x, 0.0) + jnp.log1p(jnp.exp(-jnp.abs(x)))
    lw = -jnp.exp(-softplus - 0.5)

    kk = k * kk_ref[...]
    ss = _dot_r3(kk * kk, same_head)
    kkn = kk / jnp.maximum(jnp.sqrt(ss), 1e-12)
    k2 = k * (1.0 + (a - 1.0) * ka_ref[...])
    bonus_ref[...] = _dot_r3(r * k2 * rk_ref[...], same_head) * v

    ri = lax.broadcasted_iota(jnp.int32, (tm, tm), 0)
    ci = lax.broadcasted_iota(jnp.int32, (tm, tm), 1)
    same_chunk = (ri // CHUNK) == (ci // CHUNK)
    tri = jnp.where(same_chunk & (ci <= ri), 1.0, 0.0).astype(BF16)
    allc = jnp.where(same_chunk, 1.0, 0.0).astype(BF16)
    lw_parts = _split3(lw)
    cum = sum(_dot(tri, part) for part in lw_parts)
    tot = sum(_dot(allc, part) for part in lw_parts)

    p_in = jnp.exp(cum)
    p_inv = jnp.exp(-cum)
    p_tail = jnp.exp(tot - cum)
    b = kkn * a
    rt_ref[...] = r * p_in
    at_ref[...] = -kkn * jnp.exp(cum - lw)
    kh_ref[...] = k2 * p_inv
    bh_ref[...] = b * p_inv
    kb_ref[...] = k2 * p_tail
    bb_ref[...] = b * p_tail
    pc_ref[...] = jnp.exp(tot)


def _rwkv_chunk_kernel(r_ref, k_ref, v_ref, wpre_ref, a_ref, kk_ref, ka_ref, rk_ref,
                       rt_ref, bonus_ref, x_ref, y_ref, op_ref, qb_ref, bbt_ref, g_ref, pcm_ref,
                       at_ref, kh_ref, bh_ref, kb_ref, bb_ref, pc_ref):
    _rwkv_prep(r_ref, k_ref, v_ref, wpre_ref, a_ref, kk_ref, ka_ref, rk_ref,
               rt_ref, at_ref, kh_ref, bh_ref, kb_ref, bb_ref, pc_ref, bonus_ref)
    c = CHUNK
    nheads = LANES // RWKV_HEAD_DIM
    chunks = range(rt_ref.shape[0] // c)
    pairs = range(rt_ref.shape[1] // LANES)
    rows = [slice(q * c, (q + 1) * c) for q in chunks]
    lanes = [slice(p * LANES, (p + 1) * LANES) for p in pairs]
    tiles = [(q, p) for q in chunks for p in pairs]
    probs = [(n, h) for n in range(len(tiles)) for h in range(nheads)]
    lane = lax.broadcasted_iota(jnp.int32, (c, LANES), 1)
    ri = lax.broadcasted_iota(jnp.int32, (c, c), 0)
    ci = lax.broadcasted_iota(jnp.int32, (c, c), 1)
    strict = ci < ri
    incl = ci <= ri
    eye = jnp.where(ri == ci, 1.0, 0.0)
    head_mask = [(lane // RWKV_HEAD_DIM) == h for h in range(nheads)]
    tile = lambda ref, n: ref[rows[tiles[n][0]], lanes[tiles[n][1]]]

    kh = [tile(kh_ref, n).astype(BF16) for n in range(len(tiles))]
    bh = [tile(bh_ref, n).astype(BF16) for n in range(len(tiles))]
    a_h = [jnp.where(head_mask[h], tile(at_ref, n), 0.0).astype(BF16) for n, h in probs]
    r_h = [jnp.where(head_mask[h], tile(rt_ref, n), 0.0).astype(BF16) for n, h in probs]
    v_h = [jnp.where(head_mask[h], tile(v_ref, n), 0.0).astype(BF16) for n, h in probs]
    n_ab = [jnp.where(strict, _dot_nt(a, bh[n]), 0.0) for a, (n, _) in zip(a_h, probs)]
    n_ak = [jnp.where(strict, _dot_nt(a, kh[n]), 0.0).astype(BF16) for a, (n, _) in zip(a_h, probs)]
    q_k = [jnp.where(incl, _dot_nt(r, kh[n]), 0.0).astype(BF16) for r, (n, _) in zip(r_h, probs)]
    q_b = [jnp.where(incl, _dot_nt(r, bh[n]), 0.0) for r, (n, _) in zip(r_h, probs)]
    for qb, (n, h) in zip(q_b, probs):
        q, p = tiles[n]
        col = (p * nheads + h) * c
        qb_ref[rows[q], col:col + c] = qb.astype(qb_ref.dtype)

    t = [eye + n for n in n_ab]
    pw = n_ab
    for _ in range((c - 1).bit_length() - 1):
        pb = [x.astype(BF16) for x in pw]
        pw = [_dot(x, x) for x in pb]
        t = [ti + _dot(ti.astype(BF16), pi.astype(BF16)) for ti, pi in zip(t, pw)]
    tb = [ti.astype(BF16) for ti in t]
    xs = [_dot(ti, a) for ti, a in zip(tb, a_h)]
    nv = [_dot(n, v).astype(BF16) for n, v in zip(n_ak, v_h)]
    ys = [_dot(ti, z) for ti, z in zip(tb, nv)]
    os = [_dot(qk, v) for qk, v in zip(q_k, v_h)]
    same_head = _head_pair_ones()
    for n, (q, p) in enumerate(tiles):
        mine = [m for m, (nn, _) in enumerate(probs) if nn == n]
        x_ref[rows[q], lanes[p]] = sum(xs[m] for m in mine).astype(x_ref.dtype)
        y_ref[rows[q], lanes[p]] = sum(ys[m] for m in mine)
        op_ref[rows[q], lanes[p]] = sum(os[m] for m in mine)
        kbt = tile(kb_ref, n).T.astype(BF16)
        g_ref[rows[q], lanes[p]] = jnp.where(same_head, _dot(kbt, tile(v_ref, n).astype(BF16)), 0.0)
        bbt_ref[lanes[p], rows[q]] = tile(bb_ref, n).T.astype(bbt_ref.dtype)
        pcm_ref[rows[q], lanes[p]] = tile(pc_ref, n).T


def _rwkv_chunk(proj_r, wpre, a, k_k, k_a, r_k, tc=4 * CHUNK, tw=2 * LANES):
    s, w = wpre.shape
    ncol = w // tw
    blk = lambda off: pl.BlockSpec((tc, tw), lambda i, j, off=off: (i, j + off))
    vec = pl.BlockSpec((1, tw), lambda i, j: (0, j))
    f32o = jax.ShapeDtypeStruct((s, w), F32)
    return pl.pallas_call(
        _rwkv_chunk_kernel,
        grid=(s // tc, ncol),
        in_specs=[blk(0), blk(ncol), blk(2 * ncol), blk(0), blk(0), vec, vec, vec],
        out_specs=[blk(0), blk(0), blk(0), blk(0), blk(0),
                   pl.BlockSpec((tc, 2 * tw), lambda i, j: (i, j)),
                   pl.BlockSpec((tw, tc), lambda i, j: (j, i)),
                   blk(0), blk(0)],
        out_shape=[f32o, f32o, jax.ShapeDtypeStruct((s, w), BF16), f32o, f32o,
                   jax.ShapeDtypeStruct((s, 2 * w), BF16),
                   jax.ShapeDtypeStruct((w, s), BF16),
                   f32o, f32o],
        scratch_shapes=[pltpu.VMEM((tc, tw), F32)] * 6,
        compiler_params=_params(("parallel", "parallel")),
        name="rwkv_chunk",
    )(proj_r, proj_r, proj_r, wpre, a, k_k.reshape(1, w), k_a.reshape(1, w), r_k.reshape(1, w))


def _rwkv_scan_kernel(x_ref, y_ref, rt_ref, op_ref, qb_ref, bbt_ref, g_ref, pcm_ref,
                      bonus_ref, gate_ref, lng_ref, lnb_ref, ya_ref, h_ref):
    c = CHUNK
    pairs = range(x_ref.shape[1] // LANES)
    lanes = [slice(p * LANES, (p + 1) * LANES) for p in pairs]

    @pl.when(pl.program_id(1) == 0)
    def _():
        h_ref[...] = jnp.zeros_like(h_ref)

    lane = lax.broadcasted_iota(jnp.int32, (c, LANES), 1)
    first = lane < RWKV_HEAD_DIM
    same_head = _head_pair_ones()
    ones_head = jnp.where(same_head, 1.0, 0.0).astype(BF16)
    inv_n = 1.0 / RWKV_HEAD_DIM
    for q in range(x_ref.shape[0] // c):
        rows = slice(q * c, (q + 1) * c)
        hs = [h_ref[p] for p in pairs]
        hb = [h.astype(BF16) for h in hs]
        us = [_dot(x_ref[rows, lanes[p]], hb[p]) + y_ref[rows, lanes[p]] for p in pairs]
        u2 = [jnp.concatenate([jnp.where(first, u, 0.0), jnp.where(first, 0.0, u)], axis=0).astype(BF16)
              for u in us]
        os = [_dot(rt_ref[rows, lanes[p]].astype(BF16), hb[p]) + op_ref[rows, lanes[p]]
              + _dot(qb_ref[rows, p * 2 * c:(p + 1) * 2 * c], u2[p]) for p in pairs]
        upd = [_dot(bbt_ref[lanes[p], rows], us[p].astype(BF16)) for p in pairs]
        for p in pairs:
            h_ref[p] = (pcm_ref[rows, lanes[p]] * hs[p] + g_ref[rows, lanes[p]]
                        + jnp.where(same_head, upd[p], 0.0))
        mus = [_dot_r3(o, ones_head) * inv_n for o in os]
        ds = [o - mu for o, mu in zip(os, mus)]
        vs = [_dot_r3(d * d, ones_head) * inv_n for d in ds]
        for p in pairs:
            y = ds[p] * lax.rsqrt(vs[p] + LN_X_EPS) * lng_ref[:, lanes[p]] + lnb_ref[:, lanes[p]]
            ya_ref[rows, lanes[p]] = ((y + bonus_ref[rows, lanes[p]])
                                      * gate_ref[rows, lanes[p]]).astype(ya_ref.dtype)


def _rwkv_scan(x, y, rt, opre, qb2, bbt, g, pcm, bonus, gate, lnx_g, lnx_b, ts=512, tw=4 * LANES):
    s, w = y.shape
    ts = min(ts, s)
    blk = pl.BlockSpec((ts, tw), lambda j, i: (i, j))
    vec = pl.BlockSpec((1, tw), lambda j, i: (0, j))
    return pl.pallas_call(
        _rwkv_scan_kernel,
        grid=(w // tw, s // ts),
        in_specs=[blk, blk, blk, blk,
                  pl.BlockSpec((ts, 2 * tw), lambda j, i: (i, j)),
                  pl.BlockSpec((tw, ts), lambda j, i: (j, i)),
                  blk, blk, blk, blk, vec, vec],
        out_specs=blk,
        out_shape=jax.ShapeDtypeStruct((s, w), BF16),
        scratch_shapes=[pltpu.VMEM((tw // LANES, LANES, LANES), F32)],
        compiler_params=_params(("parallel", "arbitrary")),
        name="rwkv_scan",
    )(x, y, rt, opre, qb2, bbt, g, pcm, bonus, gate, lnx_g.reshape(1, w), lnx_b.reshape(1, w))


V_ONES_ROWS = 16


LOG2_E = 1.4426950408889634
MOBA_LOG2_SCALE = (MOBA_HEAD_DIM ** -0.5) * LOG2_E


def _moba_prep_kernel(q_ref, k_ref, v_ref, cos_ref, sin_ref, qo_ref, qs_ref, ko_ref, vt_ref, km_ref):
    cos = cos_ref[...]
    sin = sin_ref[...]
    hd = MOBA_HEAD_DIM
    half = hd // 2
    tb = q_ref.shape[0]
    for h in range(q_ref.shape[1] // hd):
        cols = slice(h * hd, (h + 1) * hd)
        q = q_ref[:, cols]
        k = k_ref[:, cols]
        qr = q * cos + pltpu.roll(q, half, axis=1) * sin
        kr = k * cos + pltpu.roll(k, half, axis=1) * sin
        qo_ref[:, cols] = qr.astype(qo_ref.dtype)
        qs_ref[:, cols] = (qr * MOBA_LOG2_SCALE).astype(qs_ref.dtype)
        ko_ref[:, cols] = kr.astype(ko_ref.dtype)
        km_ref[0, :, cols] = jnp.mean(kr, axis=0, keepdims=True)
        vt_ref[h, 0, :hd, :] = v_ref[:, cols].T.astype(vt_ref.dtype)
        vt_ref[h, 0, hd:, :] = jnp.ones((V_ONES_ROWS, tb), vt_ref.dtype)


def _moba_prep(proj_m, cos, sin):
    s = proj_m.shape[0]
    w = proj_m.shape[1] // 3
    tb = MOBA_BLOCK
    nb = s // tb
    nh = w // MOBA_HEAD_DIM
    vrows = MOBA_HEAD_DIM + V_ONES_ROWS
    blk = lambda off: pl.BlockSpec((tb, w), lambda i, off=off: (i, off))
    tab = pl.BlockSpec((tb, MOBA_HEAD_DIM), lambda i: (i, 0))
    bo = jax.ShapeDtypeStruct((s, w), BF16)
    return pl.pallas_call(
        _moba_prep_kernel,
        grid=(nb,),
        in_specs=[blk(0), blk(1), blk(2), tab, tab],
        out_specs=[blk(0), blk(0), blk(0),
                   pl.BlockSpec((nh, 1, vrows, tb), lambda i: (0, i, 0, 0)),
                   pl.BlockSpec((1, 1, w), lambda i: (i, 0, 0))],
        out_shape=[bo, bo, bo, jax.ShapeDtypeStruct((nh, nb, vrows, tb), BF16),
                   jax.ShapeDtypeStruct((nb, 1, w), F32)],
        compiler_params=_params(("parallel",)),
        name="moba_prep",
    )(proj_m, proj_m, proj_m, cos, sin)


MOBA_TRIP_BLOCKS = 8
MOBA_HEADS_PER_STEP = 4


def _moba_attn_kernel(q_ref, qs_ref, k_ref, vt_ref, km_ref, o_ref, bias_ref, s_ref):
    i = pl.program_id(1)
    tb = MOBA_BLOCK
    nb = km_ref.shape[0]
    hd = MOBA_HEAD_DIM
    heads = range(q_ref.shape[1] // hd)
    cols = [slice(h * hd, (h + 1) * hd) for h in heads]
    q = [qs_ref[:, c] for c in cols]

    own = pl.multiple_of(i * tb, tb)
    s_own = [_dot_nt(k_ref[pl.ds(own, tb), cols[h]], q[h]) for h in heads]

    gates = [_dot_nt(km_ref[:, cols[h]].astype(BF16), q_ref[:, cols[h]]) for h in heads]
    rid = lax.broadcasted_iota(jnp.int32, (nb, tb), 0).astype(F32)
    for h in heads:
        gate = jnp.where(rid < i.astype(F32), gates[h], -jnp.inf)
        bias = jnp.full(gate.shape, -jnp.inf, F32)
        for _ in range(MOBA_TOPK):
            m = jnp.max(gate, axis=0, keepdims=True)
            first = jnp.min(jnp.where(gate == m, rid, float(nb)), axis=0, keepdims=True)
            pick = (rid == first) & (m > -jnp.inf)
            bias = jnp.where(pick, 0.0, bias)
            gate = jnp.where(pick, -jnp.inf, gate)
        bias_ref[h] = bias

    ki = lax.broadcasted_iota(jnp.int32, (tb, tb), 0)
    qi = lax.broadcasted_iota(jnp.int32, (tb, tb), 1)
    s_own = [jnp.where(ki <= qi, s, -jnp.inf).astype(BF16) for s in s_own]
    m_own = tuple(jnp.max(s, axis=0, keepdims=True) for s in s_own)

    def blocks(t):
        return [jnp.minimum(t * MOBA_TRIP_BLOCKS + g, nb - 1) for g in range(MOBA_TRIP_BLOCKS)]

    trips = (i + MOBA_TRIP_BLOCKS - 1) // MOBA_TRIP_BLOCKS

    def score_pass(t, ms):
        out = []
        for h in heads:
            ss = [_dot_nt(k_ref[pl.ds(pl.multiple_of(j * tb, tb), tb), cols[h]], q[h]).astype(BF16)
                  + bias_ref[h, pl.ds(j, 1), :].astype(BF16) for j in blocks(t)]
            m = ms[h]
            for j, sj in zip(blocks(t), ss):
                s_ref[h, j] = sj
                m = jnp.maximum(m, jnp.max(sj, axis=0, keepdims=True))
            out.append(m)
        return tuple(out)

    ms = lax.fori_loop(0, trips, score_pass, m_own)

    def value_pass(t, accs):
        out = []
        for h in heads:
            ps = [jnp.exp2(s_ref[h, j] - ms[h]) for j in blocks(t)]
            acc = accs[h]
            for j, pj in zip(blocks(t), ps):
                acc = acc + _dot(vt_ref[h, j], pj)
            out.append(acc)
        return tuple(out)

    acc0 = tuple(_dot(vt_ref[h, i], jnp.exp2(s_own[h] - ms[h])) for h in heads)
    accs = lax.fori_loop(0, trips, value_pass, acc0)
    for h in heads:
        out = accs[h][:hd, :] / accs[h][hd:hd + 1, :]
        o_ref[:, cols[h]] = out.T.astype(o_ref.dtype)


def _moba_attn(q, qs, k, vt, kmean):
    s, w = q.shape
    tb = MOBA_BLOCK
    nb = s // tb
    nh = MOBA_HEADS_PER_STEP
    hd = nh * MOBA_HEAD_DIM
    vrows = vt.shape[2]
    return pl.pallas_call(
        _moba_attn_kernel,
        grid=(w // hd, nb),
        in_specs=[pl.BlockSpec((tb, hd), lambda h, i: (i, h)),
                  pl.BlockSpec((tb, hd), lambda h, i: (i, h)),
                  pl.BlockSpec((s, hd), lambda h, i: (0, h), pipeline_mode=pl.Buffered(1)),
                  pl.BlockSpec((nh, nb, vrows, tb), lambda h, i: (h, 0, 0, 0),
                               pipeline_mode=pl.Buffered(1)),
                  pl.BlockSpec((nb, hd), lambda h, i: (0, h))],
        out_specs=pl.BlockSpec((tb, hd), lambda h, i: (i, h)),
        out_shape=jax.ShapeDtypeStruct((s, w), BF16),
        scratch_shapes=[pltpu.VMEM((nh, nb, tb), F32), pltpu.VMEM((nh, nb, tb, tb), BF16)],
        compiler_params=_params(("parallel", "arbitrary")),
        name="moba_attn",
    )(q, qs, k, vt, kmean)


def _top_rows(x, n):
    rows = x.shape[0]
    rid = lax.broadcasted_iota(jnp.int32, x.shape, 0).astype(F32)
    vals, idxs = [], []
    for _ in range(n):
        m = jnp.max(x, axis=0, keepdims=True)
        first = jnp.min(jnp.where(x == m, rid, float(rows)), axis=0, keepdims=True)
        vals.append(m)
        idxs.append(first)
        x = jnp.where(rid == first, -jnp.inf, x)
    return jnp.concatenate(vals, axis=0), jnp.concatenate(idxs, axis=0)


def _peer_route_kernel(q_ref, keys_ref, e1_ref, e2_ref, g_ref):
    n = PEER_TOPK
    tm = q_ref.shape[0]
    e1s, e2s, gs = [], [], []
    for h in range(PEER_HEADS):
        tops = []
        for p in range(2):
            hp = 2 * h + p
            cols = slice(hp * PEER_HALF, (hp + 1) * PEER_HALF)
            rows = slice(hp * PEER_N_KEYS, (hp + 1) * PEER_N_KEYS)
            st = _dot_nt(keys_ref[rows, :], q_ref[:, cols].astype(BF16))
            tops.append(_top_rows(st, n))
        (s1, i1), (s2, i2) = tops
        hn = n // 2
        cand = jnp.concatenate([s1[0:1, :] + s2]
                               + [s1[a:a + 1, :] + s2[:hn, :] for a in range(1, hn)]
                               + [s1[hn:, :] + s2[0:1, :]], axis=0)
        f_s, f_pos = _top_rows(cand, n)
        mid = jnp.floor((f_pos - n) * (1.0 / hn))
        tail0 = float(n + (hn - 1) * hn)
        pa = jnp.where(f_pos < n, 0.0, jnp.where(f_pos < tail0, 1.0 + mid, f_pos - tail0 + hn))
        pb = jnp.where(f_pos < n, f_pos, jnp.where(f_pos < tail0, f_pos - n - mid * hn, 0.0))
        e1 = jnp.zeros((n, tm), F32)
        e2 = jnp.zeros((n, tm), F32)
        for a in range(n):
            e1 = jnp.where(pa == float(a), i1[a:a + 1, :], e1)
            e2 = jnp.where(pb == float(a), i2[a:a + 1, :], e2)
        ex = jnp.exp(f_s - f_s[0:1, :])
        gs.append(ex / jnp.sum(ex, axis=0, keepdims=True))
        e1s.append(e1)
        e2s.append(e2)
    e1_ref[...] = jnp.concatenate(e1s, axis=0).T
    e2_ref[...] = jnp.concatenate(e2s, axis=0).T
    g_ref[...] = jnp.concatenate(gs, axis=0).T


def _peer_route(q, keys, tm=256):
    s, w = q.shape
    nsel = PEER_HEADS * PEER_TOPK
    tm = min(tm, s)
    out = jax.ShapeDtypeStruct((s, nsel), F32)
    ob = pl.BlockSpec((tm, nsel), lambda i: (i, 0))
    return pl.pallas_call(
        _peer_route_kernel,
        grid=(s // tm,),
        in_specs=[pl.BlockSpec((tm, w), lambda i: (i, 0)),
                  pl.BlockSpec(keys.shape, lambda i: (0, 0))],
        out_specs=[ob, ob, ob],
        out_shape=[out, out, out],
        compiler_params=_params(("parallel",)),
        name="peer_route",
    )(q, keys)


EXPAND_UNROLL = 16
EXPAND_GROUP = 4


def _peer_expand_kernel(e1_ref, e2_ref, g_ref, w_ref, stage_ref):
    nk = PEER_N_KEYS
    nsel = e1_ref.shape[1]
    rid = lax.broadcasted_iota(jnp.int32, (nk, nsel), 0).astype(F32)
    un = EXPAND_UNROLL

    def body(tt, carry):
        base = pl.multiple_of(tt * un, un)
        e1 = e1_ref[pl.ds(base, un), :]
        e2 = e2_ref[pl.ds(base, un), :]
        g = g_ref[pl.ds(base, un), :]
        for u0 in range(0, un, EXPAND_GROUP):
            us = range(u0, u0 + EXPAND_GROUP)
            lefts = [jnp.where(rid == e1[u:u + 1, :], g[u:u + 1, :], 0.0).astype(BF16) for u in us]
            rights = [jnp.where(rid == e2[u:u + 1, :], 1.0, 0.0).astype(BF16) for u in us]
            for u, l, r in zip(us, lefts, rights):
                stage_ref[u * nk:(u + 1) * nk, :] = _dot_nt(l, r)
        by_key = jnp.swapaxes(stage_ref[...].reshape(un, nk, nk), 0, 1)
        for a in range(nk):
            w_ref[pl.ds(base, un), a * nk:(a + 1) * nk] = by_key[a].astype(w_ref.dtype)
        return carry

    lax.fori_loop(0, e1_ref.shape[0] // un, body, 0)


def _peer_expand(e1, e2, g, tm=128):
    s, nsel = e1.shape
    nk = PEER_N_KEYS
    tm = min(tm, s)
    ib = pl.BlockSpec((tm, nsel), lambda i: (i, 0))
    return pl.pallas_call(
        _peer_expand_kernel,
        grid=(s // tm,),
        in_specs=[ib, ib, ib],
        out_specs=pl.BlockSpec((tm, nk * nk), lambda i: (i, 0)),
        out_shape=jax.ShapeDtypeStruct((s, nk * nk), BF16),
        scratch_shapes=[pltpu.VMEM((EXPAND_UNROLL * nk, nk), F32)],
        compiler_params=_params(("parallel",)),
        name="peer_expand",
    )(e1, e2, g)


FP8 = jnp.float8_e4m3fn
FP8_MAX = 448.0
SCALE_ROWS = 8


def _rmsnorm_fp8_kernel(x_ref, g_ref, o_ref, o8_ref, s_ref):
    x = x_ref[...]
    ms = jnp.mean(x * x, axis=-1, keepdims=True)
    y = x * lax.rsqrt(ms + NORM_EPS) * g_ref[...]
    o_ref[...] = y.astype(o_ref.dtype)
    scale = jnp.maximum(jnp.max(jnp.abs(y), axis=-1, keepdims=True), 1e-30) * (1.0 / FP8_MAX)
    o8_ref[...] = (y / scale).astype(o8_ref.dtype)
    s_ref[...] = jnp.broadcast_to(scale, s_ref.shape)


def _rmsnorm_fp8(x, g, tm=256):
    s, d = x.shape
    blk = pl.BlockSpec((tm, d), lambda i: (i, 0))
    return pl.pallas_call(
        _rmsnorm_fp8_kernel,
        grid=(s // tm,),
        in_specs=[blk, pl.BlockSpec((1, d), lambda i: (0, 0))],
        out_specs=[blk, blk, pl.BlockSpec((tm, LANES), lambda i: (i, 0))],
        out_shape=[jax.ShapeDtypeStruct((s, d), BF16), jax.ShapeDtypeStruct((s, d), FP8),
                   jax.ShapeDtypeStruct((s, LANES), F32)],
        compiler_params=_params(("parallel",)),
        name="rmsnorm_fp8",
    )(x, g.reshape(1, d))


def _quant_rows_kernel(u_ref, u8_ref, s_ref):
    u = u_ref[...]
    scale = jnp.maximum(jnp.max(jnp.abs(u), axis=-1, keepdims=True), 1e-30) * (1.0 / FP8_MAX)
    u8_ref[...] = (u / scale).astype(u8_ref.dtype)
    s_ref[...] = jnp.broadcast_to(scale, (u.shape[0], LANES)).T[:SCALE_ROWS, :]


def _quant_rows(u, te=512):
    ne, d = u.shape
    return pl.pallas_call(
        _quant_rows_kernel,
        grid=(ne // te,),
        in_specs=[pl.BlockSpec((te, d), lambda e: (e, 0))],
        out_specs=[pl.BlockSpec((te, d), lambda e: (e, 0)),
                   pl.BlockSpec((SCALE_ROWS, te), lambda e: (0, e))],
        out_shape=[jax.ShapeDtypeStruct((ne, d), FP8), jax.ShapeDtypeStruct((SCALE_ROWS, ne), F32)],
        compiler_params=_params(("parallel",)),
        name="peer_quant_u",
    )(u)


def _peer_dense_kernel(h_ref, hs_ref, u_ref, us_ref, v_ref, vs_ref, w_ref, o_ref):
    @pl.when(pl.program_id(1) == 0)
    def _():
        o_ref[...] = jnp.zeros_like(o_ref)

    act = _dot_nt(h_ref[...], u_ref[...]) * hs_ref[:, 0:1] * us_ref[0:1, :]
    gelu = 0.5 * act * (1.0 + lax.erf(act * (2.0 ** -0.5)))
    mix = w_ref[...].astype(F32) * gelu * vs_ref[0:1, :]
    scale = jnp.maximum(jnp.max(jnp.abs(mix), axis=-1, keepdims=True), 1e-30) * (1.0 / FP8_MAX)
    o_ref[...] += _dot((mix / scale).astype(FP8), v_ref[...]) * scale


def _peer_dense(h8, hs, u8, us, v8, vs, wdense, tm=512, te=1024):
    s, d = h8.shape
    ne = u8.shape[0]
    tm = min(tm, s)
    return pl.pallas_call(
        _peer_dense_kernel,
        grid=(s // tm, ne // te),
        in_specs=[pl.BlockSpec((tm, d), lambda i, e: (i, 0)),
                  pl.BlockSpec((tm, LANES), lambda i, e: (i, 0)),
                  pl.BlockSpec((te, d), lambda i, e: (e, 0)),
                  pl.BlockSpec((SCALE_ROWS, te), lambda i, e: (0, e)),
                  pl.BlockSpec((te, d), lambda i, e: (e, 0)),
                  pl.BlockSpec((SCALE_ROWS, te), lambda i, e: (0, e)),
                  pl.BlockSpec((tm, te), lambda i, e: (i, e))],
        out_specs=pl.BlockSpec((tm, d), lambda i, e: (i, 0)),
        out_shape=jax.ShapeDtypeStruct((s, d), F32),
        compiler_params=_params(("parallel", "arbitrary")),
        name="peer_dense",
    )(h8, hs, u8, us, v8, vs, wdense)


def _add_rmsnorm_kernel(x_ref, y_ref, g_ref, o_ref):
    x = x_ref[...] + y_ref[...]
    ms = jnp.mean(x * x, axis=-1, keepdims=True)
    o_ref[...] = x * lax.rsqrt(ms + NORM_EPS) * g_ref[...]


def _add_rmsnorm(x, y, g, tm=256):
    s, d = x.shape
    blk = pl.BlockSpec((tm, d), lambda i: (i, 0))
    return pl.pallas_call(
        _add_rmsnorm_kernel,
        grid=(s // tm,),
        in_specs=[blk, blk, pl.BlockSpec((1, d), lambda i: (0, 0))],
        out_specs=blk,
        out_shape=jax.ShapeDtypeStruct((s, d), F32),
        compiler_params=_params(("parallel",)),
        name="residual_final_norm",
    )(x, y, g.reshape(1, d))


def _rwkv_branch(proj_r, w0, w_up, a0, a_up, g_up, k_k, k_a, r_k, lnx_g, lnx_b):
    g_up_pad = jnp.pad(g_up, ((0, GATE_LORA_PAD - GATE_LORA), (0, 0))).astype(BF16)
    wpre, a, g = _rwkv_lora(proj_r, w0, w_up.astype(BF16), a0, a_up.astype(BF16), g_up_pad)
    rt, bonus, x, y, opre, qb2, bbt, gst, pcm = _rwkv_chunk(proj_r, wpre, a, k_k, k_a, r_k.reshape(-1))
    return _rwkv_scan(x, y, rt, opre, qb2, bbt, gst, pcm, bonus, g, lnx_g, lnx_b)


def _rope_tables(s):
    half = MOBA_HEAD_DIM // 2
    inv_freq = ROPE_THETA ** (-jnp.arange(half, dtype=F32) / half)
    ang = jnp.arange(s, dtype=jnp.int32).astype(F32)[:, None] * inv_freq[None, :]
    cos = jnp.cos(ang)
    sin = jnp.sin(ang)
    return jnp.concatenate([cos, cos], axis=-1), jnp.concatenate([-sin, sin], axis=-1)


def _moba_branch(proj_m):
    s = proj_m.shape[0]
    cos, sin = _rope_tables(s)
    q, qs, k, vt, kmean = _moba_prep(proj_m, cos, sin)
    return _moba_attn(q, qs, k, vt, kmean.reshape(kmean.shape[0], kmean.shape[2]))


def _peer_layer(x1, norm2_g, w_q, sub_keys, expert_u, expert_v, final_g):
    s, d = x1.shape
    h2, h8, hs = _rmsnorm_fp8(x1, norm2_g)
    q = _matmul(h2, w_q.astype(BF16), F32, tm=min(1024, s), tn=512, name="peer_query")
    keys = sub_keys.reshape(PEER_HEADS * 2 * PEER_N_KEYS, PEER_HALF).astype(BF16)
    e1, e2, g = _peer_route(q, keys)
    wdense = _peer_expand(e1, e2, g)
    u8, us = _quant_rows(expert_u)
    v8, vs = _quant_rows(expert_v)
    peer = _peer_dense(h8, hs, u8, us, v8, vs, wdense)
    return _add_rmsnorm(x1, peer, final_g)


def kernel(x, norm1_g, w_in, rwkv_mu, rwkv_w0, rwkv_w_up, rwkv_a0, rwkv_a_up, rwkv_g_up, rwkv_k_k, rwkv_k_a, rwkv_r_k, rwkv_lnx_g, rwkv_lnx_b, w_branch_rwkv, w_branch_moba, w_out, norm2_g, peer_w_q, peer_sub_keys, peer_u, peer_v, final_g):
    b, s, d = x.shape
    depth = w_in.shape[0]
    assert b == 1 and depth == 1
    x2d = x.reshape(s, d)
    l = 0
    rw = rwkv_w0.shape[-1]
    shift_w = 3 * rw + DECAY_LORA + AAA_LORA + GATE_LORA
    tn = 512
    shift_pad = -(-shift_w // tn) * tn
    mw = w_branch_moba.shape[1]
    tm = min(1024, s)

    h = _rmsnorm(x2d, norm1_g[l], BF16)
    wt = jnp.transpose(w_in[l])
    mu = jnp.pad(rwkv_mu[l], (0, shift_pad - shift_w)).reshape(1, shift_pad)
    proj_r = _matmul_shift(h, wt, mu, tm=tm, tn=tn)
    proj_m = _matmul_nt(h, wt, F32, tm=tm, tn=tn, name="in_proj_moba", row0=shift_w, n=3 * mw)
    gates = _matmul_nt(h, wt, F32, tm=tm, tn=tn, name="in_proj_gates", row0=shift_w + 3 * mw, n=2 * d)

    y_a = _rwkv_branch(proj_r, rwkv_w0[l], rwkv_w_up[l], rwkv_a0[l], rwkv_a_up[l], rwkv_g_up[l],
                       rwkv_k_k[l], rwkv_k_a[l], rwkv_r_k[l], rwkv_lnx_g[l], rwkv_lnx_b[l])
    y_b = _moba_branch(proj_m)
    mixed = _gated_pair(y_a, w_branch_rwkv[l].astype(BF16), y_b, w_branch_moba[l].astype(BF16),
                        gates, tm=tm, tn=512)
    x1 = _matmul_residual(mixed, w_out[l].astype(BF16), x2d, tm=tm, tn=512)
    out = _peer_layer(x1, norm2_g[l], peer_w_q[l], peer_sub_keys[l], peer_u[l], peer_v[l], final_g)
    return out.reshape(b, s, d)
```

```python
import jax
import jax.numpy as jnp
from jax import lax
from jax.experimental import pallas as pl
from jax.experimental.pallas import tpu as pltpu

F32 = jnp.float32
BF16 = jnp.bfloat16

NORM_EPS = 1e-6
LANES = 128
F32_SUBLANES = 8
RWKV_HEAD_DIM = 64
DECAY_LORA = 128
AAA_LORA = 128
GATE_LORA = 480
GATE_LORA_PAD = 512
LN_X_EPS = 64e-5
CHUNK = 128
MOBA_HEAD_DIM = 128
MOBA_BLOCK = 256
MOBA_TOPK = 3
ROPE_THETA = 10000.0
PEER_HEADS = 8
PEER_N_KEYS = 128
PEER_HALF = 128
PEER_TOPK = 16

VMEM_LIMIT = 56 * 1024 * 1024

_NT = (((1,), (1,)), ((), ()))


def _params(sem, vmem=VMEM_LIMIT):
    return pltpu.CompilerParams(dimension_semantics=sem, vmem_limit_bytes=vmem)


def _dot(a, b):
    return jnp.dot(a, b, preferred_element_type=F32)


def _dot_nt(a, b):
    return lax.dot_general(a, b, _NT, preferred_element_type=F32)


def _split3(x):
    hi = x.astype(BF16)
    r1 = x - hi.astype(F32)
    mid = r1.astype(BF16)
    lo = (r1 - mid.astype(F32)).astype(BF16)
    return hi, mid, lo


def _dot_r3(x, r_bf16):
    hi, mid, lo = _split3(x)
    return _dot(hi, r_bf16) + _dot(mid, r_bf16) + _dot(lo, r_bf16)


def _rmsnorm_kernel(x_ref, g_ref, o_ref):
    x = x_ref[...]
    ms = jnp.mean(x * x, axis=-1, keepdims=True)
    o_ref[...] = (x * lax.rsqrt(ms + NORM_EPS) * g_ref[...]).astype(o_ref.dtype)


def _rmsnorm(x, g, out_dtype, tm=256):
    s, d = x.shape
    return pl.pallas_call(
        _rmsnorm_kernel,
        grid=(s // tm,),
        in_specs=[pl.BlockSpec((tm, d), lambda i: (i, 0)),
                  pl.BlockSpec((1, d), lambda i: (0, 0))],
        out_specs=pl.BlockSpec((tm, d), lambda i: (i, 0)),
        out_shape=jax.ShapeDtypeStruct((s, d), out_dtype),
        compiler_params=_params(("parallel",)),
        name="rmsnorm",
    )(x, g.reshape(1, d))


def _mm_kernel(a_ref, b_ref, o_ref):
    o_ref[...] = _dot(a_ref[...], b_ref[...]).astype(o_ref.dtype)


def _matmul(a, b, out_dtype, tm, tn, name, col0=0, n=None):
    m, k = a.shape
    n = b.shape[1] if n is None else n
    off = col0 // tn
    return pl.pallas_call(
        _mm_kernel,
        grid=(m // tm, n // tn),
        in_specs=[pl.BlockSpec((tm, k), lambda i, j: (i, 0)),
                  pl.BlockSpec((k, tn), lambda i, j: (0, j + off))],
        out_specs=pl.BlockSpec((tm, tn), lambda i, j: (i, j)),
        out_shape=jax.ShapeDtypeStruct((m, n), out_dtype),
        compiler_params=_params(("parallel", "parallel")),
        name=name,
    )(a, b)


BF16_SUBLANES = 16


def _mm_nt_kernel(a_ref, bt_ref, o_ref, b16_ref):
    @pl.when(pl.program_id(1) == 0)
    def _():
        b16_ref[...] = bt_ref[...].astype(b16_ref.dtype)

    o_ref[...] = _dot_nt(a_ref[...], b16_ref[...]).astype(o_ref.dtype)


def _matmul_nt(a, bt, out_dtype, tm, tn, name, row0, n):
    m, k = a.shape
    return pl.pallas_call(
        _mm_nt_kernel,
        grid=(n // tn, m // tm),
        in_specs=[pl.BlockSpec((tm, k), lambda j, i: (i, 0)),
                  pl.BlockSpec((pl.Element(tn), pl.Element(k)),
                               lambda j, i: (pl.multiple_of(row0 + j * tn, F32_SUBLANES), 0))],
        out_specs=pl.BlockSpec((tm, tn), lambda j, i: (i, j)),
        out_shape=jax.ShapeDtypeStruct((m, n), out_dtype),
        scratch_shapes=[pltpu.VMEM((tn, k), BF16)],
        compiler_params=_params(("parallel", "arbitrary")),
        name=name,
    )(a, bt)


def _mm_shift_kernel(a_ref, ap_ref, bt_ref, mu_ref, o_ref, b16_ref):
    i = pl.program_id(1)

    @pl.when(i == 0)
    def _():
        b16_ref[...] = bt_ref[...].astype(b16_ref.dtype)

    b = b16_ref[...]
    z = _dot_nt(a_ref[...], b)
    zp = _dot_nt(ap_ref[...], b)
    prev = jnp.where(i > 0, zp[BF16_SUBLANES - 1:BF16_SUBLANES, :], 0.0)
    row = lax.broadcasted_iota(jnp.int32, z.shape, 0)
    z_prev = jnp.where(row == 0, prev, pltpu.roll(z, 1, axis=0))
    o_ref[...] = z + (z_prev - z) * mu_ref[...]


def _matmul_shift(a, bt, mu, tm, tn):
    m, k = a.shape
    n = mu.shape[1]
    per = tm // BF16_SUBLANES
    return pl.pallas_call(
        _mm_shift_kernel,
        grid=(n // tn, m // tm),
        in_specs=[pl.BlockSpec((tm, k), lambda j, i: (i, 0)),
                  pl.BlockSpec((BF16_SUBLANES, k), lambda j, i: (jnp.maximum(i * per - 1, 0), 0)),
                  pl.BlockSpec((tn, k), lambda j, i: (j, 0)),
                  pl.BlockSpec((1, tn), lambda j, i: (0, j))],
        out_specs=pl.BlockSpec((tm, tn), lambda j, i: (i, j)),
        out_shape=jax.ShapeDtypeStruct((m, n), F32),
        scratch_shapes=[pltpu.VMEM((tn, k), BF16)],
        compiler_params=_params(("parallel", "arbitrary")),
        name="in_proj_rwkv",
    )(a, a, bt, mu)


def _gated_pair_kernel(ya_ref, wa_ref, yb_ref, wb_ref, ga_ref, gb_ref, o_ref):
    pa = _dot(ya_ref[...], wa_ref[...])
    pb = _dot(yb_ref[...], wb_ref[...])
    mixed = jax.nn.sigmoid(ga_ref[...]) * pa + jax.nn.sigmoid(gb_ref[...]) * pb
    o_ref[...] = mixed.astype(o_ref.dtype)


def _gated_pair(ya, wa, yb, wb, gates, tm, tn):
    m, k = ya.shape
    n = wa.shape[1]
    nb = n // tn
    return pl.pallas_call(
        _gated_pair_kernel,
        grid=(m // tm, nb),
        in_specs=[pl.BlockSpec((tm, k), lambda i, j: (i, 0)),
                  pl.BlockSpec((k, tn), lambda i, j: (0, j)),
                  pl.BlockSpec((tm, k), lambda i, j: (i, 0)),
                  pl.BlockSpec((k, tn), lambda i, j: (0, j)),
                  pl.BlockSpec((tm, tn), lambda i, j: (i, j)),
                  pl.BlockSpec((tm, tn), lambda i, j: (i, j + nb))],
        out_specs=pl.BlockSpec((tm, tn), lambda i, j: (i, j)),
        out_shape=jax.ShapeDtypeStruct((m, n), BF16),
        compiler_params=_params(("parallel", "parallel")),
        name="branch_merge",
    )(ya, wa, yb, wb, gates, gates)


def _mm_res_kernel(a_ref, b_ref, r_ref, o_ref):
    o_ref[...] = r_ref[...] + _dot(a_ref[...], b_ref[...])


def _matmul_residual(a, b, res, tm, tn):
    m, k = a.shape
    n = b.shape[1]
    return pl.pallas_call(
        _mm_res_kernel,
        grid=(m // tm, n // tn),
        in_specs=[pl.BlockSpec((tm, k), lambda i, j: (i, 0)),
                  pl.BlockSpec((k, tn), lambda i, j: (0, j)),
                  pl.BlockSpec((tm, tn), lambda i, j: (i, j))],
        out_specs=pl.BlockSpec((tm, tn), lambda i, j: (i, j)),
        out_shape=jax.ShapeDtypeStruct((m, n), F32),
        compiler_params=_params(("parallel", "parallel")),
        name="out_proj",
    )(a, b, res)


def _rwkv_lora_kernel(wl_ref, al_ref, gl0_ref, gl1_ref, w0_ref, wup_ref, a0_ref, aup_ref,
                      gup_ref, wpre_ref, a_ref, g_ref):
    half = GATE_LORA_PAD // 2
    wpre_ref[...] = w0_ref[...] + _dot(jnp.tanh(wl_ref[...]).astype(BF16), wup_ref[...])
    a_ref[...] = jax.nn.sigmoid(a0_ref[...] + _dot(al_ref[...].astype(BF16), aup_ref[...]))
    g_ref[...] = (_dot(jax.nn.sigmoid(gl0_ref[...]).astype(BF16), gup_ref[:half, :])
                  + _dot(jax.nn.sigmoid(gl1_ref[...]).astype(BF16), gup_ref[half:, :]))


def _rwkv_lora(proj_r, w0, w_up, a0, a_up, g_up_pad, tm=256):
    s = proj_r.shape[0]
    w = w0.shape[-1]
    half = GATE_LORA_PAD // 2
    c_wl = 3 * w // DECAY_LORA
    c_al = (3 * w + DECAY_LORA) // AAA_LORA
    c_gl = (3 * w + DECAY_LORA + AAA_LORA) // half
    row = lambda i: (i, 0)
    const = lambda i: (0, 0)
    out = jax.ShapeDtypeStruct((s, w), F32)
    return pl.pallas_call(
        _rwkv_lora_kernel,
        grid=(s // tm,),
        in_specs=[pl.BlockSpec((tm, DECAY_LORA), lambda i: (i, c_wl)),
                  pl.BlockSpec((tm, AAA_LORA), lambda i: (i, c_al)),
                  pl.BlockSpec((tm, half), lambda i: (i, c_gl)),
                  pl.BlockSpec((tm, half), lambda i: (i, c_gl + 1)),
                  pl.BlockSpec((1, w), const),
                  pl.BlockSpec((DECAY_LORA, w), const),
                  pl.BlockSpec((1, w), const),
                  pl.BlockSpec((AAA_LORA, w), const),
                  pl.BlockSpec((GATE_LORA_PAD, w), const)],
        out_specs=[pl.BlockSpec((tm, w), row)] * 3,
        out_shape=[out, out, out],
        compiler_params=_params(("parallel",)),
        name="rwkv_lora",
    )(proj_r, proj_r, proj_r, proj_r, w0.reshape(1, w), w_up, a0.reshape(1, w), a_up, g_up_pad)


def _head_pair_ones(width=LANES):
    r = lax.broadcasted_iota(jnp.int32, (width, width), 0) // RWKV_HEAD_DIM
    c = lax.broadcasted_iota(jnp.int32, (width, width), 1) // RWKV_HEAD_DIM
    return r == c


def _rwkv_prep(r_ref, k_ref, v_ref, wpre_ref, a_ref, kk_ref, ka_ref, rk_ref,
               rt_ref, at_ref, kh_ref, bh_ref, kb_ref, bb_ref, pc_ref, bonus_ref):
    tm = r_ref.shape[0]
    r = r_ref[...]
    k = k_ref[...]
    v = v_ref[...]
    a = a_ref[...]
    same_head = jnp.where(_head_pair_ones(r.shape[1]), 1.0, 0.0).astype(BF16)

    x = -wpre_ref[...]
    softplus = jnp.maximum(x, 0.0) + jnp.log1p(jnp.exp(-jnp.abs(x)))
    lw = -jnp.exp(-softplus - 0.5)

    kk = k * kk_ref[...]
    ss = _dot_r3(kk * kk, same_head)
    kkn = kk / jnp.maximum(jnp.sqrt(ss), 1e-12)
    k2 = k * (1.0 + (a - 1.0) * ka_ref[...])
    bonus_ref[...] = _dot_r3(r * k2 * rk_ref[...], same_head) * v

    ri = lax.broadcasted_iota(jnp.int32, (tm, tm), 0)
    ci = lax.broadcasted_iota(jnp.int32, (tm, tm), 1)
    same_chunk = (ri // CHUNK) == (ci // CHUNK)
    tri = jnp.where(same_chunk & (ci <= ri), 1.0, 0.0).astype(BF16)
    allc = jnp.where(same_chunk, 1.0, 0.0).astype(BF16)
    lw_parts = _split3(lw)
    cum = sum(_dot(tri, part) for part in lw_parts)
    tot = sum(_dot(allc, part) for part in lw_parts)

    p_in = jnp.exp(cum)
    p_inv = jnp.exp(-cum)
    p_tail = jnp.exp(tot - cum)
    b = kkn * a
    rt_ref[...] = r * p_in
    at_ref[...] = -kkn * jnp.exp(cum - lw)
    kh_ref[...] = k2 * p_inv
    bh_ref[...] = b * p_inv
    kb_ref[...] = k2 * p_tail
    bb_ref[...] = b * p_tail
    pc_ref[...] = jnp.exp(tot)


def _rwkv_chunk_kernel(r_ref, k_ref, v_ref, wpre_ref, a_ref, kk_ref, ka_ref, rk_ref,
                       rt_ref, bonus_ref, x_ref, y_ref, op_ref, qb_ref, bbt_ref, g_ref, pcm_ref,
                       at_ref, kh_ref, bh_ref, kb_ref, bb_ref, pc_ref):
    _rwkv_prep(r_ref, k_ref, v_ref, wpre_ref, a_ref, kk_ref, ka_ref, rk_ref,
               rt_ref, at_ref, kh_ref, bh_ref, kb_ref, bb_ref, pc_ref, bonus_ref)
    c = CHUNK
    nheads = LANES // RWKV_HEAD_DIM
    chunks = range(rt_ref.shape[0] // c)
    pairs = range(rt_ref.shape[1] // LANES)
    rows = [slice(q * c, (q + 1) * c) for q in chunks]
    lanes = [slice(p * LANES, (p + 1) * LANES) for p in pairs]
    tiles = [(q, p) for q in chunks for p in pairs]
    probs = [(n, h) for n in range(len(tiles)) for h in range(nheads)]
    lane = lax.broadcasted_iota(jnp.int32, (c, LANES), 1)
    ri = lax.broadcasted_iota(jnp.int32, (c, c), 0)
    ci = lax.broadcasted_iota(jnp.int32, (c, c), 1)
    strict = ci < ri
    incl = ci <= ri
    eye = jnp.where(ri == ci, 1.0, 0.0)
    head_mask = [(lane // RWKV_HEAD_DIM) == h for h in range(nheads)]
    tile = lambda ref, n: ref[rows[tiles[n][0]], lanes[tiles[n][1]]]

    kh = [tile(kh_ref, n).astype(BF16) for n in range(len(tiles))]
    bh = [tile(bh_ref, n).astype(BF16) for n in range(len(tiles))]
    a_h = [jnp.where(head_mask[h], tile(at_ref, n), 0.0).astype(BF16) for n, h in probs]
    r_h = [jnp.where(head_mask[h], tile(rt_ref, n), 0.0).astype(BF16) for n, h in probs]
    v_h = [jnp.where(head_mask[h], tile(v_ref, n), 0.0).astype(BF16) for n, h in probs]
    n_ab = [jnp.where(strict, _dot_nt(a, bh[n]), 0.0) for a, (n, _) in zip(a_h, probs)]
    n_ak = [jnp.where(strict, _dot_nt(a, kh[n]), 0.0).astype(BF16) for a, (n, _) in zip(a_h, probs)]
    q_k = [jnp.where(incl, _dot_nt(r, kh[n]), 0.0).astype(BF16) for r, (n, _) in zip(r_h, probs)]
    q_b = [jnp.where(incl, _dot_nt(r, bh[n]), 0.0) for r, (n, _) in zip(r_h, probs)]
    for qb, (n, h) in zip(q_b, probs):
        q, p = tiles[n]
        col = (p * nheads + h) * c
        qb_ref[rows[q], col:col + c] = qb.astype(qb_ref.dtype)

    t = [eye + n for n in n_ab]
    pw = n_ab
    for _ in range((c - 1).bit_length() - 1):
        pb = [x.astype(BF16) for x in pw]
        pw = [_dot(x, x) for x in pb]
        t = [ti + _dot(ti.astype(BF16), pi.astype(BF16)) for ti, pi in zip(t, pw)]
    tb = [ti.astype(BF16) for ti in t]
    xs = [_dot(ti, a) for ti, a in zip(tb, a_h)]
    nv = [_dot(n, v).astype(BF16) for n, v in zip(n_ak, v_h)]
    ys = [_dot(ti, z) for ti, z in zip(tb, nv)]
    os = [_dot(qk, v) for qk, v in zip(q_k, v_h)]
    same_head = _head_pair_ones()
    for n, (q, p) in enumerate(tiles):
        mine = [m for m, (nn, _) in enumerate(probs) if nn == n]
        x_ref[rows[q], lanes[p]] = sum(xs[m] for m in mine).astype(x_ref.dtype)
        y_ref[rows[q], lanes[p]] = sum(ys[m] for m in mine)
        op_ref[rows[q], lanes[p]] = sum(os[m] for m in mine)
        kbt = tile(kb_ref, n).T.astype(BF16)
        g_ref[rows[q], lanes[p]] = jnp.where(same_head, _dot(kbt, tile(v_ref, n).astype(BF16)), 0.0)
        bbt_ref[lanes[p], rows[q]] = tile(bb_ref, n).T.astype(bbt_ref.dtype)
        pcm_ref[rows[q], lanes[p]] = tile(pc_ref, n).T


def _rwkv_chunk(proj_r, wpre, a, k_k, k_a, r_k, tc=4 * CHUNK, tw=2 * LANES):
    s, w = wpre.shape
    ncol = w // tw
    blk = lambda off: pl.BlockSpec((tc, tw), lambda i, j, off=off: (i, j + off))
    vec = pl.BlockSpec((1, tw), lambda i, j: (0, j))
    f32o = jax.ShapeDtypeStruct((s, w), F32)
    return pl.pallas_call(
        _rwkv_chunk_kernel,
        grid=(s // tc, ncol),
        in_specs=[blk(0), blk(ncol), blk(2 * ncol), blk(0), blk(0), vec, vec, vec],
        out_specs=[blk(0), blk(0), blk(0), blk(0), blk(0),
                   pl.BlockSpec((tc, 2 * tw), lambda i, j: (i, j)),
                   pl.BlockSpec((tw, tc), lambda i, j: (j, i)),
                   blk(0), blk(0)],
        out_shape=[f32o, f32o, jax.ShapeDtypeStruct((s, w), BF16), f32o, f32o,
                   jax.ShapeDtypeStruct((s, 2 * w), BF16),
                   jax.ShapeDtypeStruct((w, s), BF16),
                   f32o, f32o],
        scratch_shapes=[pltpu.VMEM((tc, tw), F32)] * 6,
        compiler_params=_params(("parallel", "parallel")),
        name="rwkv_chunk",
    )(proj_r, proj_r, proj_r, wpre, a, k_k.reshape(1, w), k_a.reshape(1, w), r_k.reshape(1, w))


def _rwkv_scan_kernel(x_ref, y_ref, rt_ref, op_ref, qb_ref, bbt_ref, g_ref, pcm_ref,
                      bonus_ref, gate_ref, lng_ref, lnb_ref, ya_ref, h_ref):
    c = CHUNK
    pairs = range(x_ref.shape[1] // LANES)
    lanes = [slice(p * LANES, (p + 1) * LANES) for p in pairs]

    @pl.when(pl.program_id(1) == 0)
    def _():
        h_ref[...] = jnp.zeros_like(h_ref)

    lane = lax.broadcasted_iota(jnp.int32, (c, LANES), 1)
    first = lane < RWKV_HEAD_DIM
    same_head = _head_pair_ones()
    ones_head = jnp.where(same_head, 1.0, 0.0).astype(BF16)
    inv_n = 1.0 / RWKV_HEAD_DIM
    for q in range(x_ref.shape[0] // c):
        rows = slice(q * c, (q + 1) * c)
        hs = [h_ref[p] for p in pairs]
        hb = [h.astype(BF16) for h in hs]
        us = [_dot(x_ref[rows, lanes[p]], hb[p]) + y_ref[rows, lanes[p]] for p in pairs]
        u2 = [jnp.concatenate([jnp.where(first, u, 0.0), jnp.where(first, 0.0, u)], axis=0).astype(BF16)
              for u in us]
        os = [_dot(rt_ref[rows, lanes[p]].astype(BF16), hb[p]) + op_ref[rows, lanes[p]]
              + _dot(qb_ref[rows, p * 2 * c:(p + 1) * 2 * c], u2[p]) for p in pairs]
        upd = [_dot(bbt_ref[lanes[p], rows], us[p].astype(BF16)) for p in pairs]
        for p in pairs:
            h_ref[p] = (pcm_ref[rows, lanes[p]] * hs[p] + g_ref[rows, lanes[p]]
                        + jnp.where(same_head, upd[p], 0.0))
        mus = [_dot_r3(o, ones_head) * inv_n for o in os]
        ds = [o - mu for o, mu in zip(os, mus)]
        vs = [_dot_r3(d * d, ones_head) * inv_n for d in ds]
        for p in pairs:
            y = ds[p] * lax.rsqrt(vs[p] + LN_X_EPS) * lng_ref[:, lanes[p]] + lnb_ref[:, lanes[p]]
            ya_ref[rows, lanes[p]] = ((y + bonus_ref[rows, lanes[p]])
                                      * gate_ref[rows, lanes[p]]).astype(ya_ref.dtype)


def _rwkv_scan(x, y, rt, opre, qb2, bbt, g, pcm, bonus, gate, lnx_g, lnx_b, ts=512, tw=4 * LANES):
    s, w = y.shape
    ts = min(ts, s)
    blk = pl.BlockSpec((ts, tw), lambda j, i: (i, j))
    vec = pl.BlockSpec((1, tw), lambda j, i: (0, j))
    return pl.pallas_call(
        _rwkv_scan_kernel,
        grid=(w // tw, s // ts),
        in_specs=[blk, blk, blk, blk,
                  pl.BlockSpec((ts, 2 * tw), lambda j, i: (i, j)),
                  pl.BlockSpec((tw, ts), lambda j, i: (j, i)),
                  blk, blk, blk, blk, vec, vec],
        out_specs=blk,
        out_shape=jax.ShapeDtypeStruct((s, w), BF16),
        scratch_shapes=[pltpu.VMEM((tw // LANES, LANES, LANES), F32)],
        compiler_params=_params(("parallel", "arbitrary")),
        name="rwkv_scan",
    )(x, y, rt, opre, qb2, bbt, g, pcm, bonus, gate, lnx_g.reshape(1, w), lnx_b.reshape(1, w))


V_ONES_ROWS = 16


LOG2_E = 1.4426950408889634
MOBA_LOG2_SCALE = (MOBA_HEAD_DIM ** -0.5) * LOG2_E


def _moba_prep_kernel(q_ref, k_ref, v_ref, cos_ref, sin_ref, qo_ref, qs_ref, ko_ref, vt_ref, km_ref):
    cos = cos_ref[...]
    sin = sin_ref[...]
    hd = MOBA_HEAD_DIM
    half = hd // 2
    tb = q_ref.shape[0]
    for h in range(q_ref.shape[1] // hd):
        cols = slice(h * hd, (h + 1) * hd)
        q = q_ref[:, cols]
        k = k_ref[:, cols]
        qr = q * cos + pltpu.roll(q, half, axis=1) * sin
        kr = k * cos + pltpu.roll(k, half, axis=1) * sin
        qo_ref[:, cols] = qr.astype(qo_ref.dtype)
        qs_ref[:, cols] = (qr * MOBA_LOG2_SCALE).astype(qs_ref.dtype)
        ko_ref[:, cols] = kr.astype(ko_ref.dtype)
        km_ref[0, :, cols] = jnp.mean(kr, axis=0, keepdims=True)
        vt_ref[h, 0, :hd, :] = v_ref[:, cols].T.astype(vt_ref.dtype)
        vt_ref[h, 0, hd:, :] = jnp.ones((V_ONES_ROWS, tb), vt_ref.dtype)


def _moba_prep(proj_m, cos, sin):
    s = proj_m.shape[0]
    w = proj_m.shape[1] // 3
    tb = MOBA_BLOCK
    nb = s // tb
    nh = w // MOBA_HEAD_DIM
    vrows = MOBA_HEAD_DIM + V_ONES_ROWS
    blk = lambda off: pl.BlockSpec((tb, w), lambda i, off=off: (i, off))
    tab = pl.BlockSpec((tb, MOBA_HEAD_DIM), lambda i: (i, 0))
    bo = jax.ShapeDtypeStruct((s, w), BF16)
    return pl.pallas_call(
        _moba_prep_kernel,
        grid=(nb,),
        in_specs=[blk(0), blk(1), blk(2), tab, tab],
        out_specs=[blk(0), blk(0), blk(0),
                   pl.BlockSpec((nh, 1, vrows, tb), lambda i: (0, i, 0, 0)),
                   pl.BlockSpec((1, 1, w), lambda i: (i, 0, 0))],
        out_shape=[bo, bo, bo, jax.ShapeDtypeStruct((nh, nb, vrows, tb), BF16),
                   jax.ShapeDtypeStruct((nb, 1, w), F32)],
        compiler_params=_params(("parallel",)),
        name="moba_prep",
    )(proj_m, proj_m, proj_m, cos, sin)


MOBA_TRIP_BLOCKS = 8
MOBA_HEADS_PER_STEP = 4


def _moba_attn_kernel(q_ref, qs_ref, k_ref, vt_ref, km_ref, o_ref, bias_ref, s_ref):
    i = pl.program_id(1)
    tb = MOBA_BLOCK
    nb = km_ref.shape[0]
    hd = MOBA_HEAD_DIM
    heads = range(q_ref.shape[1] // hd)
    cols = [slice(h * hd, (h + 1) * hd) for h in heads]
    q = [qs_ref[:, c] for c in cols]

    own = pl.multiple_of(i * tb, tb)
    s_own = [_dot_nt(k_ref[pl.ds(own, tb), cols[h]], q[h]) for h in heads]

    gates = [_dot_nt(km_ref[:, cols[h]].astype(BF16), q_ref[:, cols[h]]) for h in heads]
    rid = lax.broadcasted_iota(jnp.int32, (nb, tb), 0).astype(F32)
    for h in heads:
        gate = jnp.where(rid < i.astype(F32), gates[h], -jnp.inf)
        bias = jnp.full(gate.shape, -jnp.inf, F32)
        for _ in range(MOBA_TOPK):
            m = jnp.max(gate, axis=0, keepdims=True)
            first = jnp.min(jnp.where(gate == m, rid, float(nb)), axis=0, keepdims=True)
            pick = (rid == first) & (m > -jnp.inf)
            bias = jnp.where(pick, 0.0, bias)
            gate = jnp.where(pick, -jnp.inf, gate)
        bias_ref[h] = bias

    ki = lax.broadcasted_iota(jnp.int32, (tb, tb), 0)
    qi = lax.broadcasted_iota(jnp.int32, (tb, tb), 1)
    s_own = [jnp.where(ki <= qi, s, -jnp.inf).astype(BF16) for s in s_own]
    m_own = tuple(jnp.max(s, axis=0, keepdims=True) for s in s_own)

    def blocks(t):
        return [jnp.minimum(t * MOBA_TRIP_BLOCKS + g, nb - 1) for g in range(MOBA_TRIP_BLOCKS)]

    trips = (i + MOBA_TRIP_BLOCKS - 1) // MOBA_TRIP_BLOCKS

    def score_pass(t, ms):
        out = []
        for h in heads:
            ss = [_dot_nt(k_ref[pl.ds(pl.multiple_of(j * tb, tb), tb), cols[h]], q[h]).astype(BF16)
                  + bias_ref[h, pl.ds(j, 1), :].astype(BF16) for j in blocks(t)]
            m = ms[h]
            for j, sj in zip(blocks(t), ss):
                s_ref[h, j] = sj
                m = jnp.maximum(m, jnp.max(sj, axis=0, keepdims=True))
            out.append(m)
        return tuple(out)

    ms = lax.fori_loop(0, trips, score_pass, m_own)

    def value_pass(t, accs):
        out = []
        for h in heads:
            ps = [jnp.exp2(s_ref[h, j] - ms[h]) for j in blocks(t)]
            acc = accs[h]
            for j, pj in zip(blocks(t), ps):
                acc = acc + _dot(vt_ref[h, j], pj)
            out.append(acc)
        return tuple(out)

    acc0 = tuple(_dot(vt_ref[h, i], jnp.exp2(s_own[h] - ms[h])) for h in heads)
    accs = lax.fori_loop(0, trips, value_pass, acc0)
    for h in heads:
        out = accs[h][:hd, :] / accs[h][hd:hd + 1, :]
        o_ref[:, cols[h]] = out.T.astype(o_ref.dtype)


def _moba_attn(q, qs, k, vt, kmean):
    s, w = q.shape
    tb = MOBA_BLOCK
    nb = s // tb
    nh = MOBA_HEADS_PER_STEP
    hd = nh * MOBA_HEAD_DIM
    vrows = vt.shape[2]
    return pl.pallas_call(
        _moba_attn_kernel,
        grid=(w // hd, nb),
        in_specs=[pl.BlockSpec((tb, hd), lambda h, i: (i, h)),
                  pl.BlockSpec((tb, hd), lambda h, i: (i, h)),
                  pl.BlockSpec((s, hd), lambda h, i: (0, h), pipeline_mode=pl.Buffered(1)),
                  pl.BlockSpec((nh, nb, vrows, tb), lambda h, i: (h, 0, 0, 0),
                               pipeline_mode=pl.Buffered(1)),
                  pl.BlockSpec((nb, hd), lambda h, i: (0, h))],
        out_specs=pl.BlockSpec((tb, hd), lambda h, i: (i, h)),
        out_shape=jax.ShapeDtypeStruct((s, w), BF16),
        scratch_shapes=[pltpu.VMEM((nh, nb, tb), F32), pltpu.VMEM((nh, nb, tb, tb), BF16)],
        compiler_params=_params(("parallel", "arbitrary")),
        name="moba_attn",
    )(q, qs, k, vt, kmean)


def _top_rows(x, n):
    rows = x.shape[0]
    rid = lax.broadcasted_iota(jnp.int32, x.shape, 0).astype(F32)
    vals, idxs = [], []
    for _ in range(n):
        m = jnp.max(x, axis=0, keepdims=True)
        first = jnp.min(jnp.where(x == m, rid, float(rows)), axis=0, keepdims=True)
        vals.append(m)
        idxs.append(first)
        x = jnp.where(rid == first, -jnp.inf, x)
    return jnp.concatenate(vals, axis=0), jnp.concatenate(idxs, axis=0)


def _peer_route_kernel(q_ref, keys_ref, e1_ref, e2_ref, g_ref):
    n = PEER_TOPK
    tm = q_ref.shape[0]
    e1s, e2s, gs = [], [], []
    for h in range(PEER_HEADS):
        tops = []
        for p in range(2):
            hp = 2 * h + p
            cols = slice(hp * PEER_HALF, (hp + 1) * PEER_HALF)
            rows = slice(hp * PEER_N_KEYS, (hp + 1) * PEER_N_KEYS)
            st = _dot_nt(keys_ref[rows, :], q_ref[:, cols].astype(BF16))
            tops.append(_top_rows(st, n))
        (s1, i1), (s2, i2) = tops
        hn = n // 2
        cand = jnp.concatenate([s1[0:1, :] + s2]
                               + [s1[a:a + 1, :] + s2[:hn, :] for a in range(1, hn)]
                               + [s1[hn:, :] + s2[0:1, :]], axis=0)
        f_s, f_pos = _top_rows(cand, n)
        mid = jnp.floor((f_pos - n) * (1.0 / hn))
        tail0 = float(n + (hn - 1) * hn)
        pa = jnp.where(f_pos < n, 0.0, jnp.where(f_pos < tail0, 1.0 + mid, f_pos - tail0 + hn))
        pb = jnp.where(f_pos < n, f_pos, jnp.where(f_pos < tail0, f_pos - n - mid * hn, 0.0))
        e1 = jnp.zeros((n, tm), F32)
        e2 = jnp.zeros((n, tm), F32)
        for a in range(n):
            e1 = jnp.where(pa == float(a), i1[a:a + 1, :], e1)
            e2 = jnp.where(pb == float(a), i2[a:a + 1, :], e2)
        ex = jnp.exp(f_s - f_s[0:1, :])
        gs.append(ex / jnp.sum(ex, axis=0, keepdims=True))
        e1s.append(e1)
        e2s.append(e2)
    e1_ref[...] = jnp.concatenate(e1s, axis=0).T
    e2_ref[...] = jnp.concatenate(e2s, axis=0).T
    g_ref[...] = jnp.concatenate(gs, axis=0).T


def _peer_route(q, keys, tm=256):
    s, w = q.shape
    nsel = PEER_HEADS * PEER_TOPK
    tm = min(tm, s)
    out = jax.ShapeDtypeStruct((s, nsel), F32)
    ob = pl.BlockSpec((tm, nsel), lambda i: (i, 0))
    return pl.pallas_call(
        _peer_route_kernel,
        grid=(s // tm,),
        in_specs=[pl.BlockSpec((tm, w), lambda i: (i, 0)),
                  pl.BlockSpec(keys.shape, lambda i: (0, 0))],
        out_specs=[ob, ob, ob],
        out_shape=[out, out, out],
        compiler_params=_params(("parallel",)),
        name="peer_route",
    )(q, keys)


EXPAND_UNROLL = 16
EXPAND_GROUP = 4


def _peer_expand_kernel(e1_ref, e2_ref, g_ref, w_ref, stage_ref):
    nk = PEER_N_KEYS
    nsel = e1_ref.shape[1]
    rid = lax.broadcasted_iota(jnp.int32, (nk, nsel), 0).astype(F32)
    un = EXPAND_UNROLL

    def body(tt, carry):
        base = pl.multiple_of(tt * un, un)
        e1 = e1_ref[pl.ds(base, un), :]
        e2 = e2_ref[pl.ds(base, un), :]
        g = g_ref[pl.ds(base, un), :]
        for u0 in range(0, un, EXPAND_GROUP):
            us = range(u0, u0 + EXPAND_GROUP)
            lefts = [jnp.where(rid == e1[u:u + 1, :], g[u:u + 1, :], 0.0).astype(BF16) for u in us]
            rights = [jnp.where(rid == e2[u:u + 1, :], 1.0, 0.0).astype(BF16) for u in us]
            for u, l, r in zip(us, lefts, rights):
                stage_ref[u * nk:(u + 1) * nk, :] = _dot_nt(l, r)
        by_key = jnp.swapaxes(stage_ref[...].reshape(un, nk, nk), 0, 1)
        for a in range(nk):
            w_ref[pl.ds(base, un), a * nk:(a + 1) * nk] = by_key[a].astype(w_ref.dtype)
        return carry

    lax.fori_loop(0, e1_ref.shape[0] // un, body, 0)


def _peer_expand(e1, e2, g, tm=128):
    s, nsel = e1.shape
    nk = PEER_N_KEYS
    tm = min(tm, s)
    ib = pl.BlockSpec((tm, nsel), lambda i: (i, 0))
    return pl.pallas_call(
        _peer_expand_kernel,
        grid=(s // tm,),
        in_specs=[ib, ib, ib],
        out_specs=pl.BlockSpec((tm, nk * nk), lambda i: (i, 0)),
        out_shape=jax.ShapeDtypeStruct((s, nk * nk), BF16),
        scratch_shapes=[pltpu.VMEM((EXPAND_UNROLL * nk, nk), F32)],
        compiler_params=_params(("parallel",)),
        name="peer_expand",
    )(e1, e2, g)


FP8 = jnp.float8_e4m3fn
FP8_MAX = 448.0
SCALE_ROWS = F32_SUBLANES


def _rmsnorm_fp8_kernel(x_ref, g_ref, o_ref, o8_ref, s_ref):
    x = x_ref[...]
    ms = jnp.mean(x * x, axis=-1, keepdims=True)
    y = x * lax.rsqrt(ms + NORM_EPS) * g_ref[...]
    o_ref[...] = y.astype(o_ref.dtype)
    scale = jnp.maximum(jnp.max(jnp.abs(y), axis=-1, keepdims=True), 1e-30) * (1.0 / FP8_MAX)
    o8_ref[...] = (y / scale).astype(o8_ref.dtype)
    s_ref[...] = jnp.broadcast_to(scale, s_ref.shape)


def _rmsnorm_fp8(x, g, tm=256):
    s, d = x.shape
    blk = pl.BlockSpec((tm, d), lambda i: (i, 0))
    return pl.pallas_call(
        _rmsnorm_fp8_kernel,
        grid=(s // tm,),
        in_specs=[blk, pl.BlockSpec((1, d), lambda i: (0, 0))],
        out_specs=[blk, blk, pl.BlockSpec((tm, LANES), lambda i: (i, 0))],
        out_shape=[jax.ShapeDtypeStruct((s, d), BF16), jax.ShapeDtypeStruct((s, d), FP8),
                   jax.ShapeDtypeStruct((s, LANES), F32)],
        compiler_params=_params(("parallel",)),
        name="rmsnorm_fp8",
    )(x, g.reshape(1, d))


def _quant_rows_kernel(u_ref, u8_ref, s_ref):
    u = u_ref[...]
    scale = jnp.maximum(jnp.max(jnp.abs(u), axis=-1, keepdims=True), 1e-30) * (1.0 / FP8_MAX)
    u8_ref[...] = (u / scale).astype(u8_ref.dtype)
    s_ref[...] = jnp.broadcast_to(scale, (u.shape[0], LANES)).T[:SCALE_ROWS, :]


def _quant_rows(u, te=512):
    ne, d = u.shape
    return pl.pallas_call(
        _quant_rows_kernel,
        grid=(ne // te,),
        in_specs=[pl.BlockSpec((te, d), lambda e: (e, 0))],
        out_specs=[pl.BlockSpec((te, d), lambda e: (e, 0)),
                   pl.BlockSpec((SCALE_ROWS, te), lambda e: (0, e))],
        out_shape=[jax.ShapeDtypeStruct((ne, d), FP8), jax.ShapeDtypeStruct((SCALE_ROWS, ne), F32)],
        compiler_params=_params(("parallel",)),
        name="peer_quant_u",
    )(u)


def _peer_dense_kernel(h_ref, hs_ref, u_ref, us_ref, v_ref, vs_ref, w_ref, o_ref):
    @pl.when(pl.program_id(1) == 0)
    def _():
        o_ref[...] = jnp.zeros_like(o_ref)

    act = _dot_nt(h_ref[...], u_ref[...]) * hs_ref[:, 0:1] * us_ref[0:1, :]
    gelu = 0.5 * act * (1.0 + lax.erf(act * (2.0 ** -0.5)))
    mix = w_ref[...].astype(F32) * gelu * vs_ref[0:1, :]
    scale = jnp.maximum(jnp.max(jnp.abs(mix), axis=-1, keepdims=True), 1e-30) * (1.0 / FP8_MAX)
    o_ref[...] += _dot((mix / scale).astype(FP8), v_ref[...]) * scale


def _peer_dense(h8, hs, u8, us, v8, vs, wdense, tm=512, te=1024):
    s, d = h8.shape
    ne = u8.shape[0]
    tm = min(tm, s)
    return pl.pallas_call(
        _peer_dense_kernel,
        grid=(s // tm, ne // te),
        in_specs=[pl.BlockSpec((tm, d), lambda i, e: (i, 0)),
                  pl.BlockSpec((tm, LANES), lambda i, e: (i, 0)),
                  pl.BlockSpec((te, d), lambda i, e: (e, 0)),
                  pl.BlockSpec((SCALE_ROWS, te), lambda i, e: (0, e)),
                  pl.BlockSpec((te, d), lambda i, e: (e, 0)),
                  pl.BlockSpec((SCALE_ROWS, te), lambda i, e: (0, e)),
                  pl.BlockSpec((tm, te), lambda i, e: (i, e))],
        out_specs=pl.BlockSpec((tm, d), lambda i, e: (i, 0)),
        out_shape=jax.ShapeDtypeStruct((s, d), F32),
        compiler_params=_params(("parallel", "arbitrary")),
        name="peer_dense",
    )(h8, hs, u8, us, v8, vs, wdense)


def _add_rmsnorm_kernel(x_ref, y_ref, g_ref, o_ref):
    x = x_ref[...] + y_ref[...]
    ms = jnp.mean(x * x, axis=-1, keepdims=True)
    o_ref[...] = x * lax.rsqrt(ms + NORM_EPS) * g_ref[...]


def _add_rmsnorm(x, y, g, tm=256):
    s, d = x.shape
    blk = pl.BlockSpec((tm, d), lambda i: (i, 0))
    return pl.pallas_call(
        _add_rmsnorm_kernel,
        grid=(s // tm,),
        in_specs=[blk, blk, pl.BlockSpec((1, d), lambda i: (0, 0))],
        out_specs=blk,
        out_shape=jax.ShapeDtypeStruct((s, d), F32),
        compiler_params=_params(("parallel",)),
        name="residual_final_norm",
    )(x, y, g.reshape(1, d))


def _rwkv_branch(proj_r, w0, w_up, a0, a_up, g_up, k_k, k_a, r_k, lnx_g, lnx_b):
    g_up_pad = jnp.pad(g_up, ((0, GATE_LORA_PAD - GATE_LORA), (0, 0))).astype(BF16)
    wpre, a, g = _rwkv_lora(proj_r, w0, w_up.astype(BF16), a0, a_up.astype(BF16), g_up_pad)
    rt, bonus, x, y, opre, qb2, bbt, gst, pcm = _rwkv_chunk(proj_r, wpre, a, k_k, k_a, r_k.reshape(-1))
    return _rwkv_scan(x, y, rt, opre, qb2, bbt, gst, pcm, bonus, g, lnx_g, lnx_b)


def _rope_tables(s):
    half = MOBA_HEAD_DIM // 2
    inv_freq = ROPE_THETA ** (-jnp.arange(half, dtype=F32) / half)
    ang = jnp.arange(s, dtype=jnp.int32).astype(F32)[:, None] * inv_freq[None, :]
    cos = jnp.cos(ang)
    sin = jnp.sin(ang)
    return jnp.concatenate([cos, cos], axis=-1), jnp.concatenate([-sin, sin], axis=-1)


def _moba_branch(proj_m):
    s = proj_m.shape[0]
    cos, sin = _rope_tables(s)
    q, qs, k, vt, kmean = _moba_prep(proj_m, cos, sin)
    return _moba_attn(q, qs, k, vt, kmean.reshape(kmean.shape[0], kmean.shape[2]))


def _peer_layer(x1, norm2_g, w_q, sub_keys, expert_u, expert_v, final_g):
    s, d = x1.shape
    h2, h8, hs = _rmsnorm_fp8(x1, norm2_g)
    q = _matmul(h2, w_q.astype(BF16), F32, tm=min(1024, s), tn=512, name="peer_query")
    keys = sub_keys.reshape(PEER_HEADS * 2 * PEER_N_KEYS, PEER_HALF).astype(BF16)
    e1, e2, g = _peer_route(q, keys)
    wdense = _peer_expand(e1, e2, g)
    u8, us = _quant_rows(expert_u)
    v8, vs = _quant_rows(expert_v)
    peer = _peer_dense(h8, hs, u8, us, v8, vs, wdense)
    return _add_rmsnorm(x1, peer, final_g)


def kernel(x, norm1_g, w_in, rwkv_mu, rwkv_w0, rwkv_w_up, rwkv_a0, rwkv_a_up, rwkv_g_up, rwkv_k_k, rwkv_k_a, rwkv_r_k, rwkv_lnx_g, rwkv_lnx_b, w_branch_rwkv, w_branch_moba, w_out, norm2_g, peer_w_q, peer_sub_keys, peer_u, peer_v, final_g):
    b, s, d = x.shape
    depth = w_in.shape[0]
    assert b == 1 and depth == 1
    x2d = x.reshape(s, d)
    l = 0
    rw = rwkv_w0.shape[-1]
    shift_w = 3 * rw + DECAY_LORA + AAA_LORA + GATE_LORA
    tn = 512
    shift_pad = -(-shift_w // tn) * tn
    mw = w_branch_moba.shape[1]
    tm = min(1024, s)

    h = _rmsnorm(x2d, norm1_g[l], BF16)
    wt = jnp.transpose(w_in[l])
    mu = jnp.pad(rwkv_mu[l], (0, shift_pad - shift_w)).reshape(1, shift_pad)
    proj_r = _matmul_shift(h, wt, mu, tm=tm, tn=tn)
    proj_m = _matmul_nt(h, wt, F32, tm=tm, tn=tn, name="in_proj_moba", row0=shift_w, n=3 * mw)
    gates = _matmul_nt(h, wt, F32, tm=tm, tn=tn, name="in_proj_gates", row0=shift_w + 3 * mw, n=2 * d)

    y_a = _rwkv_branch(proj_r, rwkv_w0[l], rwkv_w_up[l], rwkv_a0[l], rwkv_a_up[l], rwkv_g_up[l],
                       rwkv_k_k[l], rwkv_k_a[l], rwkv_r_k[l], rwkv_lnx_g[l], rwkv_lnx_b[l])
    y_b = _moba_branch(proj_m)
    mixed = _gated_pair(y_a, w_branch_rwkv[l].astype(BF16), y_b, w_branch_moba[l].astype(BF16),
                        gates, tm=tm, tn=512)
    x1 = _matmul_residual(mixed, w_out[l].astype(BF16), x2d, tm=tm, tn=512)
    out = _peer_layer(x1, norm2_g[l], peer_w_q[l], peer_sub_keys[l], peer_u[l], peer_v[l], final_g)
    return out.reshape(b, s, d)
```

```python
import jax
import jax.numpy as jnp
from jax import lax
from jax.experimental import pallas as pl
from jax.experimental.pallas import tpu as pltpu

F32 = jnp.float32
BF16 = jnp.bfloat16

NORM_EPS = 1e-6
LANES = 128
F32_SUBLANES = 8
RWKV_HEAD_DIM = 64
DECAY_LORA = 128
AAA_LORA = 128
GATE_LORA = 480
GATE_LORA_PAD = 512
LN_X_EPS = 64e-5
CHUNK = 128
MOBA_HEAD_DIM = 128
MOBA_BLOCK = 256
MOBA_TOPK = 3
ROPE_THETA = 10000.0
PEER_HEADS = 8
PEER_N_KEYS = 128
PEER_HALF = 128
PEER_TOPK = 16

VMEM_LIMIT = 56 * 1024 * 1024

_NT = (((1,), (1,)), ((), ()))


def _params(sem, vmem=VMEM_LIMIT):
    return pltpu.CompilerParams(dimension_semantics=sem, vmem_limit_bytes=vmem)


def _dot(a, b):
    return jnp.dot(a, b, preferred_element_type=F32)


def _dot_nt(a, b):
    return lax.dot_general(a, b, _NT, preferred_element_type=F32)


def _split3(x):
    hi = x.astype(BF16)
    r1 = x - hi.astype(F32)
    mid = r1.astype(BF16)
    lo = (r1 - mid.astype(F32)).astype(BF16)
    return hi, mid, lo


def _dot_r3(x, r_bf16):
    hi, mid, lo = _split3(x)
    return _dot(hi, r_bf16) + _dot(mid, r_bf16) + _dot(lo, r_bf16)


def _rmsnorm_kernel(x_ref, g_ref, o_ref):
    x = x_ref[...]
    ms = jnp.mean(x * x, axis=-1, keepdims=True)
    o_ref[...] = (x * lax.rsqrt(ms + NORM_EPS) * g_ref[...]).astype(o_ref.dtype)


def _rmsnorm(x, g, out_dtype, tm=256):
    s, d = x.shape
    return pl.pallas_call(
        _rmsnorm_kernel,
        grid=(s // tm,),
        in_specs=[pl.BlockSpec((tm, d), lambda i: (i, 0)),
                  pl.BlockSpec((1, d), lambda i: (0, 0))],
        out_specs=pl.BlockSpec((tm, d), lambda i: (i, 0)),
        out_shape=jax.ShapeDtypeStruct((s, d), out_dtype),
        compiler_params=_params(("parallel",)),
        name="rmsnorm",
    )(x, g.reshape(1, d))


def _mm_kernel(a_ref, b_ref, o_ref):
    o_ref[...] = _dot(a_ref[...], b_ref[...]).astype(o_ref.dtype)


def _matmul(a, b, out_dtype, tm, tn, name, col0=0, n=None):
    m, k = a.shape
    n = b.shape[1] if n is None else n
    off = col0 // tn
    return pl.pallas_call(
        _mm_kernel,
        grid=(m // tm, n // tn),
        in_specs=[pl.BlockSpec((tm, k), lambda i, j: (i, 0)),
                  pl.BlockSpec((k, tn), lambda i, j: (0, j + off))],
        out_specs=pl.BlockSpec((tm, tn), lambda i, j: (i, j)),
        out_shape=jax.ShapeDtypeStruct((m, n), out_dtype),
        compiler_params=_params(("parallel", "parallel")),
        name=name,
    )(a, b)


BF16_SUBLANES = 16


def _mm_nt_kernel(a_ref, bt_ref, o_ref, b16_ref):
    @pl.when(pl.program_id(1) == 0)
    def _():
        b16_ref[...] = bt_ref[...].astype(b16_ref.dtype)

    o_ref[...] = _dot_nt(a_ref[...], b16_ref[...]).astype(o_ref.dtype)


def _matmul_nt(a, bt, out_dtype, tm, tn, name, row0, n):
    m, k = a.shape
    return pl.pallas_call(
        _mm_nt_kernel,
        grid=(n // tn, m // tm),
        in_specs=[pl.BlockSpec((tm, k), lambda j, i: (i, 0)),
                  pl.BlockSpec((pl.Element(tn), pl.Element(k)),
                               lambda j, i: (pl.multiple_of(row0 + j * tn, F32_SUBLANES), 0))],
        out_specs=pl.BlockSpec((tm, tn), lambda j, i: (i, j)),
        out_shape=jax.ShapeDtypeStruct((m, n), out_dtype),
        scratch_shapes=[pltpu.VMEM((tn, k), BF16)],
        compiler_params=_params(("parallel", "arbitrary")),
        name=name,
    )(a, bt)


def _mm_shift_kernel(a_ref, ap_ref, bt_ref, mu_ref, o_ref, b16_ref):
    i = pl.program_id(1)

    @pl.when(i == 0)
    def _():
        b16_ref[...] = bt_ref[...].astype(b16_ref.dtype)

    b = b16_ref[...]
    z = _dot_nt(a_ref[...], b)
    zp = _dot_nt(ap_ref[...], b)
    prev = jnp.where(i > 0, zp[BF16_SUBLANES - 1:BF16_SUBLANES, :], 0.0)
    row = lax.broadcasted_iota(jnp.int32, z.shape, 0)
    z_prev = jnp.where(row == 0, prev, pltpu.roll(z, 1, axis=0))
    o_ref[...] = z + (z_prev - z) * mu_ref[...]


def _matmul_shift(a, bt, mu, tm, tn):
    m, k = a.shape
    n = mu.shape[1]
    per = tm // BF16_SUBLANES
    return pl.pallas_call(
        _mm_shift_kernel,
        grid=(n // tn, m // tm),
        in_specs=[pl.BlockSpec((tm, k), lambda j, i: (i, 0)),
                  pl.BlockSpec((BF16_SUBLANES, k), lambda j, i: (jnp.maximum(i * per - 1, 0), 0)),
                  pl.BlockSpec((tn, k), lambda j, i: (j, 0)),
                  pl.BlockSpec((1, tn), lambda j, i: (0, j))],
        out_specs=pl.BlockSpec((tm, tn), lambda j, i: (i, j)),
        out_shape=jax.ShapeDtypeStruct((m, n), F32),
        scratch_shapes=[pltpu.VMEM((tn, k), BF16)],
        compiler_params=_params(("parallel", "arbitrary")),
        name="in_proj_rwkv",
    )(a, a, bt, mu)


def _gated_pair_kernel(ya_ref, wa_ref, yb_ref, wb_ref, ga_ref, gb_ref, o_ref):
    pa = _dot(ya_ref[...], wa_ref[...])
    pb = _dot(yb_ref[...], wb_ref[...])
    mixed = (jax.nn.sigmoid(ga_ref[...].astype(F32)) * pa
             + jax.nn.sigmoid(gb_ref[...].astype(F32)) * pb)
    o_ref[...] = mixed.astype(o_ref.dtype)


def _gated_pair(ya, wa, yb, wb, gates, tm, tn):
    m, k = ya.shape
    n = wa.shape[1]
    nb = n // tn
    return pl.pallas_call(
        _gated_pair_kernel,
        grid=(m // tm, nb),
        in_specs=[pl.BlockSpec((tm, k), lambda i, j: (i, 0)),
                  pl.BlockSpec((k, tn), lambda i, j: (0, j)),
                  pl.BlockSpec((tm, k), lambda i, j: (i, 0)),
                  pl.BlockSpec((k, tn), lambda i, j: (0, j)),
                  pl.BlockSpec((tm, tn), lambda i, j: (i, j)),
                  pl.BlockSpec((tm, tn), lambda i, j: (i, j + nb))],
        out_specs=pl.BlockSpec((tm, tn), lambda i, j: (i, j)),
        out_shape=jax.ShapeDtypeStruct((m, n), BF16),
        compiler_params=_params(("parallel", "parallel")),
        name="branch_merge",
    )(ya, wa, yb, wb, gates, gates)


def _mm_res_kernel(a_ref, b_ref, r_ref, o_ref):
    o_ref[...] = r_ref[...] + _dot(a_ref[...], b_ref[...])


def _matmul_residual(a, b, res, tm, tn):
    m, k = a.shape
    n = b.shape[1]
    return pl.pallas_call(
        _mm_res_kernel,
        grid=(m // tm, n // tn),
        in_specs=[pl.BlockSpec((tm, k), lambda i, j: (i, 0)),
                  pl.BlockSpec((k, tn), lambda i, j: (0, j)),
                  pl.BlockSpec((tm, tn), lambda i, j: (i, j))],
        out_specs=pl.BlockSpec((tm, tn), lambda i, j: (i, j)),
        out_shape=jax.ShapeDtypeStruct((m, n), F32),
        compiler_params=_params(("parallel", "parallel")),
        name="out_proj",
    )(a, b, res)


def _rwkv_lora_kernel(wl_ref, al_ref, gl0_ref, gl1_ref, w0_ref, wup_ref, a0_ref, aup_ref,
                      gup_ref, wpre_ref, a_ref, g_ref):
    half = GATE_LORA_PAD // 2
    wpre_ref[...] = w0_ref[...] + _dot(jnp.tanh(wl_ref[...]).astype(BF16), wup_ref[...])
    a_ref[...] = jax.nn.sigmoid(a0_ref[...] + _dot(al_ref[...].astype(BF16), aup_ref[...]))
    g_ref[...] = (_dot(jax.nn.sigmoid(gl0_ref[...]).astype(BF16), gup_ref[:half, :])
                  + _dot(jax.nn.sigmoid(gl1_ref[...]).astype(BF16), gup_ref[half:, :]))


def _rwkv_lora(proj_r, w0, w_up, a0, a_up, g_up_pad, tm=256):
    s = proj_r.shape[0]
    w = w0.shape[-1]
    half = GATE_LORA_PAD // 2
    c_wl = 3 * w // DECAY_LORA
    c_al = (3 * w + DECAY_LORA) // AAA_LORA
    c_gl = (3 * w + DECAY_LORA + AAA_LORA) // half
    row = lambda i: (i, 0)
    const = lambda i: (0, 0)
    out = jax.ShapeDtypeStruct((s, w), F32)
    return pl.pallas_call(
        _rwkv_lora_kernel,
        grid=(s // tm,),
        in_specs=[pl.BlockSpec((tm, DECAY_LORA), lambda i: (i, c_wl)),
                  pl.BlockSpec((tm, AAA_LORA), lambda i: (i, c_al)),
                  pl.BlockSpec((tm, half), lambda i: (i, c_gl)),
                  pl.BlockSpec((tm, half), lambda i: (i, c_gl + 1)),
                  pl.BlockSpec((1, w), const),
                  pl.BlockSpec((DECAY_LORA, w), const),
                  pl.BlockSpec((1, w), const),
                  pl.BlockSpec((AAA_LORA, w), const),
                  pl.BlockSpec((GATE_LORA_PAD, w), const)],
        out_specs=[pl.BlockSpec((tm, w), row)] * 3,
        out_shape=[out, out, out],
        compiler_params=_params(("parallel",)),
        name="rwkv_lora",
    )(proj_r, proj_r, proj_r, proj_r, w0.reshape(1, w), w_up, a0.reshape(1, w), a_up, g_up_pad)


def _head_pair_ones(width=LANES):
    r = lax.broadcasted_iota(jnp.int32, (width, width), 0) // RWKV_HEAD_DIM
    c = lax.broadcasted_iota(jnp.int32, (width, width), 1) // RWKV_HEAD_DIM
    return r == c


def _rwkv_prep(r_ref, k_ref, v_ref, wpre_ref, a_ref, kk_ref, ka_ref, rk_ref,
               rt_ref, at_ref, kh_ref, bh_ref, kb_ref, bb_ref, pc_ref, bonus_ref):
    tm = r_ref.shape[0]
    r = r_ref[...]
    k = k_ref[...]
    v = v_ref[...]
    a = a_ref[...]
    same_head = jnp.where(_head_pair_ones(r.shape[1]), 1.0, 0.0).astype(BF16)

    x = -wpre_ref[...]
    softplus = jnp.maximum(x, 0.0) + jnp.log1p(jnp.exp(-jnp.abs(x)))
    lw = -jnp.exp(-softplus - 0.5)

    kk = k * kk_ref[...]
    ss = _dot_r3(kk * kk, same_head)
    kkn = kk / jnp.maximum(jnp.sqrt(ss), 1e-12)
    k2 = k * (1.0 + (a - 1.0) * ka_ref[...])
    bonus_ref[...] = _dot_r3(r * k2 * rk_ref[...], same_head) * v

    ri = lax.broadcasted_iota(jnp.int32, (tm, tm), 0)
    ci = lax.broadcasted_iota(jnp.int32, (tm, tm), 1)
    same_chunk = (ri // CHUNK) == (ci // CHUNK)
    tri = jnp.where(same_chunk & (ci <= ri), 1.0, 0.0).astype(BF16)
    allc = jnp.where(same_chunk, 1.0, 0.0).astype(BF16)
    lw_parts = _split3(lw)
    cum = sum(_dot(tri, part) for part in lw_parts)
    tot = sum(_dot(allc, part) for part in lw_parts)

    p_in = jnp.exp(cum)
    p_inv = jnp.exp(-cum)
    p_tail = jnp.exp(tot - cum)
    b = kkn * a
    rt_ref[...] = r * p_in
    at_ref[...] = -kkn * jnp.exp(cum - lw)
    kh_ref[...] = k2 * p_inv
    bh_ref[...] = b * p_inv
    kb_ref[...] = k2 * p_tail
    bb_ref[...] = b * p_tail
    pc_ref[...] = jnp.exp(tot)


def _rwkv_chunk_kernel(r_ref, k_ref, v_ref, wpre_ref, a_ref, kk_ref, ka_ref, rk_ref,
                       rt_ref, bonus_ref, x_ref, y_ref, op_ref, qb_ref, bbt_ref, g_ref, pcm_ref,
                       at_ref, kh_ref, bh_ref, kb_ref, bb_ref, pc_ref):
    _rwkv_prep(r_ref, k_ref, v_ref, wpre_ref, a_ref, kk_ref, ka_ref, rk_ref,
               rt_ref, at_ref, kh_ref, bh_ref, kb_ref, bb_ref, pc_ref, bonus_ref)
    c = CHUNK
    nheads = LANES // RWKV_HEAD_DIM
    chunks = range(rt_ref.shape[0] // c)
    pairs = range(rt_ref.shape[1] // LANES)
    rows = [slice(q * c, (q + 1) * c) for q in chunks]
    lanes = [slice(p * LANES, (p + 1) * LANES) for p in pairs]
    tiles = [(q, p) for q in chunks for p in pairs]
    probs = [(n, h) for n in range(len(tiles)) for h in range(nheads)]
    lane = lax.broadcasted_iota(jnp.int32, (c, LANES), 1)
    ri = lax.broadcasted_iota(jnp.int32, (c, c), 0)
    ci = lax.broadcasted_iota(jnp.int32, (c, c), 1)
    strict = ci < ri
    incl = ci <= ri
    eye = jnp.where(ri == ci, 1.0, 0.0)
    head_mask = [(lane // RWKV_HEAD_DIM) == h for h in range(nheads)]
    tile = lambda ref, n: ref[rows[tiles[n][0]], lanes[tiles[n][1]]]

    kh = [tile(kh_ref, n).astype(BF16) for n in range(len(tiles))]
    bh = [tile(bh_ref, n).astype(BF16) for n in range(len(tiles))]
    a_h = [jnp.where(head_mask[h], tile(at_ref, n), 0.0).astype(BF16) for n, h in probs]
    r_h = [jnp.where(head_mask[h], tile(rt_ref, n), 0.0).astype(BF16) for n, h in probs]
    v_h = [jnp.where(head_mask[h], tile(v_ref, n), 0.0).astype(BF16) for n, h in probs]
    n_ab = [jnp.where(strict, _dot_nt(a, bh[n]), 0.0) for a, (n, _) in zip(a_h, probs)]
    n_ak = [jnp.where(strict, _dot_nt(a, kh[n]), 0.0).astype(BF16) for a, (n, _) in zip(a_h, probs)]
    q_k = [jnp.where(incl, _dot_nt(r, kh[n]), 0.0).astype(BF16) for r, (n, _) in zip(r_h, probs)]
    q_b = [jnp.where(incl, _dot_nt(r, bh[n]), 0.0) for r, (n, _) in zip(r_h, probs)]
    for qb, (n, h) in zip(q_b, probs):
        q, p = tiles[n]
        col = (p * nheads + h) * c
        qb_ref[rows[q], col:col + c] = qb.astype(qb_ref.dtype)

    t = [eye + n for n in n_ab]
    pw = n_ab
    for _ in range((c - 1).bit_length() - 1):
        pb = [x.astype(BF16) for x in pw]
        pw = [_dot(x, x) for x in pb]
        t = [ti + _dot(ti.astype(BF16), pi.astype(BF16)) for ti, pi in zip(t, pw)]
    tb = [ti.astype(BF16) for ti in t]
    xs = [_dot(ti, a) for ti, a in zip(tb, a_h)]
    nv = [_dot(n, v).astype(BF16) for n, v in zip(n_ak, v_h)]
    ys = [_dot(ti, z) for ti, z in zip(tb, nv)]
    os = [_dot(qk, v) for qk, v in zip(q_k, v_h)]
    same_head = _head_pair_ones()
    for n, (q, p) in enumerate(tiles):
        mine = [m for m, (nn, _) in enumerate(probs) if nn == n]
        x_ref[rows[q], lanes[p]] = sum(xs[m] for m in mine).astype(x_ref.dtype)
        y_ref[rows[q], lanes[p]] = sum(ys[m] for m in mine)
        op_ref[rows[q], lanes[p]] = sum(os[m] for m in mine)
        kbt = tile(kb_ref, n).T.astype(BF16)
        g_ref[rows[q], lanes[p]] = jnp.where(same_head, _dot(kbt, tile(v_ref, n).astype(BF16)), 0.0)
        bbt_ref[lanes[p], rows[q]] = tile(bb_ref, n).T.astype(bbt_ref.dtype)
        pcm_ref[rows[q], lanes[p]] = tile(pc_ref, n).T


def _rwkv_chunk(proj_r, wpre, a, k_k, k_a, r_k, tc=4 * CHUNK, tw=2 * LANES):
    s, w = wpre.shape
    ncol = w // tw
    blk = lambda off: pl.BlockSpec((tc, tw), lambda i, j, off=off: (i, j + off))
    vec = pl.BlockSpec((1, tw), lambda i, j: (0, j))
    f32o = jax.ShapeDtypeStruct((s, w), F32)
    return pl.pallas_call(
        _rwkv_chunk_kernel,
        grid=(s // tc, ncol),
        in_specs=[blk(0), blk(ncol), blk(2 * ncol), blk(0), blk(0), vec, vec, vec],
        out_specs=[blk(0), blk(0), blk(0), blk(0), blk(0),
                   pl.BlockSpec((tc, 2 * tw), lambda i, j: (i, j)),
                   pl.BlockSpec((tw, tc), lambda i, j: (j, i)),
                   blk(0), blk(0)],
        out_shape=[f32o, f32o, jax.ShapeDtypeStruct((s, w), BF16), f32o, f32o,
                   jax.ShapeDtypeStruct((s, 2 * w), BF16),
                   jax.ShapeDtypeStruct((w, s), BF16),
                   f32o, f32o],
        scratch_shapes=[pltpu.VMEM((tc, tw), F32)] * 6,
        compiler_params=_params(("parallel", "parallel")),
        name="rwkv_chunk",
    )(proj_r, proj_r, proj_r, wpre, a, k_k.reshape(1, w), k_a.reshape(1, w), r_k.reshape(1, w))


def _rwkv_scan_kernel(x_ref, y_ref, rt_ref, op_ref, qb_ref, bbt_ref, g_ref, pcm_ref,
                      bonus_ref, gate_ref, lng_ref, lnb_ref, ya_ref, h_ref):
    c = CHUNK
    pairs = range(x_ref.shape[1] // LANES)
    lanes = [slice(p * LANES, (p + 1) * LANES) for p in pairs]

    @pl.when(pl.program_id(1) == 0)
    def _():
        h_ref[...] = jnp.zeros_like(h_ref)

    lane = lax.broadcasted_iota(jnp.int32, (c, LANES), 1)
    first = lane < RWKV_HEAD_DIM
    same_head = _head_pair_ones()
    ones_head = jnp.where(same_head, 1.0, 0.0).astype(BF16)
    inv_n = 1.0 / RWKV_HEAD_DIM
    for q in range(x_ref.shape[0] // c):
        rows = slice(q * c, (q + 1) * c)
        hs = [h_ref[p] for p in pairs]
        hb = [h.astype(BF16) for h in hs]
        us = [_dot(x_ref[rows, lanes[p]], hb[p]) + y_ref[rows, lanes[p]] for p in pairs]
        u2 = [jnp.concatenate([jnp.where(first, u, 0.0), jnp.where(first, 0.0, u)], axis=0).astype(BF16)
              for u in us]
        os = [_dot(rt_ref[rows, lanes[p]].astype(BF16), hb[p]) + op_ref[rows, lanes[p]]
              + _dot(qb_ref[rows, p * 2 * c:(p + 1) * 2 * c], u2[p]) for p in pairs]
        upd = [_dot(bbt_ref[lanes[p], rows], us[p].astype(BF16)) for p in pairs]
        for p in pairs:
            h_ref[p] = (pcm_ref[rows, lanes[p]] * hs[p] + g_ref[rows, lanes[p]]
                        + jnp.where(same_head, upd[p], 0.0))
        mus = [_dot_r3(o, ones_head) * inv_n for o in os]
        ds = [o - mu for o, mu in zip(os, mus)]
        vs = [_dot_r3(d * d, ones_head) * inv_n for d in ds]
        for p in pairs:
            y = ds[p] * lax.rsqrt(vs[p] + LN_X_EPS) * lng_ref[:, lanes[p]] + lnb_ref[:, lanes[p]]
            ya_ref[rows, lanes[p]] = ((y + bonus_ref[rows, lanes[p]])
                                      * gate_ref[rows, lanes[p]]).astype(ya_ref.dtype)


def _rwkv_scan(x, y, rt, opre, qb2, bbt, g, pcm, bonus, gate, lnx_g, lnx_b, ts=512, tw=4 * LANES):
    s, w = y.shape
    ts = min(ts, s)
    blk = pl.BlockSpec((ts, tw), lambda j, i: (i, j))
    vec = pl.BlockSpec((1, tw), lambda j, i: (0, j))
    return pl.pallas_call(
        _rwkv_scan_kernel,
        grid=(w // tw, s // ts),
        in_specs=[blk, blk, blk, blk,
                  pl.BlockSpec((ts, 2 * tw), lambda j, i: (i, j)),
                  pl.BlockSpec((tw, ts), lambda j, i: (j, i)),
                  blk, blk, blk, blk, vec, vec],
        out_specs=blk,
        out_shape=jax.ShapeDtypeStruct((s, w), BF16),
        scratch_shapes=[pltpu.VMEM((tw // LANES, LANES, LANES), F32)],
        compiler_params=_params(("parallel", "arbitrary")),
        name="rwkv_scan",
    )(x, y, rt, opre, qb2, bbt, g, pcm, bonus, gate, lnx_g.reshape(1, w), lnx_b.reshape(1, w))


V_ONES_ROWS = 16


LOG2_E = 1.4426950408889634
MOBA_LOG2_SCALE = (MOBA_HEAD_DIM ** -0.5) * LOG2_E


def _moba_prep_kernel(q_ref, k_ref, v_ref, cos_ref, sin_ref, qo_ref, qs_ref, ko_ref, vt_ref, km_ref):
    cos = cos_ref[...]
    sin = sin_ref[...]
    hd = MOBA_HEAD_DIM
    half = hd // 2
    tb = q_ref.shape[0]
    for h in range(q_ref.shape[1] // hd):
        cols = slice(h * hd, (h + 1) * hd)
        q = q_ref[:, cols]
        k = k_ref[:, cols]
        qr = q * cos + pltpu.roll(q, half, axis=1) * sin
        kr = k * cos + pltpu.roll(k, half, axis=1) * sin
        qo_ref[:, cols] = qr.astype(qo_ref.dtype)
        qs_ref[:, cols] = (qr * MOBA_LOG2_SCALE).astype(qs_ref.dtype)
        ko_ref[:, cols] = kr.astype(ko_ref.dtype)
        km_ref[0, :, cols] = jnp.mean(kr, axis=0, keepdims=True)
        vt_ref[h, 0, :hd, :] = v_ref[:, cols].T.astype(vt_ref.dtype)
        vt_ref[h, 0, hd:, :] = jnp.ones((V_ONES_ROWS, tb), vt_ref.dtype)


def _moba_prep(proj_m, cos, sin):
    s = proj_m.shape[0]
    w = proj_m.shape[1] // 3
    tb = MOBA_BLOCK
    nb = s // tb
    nh = w // MOBA_HEAD_DIM
    vrows = MOBA_HEAD_DIM + V_ONES_ROWS
    blk = lambda off: pl.BlockSpec((tb, w), lambda i, off=off: (i, off))
    tab = pl.BlockSpec((tb, MOBA_HEAD_DIM), lambda i: (i, 0))
    bo = jax.ShapeDtypeStruct((s, w), BF16)
    return pl.pallas_call(
        _moba_prep_kernel,
        grid=(nb,),
        in_specs=[blk(0), blk(1), blk(2), tab, tab],
        out_specs=[blk(0), blk(0), blk(0),
                   pl.BlockSpec((nh, 1, vrows, tb), lambda i: (0, i, 0, 0)),
                   pl.BlockSpec((1, 1, w), lambda i: (i, 0, 0))],
        out_shape=[bo, bo, bo, jax.ShapeDtypeStruct((nh, nb, vrows, tb), BF16),
                   jax.ShapeDtypeStruct((nb, 1, w), F32)],
        compiler_params=_params(("parallel",)),
        name="moba_prep",
    )(proj_m, proj_m, proj_m, cos, sin)


MOBA_TRIP_BLOCKS = 8
MOBA_HEADS_PER_STEP = 4


def _moba_attn_kernel(q_ref, qs_ref, k_ref, vt_ref, km_ref, o_ref, bias_ref, s_ref):
    i = pl.program_id(1)
    tb = MOBA_BLOCK
    nb = km_ref.shape[0]
    hd = MOBA_HEAD_DIM
    heads = range(q_ref.shape[1] // hd)
    cols = [slice(h * hd, (h + 1) * hd) for h in heads]
    q = [qs_ref[:, c] for c in cols]

    own = pl.multiple_of(i * tb, tb)
    s_own = [_dot_nt(k_ref[pl.ds(own, tb), cols[h]], q[h]) for h in heads]

    gates = [_dot_nt(km_ref[:, cols[h]].astype(BF16), q_ref[:, cols[h]]) for h in heads]
    rid = lax.broadcasted_iota(jnp.int32, (nb, tb), 0).astype(F32)
    for h in heads:
        gate = jnp.where(rid < i.astype(F32), gates[h], -jnp.inf)
        bias = jnp.full(gate.shape, -jnp.inf, F32)
        for _ in range(MOBA_TOPK):
            m = jnp.max(gate, axis=0, keepdims=True)
            first = jnp.min(jnp.where(gate == m, rid, float(nb)), axis=0, keepdims=True)
            pick = (rid == first) & (m > -jnp.inf)
            bias = jnp.where(pick, 0.0, bias)
            gate = jnp.where(pick, -jnp.inf, gate)
        bias_ref[h] = bias

    ki = lax.broadcasted_iota(jnp.int32, (tb, tb), 0)
    qi = lax.broadcasted_iota(jnp.int32, (tb, tb), 1)
    s_own = [jnp.where(ki <= qi, s, -jnp.inf).astype(BF16) for s in s_own]
    m_own = tuple(jnp.max(s, axis=0, keepdims=True) for s in s_own)

    def blocks(t):
        return [jnp.minimum(t * MOBA_TRIP_BLOCKS + g, nb - 1) for g in range(MOBA_TRIP_BLOCKS)]

    trips = (i + MOBA_TRIP_BLOCKS - 1) // MOBA_TRIP_BLOCKS

    def score_pass(t, ms):
        out = []
        for h in heads:
            ss = [_dot_nt(k_ref[pl.ds(pl.multiple_of(j * tb, tb), tb), cols[h]], q[h]).astype(BF16)
                  + bias_ref[h, pl.ds(j, 1), :].astype(BF16) for j in blocks(t)]
            m = ms[h]
            for j, sj in zip(blocks(t), ss):
                s_ref[h, j] = sj
                m = jnp.maximum(m, jnp.max(sj, axis=0, keepdims=True))
            out.append(m)
        return tuple(out)

    ms = lax.fori_loop(0, trips, score_pass, m_own)

    def value_pass(t, accs):
        out = []
        for h in heads:
            ps = [jnp.exp2(s_ref[h, j] - ms[h]) for j in blocks(t)]
            acc = accs[h]
            for j, pj in zip(blocks(t), ps):
                acc = acc + _dot(vt_ref[h, j], pj)
            out.append(acc)
        return tuple(out)

    acc0 = tuple(_dot(vt_ref[h, i], jnp.exp2(s_own[h] - ms[h])) for h in heads)
    accs = lax.fori_loop(0, trips, value_pass, acc0)
    for h in heads:
        out = accs[h][:hd, :] / accs[h][hd:hd + 1, :]
        o_ref[:, cols[h]] = out.T.astype(o_ref.dtype)


def _moba_attn(q, qs, k, vt, kmean):
    s, w = q.shape
    tb = MOBA_BLOCK
    nb = s // tb
    nh = MOBA_HEADS_PER_STEP
    hd = nh * MOBA_HEAD_DIM
    vrows = vt.shape[2]
    return pl.pallas_call(
        _moba_attn_kernel,
        grid=(w // hd, nb),
        in_specs=[pl.BlockSpec((tb, hd), lambda h, i: (i, h)),
                  pl.BlockSpec((tb, hd), lambda h, i: (i, h)),
                  pl.BlockSpec((s, hd), lambda h, i: (0, h), pipeline_mode=pl.Buffered(1)),
                  pl.BlockSpec((nh, nb, vrows, tb), lambda h, i: (h, 0, 0, 0),
                               pipeline_mode=pl.Buffered(1)),
                  pl.BlockSpec((nb, hd), lambda h, i: (0, h))],
        out_specs=pl.BlockSpec((tb, hd), lambda h, i: (i, h)),
        out_shape=jax.ShapeDtypeStruct((s, w), BF16),
        scratch_shapes=[pltpu.VMEM((nh, nb, tb), F32), pltpu.VMEM((nh, nb, tb, tb), BF16)],
        compiler_params=_params(("parallel", "arbitrary")),
        name="moba_attn",
    )(q, qs, k, vt, kmean)


def _top_rows(x, n):
    rows = x.shape[0]
    rid = lax.broadcasted_iota(jnp.int32, x.shape, 0).astype(F32)
    vals, idxs = [], []
    for _ in range(n):
        m = jnp.max(x, axis=0, keepdims=True)
        first = jnp.min(jnp.where(x == m, rid, float(rows)), axis=0, keepdims=True)
        vals.append(m)
        idxs.append(first)
        x = jnp.where(rid == first, -jnp.inf, x)
    return jnp.concatenate(vals, axis=0), jnp.concatenate(idxs, axis=0)


def _peer_route_kernel(q_ref, keys_ref, e1_ref, e2_ref, g_ref):
    n = PEER_TOPK
    tm = q_ref.shape[0]
    e1s, e2s, gs = [], [], []
    for h in range(PEER_HEADS):
        tops = []
        for p in range(2):
            hp = 2 * h + p
            cols = slice(hp * PEER_HALF, (hp + 1) * PEER_HALF)
            rows = slice(hp * PEER_N_KEYS, (hp + 1) * PEER_N_KEYS)
            st = _dot_nt(keys_ref[rows, :], q_ref[:, cols].astype(BF16))
            tops.append(_top_rows(st, n))
        (s1, i1), (s2, i2) = tops
        hn = n // 2
        cand = jnp.concatenate([s1[0:1, :] + s2]
                               + [s1[a:a + 1, :] + s2[:hn, :] for a in range(1, hn)]
                               + [s1[hn:, :] + s2[0:1, :]], axis=0)
        f_s, f_pos = _top_rows(cand, n)
        mid = jnp.floor((f_pos - n) * (1.0 / hn))
        tail0 = float(n + (hn - 1) * hn)
        pa = jnp.where(f_pos < n, 0.0, jnp.where(f_pos < tail0, 1.0 + mid, f_pos - tail0 + hn))
        pb = jnp.where(f_pos < n, f_pos, jnp.where(f_pos < tail0, f_pos - n - mid * hn, 0.0))
        e1 = jnp.zeros((n, tm), F32)
        e2 = jnp.zeros((n, tm), F32)
        for a in range(n):
            e1 = jnp.where(pa == float(a), i1[a:a + 1, :], e1)
            e2 = jnp.where(pb == float(a), i2[a:a + 1, :], e2)
        ex = jnp.exp(f_s - f_s[0:1, :])
        gs.append(ex / jnp.sum(ex, axis=0, keepdims=True))
        e1s.append(e1)
        e2s.append(e2)
    e1_ref[...] = jnp.concatenate(e1s, axis=0).T
    e2_ref[...] = jnp.concatenate(e2s, axis=0).T
    g_ref[...] = jnp.concatenate(gs, axis=0).T


def _peer_route(q, keys, tm=256):
    s, w = q.shape
    nsel = PEER_HEADS * PEER_TOPK
    tm = min(tm, s)
    out = jax.ShapeDtypeStruct((s, nsel), F32)
    ob = pl.BlockSpec((tm, nsel), lambda i: (i, 0))
    return pl.pallas_call(
        _peer_route_kernel,
        grid=(s // tm,),
        in_specs=[pl.BlockSpec((tm, w), lambda i: (i, 0)),
                  pl.BlockSpec(keys.shape, lambda i: (0, 0))],
        out_specs=[ob, ob, ob],
        out_shape=[out, out, out],
        compiler_params=_params(("parallel",)),
        name="peer_route",
    )(q, keys)


EXPAND_UNROLL = 16
EXPAND_GROUP = 4


def _peer_expand_kernel(e1_ref, e2_ref, g_ref, w_ref, stage_ref):
    nk = PEER_N_KEYS
    nsel = e1_ref.shape[1]
    rid = lax.broadcasted_iota(jnp.int32, (nk, nsel), 0).astype(F32)
    un = EXPAND_UNROLL

    def body(tt, carry):
        base = pl.multiple_of(tt * un, un)
        e1 = e1_ref[pl.ds(base, un), :]
        e2 = e2_ref[pl.ds(base, un), :]
        g = g_ref[pl.ds(base, un), :]
        for u0 in range(0, un, EXPAND_GROUP):
            us = range(u0, u0 + EXPAND_GROUP)
            lefts = [jnp.where(rid == e1[u:u + 1, :], g[u:u + 1, :], 0.0).astype(BF16) for u in us]
            rights = [jnp.where(rid == e2[u:u + 1, :], 1.0, 0.0).astype(BF16) for u in us]
            for u, l, r in zip(us, lefts, rights):
                stage_ref[u * nk:(u + 1) * nk, :] = _dot_nt(l, r)
        by_key = jnp.swapaxes(stage_ref[...].reshape(un, nk, nk), 0, 1)
        for a in range(nk):
            w_ref[pl.ds(base, un), a * nk:(a + 1) * nk] = by_key[a].astype(w_ref.dtype)
        return carry

    lax.fori_loop(0, e1_ref.shape[0] // un, body, 0)


def _peer_expand(e1, e2, g, tm=128):
    s, nsel = e1.shape
    nk = PEER_N_KEYS
    tm = min(tm, s)
    ib = pl.BlockSpec((tm, nsel), lambda i: (i, 0))
    return pl.pallas_call(
        _peer_expand_kernel,
        grid=(s // tm,),
        in_specs=[ib, ib, ib],
        out_specs=pl.BlockSpec((tm, nk * nk), lambda i: (i, 0)),
        out_shape=jax.ShapeDtypeStruct((s, nk * nk), BF16),
        scratch_shapes=[pltpu.VMEM((EXPAND_UNROLL * nk, nk), F32)],
        compiler_params=_params(("parallel",)),
        name="peer_expand",
    )(e1, e2, g)


FP8 = jnp.float8_e4m3fn
FP8_MAX = 448.0
SCALE_ROWS = F32_SUBLANES


def _rmsnorm_fp8_kernel(x_ref, g_ref, o_ref, o8_ref, s_ref):
    x = x_ref[...]
    ms = jnp.mean(x * x, axis=-1, keepdims=True)
    y = x * lax.rsqrt(ms + NORM_EPS) * g_ref[...]
    o_ref[...] = y.astype(o_ref.dtype)
    scale = jnp.maximum(jnp.max(jnp.abs(y), axis=-1, keepdims=True), 1e-30) * (1.0 / FP8_MAX)
    o8_ref[...] = (y / scale).astype(o8_ref.dtype)
    s_ref[...] = jnp.broadcast_to(scale, s_ref.shape)


def _rmsnorm_fp8(x, g, tm=256):
    s, d = x.shape
    blk = pl.BlockSpec((tm, d), lambda i: (i, 0))
    return pl.pallas_call(
        _rmsnorm_fp8_kernel,
        grid=(s // tm,),
        in_specs=[blk, pl.BlockSpec((1, d), lambda i: (0, 0))],
        out_specs=[blk, blk, pl.BlockSpec((tm, LANES), lambda i: (i, 0))],
        out_shape=[jax.ShapeDtypeStruct((s, d), BF16), jax.ShapeDtypeStruct((s, d), FP8),
                   jax.ShapeDtypeStruct((s, LANES), F32)],
        compiler_params=_params(("parallel",)),
        name="rmsnorm_fp8",
    )(x, g.reshape(1, d))


def _quant_rows_kernel(u_ref, u8_ref, s_ref):
    u = u_ref[...]
    scale = jnp.maximum(jnp.max(jnp.abs(u), axis=-1, keepdims=True), 1e-30) * (1.0 / FP8_MAX)
    u8_ref[...] = (u / scale).astype(u8_ref.dtype)
    s_ref[...] = jnp.broadcast_to(scale, (u.shape[0], LANES)).T[:SCALE_ROWS, :]


def _quant_rows(u, te=512):
    ne, d = u.shape
    return pl.pallas_call(
        _quant_rows_kernel,
        grid=(ne // te,),
        in_specs=[pl.BlockSpec((te, d), lambda e: (e, 0))],
        out_specs=[pl.BlockSpec((te, d), lambda e: (e, 0)),
                   pl.BlockSpec((SCALE_ROWS, te), lambda e: (0, e))],
        out_shape=[jax.ShapeDtypeStruct((ne, d), FP8), jax.ShapeDtypeStruct((SCALE_ROWS, ne), F32)],
        compiler_params=_params(("parallel",)),
        name="peer_quant_u",
    )(u)


def _peer_dense_kernel(h_ref, hs_ref, u_ref, us_ref, v_ref, vs_ref, w_ref, o_ref):
    @pl.when(pl.program_id(1) == 0)
    def _():
        o_ref[...] = jnp.zeros_like(o_ref)

    act = _dot_nt(h_ref[...], u_ref[...]) * hs_ref[:, 0:1] * us_ref[0:1, :]
    gelu = 0.5 * act * (1.0 + lax.erf(act * (2.0 ** -0.5)))
    mix = w_ref[...].astype(F32) * gelu * vs_ref[0:1, :]
    scale = jnp.maximum(jnp.max(jnp.abs(mix), axis=-1, keepdims=True), 1e-30) * (1.0 / FP8_MAX)
    o_ref[...] += _dot((mix / scale).astype(FP8), v_ref[...]) * scale


def _peer_dense(h8, hs, u8, us, v8, vs, wdense, tm=512, te=1024):
    s, d = h8.shape
    ne = u8.shape[0]
    tm = min(tm, s)
    return pl.pallas_call(
        _peer_dense_kernel,
        grid=(s // tm, ne // te),
        in_specs=[pl.BlockSpec((tm, d), lambda i, e: (i, 0)),
                  pl.BlockSpec((tm, LANES), lambda i, e: (i, 0)),
                  pl.BlockSpec((te, d), lambda i, e: (e, 0)),
                  pl.BlockSpec((SCALE_ROWS, te), lambda i, e: (0, e)),
                  pl.BlockSpec((te, d), lambda i, e: (e, 0)),
                  pl.BlockSpec((SCALE_ROWS, te), lambda i, e: (0, e)),
                  pl.BlockSpec((tm, te), lambda i, e: (i, e))],
        out_specs=pl.BlockSpec((tm, d), lambda i, e: (i, 0)),
        out_shape=jax.ShapeDtypeStruct((s, d), F32),
        compiler_params=_params(("parallel", "arbitrary")),
        name="peer_dense",
    )(h8, hs, u8, us, v8, vs, wdense)


def _add_rmsnorm_kernel(x_ref, y_ref, g_ref, o_ref):
    x = x_ref[...] + y_ref[...]
    ms = jnp.mean(x * x, axis=-1, keepdims=True)
    o_ref[...] = x * lax.rsqrt(ms + NORM_EPS) * g_ref[...]


def _add_rmsnorm(x, y, g, tm=256):
    s, d = x.shape
    blk = pl.BlockSpec((tm, d), lambda i: (i, 0))
    return pl.pallas_call(
        _add_rmsnorm_kernel,
        grid=(s // tm,),
        in_specs=[blk, blk, pl.BlockSpec((1, d), lambda i: (0, 0))],
        out_specs=blk,
        out_shape=jax.ShapeDtypeStruct((s, d), F32),
        compiler_params=_params(("parallel",)),
        name="residual_final_norm",
    )(x, y, g.reshape(1, d))


def _rwkv_branch(proj_r, w0, w_up, a0, a_up, g_up, k_k, k_a, r_k, lnx_g, lnx_b):
    g_up_pad = jnp.pad(g_up, ((0, GATE_LORA_PAD - GATE_LORA), (0, 0))).astype(BF16)
    wpre, a, g = _rwkv_lora(proj_r, w0, w_up.astype(BF16), a0, a_up.astype(BF16), g_up_pad)
    rt, bonus, x, y, opre, qb2, bbt, gst, pcm = _rwkv_chunk(proj_r, wpre, a, k_k, k_a, r_k.reshape(-1))
    return _rwkv_scan(x, y, rt, opre, qb2, bbt, gst, pcm, bonus, g, lnx_g, lnx_b)


def _rope_tables(s):
    half = MOBA_HEAD_DIM // 2
    inv_freq = ROPE_THETA ** (-jnp.arange(half, dtype=F32) / half)
    ang = jnp.arange(s, dtype=jnp.int32).astype(F32)[:, None] * inv_freq[None, :]
    cos = jnp.cos(ang)
    sin = jnp.sin(ang)
    return jnp.concatenate([cos, cos], axis=-1), jnp.concatenate([-sin, sin], axis=-1)


def _moba_branch(proj_m):
    s = proj_m.shape[0]
    cos, sin = _rope_tables(s)
    q, qs, k, vt, kmean = _moba_prep(proj_m, cos, sin)
    return _moba_attn(q, qs, k, vt, kmean.reshape(kmean.shape[0], kmean.shape[2]))


def _peer_layer(x1, norm2_g, w_q, sub_keys, expert_u, expert_v, final_g):
    s, d = x1.shape
    h2, h8, hs = _rmsnorm_fp8(x1, norm2_g)
    q = _matmul(h2, w_q.astype(BF16), F32, tm=min(1024, s), tn=512, name="peer_query")
    keys = sub_keys.reshape(PEER_HEADS * 2 * PEER_N_KEYS, PEER_HALF).astype(BF16)
    e1, e2, g = _peer_route(q, keys)
    wdense = _peer_expand(e1, e2, g)
    u8, us = _quant_rows(expert_u)
    v8, vs = _quant_rows(expert_v)
    peer = _peer_dense(h8, hs, u8, us, v8, vs, wdense)
    return _add_rmsnorm(x1, peer, final_g)


def kernel(x, norm1_g, w_in, rwkv_mu, rwkv_w0, rwkv_w_up, rwkv_a0, rwkv_a_up, rwkv_g_up, rwkv_k_k, rwkv_k_a, rwkv_r_k, rwkv_lnx_g, rwkv_lnx_b, w_branch_rwkv, w_branch_moba, w_out, norm2_g, peer_w_q, peer_sub_keys, peer_u, peer_v, final_g):
    b, s, d = x.shape
    depth = w_in.shape[0]
    assert b == 1 and depth == 1
    x2d = x.reshape(s, d)
    l = 0
    rw = rwkv_w0.shape[-1]
    shift_w = 3 * rw + DECAY_LORA + AAA_LORA + GATE_LORA
    tn = 512
    shift_pad = -(-shift_w // tn) * tn
    mw = w_branch_moba.shape[1]
    tm = min(1024, s)

    h = _rmsnorm(x2d, norm1_g[l], BF16)
    wt = jnp.transpose(w_in[l])
    mu = jnp.pad(rwkv_mu[l], (0, shift_pad - shift_w)).reshape(1, shift_pad)
    proj_r = _matmul_shift(h, wt, mu, tm=tm, tn=tn)
    proj_m = _matmul_nt(h, wt, F32, tm=tm, tn=tn, name="in_proj_moba", row0=shift_w, n=3 * mw)
    gates = _matmul_nt(h, wt, BF16, tm=tm, tn=tn, name="in_proj_gates", row0=shift_w + 3 * mw, n=2 * d)

    y_a = _rwkv_branch(proj_r, rwkv_w0[l], rwkv_w_up[l], rwkv_a0[l], rwkv_a_up[l], rwkv_g_up[l],
                       rwkv_k_k[l], rwkv_k_a[l], rwkv_r_k[l], rwkv_lnx_g[l], rwkv_lnx_b[l])
    y_b = _moba_branch(proj_m)
    mixed = _gated_pair(y_a, w_branch_rwkv[l].astype(BF16), y_b, w_branch_moba[l].astype(BF16),
                        gates, tm=tm, tn=512)
    x1 = _matmul_residual(mixed, w_out[l].astype(BF16), x2d, tm=tm, tn=512)
    out = _peer_layer(x1, norm2_g[l], peer_w_q[l], peer_sub_keys[l], peer_u[l], peer_v[l], final_g)
    return out.reshape(b, s, d)
```
